```python
import math
import jax, jax.numpy as jnp
from jax import lax
import numpy as np

D_MODEL = 1024
BATCH = 8
SEQ = 8192
DEPTH = 2

N_MIXERS = 2
A_CONFIGS = ((128, 1), (512, 4), (2048, 16))
A_GROUPS = len(A_CONFIGS)
A_HEADS = 16
A_HEAD_DIM = D_MODEL // A_HEADS
A_WIDTH = A_HEADS * A_HEAD_DIM
N_BUCKETS = 32
MAX_DISTANCE = 2048
B_HEADS = 4
B_DK = D_MODEL // 2 // B_HEADS
B_DV = D_MODEL // B_HEADS
B_QK = B_HEADS * B_DK
B_V = B_HEADS * B_DV
B_GATE_RANK = 16
B_TAU = 16.0
B_CHUNK = 64
B_IN = 2 * B_QK + B_V + B_GATE_RANK + B_V
D_FF = 2816
CONV_W = 3
EPS = 1e-6
NEG_INF = -1e30
N_A = (DEPTH + 1) // 2
N_B = DEPTH // 2

kernel_name = "hybrid_dilated_gla_convffn"


def rmsnorm(x, g):
    x32 = x.astype(jnp.float32)
    y = x32 * lax.rsqrt(jnp.mean(x32 * x32, axis=-1, keepdims=True) + EPS)
    return (y * g.astype(jnp.float32)).astype(x.dtype)


def t5_bucket(dist):
    max_exact = N_BUCKETS // 2
    n = jnp.maximum(dist, max_exact).astype(jnp.float32)
    large = max_exact + (jnp.log(n / max_exact) / math.log(MAX_DISTANCE / max_exact)
                         * (N_BUCKETS - max_exact)).astype(jnp.int32)
    large = jnp.minimum(large, N_BUCKETS - 1)
    return jnp.where(dist < max_exact, dist, large)


def dilated_branch(q, k, v, table, window, dilation):
    B, S, H, E = q.shape
    blk = window // dilation
    L = S // dilation
    nb = -(-L // blk)
    Lp = nb * blk

    def to_blocks(t):
        t = t.reshape(B, L, dilation, H, E)
        t = jnp.pad(t, ((0, 0), (0, Lp - L), (0, 0), (0, 0), (0, 0)))
        return t.reshape(B, nb, blk, dilation, H, E)

    def with_prev(t):
        prev = jnp.pad(t[:, :-1], ((0, 0), (1, 0), (0, 0), (0, 0), (0, 0), (0, 0)))
        return jnp.concatenate([prev, t], axis=2)

    qb = to_blocks(q)
    kk = with_prev(to_blocks(k))
    vv = with_prev(to_blocks(v))

    qi = jnp.arange(blk)[:, None]
    ki = jnp.arange(2 * blk)[None, :]
    steps = qi + blk - ki
    band = (steps >= 0) & (steps <= blk)
    first = (jnp.arange(nb)[:, None, None] == 0) & (ki[None] < blk)
    mask = band[None] & ~first
    bucket = t5_bucket(jnp.clip(steps, 0, blk) * dilation)
    bias = jnp.transpose(table[bucket], (2, 0, 1)).astype(jnp.float32)

    s = jnp.einsum('bnqrhe,bnkrhe->bnrhqk', qb, kk).astype(jnp.float32) + bias[None, None, None]
    s = jnp.where(mask[None, :, None, None], s, NEG_INF)
    m = jnp.max(s, axis=-1, keepdims=True)
    p = jnp.exp(s - m)
    den = jnp.sum(p, axis=-1)
    o = jnp.einsum('bnrhqk,bnkrhe->bnqrhe', p, vv.astype(jnp.float32))
    den_t = jnp.transpose(den, (0, 1, 4, 2, 3))
    o = o / den_t[..., None]
    lse = jnp.transpose(m[..., 0] + jnp.log(den), (0, 1, 4, 2, 3))
    o = o.reshape(B, Lp, dilation, H, E)[:, :L].reshape(B, S, H, E)
    lse = lse.reshape(B, Lp, dilation, H)[:, :L].reshape(B, S, H)
    return o, lse


def dilated_mixture(h, w_in, w_out, rel_bias):
    B, S, _ = h.shape
    qkv = (h @ w_in).reshape(B, S, A_GROUPS, 3, A_HEADS, A_HEAD_DIM)
    scale = A_HEAD_DIM ** -0.5
    outs, lses = [], []
    for g, (window, dilation) in enumerate(A_CONFIGS):
        o, lse = dilated_branch(qkv[:, :, g, 0] * scale, qkv[:, :, g, 1], qkv[:, :, g, 2],
                                rel_bias[:, g * A_HEADS:(g + 1) * A_HEADS], window, dilation)
        outs.append(o)
        lses.append(lse)
    wts = jax.nn.softmax(jnp.stack(lses), axis=0)
    o = jnp.sum(wts[..., None] * jnp.stack(outs), axis=0)
    return o.reshape(B, S, A_WIDTH).astype(h.dtype) @ w_out


def gla_mixer(h, w_in, w_gate, b_gate, g_norm, w_out):
    B, S, _ = h.shape
    N = S // B_CHUNK
    proj = h @ w_in
    o0 = 0
    q = proj[..., o0:o0 + B_QK]; o0 += B_QK
    k = proj[..., o0:o0 + B_QK]; o0 += B_QK
    v = proj[..., o0:o0 + B_V]; o0 += B_V
    glr = proj[..., o0:o0 + B_GATE_RANK]; o0 += B_GATE_RANK
    r = proj[..., o0:o0 + B_V]
    gk = jax.nn.log_sigmoid((glr @ w_gate + b_gate).astype(jnp.float32)) / B_TAU

    def chunks(t, e):
        t = t.astype(jnp.float32).reshape(B, N, B_CHUNK, B_HEADS, e)
        return jnp.transpose(t, (0, 3, 1, 2, 4))

    q, k, gk = chunks(q, B_DK) * (B_DK ** -0.5), chunks(k, B_DK), chunks(gk, B_DK)
    v = chunks(v, B_DV)
    bcum = jnp.cumsum(gk, axis=3)
    blast = bcum[:, :, :, -1:, :]
    q_t = q * jnp.exp(bcum)
    k_t = k * jnp.exp(-bcum)
    k_d = k * jnp.exp(blast - bcum)
    causal = jnp.tril(jnp.ones((B_CHUNK, B_CHUNK), dtype=bool))
    A = jnp.where(causal, jnp.einsum('bhncd,bhnsd->bhncs', q_t, k_t), 0.0)
    o_intra = jnp.einsum('bhncs,bhnse->bhnce', A, v)
    kv = jnp.einsum('bhncd,bhnce->bhnde', k_d, v)
    decay = jnp.exp(blast[:, :, :, 0, :])

    def step(state, inp):
        dec, inc = inp
        return dec[..., None] * state + inc, state

    _, s_prev = lax.scan(step, jnp.zeros((B, B_HEADS, B_DK, B_DV), jnp.float32),
                         (jnp.moveaxis(decay, 2, 0), jnp.moveaxis(kv, 2, 0)))
    s_prev = jnp.moveaxis(s_prev, 0, 2)
    o = o_intra + jnp.einsum('bhncd,bhnde->bhnce', q_t, s_prev)
    o = jnp.transpose(o, (0, 2, 3, 1, 4)).reshape(B, S, B_HEADS, B_DV)
    o = rmsnorm(o, g_norm).reshape(B, S, B_V).astype(h.dtype)
    return (o * jax.nn.silu(r)) @ w_out


def conv_ffn(h, w_up, conv_w, conv_b, w_down):
    u = h @ w_up
    up = jnp.pad(u, ((0, 0), (CONV_W - 1, 0), (0, 0)))
    S = h.shape[1]
    u = conv_b + sum(conv_w[j] * up[:, j:j + S] for j in range(CONV_W))
    a, b = jnp.split(u, 2, axis=-1)
    return (jax.nn.silu(a) * b) @ w_down


def _fwd_setup_inputs(seed: int = 0) -> dict:
    key = jax.random.key(seed)
    ks = jax.random.split(key, 24)
    f32 = jnp.float32
    nrm = lambda k, shape, s: jax.random.normal(k, shape, f32) * s
    D = D_MODEL
    return {
        "x": nrm(ks[0], (BATCH, SEQ, D), 1.0),
        "c": nrm(ks[1], (BATCH, D), 1.0),
        "w_in_a": nrm(ks[2], (N_A, D, A_GROUPS * 3 * A_WIDTH), D ** -0.5),
        "w_out_a": nrm(ks[3], (N_A, A_WIDTH, D), A_WIDTH ** -0.5),
        "rel_bias": nrm(ks[4], (N_BUCKETS, A_GROUPS * A_HEADS), 0.5),
        "w_in_b": nrm(ks[5], (N_B, D, B_IN), D ** -0.5),
        "w_gate_b": nrm(ks[6], (N_B, B_GATE_RANK, B_QK), B_GATE_RANK ** -0.5),
        "b_gate_b": nrm(ks[7], (N_B, B_QK), 0.1),
        "gnorm_b": 1.0 + nrm(ks[8], (N_B, B_DV), 0.02),
        "w_out_b": nrm(ks[9], (N_B, B_V, D), B_V ** -0.5),
        "norm_mix": 1.0 + nrm(ks[10], (DEPTH, D), 0.02),
        "norm_ffn": 1.0 + nrm(ks[11], (DEPTH, D), 0.02),
        "w_ada": nrm(ks[12], (DEPTH, D, 6 * D), 0.5 * D ** -0.5),
        "b_ada": nrm(ks[13], (DEPTH, 6 * D), 0.02),
        "w_up": nrm(ks[14], (DEPTH, D, 2 * D_FF), D ** -0.5),
        "conv_w": nrm(ks[15], (DEPTH, CONV_W, 2 * D_FF), CONV_W ** -0.5),
        "conv_b": nrm(ks[16], (DEPTH, 2 * D_FF), 0.02),
        "w_down": nrm(ks[17], (DEPTH, D_FF, D), D_FF ** -0.5),
        "norm_final": 1.0 + nrm(ks[18], (D,), 0.02),
    }


def _fwd_reference(x, c, w_in_a, w_out_a, rel_bias, w_in_b, w_gate_b, b_gate_b, gnorm_b, w_out_b,
              norm_mix, norm_ffn, w_ada, b_ada, w_up, conv_w, conv_b, w_down, norm_final):
    for i in range(DEPTH):
        mod = jax.nn.silu(c) @ w_ada[i] + b_ada[i]
        sh1, sc1, g1, sh2, sc2, g2 = [m[:, None, :] for m in jnp.split(mod, 6, axis=-1)]
        h = rmsnorm(x, norm_mix[i]) * (1.0 + sc1) + sh1
        if i % N_MIXERS == 0:
            j = i // N_MIXERS
            y = dilated_mixture(h, w_in_a[j], w_out_a[j], rel_bias)
        else:
            j = i // N_MIXERS
            y = gla_mixer(h, w_in_b[j], w_gate_b[j], b_gate_b[j], gnorm_b[j], w_out_b[j])
        x = x + g1 * y
        h = rmsnorm(x, norm_ffn[i]) * (1.0 + sc2) + sh2
        x = x + g2 * conv_ffn(h, w_up[i], conv_w[i], conv_b[i], w_down[i])
    return rmsnorm(x, norm_final)


import jax as _jax
import jax.numpy as _jnp

TWIN_FORMAT = 'train_step'
FWD_PARAMS = ['x', 'c', 'w_in_a', 'w_out_a', 'rel_bias', 'w_in_b', 'w_gate_b', 'b_gate_b', 'gnorm_b', 'w_out_b', 'norm_mix', 'norm_ffn', 'w_ada', 'b_ada', 'w_up', 'conv_w', 'conv_b', 'w_down', 'norm_final']
TWIN_WEIGHTS = ['w_in_a', 'w_out_a', 'rel_bias', 'w_in_b', 'w_gate_b', 'b_gate_b', 'gnorm_b', 'w_out_b', 'norm_mix', 'norm_ffn', 'w_ada', 'b_ada', 'w_up', 'conv_w', 'conv_b', 'w_down', 'norm_final']
TWIN_DIFF_INPUT = 'x'
TWIN_INPUTS = ['x', 'c', 'w_in_a', 'w_out_a', 'rel_bias', 'w_in_b', 'w_gate_b', 'b_gate_b', 'gnorm_b', 'w_out_b', 'norm_mix', 'norm_ffn', 'w_ada', 'b_ada', 'w_up', 'conv_w', 'conv_b', 'w_down', 'norm_final', 'loss_target', 'm_w_in_a', 'm_w_out_a', 'm_rel_bias', 'm_w_in_b', 'm_w_gate_b', 'm_b_gate_b', 'm_gnorm_b', 'm_w_out_b', 'm_norm_mix', 'm_norm_ffn', 'm_w_ada', 'm_b_ada', 'm_w_up', 'm_conv_w', 'm_conv_b', 'm_w_down', 'm_norm_final', 'v_w_in_a', 'v_w_out_a', 'v_rel_bias', 'v_w_in_b', 'v_w_gate_b', 'v_b_gate_b', 'v_gnorm_b', 'v_w_out_b', 'v_norm_mix', 'v_norm_ffn', 'v_w_ada', 'v_b_ada', 'v_w_up', 'v_conv_w', 'v_conv_b', 'v_w_down', 'v_norm_final']
TWIN_OUTPUTS = ['loss', 'grad_x', 'grad_w_in_a', 'grad_w_out_a', 'grad_rel_bias', 'grad_w_in_b', 'grad_w_gate_b', 'grad_b_gate_b', 'grad_gnorm_b', 'grad_w_out_b', 'grad_norm_mix', 'grad_norm_ffn', 'grad_w_ada', 'grad_b_ada', 'grad_w_up', 'grad_conv_w', 'grad_conv_b', 'grad_w_down', 'grad_norm_final', 'delta_w_in_a', 'delta_w_out_a', 'delta_rel_bias', 'delta_w_in_b', 'delta_w_gate_b', 'delta_b_gate_b', 'delta_gnorm_b', 'delta_w_out_b', 'delta_norm_mix', 'delta_norm_ffn', 'delta_w_ada', 'delta_b_ada', 'delta_w_up', 'delta_conv_w', 'delta_conv_b', 'delta_w_down', 'delta_norm_final', 'new_m_w_in_a', 'new_m_w_out_a', 'new_m_rel_bias', 'new_m_w_in_b', 'new_m_w_gate_b', 'new_m_b_gate_b', 'new_m_gnorm_b', 'new_m_w_out_b', 'new_m_norm_mix', 'new_m_norm_ffn', 'new_m_w_ada', 'new_m_b_ada', 'new_m_w_up', 'new_m_conv_w', 'new_m_conv_b', 'new_m_w_down', 'new_m_norm_final', 'new_v_w_in_a', 'new_v_w_out_a', 'new_v_rel_bias', 'new_v_w_in_b', 'new_v_w_gate_b', 'new_v_b_gate_b', 'new_v_gnorm_b', 'new_v_w_out_b', 'new_v_norm_mix', 'new_v_norm_ffn', 'new_v_w_ada', 'new_v_b_ada', 'new_v_w_up', 'new_v_conv_w', 'new_v_conv_b', 'new_v_w_down', 'new_v_norm_final']
TWIN_LEAF_KINDS = {'loss': 'loss', 'grad_x': 'grad_x', 'grad_w_in_a': 'grad_w', 'grad_w_out_a': 'grad_w', 'grad_rel_bias': 'grad_w', 'grad_w_in_b': 'grad_w', 'grad_w_gate_b': 'grad_w', 'grad_b_gate_b': 'grad_w', 'grad_gnorm_b': 'grad_w', 'grad_w_out_b': 'grad_w', 'grad_norm_mix': 'grad_w', 'grad_norm_ffn': 'grad_w', 'grad_w_ada': 'grad_w', 'grad_b_ada': 'grad_w', 'grad_w_up': 'grad_w', 'grad_conv_w': 'grad_w', 'grad_conv_b': 'grad_w', 'grad_w_down': 'grad_w', 'grad_norm_final': 'grad_w', 'delta_w_in_a': 'delta_w', 'delta_w_out_a': 'delta_w', 'delta_rel_bias': 'delta_w', 'delta_w_in_b': 'delta_w', 'delta_w_gate_b': 'delta_w', 'delta_b_gate_b': 'delta_w', 'delta_gnorm_b': 'delta_w', 'delta_w_out_b': 'delta_w', 'delta_norm_mix': 'delta_w', 'delta_norm_ffn': 'delta_w', 'delta_w_ada': 'delta_w', 'delta_b_ada': 'delta_w', 'delta_w_up': 'delta_w', 'delta_conv_w': 'delta_w', 'delta_conv_b': 'delta_w', 'delta_w_down': 'delta_w', 'delta_norm_final': 'delta_w', 'new_m_w_in_a': 'new_m', 'new_m_w_out_a': 'new_m', 'new_m_rel_bias': 'new_m', 'new_m_w_in_b': 'new_m', 'new_m_w_gate_b': 'new_m', 'new_m_b_gate_b': 'new_m', 'new_m_gnorm_b': 'new_m', 'new_m_w_out_b': 'new_m', 'new_m_norm_mix': 'new_m', 'new_m_norm_ffn': 'new_m', 'new_m_w_ada': 'new_m', 'new_m_b_ada': 'new_m', 'new_m_w_up': 'new_m', 'new_m_conv_w': 'new_m', 'new_m_conv_b': 'new_m', 'new_m_w_down': 'new_m', 'new_m_norm_final': 'new_m', 'new_v_w_in_a': 'new_v', 'new_v_w_out_a': 'new_v', 'new_v_rel_bias': 'new_v', 'new_v_w_in_b': 'new_v', 'new_v_w_gate_b': 'new_v', 'new_v_b_gate_b': 'new_v', 'new_v_gnorm_b': 'new_v', 'new_v_w_out_b': 'new_v', 'new_v_norm_mix': 'new_v', 'new_v_norm_ffn': 'new_v', 'new_v_w_ada': 'new_v', 'new_v_b_ada': 'new_v', 'new_v_w_up': 'new_v', 'new_v_conv_w': 'new_v', 'new_v_conv_b': 'new_v', 'new_v_w_down': 'new_v', 'new_v_norm_final': 'new_v'}


def _forward(args):
    return _fwd_reference(*[args[k] for k in FWD_PARAMS])


def _output_shape():
    def fwd():
        inp = _fwd_setup_inputs(0)
        return _fwd_reference(*[inp[k] for k in FWD_PARAMS])
    out = _jax.eval_shape(fwd)
    return out.shape, out.dtype

N_MICROBATCH = 1
ADAM_LR = 0.001
ADAM_B1 = 0.9
ADAM_B2 = 0.999
ADAM_EPS = 1e-08
ADAM_WD = 0.01
ADAM_STEP = 10
PER_EXAMPLE_BATCH_AXIS = {'x': 0, 'c': 0, 'loss_target': 0}
SHARED_INPUTS = []
_WEIGHT_DTYPES = {'w_in_a': _jnp.float32, 'w_out_a': _jnp.float32, 'rel_bias': _jnp.float32, 'w_in_b': _jnp.float32, 'w_gate_b': _jnp.float32, 'b_gate_b': _jnp.float32, 'gnorm_b': _jnp.float32, 'w_out_b': _jnp.float32, 'norm_mix': _jnp.float32, 'norm_ffn': _jnp.float32, 'w_ada': _jnp.float32, 'b_ada': _jnp.float32, 'w_up': _jnp.float32, 'conv_w': _jnp.float32, 'conv_b': _jnp.float32, 'w_down': _jnp.float32, 'norm_final': _jnp.float32}
MOMENT_SCALE = {'w_in_a': 1.312474e-02, 'w_out_a': 2.785117e-02, 'rel_bias': 1.441069e-02, 'w_in_b': 5.962519e-02, 'w_gate_b': 1.511556e-02, 'b_gate_b': 4.577033e-02, 'gnorm_b': 1.330466e-01, 'w_out_b': 4.937510e-02, 'norm_mix': 7.278060e-02, 'norm_ffn': 8.192616e-02, 'w_ada': 6.993647e-02, 'b_ada': 1.182369e-01, 'w_up': 3.402558e-02, 'conv_w': 3.394477e-02, 'conv_b': 3.074854e-02, 'w_down': 5.559792e-02, 'norm_final': 6.405437e+01}


def _to_microbatches(a, axis):
    t = _jnp.moveaxis(a, axis, 0)
    t = t.reshape((N_MICROBATCH, t.shape[0] // N_MICROBATCH) + t.shape[1:])
    return _jnp.moveaxis(t, 1, axis + 1)


def setup_inputs(seed: int = 0) -> dict:
    inp = _fwd_setup_inputs(seed)
    key = _jax.random.fold_in(_jax.random.key(seed), 7919)
    shape, _ = _output_shape()
    out = dict(inp)
    out["loss_target"] = _jax.random.normal(_jax.random.fold_in(key, 0), shape, _jnp.float32)
    for i, name in enumerate(TWIN_WEIGHTS):
        w = inp[name].astype(_jnp.float32)
        if MOMENT_SCALE is None:
            s = _jnp.sqrt(_jnp.mean(_jnp.square(w)) + 1e-30)
        else:
            s = MOMENT_SCALE[name]
        km, kv = _jax.random.split(_jax.random.fold_in(key, i + 1))
        out[name] = w
        out["m_" + name] = s * _jax.random.normal(km, w.shape, _jnp.float32)
        out["v_" + name] = (s * s) * _jax.random.uniform(kv, w.shape, _jnp.float32, 0.5, 1.5)
    if N_MICROBATCH > 1:
        for name, axis in PER_EXAMPLE_BATCH_AXIS.items():
            out[name] = _to_microbatches(out[name], axis)
    return {'x': out['x'], 'c': out['c'], 'w_in_a': out['w_in_a'], 'w_out_a': out['w_out_a'], 'rel_bias': out['rel_bias'], 'w_in_b': out['w_in_b'], 'w_gate_b': out['w_gate_b'], 'b_gate_b': out['b_gate_b'], 'gnorm_b': out['gnorm_b'], 'w_out_b': out['w_out_b'], 'norm_mix': out['norm_mix'], 'norm_ffn': out['norm_ffn'], 'w_ada': out['w_ada'], 'b_ada': out['b_ada'], 'w_up': out['w_up'], 'conv_w': out['conv_w'], 'conv_b': out['conv_b'], 'w_down': out['w_down'], 'norm_final': out['norm_final'], 'loss_target': out['loss_target'], 'm_w_in_a': out['m_w_in_a'], 'm_w_out_a': out['m_w_out_a'], 'm_rel_bias': out['m_rel_bias'], 'm_w_in_b': out['m_w_in_b'], 'm_w_gate_b': out['m_w_gate_b'], 'm_b_gate_b': out['m_b_gate_b'], 'm_gnorm_b': out['m_gnorm_b'], 'm_w_out_b': out['m_w_out_b'], 'm_norm_mix': out['m_norm_mix'], 'm_norm_ffn': out['m_norm_ffn'], 'm_w_ada': out['m_w_ada'], 'm_b_ada': out['m_b_ada'], 'm_w_up': out['m_w_up'], 'm_conv_w': out['m_conv_w'], 'm_conv_b': out['m_conv_b'], 'm_w_down': out['m_w_down'], 'm_norm_final': out['m_norm_final'], 'v_w_in_a': out['v_w_in_a'], 'v_w_out_a': out['v_w_out_a'], 'v_rel_bias': out['v_rel_bias'], 'v_w_in_b': out['v_w_in_b'], 'v_w_gate_b': out['v_w_gate_b'], 'v_b_gate_b': out['v_b_gate_b'], 'v_gnorm_b': out['v_gnorm_b'], 'v_w_out_b': out['v_w_out_b'], 'v_norm_mix': out['v_norm_mix'], 'v_norm_ffn': out['v_norm_ffn'], 'v_w_ada': out['v_w_ada'], 'v_b_ada': out['v_b_ada'], 'v_w_up': out['v_w_up'], 'v_conv_w': out['v_conv_w'], 'v_conv_b': out['v_conv_b'], 'v_w_down': out['v_w_down'], 'v_norm_final': out['v_norm_final']}


def _loss(weights, diff, rest, loss_target):
    with _jax.named_scope("forward"):
        args = {**rest, TWIN_DIFF_INPUT: diff, **{k: w.astype(_WEIGHT_DTYPES[k]) for k, w in weights.items()}}
        y = _forward(args)
    with _jax.named_scope("loss_head"):
        err = _jnp.square(y.astype(_jnp.float32) - loss_target)
        return 0.5 * _jnp.sum(_jnp.mean(err, axis=-1)) if err.ndim else 0.5 * err


def _adamw(w, g, m, v):
    m = ADAM_B1 * m + (1.0 - ADAM_B1) * g
    v = ADAM_B2 * v + (1.0 - ADAM_B2) * _jnp.square(g)
    m_hat = m / (1.0 - ADAM_B1 ** ADAM_STEP)
    v_hat = v / (1.0 - ADAM_B2 ** ADAM_STEP)
    delta = -ADAM_LR * (m_hat / (_jnp.sqrt(v_hat) + ADAM_EPS) + ADAM_WD * w)
    return delta, m, v


def reference(x, c, w_in_a, w_out_a, rel_bias, w_in_b, w_gate_b, b_gate_b, gnorm_b, w_out_b, norm_mix, norm_ffn, w_ada, b_ada, w_up, conv_w, conv_b, w_down, norm_final, loss_target, m_w_in_a, m_w_out_a, m_rel_bias, m_w_in_b, m_w_gate_b, m_b_gate_b, m_gnorm_b, m_w_out_b, m_norm_mix, m_norm_ffn, m_w_ada, m_b_ada, m_w_up, m_conv_w, m_conv_b, m_w_down, m_norm_final, v_w_in_a, v_w_out_a, v_rel_bias, v_w_in_b, v_w_gate_b, v_b_gate_b, v_gnorm_b, v_w_out_b, v_norm_mix, v_norm_ffn, v_w_ada, v_b_ada, v_w_up, v_conv_w, v_conv_b, v_w_down, v_norm_final):
    given = dict(x=x, c=c, w_in_a=w_in_a, w_out_a=w_out_a, rel_bias=rel_bias, w_in_b=w_in_b, w_gate_b=w_gate_b, b_gate_b=b_gate_b, gnorm_b=gnorm_b, w_out_b=w_out_b, norm_mix=norm_mix, norm_ffn=norm_ffn, w_ada=w_ada, b_ada=b_ada, w_up=w_up, conv_w=conv_w, conv_b=conv_b, w_down=w_down, norm_final=norm_final, loss_target=loss_target, m_w_in_a=m_w_in_a, m_w_out_a=m_w_out_a, m_rel_bias=m_rel_bias, m_w_in_b=m_w_in_b, m_w_gate_b=m_w_gate_b, m_b_gate_b=m_b_gate_b, m_gnorm_b=m_gnorm_b, m_w_out_b=m_w_out_b, m_norm_mix=m_norm_mix, m_norm_ffn=m_norm_ffn, m_w_ada=m_w_ada, m_b_ada=m_b_ada, m_w_up=m_w_up, m_conv_w=m_conv_w, m_conv_b=m_conv_b, m_w_down=m_w_down, m_norm_final=m_norm_final, v_w_in_a=v_w_in_a, v_w_out_a=v_w_out_a, v_rel_bias=v_rel_bias, v_w_in_b=v_w_in_b, v_w_gate_b=v_w_gate_b, v_b_gate_b=v_b_gate_b, v_gnorm_b=v_gnorm_b, v_w_out_b=v_w_out_b, v_norm_mix=v_norm_mix, v_norm_ffn=v_norm_ffn, v_w_ada=v_w_ada, v_b_ada=v_b_ada, v_w_up=v_w_up, v_conv_w=v_conv_w, v_conv_b=v_conv_b, v_w_down=v_w_down, v_norm_final=v_norm_final)
    weights = {n: given[n] for n in TWIN_WEIGHTS}
    shared = {n: given[n] for n in SHARED_INPUTS}
    per_example = {n: given[n] for n in ['x', 'c']}
    grad_fn = _jax.value_and_grad(_loss, argnums=(0, 1))

    def one_microbatch(ex, loss_target):
        ex = dict(ex)
        diff = ex.pop(TWIN_DIFF_INPUT)
        return grad_fn(weights, diff, {**shared, **ex}, loss_target)

    if N_MICROBATCH == 1:
        loss, (grad_w, grad_x) = one_microbatch(per_example, given["loss_target"])
    else:
        def body(carry, xs):
            loss_sum, grad_sum = carry
            l_k, (gw_k, gx_k) = one_microbatch(xs[0], xs[1])
            with _jax.named_scope("update"):
                return (loss_sum + l_k, _jax.tree.map(_jnp.add, grad_sum, gw_k)), gx_k

        init = (_jnp.zeros((), _jnp.float32), _jax.tree.map(_jnp.zeros_like, weights))
        (loss, grad_w), grad_x = _jax.lax.scan(body, init, (per_example, given["loss_target"]))
    with _jax.named_scope("update"):
        delta_w, new_m, new_v = {}, {}, {}
        for n in TWIN_WEIGHTS:
            delta_w[n], new_m[n], new_v[n] = _adamw(weights[n], grad_w[n], given["m_" + n], given["v_" + n])
    return (loss, grad_x, *[grad_w[n] for n in TWIN_WEIGHTS], *[delta_w[n] for n in TWIN_WEIGHTS],
            *[new_m[n] for n in TWIN_WEIGHTS], *[new_v[n] for n in TWIN_WEIGHTS])
```

```python
import functools
import math

import numpy as np
import jax
import jax.numpy as jnp
from jax import lax
from jax.experimental import pallas as pl
from jax.experimental.pallas import tpu as pltpu

F32 = jnp.float32
BF16 = jnp.bfloat16
MESH = pl.DeviceIdType.MESH

D_MODEL = 1024
A_CONFIGS = ((128, 1), (512, 4), (2048, 16))
A_HEADS = 16
A_HEAD_DIM = 64
A_BLK = 128
N_BUCKETS = 32
MAX_DISTANCE = 2048
B_HEADS = 4
B_DK = 128
B_DV = 256
B_QK = 512
B_V = 1024
B_GATE_RANK = 16
B_TAU = 16.0
B_CHUNK = 64
B_IN_PAD = 3200
D_FF = 2816
EPS = 1e-6
NEG_INF = -1e30
ADAM_LR = 0.001
ADAM_B1 = 0.9
ADAM_B2 = 0.999
ADAM_EPS = 1e-08
ADAM_WD = 0.01
ADAM_STEP = 10

LANES = 1024
VMEM_LIMIT = 48 * 1024 * 1024
ROW_TILE = 256
GLA_ROWS = 512

HBM = pl.BlockSpec(memory_space=pl.ANY)


def _params(n_axes):
    return pltpu.CompilerParams(dimension_semantics=("arbitrary",) * n_axes, vmem_limit_bytes=VMEM_LIMIT)


def _pick(n, cap, mult=128):
    best = None
    for t in range(mult, min(n, cap) + 1, mult):
        if n % t == 0:
            best = t
    return n if best is None else best


def _matmul(a, b, mode, out_dtype, name, add=None):
    if mode == "nn":
        (M, K), (_, N) = a.shape, b.shape
        dims = (((1,), (0,)), ((), ()))
    elif mode == "nt":
        (M, K), (N, _) = a.shape, b.shape
        dims = (((1,), (1,)), ((), ()))
    else:
        (K, M), (_, N) = a.shape, b.shape
        dims = (((0,), (0,)), ((), ()))
    tm = _pick(M, 1024, 128 if mode == "tn" else 8)
    tn = _pick(N, 1536)
    tk = _pick(K, 1024 if mode != "tn" else 512)
    nk = K // tk
    a_spec = {"nn": pl.BlockSpec((tm, tk), lambda i, j, k: (i, k)),
              "nt": pl.BlockSpec((tm, tk), lambda i, j, k: (i, k)),
              "tn": pl.BlockSpec((tk, tm), lambda i, j, k: (k, i))}[mode]
    b_spec = {"nn": pl.BlockSpec((tk, tn), lambda i, j, k: (k, j)),
              "nt": pl.BlockSpec((tn, tk), lambda i, j, k: (j, k)),
              "tn": pl.BlockSpec((tk, tn), lambda i, j, k: (k, j))}[mode]
    o_spec = pl.BlockSpec((tm, tn), lambda i, j, k: (i, j))
    has_add = add is not None

    def body(*refs):
        if has_add:
            a_ref, b_ref, c_ref, o_ref, acc_ref = refs
        else:
            a_ref, b_ref, o_ref, acc_ref = refs
        k = pl.program_id(2)
        part = lax.dot_general(a_ref[...].astype(BF16), b_ref[...].astype(BF16), dims, preferred_element_type=F32)

        @pl.when(k == 0)
        def _():
            acc_ref[...] = part

        @pl.when(k > 0)
        def _():
            acc_ref[...] += part

        @pl.when(k == nk - 1)
        def _():
            r = acc_ref[...]
            if has_add:
                r = r + c_ref[...].astype(F32)
            o_ref[...] = r.astype(o_ref.dtype)

    ins = [a, b] + ([add] if has_add else [])
    in_specs = [a_spec, b_spec] + ([o_spec] if has_add else [])
    return pl.pallas_call(
        body, name=name, grid=(M // tm, N // tn, nk), in_specs=in_specs, out_specs=o_spec,
        out_shape=jax.ShapeDtypeStruct((M, N), out_dtype), scratch_shapes=[pltpu.VMEM((tm, tn), F32)],
        compiler_params=_params(3))(*ins)


def _row(tr, d=D_MODEL):
    return pl.BlockSpec((tr, d), lambda i: (i, 0))


def _vec(d=D_MODEL):
    return pl.BlockSpec((1, d), lambda i: (0, 0))


def _modspec(j):
    return pl.BlockSpec((8, D_MODEL), lambda i: (0, j))


def _silu(x):
    return x * jax.nn.sigmoid(x)


def _dsilu(x):
    s = jax.nn.sigmoid(x)
    return s * (1.0 + x * (1.0 - s))


def _ada_mod(c, w_ada, b_ada, name):
    n = w_ada.shape[1]
    tn = _pick(n, 1536)

    def body(c_ref, w_ref, b_ref, o_ref):
        sc = jnp.broadcast_to(_silu(c_ref[...]), (8, D_MODEL)).astype(BF16)
        o_ref[...] = jnp.dot(sc, w_ref[...], preferred_element_type=F32) + b_ref[...]

    return pl.pallas_call(
        body, name=name, grid=(n // tn,),
        in_specs=[_vec(), pl.BlockSpec((D_MODEL, tn), lambda j: (0, j)), pl.BlockSpec((1, tn), lambda j: (0, j))],
        out_specs=pl.BlockSpec((8, tn), lambda j: (0, j)), out_shape=jax.ShapeDtypeStruct((8, n), F32),
        compiler_params=_params(1))(c, w_ada, b_ada)


def _ada_outer(c, dmod, name):
    n = dmod.shape[1]
    tn = _pick(n, 1536)

    def body(c_ref, d_ref, o_ref):
        row = lax.broadcasted_iota(jnp.int32, (8, 1), 0) == 0
        a = jnp.where(row, jnp.broadcast_to(_silu(c_ref[...]), (8, D_MODEL)), 0.0).astype(BF16)
        b = jnp.where(row, jnp.broadcast_to(d_ref[...], (8, tn)), 0.0).astype(BF16)
        o_ref[...] = lax.dot_general(a, b, (((0,), (0,)), ((), ())), preferred_element_type=F32)

    return pl.pallas_call(
        body, name=name, grid=(n // tn,),
        in_specs=[_vec(), pl.BlockSpec((1, tn), lambda j: (0, j))],
        out_specs=pl.BlockSpec((D_MODEL, tn), lambda j: (0, j)),
        out_shape=jax.ShapeDtypeStruct((D_MODEL, n), F32), compiler_params=_params(1))(c, dmod)


def _norm_mod(x, gamma, mod, j_sc, j_sh, name, resid=None, gate_mod=None, j_gate=None):
    S = x.shape[0]
    tr = ROW_TILE
    has_res = resid is not None

    def body(*refs):
        if has_res:
            x_ref, y_ref, gate_ref, g_ref, sc_ref, sh_ref, xo_ref, h_ref = refs
            xn = x_ref[...] + gate_ref[0:1, :] * y_ref[...]
            xo_ref[...] = xn
        else:
            x_ref, g_ref, sc_ref, sh_ref, h_ref = refs
            xn = x_ref[...]
        r = lax.rsqrt(jnp.mean(xn * xn, axis=-1, keepdims=True) + EPS)
        n = (xn * r) * g_ref[...]
        h_ref[...] = (n * (1.0 + sc_ref[0:1, :]) + sh_ref[0:1, :]).astype(BF16)

    if has_res:
        ins = [x, resid, gate_mod, gamma, mod, mod]
        in_specs = [_row(tr), _row(tr), _modspec(j_gate), _vec(), _modspec(j_sc), _modspec(j_sh)]
        out_specs = [_row(tr), _row(tr)]
        out_shape = [jax.ShapeDtypeStruct((S, D_MODEL), F32), jax.ShapeDtypeStruct((S, D_MODEL), BF16)]
    else:
        ins = [x, gamma, mod, mod]
        in_specs = [_row(tr), _vec(), _modspec(j_sc), _modspec(j_sh)]
        out_specs = _row(tr)
        out_shape = jax.ShapeDtypeStruct((S, D_MODEL), BF16)
    return pl.pallas_call(body, name=name, grid=(S // tr,), in_specs=in_specs, out_specs=out_specs,
                          out_shape=out_shape, compiler_params=_params(1))(*ins)


def _final_loss(x, resid, mod, j_gate, gamma, tgt, name):
    S = x.shape[0]
    tr = ROW_TILE

    def body(x_ref, y_ref, gate_ref, g_ref, t_ref, dx_ref, loss_ref, dg_ref):
        @pl.when(pl.program_id(0) == 0)
        def _():
            loss_ref[...] = jnp.zeros_like(loss_ref)
            dg_ref[...] = jnp.zeros_like(dg_ref)

        xn = x_ref[...] + gate_ref[0:1, :] * y_ref[...]
        r = lax.rsqrt(jnp.mean(xn * xn, axis=-1, keepdims=True) + EPS)
        xhat = xn * r
        err = xhat * g_ref[...] - t_ref[...]
        loss_ref[...] += 0.5 * jnp.sum(jnp.mean(err * err, axis=-1, keepdims=True))
        dy = err * (1.0 / D_MODEL)
        dg_ref[...] += jnp.sum(dy * xhat, axis=0, keepdims=True)
        dxh = dy * g_ref[...]
        dx_ref[...] = r * (dxh - xhat * jnp.mean(dxh * xhat, axis=-1, keepdims=True))

    return pl.pallas_call(
        body, name=name, grid=(S // tr,),
        in_specs=[_row(tr), _row(tr), _modspec(j_gate), _vec(), _row(tr)],
        out_specs=[_row(tr), pl.BlockSpec((1, 128), lambda i: (0, 0)), _vec()],
        out_shape=[jax.ShapeDtypeStruct((S, D_MODEL), F32), jax.ShapeDtypeStruct((1, 128), F32),
                   jax.ShapeDtypeStruct((1, D_MODEL), F32)],
        compiler_params=_params(1))(x, resid, mod, gamma, tgt)


def _gate_bwd(dx, y, mod, j_gate, name):
    S = dx.shape[0]
    tr = ROW_TILE

    def body(dx_ref, y_ref, gate_ref, dy_ref, dg_ref):
        @pl.when(pl.program_id(0) == 0)
        def _():
            dg_ref[...] = jnp.zeros_like(dg_ref)

        dx_v = dx_ref[...]
        dy_ref[...] = (gate_ref[0:1, :] * dx_v).astype(BF16)
        dg_ref[...] += jnp.sum(dx_v * y_ref[...], axis=0, keepdims=True)

    return pl.pallas_call(
        body, name=name, grid=(S // tr,), in_specs=[_row(tr), _row(tr), _modspec(j_gate)],
        out_specs=[_row(tr), _vec()],
        out_shape=[jax.ShapeDtypeStruct((S, D_MODEL), BF16), jax.ShapeDtypeStruct((1, D_MODEL), F32)],
        compiler_params=_params(1))(dx, y, mod)


def _norm_mod_bwd(dh, x, dx_res, gamma, mod, j_sc, name):
    S = x.shape[0]
    tr = ROW_TILE

    def body(dh_ref, x_ref, dr_ref, g_ref, sc_ref, dx_ref, dsc_ref, dsh_ref, dg_ref):
        @pl.when(pl.program_id(0) == 0)
        def _():
            dsc_ref[...] = jnp.zeros_like(dsc_ref)
            dsh_ref[...] = jnp.zeros_like(dsh_ref)
            dg_ref[...] = jnp.zeros_like(dg_ref)

        xv = x_ref[...]
        dh_v = dh_ref[...]
        r = lax.rsqrt(jnp.mean(xv * xv, axis=-1, keepdims=True) + EPS)
        xhat = xv * r
        dsh_ref[...] += jnp.sum(dh_v, axis=0, keepdims=True)
        dsc_ref[...] += jnp.sum(dh_v * (xhat * g_ref[...]), axis=0, keepdims=True)
        dn = dh_v * (1.0 + sc_ref[0:1, :])
        dg_ref[...] += jnp.sum(dn * xhat, axis=0, keepdims=True)
        dxh = dn * g_ref[...]
        dx_ref[...] = dr_ref[...] + r * (dxh - xhat * jnp.mean(dxh * xhat, axis=-1, keepdims=True))

    vec = jax.ShapeDtypeStruct((1, D_MODEL), F32)
    return pl.pallas_call(
        body, name=name, grid=(S // tr,),
        in_specs=[_row(tr), _row(tr), _row(tr), _vec(), _modspec(j_sc)],
        out_specs=[_row(tr), _vec(), _vec(), _vec()],
        out_shape=[jax.ShapeDtypeStruct((S, D_MODEL), F32), vec, vec, vec],
        compiler_params=_params(1))(dh, x, dx_res, gamma, mod)


def _shift_down(u, halo, s):
    r = pltpu.roll(u, s, 0)
    hr = pltpu.roll(halo, s, 0)
    rid = lax.broadcasted_iota(jnp.int32, hr.shape, 0)
    top = jnp.where(rid < s, hr, r[0:8])
    return jnp.concatenate([top, r[8:]], axis=0)


def _shift_up(u, halo, s):
    n = u.shape[0]
    r = pltpu.roll(u, n - s, 0)
    hr = pltpu.roll(halo, 8 - s, 0)
    rid = lax.broadcasted_iota(jnp.int32, hr.shape, 0)
    bot = jnp.where(rid >= 8 - s, hr, r[n - 8:])
    return jnp.concatenate([r[:n - 8], bot], axis=0)


def _conv3(u, halo, w_ref, b_ref):
    u1 = _shift_down(u, halo, 1)
    u2 = _shift_down(u, halo, 2)
    return b_ref[...] + ((w_ref[0:1, :] * u2 + w_ref[1:2, :] * u1) + w_ref[2:3, :] * u), u1, u2


CONV_TC = 1408


def _conv_specs(tr, S):
    nh = D_FF // CONV_TC
    hb = tr // 8

    def cur(off):
        return pl.BlockSpec((tr, CONV_TC), lambda j, i: (i, j + off))

    def halo(off):
        return pl.BlockSpec((8, CONV_TC), lambda j, i: (jnp.maximum(i * hb - 1, 0), j + off))

    def w(off):
        return pl.BlockSpec((3, CONV_TC), lambda j, i: (0, j + off))

    def b(off):
        return pl.BlockSpec((1, CONV_TC), lambda j, i: (0, j + off))

    return nh, cur, halo, w, b


def _conv_gate(u, conv_w, conv_b, name):
    S = u.shape[0]
    tr = ROW_TILE
    nh, cur, halo, w, b = _conv_specs(tr, S)

    def body(ua_ref, ha_ref, ub_ref, hb_ref, wa_ref, wb_ref, ba_ref, bb_ref, o_ref):
        first = pl.program_id(1) == 0
        ha = jnp.where(first, 0.0, ha_ref[...])
        hbv = jnp.where(first, 0.0, hb_ref[...])
        a, _, _ = _conv3(ua_ref[...], ha, wa_ref, ba_ref)
        bb, _, _ = _conv3(ub_ref[...], hbv, wb_ref, bb_ref)
        o_ref[...] = (_silu(a) * bb).astype(BF16)

    return pl.pallas_call(
        body, name=name, grid=(nh, S // tr),
        in_specs=[cur(0), halo(0), cur(nh), halo(nh), w(0), w(nh), b(0), b(nh)],
        out_specs=pl.BlockSpec((tr, CONV_TC), lambda j, i: (i, j)),
        out_shape=jax.ShapeDtypeStruct((S, D_FF), BF16), compiler_params=_params(2))(
            u, u, u, u, conv_w, conv_w, conv_b, conv_b)


def _conv_gate_bwd(u, dact, conv_w, conv_b, name):
    S = u.shape[0]
    tr = ROW_TILE
    nh, cur, halo, w, b = _conv_specs(tr, S)

    def body(ua_ref, ha_ref, ub_ref, hb_ref, wa_ref, wb_ref, ba_ref, bb_ref, da_ref,
             dua_ref, dub_ref, dwa_ref, dwb_ref, dba_ref, dbb_ref):
        first = pl.program_id(1) == 0

        @pl.when(first)
        def _():
            for r in (dwa_ref, dwb_ref, dba_ref, dbb_ref):
                r[...] = jnp.zeros_like(r)

        ha = jnp.where(first, 0.0, ha_ref[...])
        hbv = jnp.where(first, 0.0, hb_ref[...])
        ua, ub = ua_ref[...], ub_ref[...]
        a, ua1, ua2 = _conv3(ua, ha, wa_ref, ba_ref)
        bb, ub1, ub2 = _conv3(ub, hbv, wb_ref, bb_ref)
        dact_v = da_ref[...]
        da = dact_v * bb * _dsilu(a)
        db = dact_v * _silu(a)
        dua_ref[...] = da
        dub_ref[...] = db
        for d, x0, x1, x2, dw_ref, dbias_ref in ((da, ua, ua1, ua2, dwa_ref, dba_ref),
                                                 (db, ub, ub1, ub2, dwb_ref, dbb_ref)):
            dbias_ref[...] += jnp.sum(d, axis=0, keepdims=True)
            dw_ref[0:1, :] += jnp.sum(d * x2, axis=0, keepdims=True)
            dw_ref[1:2, :] += jnp.sum(d * x1, axis=0, keepdims=True)
            dw_ref[2:3, :] += jnp.sum(d * x0, axis=0, keepdims=True)

    half = pl.BlockSpec((tr, CONV_TC), lambda j, i: (i, j))
    dw = pl.BlockSpec((8, CONV_TC), lambda j, i: (0, j))
    dbs = pl.BlockSpec((1, CONV_TC), lambda j, i: (0, j))
    f = lambda r, c: jax.ShapeDtypeStruct((r, c), F32)
    return pl.pallas_call(
        body, name=name, grid=(nh, S // tr),
        in_specs=[cur(0), halo(0), cur(nh), halo(nh), w(0), w(nh), b(0), b(nh), half],
        out_specs=[half, half, dw, dw, dbs, dbs],
        out_shape=[f(S, D_FF), f(S, D_FF), f(8, D_FF), f(8, D_FF), f(1, D_FF), f(1, D_FF)],
        compiler_params=_params(2))(u, u, u, u, conv_w, conv_w, conv_b, conv_b, dact)


def _conv_t(duc, conv_w, half, prev, name):
    S = duc.shape[0]
    tr = ROW_TILE
    nh = D_FF // CONV_TC
    hb = tr // 8
    nlast = S // 8 - 1
    nsteps = S // tr
    off = half * nh

    def body(*refs):
        if prev is None:
            d_ref, h_ref, w_ref, o_ref = refs
        else:
            d_ref, h_ref, w_ref, _, o_ref = refs
        last = pl.program_id(1) == nsteps - 1
        hv = jnp.where(last, 0.0, h_ref[...])
        d = d_ref[...]
        d1 = _shift_up(d, hv, 1)
        d2 = _shift_up(d, hv, 2)
        o_ref[...] = ((w_ref[2:3, :] * d + w_ref[1:2, :] * d1) + w_ref[0:1, :] * d2).astype(BF16)

    in_specs = [pl.BlockSpec((tr, CONV_TC), lambda j, i: (i, j)),
                pl.BlockSpec((8, CONV_TC), lambda j, i: (jnp.minimum((i + 1) * hb, nlast), j)),
                pl.BlockSpec((3, CONV_TC), lambda j, i: (0, j + off))]
    ins = [duc, duc, conv_w]
    aliases = {}
    if prev is not None:
        in_specs.append(HBM)
        ins.append(prev)
        aliases = {3: 0}
    return pl.pallas_call(
        body, name=name, grid=(nh, nsteps), in_specs=in_specs,
        out_specs=pl.BlockSpec((tr, CONV_TC), lambda j, i: (i, j + off)),
        out_shape=jax.ShapeDtypeStruct((S, 2 * D_FF), BF16), input_output_aliases=aliases,
        compiler_params=_params(2))(*ins)


def _bucket_maps():
    qi = np.arange(A_BLK)[:, None]
    ki = np.arange(2 * A_BLK)[None, :]
    steps = np.clip(qi + A_BLK - ki, 0, A_BLK)
    out = []
    max_exact = N_BUCKETS // 2
    for _, dil in A_CONFIGS:
        dist = steps * dil
        n = np.maximum(dist, max_exact).astype(np.float32)
        large = max_exact + (np.log(n / np.float32(max_exact)) / np.float32(math.log(MAX_DISTANCE / max_exact))
                             * np.float32(N_BUCKETS - max_exact)).astype(np.int32)
        large = np.minimum(large, N_BUCKETS - 1)
        out.append(np.where(dist < max_exact, dist, large))
    return np.stack(out).astype(np.int32)


def _bias_build(rel_bias, buckets, name):
    ng = len(A_CONFIGS)

    def body(t_ref, bk_ref, o_ref):
        gh = pl.program_id(0) * A_HEADS + pl.program_id(1)
        bk = bk_ref[0]
        acc = jnp.zeros((A_BLK, 2 * A_BLK), F32)
        for b in range(N_BUCKETS):
            acc = jnp.where(bk == b, t_ref[b, gh], acc)
        o_ref[0] = acc

    return pl.pallas_call(
        body, name=name, grid=(ng, A_HEADS),
        in_specs=[pl.BlockSpec(memory_space=pltpu.SMEM), pl.BlockSpec((1, A_BLK, 2 * A_BLK), lambda g, h: (g, 0, 0))],
        out_specs=pl.BlockSpec((1, A_BLK, 2 * A_BLK), lambda g, h: (g * A_HEADS + h, 0, 0)),
        out_shape=jax.ShapeDtypeStruct((ng * A_HEADS, A_BLK, 2 * A_BLK), F32),
        compiler_params=_params(2))(rel_bias, buckets)


def _bias_bwd(dbias, buckets, name):
    ng = len(A_CONFIGS)

    def body(d_ref, bk_ref, o_ref):
        gh = pl.program_id(0) * A_HEADS + pl.program_id(1)
        bk = bk_ref[0]
        d = d_ref[0]
        for b in range(N_BUCKETS):
            o_ref[b, gh] = jnp.sum(jnp.where(bk == b, d, 0.0))

    return pl.pallas_call(
        body, name=name, grid=(ng, A_HEADS),
        in_specs=[pl.BlockSpec((1, A_BLK, 2 * A_BLK), lambda g, h: (g * A_HEADS + h, 0, 0)),
                  pl.BlockSpec((1, A_BLK, 2 * A_BLK), lambda g, h: (g, 0, 0))],
        out_specs=pl.BlockSpec(memory_space=pltpu.SMEM),
        out_shape=jax.ShapeDtypeStruct((N_BUCKETS, ng * A_HEADS), F32),
        compiler_params=_params(2))(dbias, buckets)


def _attn_mask(b):
    qi = lax.broadcasted_iota(jnp.int32, (A_BLK, 2 * A_BLK), 0)
    ki = lax.broadcasted_iota(jnp.int32, (A_BLK, 2 * A_BLK), 1)
    band = (ki >= qi) & (ki <= qi + A_BLK)
    return band & ((b > 0) | (ki >= A_BLK))


def _attn_in_specs(g, dil):
    W = A_HEADS * A_HEAD_DIM

    def spec(t, prev, nb):
        def im(r, b):
            bb = jnp.minimum(b, nb - 1)
            if prev:
                bb = jnp.maximum(bb - 1, 0)
            return (bb, r * 9 + g * 3 + t)
        return pl.BlockSpec((A_BLK, W), im)

    return lambda nb: [spec(0, False, nb), spec(1, False, nb), spec(1, True, nb), spec(2, False, nb),
                       spec(2, True, nb)]


def _attn_fwd(qkv, bias, g, name):
    S = qkv.shape[0]
    _, dil = A_CONFIGS[g]
    L = S // dil
    nb = L // A_BLK
    W = A_HEADS * A_HEAD_DIM
    qv = qkv.reshape(L, dil * 9 * W)

    def body(q_ref, kc_ref, kp_ref, vc_ref, vp_ref, bias_ref, o_ref, l_ref):
        mask = _attn_mask(pl.program_id(1))
        for h in range(A_HEADS):
            hs = slice(h * A_HEAD_DIM, (h + 1) * A_HEAD_DIM)
            qh = q_ref[:, hs] * 0.125
            k2 = jnp.concatenate([kp_ref[:, hs], kc_ref[:, hs]], axis=0)
            v2 = jnp.concatenate([vp_ref[:, hs], vc_ref[:, hs]], axis=0)
            s = lax.dot_general(qh, k2, (((1,), (1,)), ((), ())), preferred_element_type=F32) + bias_ref[h]
            s = jnp.where(mask, s, NEG_INF)
            m = jnp.max(s, axis=-1, keepdims=True)
            p = jnp.exp(s - m)
            den = jnp.sum(p, axis=-1, keepdims=True)
            o = jnp.dot(p.astype(BF16), v2, preferred_element_type=F32) / den
            o_ref[:, hs] = o
            l_ref[:, hs] = jnp.broadcast_to(m + jnp.log(den), (A_BLK, A_HEAD_DIM))

    out_spec = pl.BlockSpec((A_BLK, W), lambda r, b: (b, r))
    o, lse = pl.pallas_call(
        body, name=name, grid=(dil, nb),
        in_specs=_attn_in_specs(g, dil)(nb) + [pl.BlockSpec((A_HEADS, A_BLK, 2 * A_BLK), lambda r, b: (g, 0, 0))],
        out_specs=[out_spec, out_spec],
        out_shape=[jax.ShapeDtypeStruct((L, dil * W), F32)] * 2,
        compiler_params=_params(2))(qv, qv, qv, qv, qv, bias)
    return o.reshape(S, W), lse.reshape(S, W)


def _mix_fwd(os, ls, name):
    S = os[0].shape[0]
    tr = ROW_TILE

    def body(o0, o1, o2, l0, l1, l2, om_ref, lt_ref):
        a, b, c = l0[...], l1[...], l2[...]
        m = jnp.maximum(jnp.maximum(a, b), c)
        ea, eb, ec = jnp.exp(a - m), jnp.exp(b - m), jnp.exp(c - m)
        z = (ea + eb) + ec
        om_ref[...] = ((ea / z) * o0[...] + (eb / z) * o1[...]) + (ec / z) * o2[...]
        lt_ref[...] = m + jnp.log(z)

    return pl.pallas_call(
        body, name=name, grid=(S // tr,), in_specs=[_row(tr)] * 6, out_specs=[_row(tr)] * 2,
        out_shape=[jax.ShapeDtypeStruct((S, D_MODEL), F32)] * 2, compiler_params=_params(1))(*os, *ls)


def _attn_bwd(qkv, bias, d_o, omix, ltot, g, name):
    S = qkv.shape[0]
    _, dil = A_CONFIGS[g]
    L = S // dil
    nb = L // A_BLK
    W = A_HEADS * A_HEAD_DIM
    qv = qkv.reshape(L, dil * 9 * W)
    view = lambda t: t.reshape(L, dil * W)

    def body(q_ref, kc_ref, kp_ref, vc_ref, vp_ref, bias_ref, do_ref, om_ref, lt_ref,
             dq_ref, dk_ref, dv_ref, db_ref, ck_ref, cv_ref):
        r, b = pl.program_id(0), pl.program_id(1)

        @pl.when((r == 0) & (b == 0))
        def _():
            db_ref[...] = jnp.zeros_like(db_ref)

        @pl.when(b == 0)
        def _():
            ck_ref[...] = jnp.zeros_like(ck_ref)
            cv_ref[...] = jnp.zeros_like(cv_ref)

        @pl.when(b < nb)
        def _():
            mask = _attn_mask(b)
            for h in range(A_HEADS):
                hs = slice(h * A_HEAD_DIM, (h + 1) * A_HEAD_DIM)
                qh = q_ref[:, hs] * 0.125
                k2 = jnp.concatenate([kp_ref[:, hs], kc_ref[:, hs]], axis=0)
                v2 = jnp.concatenate([vp_ref[:, hs], vc_ref[:, hs]], axis=0)
                s = lax.dot_general(qh, k2, (((1,), (1,)), ((), ())), preferred_element_type=F32) + bias_ref[h]
                s = jnp.where(mask, s, NEG_INF)
                wp = jnp.exp(s - lt_ref[:, h * A_HEAD_DIM:h * A_HEAD_DIM + 1])
                do_h = do_ref[:, hs]
                t_h = jnp.sum(do_h * om_ref[:, hs], axis=-1, keepdims=True)
                do_b = do_h.astype(BF16)
                dp = lax.dot_general(do_b, v2, (((1,), (1,)), ((), ())), preferred_element_type=F32)
                ds = wp * (dp - t_h)
                db_ref[h] += ds
                ds_b = ds.astype(BF16)
                dv2 = lax.dot_general(wp.astype(BF16), do_b, (((0,), (0,)), ((), ())), preferred_element_type=F32)
                dk2 = lax.dot_general(ds_b, qh, (((0,), (0,)), ((), ())), preferred_element_type=F32)
                dq_ref[:, hs] = (jnp.dot(ds_b, k2, preferred_element_type=F32) * 0.125).astype(BF16)
                dk_ref[:, hs] = (ck_ref[:, hs] + dk2[:A_BLK]).astype(BF16)
                dv_ref[:, hs] = (cv_ref[:, hs] + dv2[:A_BLK]).astype(BF16)
                ck_ref[:, hs] = dk2[A_BLK:]
                cv_ref[:, hs] = dv2[A_BLK:]

        @pl.when(b == nb)
        def _():
            dk_ref[...] = ck_ref[...].astype(BF16)
            dv_ref[...] = cv_ref[...].astype(BF16)

    act = pl.BlockSpec((A_BLK, W), lambda r, b: (jnp.minimum(b, nb - 1), r))
    lag = pl.BlockSpec((A_BLK, W), lambda r, b: (jnp.maximum(b - 1, 0), r))
    full = pl.BlockSpec((A_HEADS, A_BLK, 2 * A_BLK), lambda r, b: (0, 0, 0))
    in_specs = _attn_in_specs(g, dil)(nb) + [pl.BlockSpec((A_HEADS, A_BLK, 2 * A_BLK), lambda r, b: (g, 0, 0)),
                                             act, act, act]
    dq, dk, dv, dbias = pl.pallas_call(
        body, name=name, grid=(dil, nb + 1), in_specs=in_specs,
        out_specs=[act, lag, lag, full],
        out_shape=[jax.ShapeDtypeStruct((L, dil * W), BF16)] * 3
        + [jax.ShapeDtypeStruct((A_HEADS, A_BLK, 2 * A_BLK), F32)],
        scratch_shapes=[pltpu.VMEM((A_BLK, W), F32), pltpu.VMEM((A_BLK, W), F32)],
        compiler_params=_params(2))(qv, qv, qv, qv, qv, bias, view(d_o), view(omix), view(ltot))
    return dq.reshape(S, W), dk.reshape(S, W), dv.reshape(S, W), dbias


NT = (((1,), (1,)), ((), ()))
TN = (((0,), (0,)), ((), ()))


def _dot(a, b, dims=(((1,), (0,)), ((), ()))):
    return lax.dot_general(a.astype(BF16), b.astype(BF16), dims, preferred_element_type=F32)


def _gla_gates(glr, wg_ref, bg_ref):
    z = _dot(glr, wg_ref[...]) + bg_ref[...]
    log_sig = -(jnp.maximum(-z, 0.0) + jnp.log1p(jnp.exp(-jnp.abs(z))))
    return z, log_sig / B_TAU


def _gla_chunk(q, k, gk):
    row = lax.broadcasted_iota(jnp.int32, (B_CHUNK, B_CHUNK), 0)
    col = lax.broadcasted_iota(jnp.int32, (B_CHUNK, B_CHUNK), 1)
    causal = row >= col
    bcum = jnp.dot(causal.astype(F32), gk, precision=lax.Precision.HIGHEST, preferred_element_type=F32)
    bl = bcum[B_CHUNK - 1:B_CHUNK, :]
    qt = (q * (B_DK ** -0.5)) * jnp.exp(bcum)
    kt = k * jnp.exp(-bcum)
    kd = k * jnp.exp(bl - bcum)
    a = jnp.where(causal, _dot(qt, kt, NT), 0.0)
    return causal, bcum, bl, qt, kt, kd, a


def _gla_specs(tg):
    q = pl.BlockSpec((tg, B_DK), lambda h, i: (i, h))
    k = pl.BlockSpec((tg, B_DK), lambda h, i: (i, B_HEADS + h))
    v = pl.BlockSpec((tg, B_DV), lambda h, i: (i, B_HEADS + h))
    glr = pl.BlockSpec((tg, 128), lambda h, i: (i, 24))
    wg = pl.BlockSpec((128, B_DK), lambda h, i: (0, h))
    bg = pl.BlockSpec((1, B_DK), lambda h, i: (0, h))
    return [q, k, v, glr, wg, bg]


def _gla_fwd(proj, w_gate, b_gate, name):
    S = proj.shape[0]
    tg = GLA_ROWS
    nc = tg // B_CHUNK

    def body(q_ref, k_ref, v_ref, glr_ref, wg_ref, bg_ref, o_ref, st_ref, state_ref):
        @pl.when(pl.program_id(1) == 0)
        def _():
            state_ref[...] = jnp.zeros_like(state_ref)

        _, gk_all = _gla_gates(glr_ref[...], wg_ref, bg_ref)
        st = state_ref[...]
        for c in range(nc):
            rows = slice(c * B_CHUNK, (c + 1) * B_CHUNK)
            v = v_ref[rows, :]
            _, _, bl, qt, _, kd, a = _gla_chunk(q_ref[rows, :], k_ref[rows, :], gk_all[rows, :])
            o_ref[rows, :] = _dot(a, v) + _dot(qt, st, NT)
            st_ref[c, 0] = st
            st = st * jnp.exp(bl) + _dot(v, kd, TN)
        state_ref[...] = st

    return pl.pallas_call(
        body, name=name, grid=(B_HEADS, S // tg), in_specs=_gla_specs(tg),
        out_specs=[pl.BlockSpec((tg, B_DV), lambda h, i: (i, h)),
                   pl.BlockSpec((nc, 1, B_DV, B_DK), lambda h, i: (i, h, 0, 0))],
        out_shape=[jax.ShapeDtypeStruct((S, B_V), F32),
                   jax.ShapeDtypeStruct((S // B_CHUNK, B_HEADS, B_DV, B_DK), F32)],
        scratch_shapes=[pltpu.VMEM((B_DV, B_DK), F32)], compiler_params=_params(2))(
            proj, proj, proj, proj, w_gate, b_gate)


def _gla_bwd(proj, w_gate, b_gate, states, d_o, name):
    S = proj.shape[0]
    tg = GLA_ROWS
    nc = tg // B_CHUNK
    ni = S // tg

    def rev(spec):
        return pl.BlockSpec(spec.block_shape, lambda h, i, im=spec.index_map: im(h, ni - 1 - i))

    def body(q_ref, k_ref, v_ref, glr_ref, wg_ref, bg_ref, st_ref, do_ref,
             dq_ref, dk_ref, dv_ref, dz_ref, dbg_ref, dstate_ref):
        @pl.when(pl.program_id(1) == 0)
        def _():
            dstate_ref[...] = jnp.zeros_like(dstate_ref)
            dbg_ref[...] = jnp.zeros_like(dbg_ref)

        z_all, gk_all = _gla_gates(glr_ref[...], wg_ref, bg_ref)
        dst = dstate_ref[...]
        for c in range(nc - 1, -1, -1):
            rows = slice(c * B_CHUNK, (c + 1) * B_CHUNK)
            v = v_ref[rows, :]
            d_out = do_ref[rows, :]
            st = st_ref[c, 0]
            causal, bcum, bl, qt, kt, kd, a = _gla_chunk(q_ref[rows, :], k_ref[rows, :], gk_all[rows, :])
            da = jnp.where(causal, _dot(d_out, v, NT), 0.0)
            dv_ref[rows, :] = (_dot(a, d_out, TN) + _dot(kd, dst, NT)).astype(BF16)
            dqt = _dot(da, kt) + _dot(d_out, st)
            dkt = _dot(da, qt, TN)
            dkd = _dot(v, dst)
            dec = jnp.exp(bl)
            ddec = jnp.sum(dst * st, axis=0, keepdims=True)
            dst = dst * dec + _dot(d_out, qt, TN)
            dq_ref[rows, :] = (dqt * jnp.exp(bcum) * (B_DK ** -0.5)).astype(BF16)
            dk_ref[rows, :] = (dkt * jnp.exp(-bcum) + dkd * jnp.exp(bl - bcum)).astype(BF16)
            db = (dqt * qt - dkt * kt) - dkd * kd
            dbl = jnp.sum(dkd * kd, axis=0, keepdims=True) + dec * ddec
            upper = jnp.logical_not(causal) | (lax.broadcasted_iota(jnp.int32, (B_CHUNK, B_CHUNK), 0)
                                               == lax.broadcasted_iota(jnp.int32, (B_CHUNK, B_CHUNK), 1))
            dgk = jnp.dot(upper.astype(F32), db, precision=lax.Precision.HIGHEST, preferred_element_type=F32) + dbl
            dz = dgk * (1.0 / B_TAU) * jax.nn.sigmoid(-z_all[rows, :])
            dz_ref[rows, :] = dz
            dbg_ref[...] += jnp.sum(dz, axis=0, keepdims=True)
        dstate_ref[...] = dst

    qs = pl.BlockSpec((tg, B_DK), lambda h, i: (i, h))
    vs = pl.BlockSpec((tg, B_DV), lambda h, i: (i, h))
    in_specs = [rev(s) if n < 4 else s for n, s in enumerate(_gla_specs(tg))]
    in_specs += [rev(pl.BlockSpec((nc, 1, B_DV, B_DK), lambda h, i: (i, h, 0, 0))), rev(vs)]
    return pl.pallas_call(
        body, name=name, grid=(B_HEADS, ni), in_specs=in_specs,
        out_specs=[rev(qs), rev(qs), rev(vs), rev(qs), pl.BlockSpec((1, B_DK), lambda h, i: (0, h))],
        out_shape=[jax.ShapeDtypeStruct((S, B_QK), BF16), jax.ShapeDtypeStruct((S, B_QK), BF16),
                   jax.ShapeDtypeStruct((S, B_V), BF16), jax.ShapeDtypeStruct((S, B_QK), F32),
                   jax.ShapeDtypeStruct((1, B_QK), F32)],
        scratch_shapes=[pltpu.VMEM((B_DV, B_DK), F32)], compiler_params=_params(2))(
            proj, proj, proj, proj, w_gate, b_gate, states, d_o)


def _gla_out(o, proj, gnorm, name):
    S = o.shape[0]
    tr = ROW_TILE

    def body(o_ref, r_ref, g_ref, y_ref):
        for h in range(B_HEADS):
            hs = slice(h * B_DV, (h + 1) * B_DV)
            oh = o_ref[:, hs]
            rs = lax.rsqrt(jnp.mean(oh * oh, axis=-1, keepdims=True) + EPS)
            y_ref[:, hs] = (((oh * rs) * g_ref[...]) * _silu(r_ref[:, hs])).astype(BF16)

    return pl.pallas_call(
        body, name=name, grid=(S // tr,),
        in_specs=[_row(tr), pl.BlockSpec((tr, B_V), lambda i: (i, 2)), _vec(B_DV)], out_specs=_row(tr),
        out_shape=jax.ShapeDtypeStruct((S, B_V), BF16), compiler_params=_params(1))(o, proj, gnorm)


def _gla_out_bwd(o, proj, gnorm, d_y, name):
    S = o.shape[0]
    tr = ROW_TILE

    def body(o_ref, r_ref, g_ref, dy_ref, do_ref, dr_ref, dg_ref):
        @pl.when(pl.program_id(0) == 0)
        def _():
            dg_ref[...] = jnp.zeros_like(dg_ref)

        for h in range(B_HEADS):
            hs = slice(h * B_DV, (h + 1) * B_DV)
            oh, rv, dyv = o_ref[:, hs], r_ref[:, hs], dy_ref[:, hs]
            rs = lax.rsqrt(jnp.mean(oh * oh, axis=-1, keepdims=True) + EPS)
            xhat = oh * rs
            dr_ref[:, hs] = (dyv * (xhat * g_ref[...]) * _dsilu(rv)).astype(BF16)
            dn = dyv * _silu(rv)
            dg_ref[...] += jnp.sum(dn * xhat, axis=0, keepdims=True)
            dxh = dn * g_ref[...]
            do_ref[:, hs] = rs * (dxh - xhat * jnp.mean(dxh * xhat, axis=-1, keepdims=True))

    return pl.pallas_call(
        body, name=name, grid=(S // tr,),
        in_specs=[_row(tr), pl.BlockSpec((tr, B_V), lambda i: (i, 2)), _vec(B_DV), _row(tr)],
        out_specs=[_row(tr), _row(tr), _vec(B_DV)],
        out_shape=[jax.ShapeDtypeStruct((S, B_V), F32), jax.ShapeDtypeStruct((S, B_V), BF16),
                   jax.ShapeDtypeStruct((1, B_DV), F32)],
        compiler_params=_params(1))(o, proj, gnorm, d_y)


J_SH1, J_SC1, J_G1, J_SH2, J_SC2, J_G2 = range(6)


def _ffn_fwd(h, w, i, tag):
    u = _matmul(h, w["w_up"][i], "nn", F32, f"up{tag}")
    act = _conv_gate(u, w["conv_w"][i], w["conv_b"][i:i + 1], f"conv_gate{tag}")
    f = _matmul(act, w["w_down"][i], "nn", F32, f"down{tag}")
    return u, act, f


def _ffn_bwd(dx_out, f, u, act, h, x_in, mod, w, i, tag):
    df, dg2 = _gate_bwd(dx_out, f, mod, J_G2, f"gate2_bwd{tag}")
    dact = _matmul(df, w["w_down"][i], "nt", F32, f"down_dx{tag}")
    d_w_down = _matmul(act, df, "tn", F32, f"down_dw{tag}")
    duca, ducb, dcwa, dcwb, dcba, dcbb = _conv_gate_bwd(u, dact, w["conv_w"][i], w["conv_b"][i:i + 1],
                                                        f"conv_gate_bwd{tag}")
    du = _conv_t(duca, w["conv_w"][i], 0, None, f"conv_t_a{tag}")
    du = _conv_t(ducb, w["conv_w"][i], 1, du, f"conv_t_b{tag}")
    dh = _matmul(du, w["w_up"][i], "nt", F32, f"up_dx{tag}")
    d_w_up = _matmul(h, du, "tn", F32, f"up_dw{tag}")
    dx_in, dsc2, dsh2, dgam = _norm_mod_bwd(dh, x_in, dx_out, w["norm_ffn"][i:i + 1], mod, J_SC2,
                                            f"norm_ffn_bwd{tag}")
    grads = dict(w_down=d_w_down, w_up=d_w_up, norm_ffn=dgam,
                 conv_w=jnp.concatenate([dcwa[0:3], dcwb[0:3]], axis=1),
                 conv_b=jnp.concatenate([dcba, dcbb], axis=1))
    return dx_in, (dsh2, dsc2, dg2), grads


def _local_step(x, c, tgt, w):
    buckets = jnp.asarray(_bucket_maps())
    mods = [_ada_mod(c, w["w_ada"][i], w["b_ada"][i:i + 1], f"ada_mod{i}") for i in range(2)]

    h1 = _norm_mod(x, w["norm_mix"][0:1], mods[0], J_SC1, J_SH1, "norm_mix0")
    qkv = _matmul(h1, w["w_in_a"], "nn", BF16, "in_a")
    bias = _bias_build(w["rel_bias"], buckets, "bias_build")
    os_, ls_ = zip(*[_attn_fwd(qkv, bias, g, f"attn_fwd{g}") for g in range(3)])
    omix, ltot = _mix_fwd(os_, ls_, "mix_fwd")
    y0 = _matmul(omix, w["w_out_a"], "nn", F32, "out_a")
    x1, h2 = _norm_mod(x, w["norm_ffn"][0:1], mods[0], J_SC2, J_SH2, "norm_ffn0", resid=y0, gate_mod=mods[0],
                       j_gate=J_G1)
    u0, act0, f0 = _ffn_fwd(h2, w, 0, "0")

    x2, h3 = _norm_mod(x1, w["norm_mix"][1:2], mods[1], J_SC1, J_SH1, "norm_mix1", resid=f0, gate_mod=mods[0],
                       j_gate=J_G2)
    proj = _matmul(h3, w["w_in_b"], "nn", F32, "in_b")
    o_gla, states = _gla_fwd(proj, w["w_gate_b"], w["b_gate_b"], "gla_fwd")
    on = _gla_out(o_gla, proj, w["gnorm_b"], "gla_out")
    y1 = _matmul(on, w["w_out_b"], "nn", F32, "out_b")
    x3, h4 = _norm_mod(x2, w["norm_ffn"][1:2], mods[1], J_SC2, J_SH2, "norm_ffn1", resid=y1, gate_mod=mods[1],
                       j_gate=J_G1)
    u1, act1, f1 = _ffn_fwd(h4, w, 1, "1")

    dx4, loss, d_norm_final = _final_loss(x3, f1, mods[1], J_G2, w["norm_final"], tgt, "final_loss")

    dx3, (dsh2_1, dsc2_1, dg2_1), g_ffn1 = _ffn_bwd(dx4, f1, u1, act1, h4, x3, mods[1], w, 1, "1")
    dy1, dg1_1 = _gate_bwd(dx3, y1, mods[1], J_G1, "gate1_bwd1")
    d_on = _matmul(dy1, w["w_out_b"], "nt", F32, "out_b_dx")
    d_w_out_b = _matmul(on, dy1, "tn", F32, "out_b_dw")
    d_ogla, d_r, d_gnorm = _gla_out_bwd(o_gla, proj, w["gnorm_b"], d_on, "gla_out_bwd")
    dq, dk, dv, dz, d_b_gate = _gla_bwd(proj, w["w_gate_b"], w["b_gate_b"], states, d_ogla, "gla_bwd")
    d_glr = _matmul(dz, w["w_gate_b"], "nt", BF16, "gate_dx")
    d_w_gate = _matmul(proj[:, 3072:3200], dz, "tn", F32, "gate_dw")
    dproj = jnp.concatenate([dq, dk, dv, d_r, d_glr], axis=1)
    dh3 = _matmul(dproj, w["w_in_b"], "nt", F32, "in_b_dx")
    d_w_in_b = _matmul(h3, dproj, "tn", F32, "in_b_dw")
    dx2, dsc1_1, dsh1_1, d_nmix1 = _norm_mod_bwd(dh3, x2, dx3, w["norm_mix"][1:2], mods[1], J_SC1, "norm_mix_bwd1")
    dmod1 = jnp.concatenate([dsh1_1, dsc1_1, dg1_1, dsh2_1, dsc2_1, dg2_1], axis=1)

    dx1, (dsh2_0, dsc2_0, dg2_0), g_ffn0 = _ffn_bwd(dx2, f0, u0, act0, h2, x1, mods[0], w, 0, "0")
    dy0, dg1_0 = _gate_bwd(dx1, y0, mods[0], J_G1, "gate1_bwd0")
    d_omix = _matmul(dy0, w["w_out_a"], "nt", F32, "out_a_dx")
    d_w_out_a = _matmul(omix, dy0, "tn", F32, "out_a_dw")
    parts, dbs = [], []
    for g in range(3):
        dq_g, dk_g, dv_g, db_g = _attn_bwd(qkv, bias, d_omix, omix, ltot, g, f"attn_bwd{g}")
        parts += [dq_g, dk_g, dv_g]
        dbs.append(db_g)
    dqkv = jnp.concatenate(parts, axis=1)
    d_rel_bias = _bias_bwd(jnp.concatenate(dbs, axis=0), buckets, "bias_bwd")
    dh1 = _matmul(dqkv, w["w_in_a"], "nt", F32, "in_a_dx")
    d_w_in_a = _matmul(h1, dqkv, "tn", F32, "in_a_dw")
    dx0, dsc1_0, dsh1_0, d_nmix0 = _norm_mod_bwd(dh1, x, dx1, w["norm_mix"][0:1], mods[0], J_SC1, "norm_mix_bwd0")
    dmod0 = jnp.concatenate([dsh1_0, dsc1_0, dg1_0, dsh2_0, dsc2_0, dg2_0], axis=1)

    grads = dict(
        w_in_a=d_w_in_a, w_out_a=d_w_out_a, rel_bias=d_rel_bias, w_in_b=d_w_in_b, w_gate_b=d_w_gate,
        b_gate_b=d_b_gate, gnorm_b=d_gnorm, w_out_b=d_w_out_b,
        norm_mix=jnp.concatenate([d_nmix0, d_nmix1], axis=0),
        norm_ffn=jnp.concatenate([g_ffn0["norm_ffn"], g_ffn1["norm_ffn"]], axis=0),
        w_ada=jnp.stack([_ada_outer(c, dmod0, "ada_outer0"), _ada_outer(c, dmod1, "ada_outer1")]),
        b_ada=jnp.concatenate([dmod0, dmod1], axis=0),
        w_up=jnp.stack([g_ffn0["w_up"], g_ffn1["w_up"]]),
        conv_w=jnp.stack([g_ffn0["conv_w"], g_ffn1["conv_w"]]),
        conv_b=jnp.concatenate([g_ffn0["conv_b"], g_ffn1["conv_b"]], axis=0),
        w_down=jnp.stack([g_ffn0["w_down"], g_ffn1["w_down"]]),
        norm_final=d_norm_final)
    return loss, dx0, grads


N_CHIPS = 4
N_DEV = 8
WEIGHTS = ("w_in_a", "w_out_a", "rel_bias", "w_in_b", "w_gate_b", "b_gate_b", "gnorm_b", "w_out_b", "norm_mix",
           "norm_ffn", "w_ada", "b_ada", "w_up", "conv_w", "conv_b", "w_down", "norm_final")
SHARD_AXIS = dict(w_in_a=2, w_out_a=1, w_in_b=2, w_gate_b=2, b_gate_b=1, gnorm_b=1, w_out_b=1, w_ada=2, w_up=2,
                  conv_w=2, w_down=1)
SHARDED = tuple(n for n in WEIGHTS if n in SHARD_AXIS)
REPLICATED = tuple(n for n in WEIGHTS if n not in SHARD_AXIS)
BIG = ("w_in_a", "w_out_a", "w_in_b", "w_out_b", "w_ada", "w_up", "w_down")
SMALL = ("w_gate_b", "b_gate_b", "gnorm_b", "conv_w")
R_BIG = 10912
R_SMALL = 16
R_SHARD = 11232
R_REPL = 32
LOSS_SLOT = 30208


def _pack(arrs, rows):
    flat = jnp.concatenate([a.reshape(-1) for a in arrs])
    return jnp.pad(flat, (0, rows * LANES - flat.shape[0])).reshape(rows, LANES)


def _unpack(flat2d, shapes):
    flat = flat2d.reshape(-1)
    out, off = [], 0
    for shp in shapes:
        n = math.prod(shp)
        out.append(flat[off:off + n].reshape(shp))
        off += n
    return out


def _chip_slice(a, axis, k):
    n = a.shape[axis] // N_CHIPS
    return lax.slice_in_dim(a, k * n, (k + 1) * n, axis=axis)


def _place():
    mx, my, mc = lax.axis_index("x"), lax.axis_index("y"), lax.axis_index("c")
    chips = [(1 - mx, my), (mx, 1 - my), (1 - mx, 1 - my)]
    return mx, my, mc, chips


def _rcopy(src, dst, send_sem, recv_sem, dev):
    return pltpu.make_async_remote_copy(src_ref=src, dst_ref=dst, send_sem=send_sem, recv_sem=recv_sem,
                                        device_id=dev, device_id_type=MESH)


def _comm_call(body, name, ins, out_shape, n_sems):
    multi = isinstance(out_shape, (list, tuple))
    return pl.pallas_call(
        body, name=name, in_specs=[HBM] * len(ins), out_specs=[HBM] * len(out_shape) if multi else HBM,
        out_shape=out_shape,
        scratch_shapes=[pltpu.SemaphoreType.DMA((n_sems,)), pltpu.SemaphoreType.DMA((n_sems,)),
                        pltpu.SemaphoreType.DMA(())])(*ins)


def _allgather_chips(x, name):
    R = x.shape[0]
    half = R // 2

    def body(x_ref, out_ref, send_sems, recv_sems, local_sem):
        mx, my, mc, chips = _place()
        sibling = (mx, my, 1 - mc)

        def piece(cx, cy, hc):
            return out_ref.at[2 * cx + cy, pl.ds(hc * half, half), :]

        mine = pltpu.make_async_copy(x_ref, out_ref.at[2 * mx + my], local_sem)
        mine.start()
        first = [_rcopy(x_ref.at[pl.ds(mc * half, half), :], piece(mx, my, mc), send_sems.at[j], recv_sems.at[j],
                        (cx, cy, mc)) for j, (cx, cy) in enumerate(chips)]
        for cp in first:
            cp.start()
        passed = [_rcopy(piece(cx, cy, mc), piece(cx, cy, mc), send_sems.at[3 + j], recv_sems.at[3 + j], sibling)
                  for j, (cx, cy) in enumerate(chips)]
        for j, (cx, cy) in enumerate(chips):
            _rcopy(piece(cx, cy, mc), piece(cx, cy, mc), send_sems.at[j], recv_sems.at[j], sibling).wait_recv()
            passed[j].start()
        for j, (cx, cy) in enumerate(chips):
            _rcopy(piece(cx, cy, 1 - mc), piece(cx, cy, 1 - mc), send_sems.at[3 + j], recv_sems.at[3 + j],
                   sibling).wait_recv()
        for cp in first + passed:
            cp.wait_send()
        mine.wait()

    return _comm_call(body, name, [x], jax.ShapeDtypeStruct((N_CHIPS, R, LANES), x.dtype), 6)


def _pair_exchange(g, name):
    R = g.shape[1]
    half = R // 2

    def body(g_ref, keep_ref, recv_ref, send_sems, recv_sems, local_sem):
        mx, my, mc, _ = _place()
        mine = pltpu.make_async_copy(g_ref.at[:, pl.ds(mc * half, half), :], keep_ref, local_sem)
        mine.start()
        cp = _rcopy(g_ref.at[:, pl.ds((1 - mc) * half, half), :], recv_ref, send_sems.at[0], recv_sems.at[0],
                    (mx, my, 1 - mc))
        cp.start()
        cp.wait()
        mine.wait()

    shape = jax.ShapeDtypeStruct((N_CHIPS, half, LANES), g.dtype)
    return _comm_call(body, name, [g], [shape, shape], 1)


def _chip_exchange(q, name):
    def body(q_ref, out_ref, send_sems, recv_sems, local_sem):
        mx, my, mc, chips = _place()
        chip = 2 * mx + my
        mine = pltpu.make_async_copy(q_ref.at[chip], out_ref.at[chip], local_sem)
        mine.start()
        sends = [_rcopy(q_ref.at[2 * cx + cy], out_ref.at[chip], send_sems.at[j], recv_sems.at[j], (cx, cy, mc))
                 for j, (cx, cy) in enumerate(chips)]
        for cp in sends:
            cp.start()
        for j, (cx, cy) in enumerate(chips):
            _rcopy(q_ref.at[2 * cx + cy], out_ref.at[2 * cx + cy], send_sems.at[j], recv_sems.at[j],
                   (cx, cy, mc)).wait_recv()
        for cp in sends:
            cp.wait_send()
        mine.wait()

    return _comm_call(body, name, [q], jax.ShapeDtypeStruct(q.shape, q.dtype), 3)


def _pair_gather(r, name):
    half = r.shape[0]

    def body(r_ref, out_ref, send_sems, recv_sems, local_sem):
        mx, my, mc, _ = _place()
        mine = pltpu.make_async_copy(r_ref, out_ref.at[pl.ds(mc * half, half), :], local_sem)
        mine.start()
        cp = _rcopy(r_ref, out_ref.at[pl.ds(mc * half, half), :], send_sems.at[0], recv_sems.at[0], (mx, my, 1 - mc))
        cp.start()
        cp.wait_send()
        _rcopy(r_ref, out_ref.at[pl.ds((1 - mc) * half, half), :], send_sems.at[0], recv_sems.at[0],
               (mx, my, 1 - mc)).wait_recv()
        mine.wait()

    return _comm_call(body, name, [r], jax.ShapeDtypeStruct((2 * half, LANES), r.dtype), 1)


def _gather8(x, reduce, name):
    rows = x.shape[0]

    def body(x_ref, out_ref, *rest):
        if reduce:
            buf_ref, send_sems, recv_sems = rest
        else:
            (send_sems, recv_sems), buf_ref = rest, out_ref
        mx, my, mc, _ = _place()
        me = 4 * mx + 2 * my + mc
        buf_ref[me] = x_ref[...]
        peers = []
        for j in range(1, N_DEV):
            px = 1 - mx if j & 4 else mx
            py = 1 - my if j & 2 else my
            pc = 1 - mc if j & 1 else mc
            peers.append((px, py, pc))
        sends = [_rcopy(x_ref, buf_ref.at[me], send_sems.at[j], recv_sems.at[j], p) for j, p in enumerate(peers)]
        for cp in sends:
            cp.start()
        for j, (px, py, pc) in enumerate(peers):
            _rcopy(x_ref, buf_ref.at[4 * px + 2 * py + pc], send_sems.at[j], recv_sems.at[j], (px, py, pc)).wait_recv()
        for cp in sends:
            cp.wait_send()
        if reduce:
            acc = buf_ref[0]
            for d in range(1, N_DEV):
                acc = acc + buf_ref[d]
            out_ref[...] = acc

    vmem = pl.BlockSpec(memory_space=pltpu.VMEM)
    sems = [pltpu.SemaphoreType.DMA((N_DEV - 1,)), pltpu.SemaphoreType.DMA((N_DEV - 1,))]
    if reduce:
        out_shape = jax.ShapeDtypeStruct((rows, LANES), F32)
        scratch = [pltpu.VMEM((N_DEV, rows, LANES), F32)] + sems
    else:
        out_shape = jax.ShapeDtypeStruct((N_DEV, rows, LANES), F32)
        scratch = sems
    return pl.pallas_call(body, name=name, in_specs=[vmem], out_specs=vmem, out_shape=out_shape,
                          scratch_shapes=scratch)(x)


def _add2(a, b, name):
    rows = a.shape[0]
    tr = _pick(rows, 1024, 8)
    return pl.pallas_call(
        lambda a_ref, b_ref, o_ref: o_ref.__setitem__(..., a_ref[...] + b_ref[...]), name=name,
        grid=(rows // tr,), in_specs=[_row(tr, LANES)] * 2, out_specs=_row(tr, LANES),
        out_shape=jax.ShapeDtypeStruct(a.shape, F32), compiler_params=_params(1))(a, b)


def _sum4(p, name):
    half = p.shape[1]
    tr = _pick(half, 1024, 8)

    def body(p0, p1, p2, p3, o_ref):
        o_ref[...] = ((p0[0] + p1[0]) + p2[0]) + p3[0]

    return pl.pallas_call(
        body, name=name, grid=(half // tr,),
        in_specs=[pl.BlockSpec((1, tr, LANES), lambda i, k=k: (k, i, 0)) for k in range(N_CHIPS)],
        out_specs=_row(tr, LANES), out_shape=jax.ShapeDtypeStruct((half, LANES), F32),
        compiler_params=_params(1))(p, p, p, p)


def _adamw(w, g, m, v, name):
    rows = w.shape[0]
    tr = _pick(rows, 512, 8)

    def body(w_ref, g_ref, m_ref, v_ref, d_ref, mo_ref, vo_ref):
        gv = g_ref[...]
        mn = ADAM_B1 * m_ref[...] + (1.0 - ADAM_B1) * gv
        vn = ADAM_B2 * v_ref[...] + (1.0 - ADAM_B2) * (gv * gv)
        m_hat = mn / (1.0 - ADAM_B1 ** ADAM_STEP)
        v_hat = vn / (1.0 - ADAM_B2 ** ADAM_STEP)
        d_ref[...] = -ADAM_LR * (m_hat / (jnp.sqrt(v_hat) + ADAM_EPS) + ADAM_WD * w_ref[...])
        mo_ref[...] = mn
        vo_ref[...] = vn

    shape = jax.ShapeDtypeStruct(w.shape, F32)
    return pl.pallas_call(
        body, name=name, grid=(rows // tr,), in_specs=[_row(tr, LANES)] * 4, out_specs=[_row(tr, LANES)] * 3,
        out_shape=[shape] * 3, compiler_params=_params(1))(w, g, m, v)


def _gather_weights(p):
    shard_shapes = {n: p[n].shape for n in SHARDED}
    big = _allgather_chips(_pack([p[n].astype(BF16) for n in BIG], R_BIG), "gather_weights")
    small = _gather8(_pack([p[n] for n in SMALL], R_SMALL), False, "gather_small")
    full = {}
    for names, per_chip in ((BIG, [big[k] for k in range(N_CHIPS)]),
                            (SMALL, [small[2 * k] for k in range(N_CHIPS)])):
        pieces = [_unpack(per_chip[k], [shard_shapes[n] for n in names]) for k in range(N_CHIPS)]
        for i, n in enumerate(names):
            full[n] = jnp.concatenate([pieces[k][i] for k in range(N_CHIPS)], axis=SHARD_AXIS[n])
    wb = full["w_in_b"][0]
    w = dict(
        w_in_a=full["w_in_a"][0], w_out_a=full["w_out_a"][0], w_out_b=full["w_out_b"][0],
        w_in_b=jnp.concatenate([wb[:, :2048], wb[:, 2064:3088], wb[:, 2048:2064],
                                jnp.zeros((D_MODEL, B_IN_PAD - 3088), BF16)], axis=1),
        w_gate_b=jnp.pad(full["w_gate_b"][0], ((0, 128 - B_GATE_RANK), (0, 0))),
        b_gate_b=full["b_gate_b"], gnorm_b=full["gnorm_b"],
        w_ada=full["w_ada"], w_up=full["w_up"], w_down=full["w_down"], conv_w=full["conv_w"],
        rel_bias=p["rel_bias"], norm_mix=p["norm_mix"], norm_ffn=p["norm_ffn"], b_ada=p["b_ada"],
        conv_b=p["conv_b"], norm_final=p["norm_final"].reshape(1, D_MODEL))
    return w


def _reference_shapes(g):
    gb = g["w_in_b"]
    out = dict(g)
    out.update(
        w_in_a=g["w_in_a"][None], w_out_a=g["w_out_a"][None], w_out_b=g["w_out_b"][None],
        w_in_b=jnp.concatenate([gb[:, :2048], gb[:, 3072:3088], gb[:, 2048:3072]], axis=1)[None],
        w_gate_b=g["w_gate_b"][:B_GATE_RANK][None], norm_final=g["norm_final"].reshape(D_MODEL))
    return out


def kernel(x, c, w_in_a, w_out_a, rel_bias, w_in_b, w_gate_b, b_gate_b, gnorm_b, w_out_b, norm_mix, norm_ffn, w_ada, b_ada, w_up, conv_w, conv_b, w_down, norm_final, loss_target, m_w_in_a, m_w_out_a, m_rel_bias, m_w_in_b, m_w_gate_b, m_b_gate_b, m_gnorm_b, m_w_out_b, m_norm_mix, m_norm_ffn, m_w_ada, m_b_ada, m_w_up, m_conv_w, m_conv_b, m_w_down, m_norm_final, v_w_in_a, v_w_out_a, v_rel_bias, v_w_in_b, v_w_gate_b, v_b_gate_b, v_gnorm_b, v_w_out_b, v_norm_mix, v_norm_ffn, v_w_ada, v_b_ada, v_w_up, v_conv_w, v_conv_b, v_w_down, v_norm_final):
    p = dict(zip(WEIGHTS, (w_in_a, w_out_a, rel_bias, w_in_b, w_gate_b, b_gate_b, gnorm_b, w_out_b, norm_mix,
                           norm_ffn, w_ada, b_ada, w_up, conv_w, conv_b, w_down, norm_final)))
    pm = dict(zip(WEIGHTS, (m_w_in_a, m_w_out_a, m_rel_bias, m_w_in_b, m_w_gate_b, m_b_gate_b, m_gnorm_b, m_w_out_b,
                            m_norm_mix, m_norm_ffn, m_w_ada, m_b_ada, m_w_up, m_conv_w, m_conv_b, m_w_down,
                            m_norm_final)))
    pv = dict(zip(WEIGHTS, (v_w_in_a, v_w_out_a, v_rel_bias, v_w_in_b, v_w_gate_b, v_b_gate_b, v_gnorm_b, v_w_out_b,
                            v_norm_mix, v_norm_ffn, v_w_ada, v_b_ada, v_w_up, v_conv_w, v_conv_b, v_w_down,
                            v_norm_final)))
    S = x.shape[1]

    w = _gather_weights(p)
    loss, dx0, grads = _local_step(x.reshape(S, D_MODEL), c, loss_target.reshape(S, D_MODEL), w)
    grads = _reference_shapes(grads)

    dest = jnp.stack([_pack([_chip_slice(grads[n], SHARD_AXIS[n], k) for n in SHARDED], R_SHARD)
                      for k in range(N_CHIPS)])
    keep, recv = _pair_exchange(dest, "grads_pair_exchange")
    half = R_SHARD // 2
    pair = _add2(keep.reshape(N_CHIPS * half, LANES), recv.reshape(N_CHIPS * half, LANES), "grads_pair_add")
    from_chips = _chip_exchange(pair.reshape(N_CHIPS, half, LANES), "grads_chip_exchange")
    g_shard = _pair_gather(_sum4(from_chips, "grads_chip_sum"), "grads_pair_gather")
    g_repl = _gather8(_pack([grads[n] for n in REPLICATED] + [loss[0, 0:1]], R_REPL), True, "grads_replicated_sum")

    def flat(d):
        return jnp.concatenate([_pack([d[n] for n in SHARDED], R_SHARD), _pack([d[n] for n in REPLICATED], R_REPL)])

    delta, new_m, new_v = _adamw(flat(p), jnp.concatenate([g_shard, g_repl]), flat(pm), flat(pv), "adamw")

    def named(flat2d):
        vals = _unpack(flat2d[:R_SHARD], [p[n].shape for n in SHARDED])
        vals += _unpack(flat2d[R_SHARD:], [p[n].shape for n in REPLICATED])
        d = dict(zip(SHARDED + REPLICATED, vals))
        return [d[n] for n in WEIGHTS]

    total_loss = g_repl.reshape(-1)[LOSS_SLOT]
    return (total_loss, dx0.reshape(x.shape), *named(jnp.concatenate([g_shard, g_repl])), *named(delta),
            *named(new_m), *named(new_v))
```

```python
import functools
import math

import numpy as np
import jax
import jax.numpy as jnp
from jax import lax
from jax.experimental import pallas as pl
from jax.experimental.pallas import tpu as pltpu

F32 = jnp.float32
BF16 = jnp.bfloat16
MESH = pl.DeviceIdType.MESH

D_MODEL = 1024
A_CONFIGS = ((128, 1), (512, 4), (2048, 16))
A_HEADS = 16
A_HEAD_DIM = 64
A_BLK = 128
N_BUCKETS = 32
MAX_DISTANCE = 2048
B_HEADS = 4
B_DK = 128
B_DV = 256
B_QK = 512
B_V = 1024
B_GATE_RANK = 16
B_TAU = 16.0
B_CHUNK = 64
B_IN_PAD = 3200
D_FF = 2816
EPS = 1e-6
NEG_INF = -1e30
ADAM_LR = 0.001
ADAM_B1 = 0.9
ADAM_B2 = 0.999
ADAM_EPS = 1e-08
ADAM_WD = 0.01
ADAM_STEP = 10

LANES = 1024
VMEM_LIMIT = 48 * 1024 * 1024
ROW_TILE = 256
GLA_ROWS = 512

HBM = pl.BlockSpec(memory_space=pl.ANY)


def _params(n_axes):
    return pltpu.CompilerParams(dimension_semantics=("arbitrary",) * n_axes, vmem_limit_bytes=VMEM_LIMIT)


def _pick(n, cap, mult=128):
    best = None
    for t in range(mult, min(n, cap) + 1, mult):
        if n % t == 0:
            best = t
    return n if best is None else best


def _matmul(a, b, mode, out_dtype, name, shape=None, tiles=None, b_spec=None, o_spec=None, o_shape=None, prev=None):
    dims = {"nn": (((1,), (0,)), ((), ())), "nt": NT, "tn": TN}[mode]
    if shape is None:
        if mode == "nn":
            (M, K), (_, N) = a.shape, b.shape
        elif mode == "nt":
            (M, K), (N, _) = a.shape, b.shape
        else:
            (K, M), (_, N) = a.shape, b.shape
    else:
        M, N, K = shape
    if tiles is None:
        tiles = (_pick(M, 1024, 128 if mode == "tn" else 8), _pick(N, 1536), _pick(K, 1024 if mode != "tn" else 512))
    tm, tn, tk = tiles
    nk = K // tk
    a_spec = pl.BlockSpec((tk, tm), lambda i, j, k: (k, i)) if mode == "tn" else pl.BlockSpec(
        (tm, tk), lambda i, j, k: (i, k))
    if b_spec is None:
        b_spec = pl.BlockSpec((tn, tk), lambda i, j, k: (j, k)) if mode == "nt" else pl.BlockSpec(
            (tk, tn), lambda i, j, k: (k, j))
    if o_spec is None:
        o_spec = pl.BlockSpec((tm, tn), lambda i, j, k: (i, j))
        o_shape = (M, N)

    def body(a_ref, b_ref, *rest):
        o_ref, acc_ref = rest[-2:]
        k = pl.program_id(2)
        part = lax.dot_general(a_ref[...].astype(BF16), b_ref[...].astype(BF16), dims, preferred_element_type=F32)

        @pl.when(k == 0)
        def _():
            acc_ref[...] = part

        @pl.when(k > 0)
        def _():
            acc_ref[...] += part

        @pl.when(k == nk - 1)
        def _():
            o_ref[...] = acc_ref[...].astype(o_ref.dtype)

    ins, in_specs, aliases = [a, b], [a_spec, b_spec], {}
    if prev is not None:
        ins.append(prev)
        in_specs.append(HBM)
        aliases = {2: 0}
    return pl.pallas_call(
        body, name=name, grid=(M // tm, N // tn, nk), in_specs=in_specs, out_specs=o_spec,
        out_shape=jax.ShapeDtypeStruct(o_shape, out_dtype), scratch_shapes=[pltpu.VMEM((tm, tn), F32)],
        input_output_aliases=aliases, compiler_params=_params(3))(*ins)


def _row(tr, d=D_MODEL):
    return pl.BlockSpec((tr, d), lambda i: (i, 0))


def _vec(d=D_MODEL):
    return pl.BlockSpec((1, d), lambda i: (0, 0))


def _modspec(j):
    return pl.BlockSpec((8, D_MODEL), lambda i: (0, j))


def _silu(x):
    return x * jax.nn.sigmoid(x)


def _dsilu(x):
    s = jax.nn.sigmoid(x)
    return s * (1.0 + x * (1.0 - s))


ADA_TN = 6 * D_MODEL // 4


def _ada_mod(c, w_ada, b_ada, layer, name):
    def body(c_ref, w_ref, b_ref, o_ref):
        sc = jnp.broadcast_to(_silu(c_ref[...]), (8, D_MODEL)).astype(BF16)
        o_ref[...] = jnp.dot(sc, w_ref[...], preferred_element_type=F32) + b_ref[...]

    return pl.pallas_call(
        body, name=name, grid=(4,),
        in_specs=[_vec(), pl.BlockSpec((None, D_MODEL, ADA_TN), lambda j: (j, layer, 0)),
                  pl.BlockSpec((1, ADA_TN), lambda j: (0, j))],
        out_specs=pl.BlockSpec((8, ADA_TN), lambda j: (0, j)), out_shape=jax.ShapeDtypeStruct((8, 6 * D_MODEL), F32),
        compiler_params=_params(1))(c, w_ada, b_ada)


def _ada_outer(c, dmods, name):
    def body(c_ref, d_ref, o_ref):
        row = lax.broadcasted_iota(jnp.int32, (8, 1), 0) == 0
        a = jnp.where(row, jnp.broadcast_to(_silu(c_ref[...]), (8, D_MODEL)), 0.0).astype(BF16)
        b = jnp.where(row, jnp.broadcast_to(d_ref[...], (8, ADA_TN)), 0.0).astype(BF16)
        o_ref[...] = lax.dot_general(a, b, (((0,), (0,)), ((), ())), preferred_element_type=F32)

    return pl.pallas_call(
        body, name=name, grid=(2, 4),
        in_specs=[pl.BlockSpec((1, D_MODEL), lambda l, j: (0, 0)),
                  pl.BlockSpec((None, 1, ADA_TN), lambda l, j: (l, 0, j))],
        out_specs=pl.BlockSpec((None, D_MODEL, ADA_TN), lambda l, j: (j, l, 0)),
        out_shape=jax.ShapeDtypeStruct((4, 2 * D_MODEL, ADA_TN), F32), compiler_params=_params(2))(c, dmods)


def _norm_mod(x, gamma, mod, j_sc, j_sh, name, resid=None, gate_mod=None, j_gate=None):
    S = x.shape[0]
    tr = ROW_TILE
    has_res = resid is not None

    def body(*refs):
        if has_res:
            x_ref, y_ref, gate_ref, g_ref, sc_ref, sh_ref, xo_ref, h_ref = refs
            xn = x_ref[...] + gate_ref[0:1, :] * y_ref[...]
            xo_ref[...] = xn
        else:
            x_ref, g_ref, sc_ref, sh_ref, h_ref = refs
            xn = x_ref[...]
        r = lax.rsqrt(jnp.mean(xn * xn, axis=-1, keepdims=True) + EPS)
        n = (xn * r) * g_ref[...]
        h_ref[...] = (n * (1.0 + sc_ref[0:1, :]) + sh_ref[0:1, :]).astype(BF16)

    if has_res:
        ins = [x, resid, gate_mod, gamma, mod, mod]
        in_specs = [_row(tr), _row(tr), _modspec(j_gate), _vec(), _modspec(j_sc), _modspec(j_sh)]
        out_specs = [_row(tr), _row(tr)]
        out_shape = [jax.ShapeDtypeStruct((S, D_MODEL), F32), jax.ShapeDtypeStruct((S, D_MODEL), BF16)]
    else:
        ins = [x, gamma, mod, mod]
        in_specs = [_row(tr), _vec(), _modspec(j_sc), _modspec(j_sh)]
        out_specs = _row(tr)
        out_shape = jax.ShapeDtypeStruct((S, D_MODEL), BF16)
    return pl.pallas_call(body, name=name, grid=(S // tr,), in_specs=in_specs, out_specs=out_specs,
                          out_shape=out_shape, compiler_params=_params(1))(*ins)


def _final_loss(x, resid, mod, j_gate, gamma, tgt, name):
    S = x.shape[0]
    tr = ROW_TILE

    def body(x_ref, y_ref, gate_ref, g_ref, t_ref, dx_ref, loss_ref, dg_ref):
        @pl.when(pl.program_id(0) == 0)
        def _():
            loss_ref[...] = jnp.zeros_like(loss_ref)
            dg_ref[...] = jnp.zeros_like(dg_ref)

        xn = x_ref[...] + gate_ref[0:1, :] * y_ref[...]
        r = lax.rsqrt(jnp.mean(xn * xn, axis=-1, keepdims=True) + EPS)
        xhat = xn * r
        err = xhat * g_ref[...] - t_ref[...]
        loss_ref[...] += 0.5 * jnp.sum(jnp.mean(err * err, axis=-1, keepdims=True))
        dy = err * (1.0 / D_MODEL)
        dg_ref[...] += jnp.sum(dy * xhat, axis=0, keepdims=True)
        dxh = dy * g_ref[...]
        dx_ref[...] = r * (dxh - xhat * jnp.mean(dxh * xhat, axis=-1, keepdims=True))

    return pl.pallas_call(
        body, name=name, grid=(S // tr,),
        in_specs=[_row(tr), _row(tr), _modspec(j_gate), _vec(), _row(tr)],
        out_specs=[_row(tr), pl.BlockSpec((1, 128), lambda i: (0, 0)), _vec()],
        out_shape=[jax.ShapeDtypeStruct((S, D_MODEL), F32), jax.ShapeDtypeStruct((1, 128), F32),
                   jax.ShapeDtypeStruct((1, D_MODEL), F32)],
        compiler_params=_params(1))(x, resid, mod, gamma, tgt)


def _gate_bwd(dx, y, mod, j_gate, name):
    S = dx.shape[0]
    tr = ROW_TILE

    def body(dx_ref, y_ref, gate_ref, dy_ref, dg_ref):
        @pl.when(pl.program_id(0) == 0)
        def _():
            dg_ref[...] = jnp.zeros_like(dg_ref)

        dx_v = dx_ref[...]
        dy_ref[...] = (gate_ref[0:1, :] * dx_v).astype(BF16)
        dg_ref[...] += jnp.sum(dx_v * y_ref[...], axis=0, keepdims=True)

    return pl.pallas_call(
        body, name=name, grid=(S // tr,), in_specs=[_row(tr), _row(tr), _modspec(j_gate)],
        out_specs=[_row(tr), _vec()],
        out_shape=[jax.ShapeDtypeStruct((S, D_MODEL), BF16), jax.ShapeDtypeStruct((1, D_MODEL), F32)],
        compiler_params=_params(1))(dx, y, mod)


def _norm_mod_bwd(dh, x, dx_res, gamma, mod, j_sc, name):
    S = x.shape[0]
    tr = ROW_TILE

    def body(dh_ref, x_ref, dr_ref, g_ref, sc_ref, dx_ref, dsc_ref, dsh_ref, dg_ref):
        @pl.when(pl.program_id(0) == 0)
        def _():
            dsc_ref[...] = jnp.zeros_like(dsc_ref)
            dsh_ref[...] = jnp.zeros_like(dsh_ref)
            dg_ref[...] = jnp.zeros_like(dg_ref)

        xv = x_ref[...]
        dh_v = dh_ref[...]
        r = lax.rsqrt(jnp.mean(xv * xv, axis=-1, keepdims=True) + EPS)
        xhat = xv * r
        dsh_ref[...] += jnp.sum(dh_v, axis=0, keepdims=True)
        dsc_ref[...] += jnp.sum(dh_v * (xhat * g_ref[...]), axis=0, keepdims=True)
        dn = dh_v * (1.0 + sc_ref[0:1, :])
        dg_ref[...] += jnp.sum(dn * xhat, axis=0, keepdims=True)
        dxh = dn * g_ref[...]
        dx_ref[...] = dr_ref[...] + r * (dxh - xhat * jnp.mean(dxh * xhat, axis=-1, keepdims=True))

    vec = jax.ShapeDtypeStruct((1, D_MODEL), F32)
    return pl.pallas_call(
        body, name=name, grid=(S // tr,),
        in_specs=[_row(tr), _row(tr), _row(tr), _vec(), _modspec(j_sc)],
        out_specs=[_row(tr), _vec(), _vec(), _vec()],
        out_shape=[jax.ShapeDtypeStruct((S, D_MODEL), F32), vec, vec, vec],
        compiler_params=_params(1))(dh, x, dx_res, gamma, mod)


def _shift_down(u, halo, s):
    r = pltpu.roll(u, s, 0)
    hr = pltpu.roll(halo, s, 0)
    rid = lax.broadcasted_iota(jnp.int32, hr.shape, 0)
    top = jnp.where(rid < s, hr, r[0:8])
    return jnp.concatenate([top, r[8:]], axis=0)


def _shift_up(u, halo, s):
    n = u.shape[0]
    r = pltpu.roll(u, n - s, 0)
    hr = pltpu.roll(halo, 8 - s, 0)
    rid = lax.broadcasted_iota(jnp.int32, hr.shape, 0)
    bot = jnp.where(rid >= 8 - s, hr, r[n - 8:])
    return jnp.concatenate([r[:n - 8], bot], axis=0)


def _conv3(u, halo, w_ref, b_ref):
    u1 = _shift_down(u, halo, 1)
    u2 = _shift_down(u, halo, 2)
    return b_ref[...] + ((w_ref[0:1, :] * u2 + w_ref[1:2, :] * u1) + w_ref[2:3, :] * u), u1, u2


CONV_TC = 1408


def _conv_specs(tr, S):
    nh = D_FF // CONV_TC
    hb = tr // 8

    def cur(off):
        return pl.BlockSpec((tr, CONV_TC), lambda j, i: (i, j + off))

    def halo(off):
        return pl.BlockSpec((8, CONV_TC), lambda j, i: (jnp.maximum(i * hb - 1, 0), j + off))

    def w(off):
        return pl.BlockSpec((3, CONV_TC), lambda j, i: (0, j + off))

    def b(off):
        return pl.BlockSpec((1, CONV_TC), lambda j, i: (0, j + off))

    return nh, cur, halo, w, b


def _conv_gate(u, conv_w, conv_b, name):
    S = u.shape[0]
    tr = ROW_TILE
    nh, cur, halo, w, b = _conv_specs(tr, S)

    def body(ua_ref, ha_ref, ub_ref, hb_ref, wa_ref, wb_ref, ba_ref, bb_ref, o_ref):
        first = pl.program_id(1) == 0
        ha = jnp.where(first, 0.0, ha_ref[...])
        hbv = jnp.where(first, 0.0, hb_ref[...])
        a, _, _ = _conv3(ua_ref[...], ha, wa_ref, ba_ref)
        bb, _, _ = _conv3(ub_ref[...], hbv, wb_ref, bb_ref)
        o_ref[...] = (_silu(a) * bb).astype(BF16)

    return pl.pallas_call(
        body, name=name, grid=(nh, S // tr),
        in_specs=[cur(0), halo(0), cur(nh), halo(nh), w(0), w(nh), b(0), b(nh)],
        out_specs=pl.BlockSpec((tr, CONV_TC), lambda j, i: (i, j)),
        out_shape=jax.ShapeDtypeStruct((S, D_FF), BF16), compiler_params=_params(2))(
            u, u, u, u, conv_w, conv_w, conv_b, conv_b)


def _conv_gate_bwd(u, dact, conv_w, conv_b, name):
    S = u.shape[0]
    tr = ROW_TILE
    nh, cur, halo, w, b = _conv_specs(tr, S)

    def body(ua_ref, ha_ref, ub_ref, hb_ref, wa_ref, wb_ref, ba_ref, bb_ref, da_ref,
             dua_ref, dub_ref, dwa_ref, dwb_ref, dba_ref, dbb_ref):
        first = pl.program_id(1) == 0

        @pl.when(first)
        def _():
            for r in (dwa_ref, dwb_ref, dba_ref, dbb_ref):
                r[...] = jnp.zeros_like(r)

        ha = jnp.where(first, 0.0, ha_ref[...])
        hbv = jnp.where(first, 0.0, hb_ref[...])
        ua, ub = ua_ref[...], ub_ref[...]
        a, ua1, ua2 = _conv3(ua, ha, wa_ref, ba_ref)
        bb, ub1, ub2 = _conv3(ub, hbv, wb_ref, bb_ref)
        dact_v = da_ref[...]
        da = dact_v * bb * _dsilu(a)
        db = dact_v * _silu(a)
        dua_ref[...] = da
        dub_ref[...] = db
        for d, x0, x1, x2, dw_ref, dbias_ref in ((da, ua, ua1, ua2, dwa_ref, dba_ref),
                                                 (db, ub, ub1, ub2, dwb_ref, dbb_ref)):
            dbias_ref[...] += jnp.sum(d, axis=0, keepdims=True)
            dw_ref[0:1, :] += jnp.sum(d * x2, axis=0, keepdims=True)
            dw_ref[1:2, :] += jnp.sum(d * x1, axis=0, keepdims=True)
            dw_ref[2:3, :] += jnp.sum(d * x0, axis=0, keepdims=True)

    half = pl.BlockSpec((tr, CONV_TC), lambda j, i: (i, j))
    dw = pl.BlockSpec((8, CONV_TC), lambda j, i: (0, j))
    dbs = pl.BlockSpec((1, CONV_TC), lambda j, i: (0, j))
    f = lambda r, c: jax.ShapeDtypeStruct((r, c), F32)
    return pl.pallas_call(
        body, name=name, grid=(nh, S // tr),
        in_specs=[cur(0), halo(0), cur(nh), halo(nh), w(0), w(nh), b(0), b(nh), half],
        out_specs=[half, half, dw, dw, dbs, dbs],
        out_shape=[f(S, D_FF), f(S, D_FF), f(8, D_FF), f(8, D_FF), f(1, D_FF), f(1, D_FF)],
        compiler_params=_params(2))(u, u, u, u, conv_w, conv_w, conv_b, conv_b, dact)


def _conv_t(duc, conv_w, half, prev, name):
    S = duc.shape[0]
    tr = ROW_TILE
    nh = D_FF // CONV_TC
    hb = tr // 8
    nlast = S // 8 - 1
    nsteps = S // tr
    off = half * nh

    def body(*refs):
        if prev is None:
            d_ref, h_ref, w_ref, o_ref = refs
        else:
            d_ref, h_ref, w_ref, _, o_ref = refs
        last = pl.program_id(1) == nsteps - 1
        hv = jnp.where(last, 0.0, h_ref[...])
        d = d_ref[...]
        d1 = _shift_up(d, hv, 1)
        d2 = _shift_up(d, hv, 2)
        o_ref[...] = ((w_ref[2:3, :] * d + w_ref[1:2, :] * d1) + w_ref[0:1, :] * d2).astype(BF16)

    in_specs = [pl.BlockSpec((tr, CONV_TC), lambda j, i: (i, j)),
                pl.BlockSpec((8, CONV_TC), lambda j, i: (jnp.minimum((i + 1) * hb, nlast), j)),
                pl.BlockSpec((3, CONV_TC), lambda j, i: (0, j + off))]
    ins = [duc, duc, conv_w]
    aliases = {}
    if prev is not None:
        in_specs.append(HBM)
        ins.append(prev)
        aliases = {3: 0}
    return pl.pallas_call(
        body, name=name, grid=(nh, nsteps), in_specs=in_specs,
        out_specs=pl.BlockSpec((tr, CONV_TC), lambda j, i: (i, j + off)),
        out_shape=jax.ShapeDtypeStruct((S, 2 * D_FF), BF16), input_output_aliases=aliases,
        compiler_params=_params(2))(*ins)


def _bucket_maps():
    qi = np.arange(A_BLK)[:, None]
    ki = np.arange(2 * A_BLK)[None, :]
    steps = np.clip(qi + A_BLK - ki, 0, A_BLK)
    out = []
    max_exact = N_BUCKETS // 2
    for _, dil in A_CONFIGS:
        dist = steps * dil
        n = np.maximum(dist, max_exact).astype(np.float32)
        large = max_exact + (np.log(n / np.float32(max_exact)) / np.float32(math.log(MAX_DISTANCE / max_exact))
                             * np.float32(N_BUCKETS - max_exact)).astype(np.int32)
        large = np.minimum(large, N_BUCKETS - 1)
        out.append(np.where(dist < max_exact, dist, large))
    return np.stack(out).astype(np.int32)


def _bias_build(rel_bias, buckets, name):
    ng = len(A_CONFIGS)

    def body(t_ref, bk_ref, o_ref):
        gh = pl.program_id(0) * A_HEADS + pl.program_id(1)
        bk = bk_ref[0]
        acc = jnp.zeros((A_BLK, 2 * A_BLK), F32)
        for b in range(N_BUCKETS):
            acc = jnp.where(bk == b, t_ref[b, gh], acc)
        o_ref[0] = acc

    return pl.pallas_call(
        body, name=name, grid=(ng, A_HEADS),
        in_specs=[pl.BlockSpec(memory_space=pltpu.SMEM), pl.BlockSpec((1, A_BLK, 2 * A_BLK), lambda g, h: (g, 0, 0))],
        out_specs=pl.BlockSpec((1, A_BLK, 2 * A_BLK), lambda g, h: (g * A_HEADS + h, 0, 0)),
        out_shape=jax.ShapeDtypeStruct((ng * A_HEADS, A_BLK, 2 * A_BLK), F32),
        compiler_params=_params(2))(rel_bias, buckets)


def _bias_bwd(dbias, buckets, name):
    ng = len(A_CONFIGS)

    def body(d_ref, bk_ref, o_ref):
        gh = pl.program_id(0) * A_HEADS + pl.program_id(1)
        bk = bk_ref[0]
        d = d_ref[0]
        for b in range(N_BUCKETS):
            o_ref[b, gh] = jnp.sum(jnp.where(bk == b, d, 0.0))

    return pl.pallas_call(
        body, name=name, grid=(ng, A_HEADS),
        in_specs=[pl.BlockSpec((1, A_BLK, 2 * A_BLK), lambda g, h: (g * A_HEADS + h, 0, 0)),
                  pl.BlockSpec((1, A_BLK, 2 * A_BLK), lambda g, h: (g, 0, 0))],
        out_specs=pl.BlockSpec(memory_space=pltpu.SMEM),
        out_shape=jax.ShapeDtypeStruct((N_BUCKETS, ng * A_HEADS), F32),
        compiler_params=_params(2))(dbias, buckets)


def _attn_mask(b):
    qi = lax.broadcasted_iota(jnp.int32, (A_BLK, 2 * A_BLK), 0)
    ki = lax.broadcasted_iota(jnp.int32, (A_BLK, 2 * A_BLK), 1)
    band = (ki >= qi) & (ki <= qi + A_BLK)
    return band & ((b > 0) | (ki >= A_BLK))


def _attn_in_specs(g, dil):
    W = A_HEADS * A_HEAD_DIM

    def spec(t, prev, nb):
        def im(r, b):
            bb = jnp.minimum(b, nb - 1)
            if prev:
                bb = jnp.maximum(bb - 1, 0)
            return (bb, r * 3 + t)
        return pl.BlockSpec((A_BLK, W), im)

    return lambda nb: [spec(0, False, nb), spec(1, False, nb), spec(1, True, nb), spec(2, False, nb),
                       spec(2, True, nb)]


def _attn_fwd(qkv, bias, g, name):
    S = qkv.shape[0]
    _, dil = A_CONFIGS[g]
    L = S // dil
    nb = L // A_BLK
    W = A_HEADS * A_HEAD_DIM
    qv = qkv.reshape(L, dil * 3 * W)

    def body(q_ref, kc_ref, kp_ref, vc_ref, vp_ref, bias_ref, o_ref, l_ref):
        mask = _attn_mask(pl.program_id(1))
        for h in range(A_HEADS):
            hs = slice(h * A_HEAD_DIM, (h + 1) * A_HEAD_DIM)
            qh = q_ref[:, hs] * 0.125
            k2 = jnp.concatenate([kp_ref[:, hs], kc_ref[:, hs]], axis=0)
            v2 = jnp.concatenate([vp_ref[:, hs], vc_ref[:, hs]], axis=0)
            s = lax.dot_general(qh, k2, (((1,), (1,)), ((), ())), preferred_element_type=F32) + bias_ref[h]
            s = jnp.where(mask, s, NEG_INF)
            m = jnp.max(s, axis=-1, keepdims=True)
            p = jnp.exp(s - m)
            den = jnp.sum(p, axis=-1, keepdims=True)
            o = jnp.dot(p.astype(BF16), v2, preferred_element_type=F32) / den
            o_ref[:, hs] = o
            l_ref[:, hs] = jnp.broadcast_to(m + jnp.log(den), (A_BLK, A_HEAD_DIM))

    out_spec = pl.BlockSpec((A_BLK, W), lambda r, b: (b, r))
    o, lse = pl.pallas_call(
        body, name=name, grid=(dil, nb),
        in_specs=_attn_in_specs(g, dil)(nb) + [pl.BlockSpec((A_HEADS, A_BLK, 2 * A_BLK), lambda r, b: (g, 0, 0))],
        out_specs=[out_spec, out_spec],
        out_shape=[jax.ShapeDtypeStruct((L, dil * W), F32)] * 2,
        compiler_params=_params(2))(qv, qv, qv, qv, qv, bias)
    return o.reshape(S, W), lse.reshape(S, W)


def _mix_fwd(os, ls, name):
    S = os[0].shape[0]
    tr = ROW_TILE

    def body(o0, o1, o2, l0, l1, l2, om_ref, lt_ref):
        a, b, c = l0[...], l1[...], l2[...]
        m = jnp.maximum(jnp.maximum(a, b), c)
        ea, eb, ec = jnp.exp(a - m), jnp.exp(b - m), jnp.exp(c - m)
        z = (ea + eb) + ec
        om_ref[...] = ((ea / z) * o0[...] + (eb / z) * o1[...]) + (ec / z) * o2[...]
        lt_ref[...] = m + jnp.log(z)

    return pl.pallas_call(
        body, name=name, grid=(S // tr,), in_specs=[_row(tr)] * 6, out_specs=[_row(tr)] * 2,
        out_shape=[jax.ShapeDtypeStruct((S, D_MODEL), F32)] * 2, compiler_params=_params(1))(*os, *ls)


def _attn_bwd(qkv, bias, d_o, omix, ltot, g, name):
    S = qkv.shape[0]
    _, dil = A_CONFIGS[g]
    L = S // dil
    nb = L // A_BLK
    W = A_HEADS * A_HEAD_DIM
    qv = qkv.reshape(L, dil * 3 * W)
    view = lambda t: t.reshape(L, dil * W)

    def body(q_ref, kc_ref, kp_ref, vc_ref, vp_ref, bias_ref, do_ref, om_ref, lt_ref,
             dq_ref, dk_ref, dv_ref, db_ref, ck_ref, cv_ref):
        r, b = pl.program_id(0), pl.program_id(1)

        @pl.when((r == 0) & (b == 0))
        def _():
            db_ref[...] = jnp.zeros_like(db_ref)

        @pl.when(b == 0)
        def _():
            ck_ref[...] = jnp.zeros_like(ck_ref)
            cv_ref[...] = jnp.zeros_like(cv_ref)

        @pl.when(b < nb)
        def _():
            mask = _attn_mask(b)
            for h in range(A_HEADS):
                hs = slice(h * A_HEAD_DIM, (h + 1) * A_HEAD_DIM)
                qh = q_ref[:, hs] * 0.125
                k2 = jnp.concatenate([kp_ref[:, hs], kc_ref[:, hs]], axis=0)
                v2 = jnp.concatenate([vp_ref[:, hs], vc_ref[:, hs]], axis=0)
                s = lax.dot_general(qh, k2, (((1,), (1,)), ((), ())), preferred_element_type=F32) + bias_ref[h]
                s = jnp.where(mask, s, NEG_INF)
                wp = jnp.exp(s - lt_ref[:, h * A_HEAD_DIM:h * A_HEAD_DIM + 1])
                do_h = do_ref[:, hs]
                t_h = jnp.sum(do_h * om_ref[:, hs], axis=-1, keepdims=True)
                do_b = do_h.astype(BF16)
                dp = lax.dot_general(do_b, v2, (((1,), (1,)), ((), ())), preferred_element_type=F32)
                ds = wp * (dp - t_h)
                db_ref[h] += ds
                ds_b = ds.astype(BF16)
                dv2 = lax.dot_general(wp.astype(BF16), do_b, (((0,), (0,)), ((), ())), preferred_element_type=F32)
                dk2 = lax.dot_general(ds_b, qh, (((0,), (0,)), ((), ())), preferred_element_type=F32)
                dq_ref[:, hs] = (jnp.dot(ds_b, k2, preferred_element_type=F32) * 0.125).astype(BF16)
                dk_ref[:, hs] = (ck_ref[:, hs] + dk2[:A_BLK]).astype(BF16)
                dv_ref[:, hs] = (cv_ref[:, hs] + dv2[:A_BLK]).astype(BF16)
                ck_ref[:, hs] = dk2[A_BLK:]
                cv_ref[:, hs] = dv2[A_BLK:]

        @pl.when(b == nb)
        def _():
            dk_ref[...] = ck_ref[...].astype(BF16)
            dv_ref[...] = cv_ref[...].astype(BF16)

    act = pl.BlockSpec((A_BLK, W), lambda r, b: (jnp.minimum(b, nb - 1), r))
    lag = pl.BlockSpec((A_BLK, W), lambda r, b: (jnp.maximum(b - 1, 0), r))
    full = pl.BlockSpec((A_HEADS, A_BLK, 2 * A_BLK), lambda r, b: (0, 0, 0))
    in_specs = _attn_in_specs(g, dil)(nb) + [pl.BlockSpec((A_HEADS, A_BLK, 2 * A_BLK), lambda r, b: (g, 0, 0)),
                                             act, act, act]
    dq, dk, dv, dbias = pl.pallas_call(
        body, name=name, grid=(dil, nb + 1), in_specs=in_specs,
        out_specs=[act, lag, lag, full],
        out_shape=[jax.ShapeDtypeStruct((L, dil * W), BF16)] * 3
        + [jax.ShapeDtypeStruct((A_HEADS, A_BLK, 2 * A_BLK), F32)],
        scratch_shapes=[pltpu.VMEM((A_BLK, W), F32), pltpu.VMEM((A_BLK, W), F32)],
        compiler_params=_params(2))(qv, qv, qv, qv, qv, bias, view(d_o), view(omix), view(ltot))
    return dq.reshape(S, W), dk.reshape(S, W), dv.reshape(S, W), dbias


NT = (((1,), (1,)), ((), ()))
TN = (((0,), (0,)), ((), ()))


def _dot(a, b, dims=(((1,), (0,)), ((), ()))):
    return lax.dot_general(a.astype(BF16), b.astype(BF16), dims, preferred_element_type=F32)


def _gla_gates(glr, wg_ref, bg_ref):
    z = _dot(glr, wg_ref[...]) + bg_ref[...]
    log_sig = -(jnp.maximum(-z, 0.0) + jnp.log1p(jnp.exp(-jnp.abs(z))))
    return z, log_sig / B_TAU


def _gla_chunk(q, k, gk):
    row = lax.broadcasted_iota(jnp.int32, (B_CHUNK, B_CHUNK), 0)
    col = lax.broadcasted_iota(jnp.int32, (B_CHUNK, B_CHUNK), 1)
    causal = row >= col
    bcum = jnp.dot(causal.astype(F32), gk, precision=lax.Precision.HIGHEST, preferred_element_type=F32)
    bl = bcum[B_CHUNK - 1:B_CHUNK, :]
    qt = (q * (B_DK ** -0.5)) * jnp.exp(bcum)
    kt = k * jnp.exp(-bcum)
    kd = k * jnp.exp(bl - bcum)
    a = jnp.where(causal, _dot(qt, kt, NT), 0.0)
    return causal, bcum, bl, qt, kt, kd, a


def _gla_specs(tg):
    q = pl.BlockSpec((tg, B_DK), lambda h, i: (i, h))
    k = pl.BlockSpec((tg, B_DK), lambda h, i: (i, B_HEADS + h))
    v = pl.BlockSpec((tg, B_DV), lambda h, i: (i, B_HEADS + h))
    glr = pl.BlockSpec((tg, 128), lambda h, i: (i, 24))
    wg = pl.BlockSpec((128, B_DK), lambda h, i: (0, h))
    bg = pl.BlockSpec((1, B_DK), lambda h, i: (0, h))
    return [q, k, v, glr, wg, bg]


def _gla_fwd(proj, w_gate, b_gate, name):
    S = proj.shape[0]
    tg = GLA_ROWS
    nc = tg // B_CHUNK

    def body(q_ref, k_ref, v_ref, glr_ref, wg_ref, bg_ref, o_ref, st_ref, state_ref):
        @pl.when(pl.program_id(1) == 0)
        def _():
            state_ref[...] = jnp.zeros_like(state_ref)

        _, gk_all = _gla_gates(glr_ref[...], wg_ref, bg_ref)
        st = state_ref[...]
        for c in range(nc):
            rows = slice(c * B_CHUNK, (c + 1) * B_CHUNK)
            v = v_ref[rows, :]
            _, _, bl, qt, _, kd, a = _gla_chunk(q_ref[rows, :], k_ref[rows, :], gk_all[rows, :])
            o_ref[rows, :] = _dot(a, v) + _dot(qt, st, NT)
            st_ref[c, 0] = st
            st = st * jnp.exp(bl) + _dot(v, kd, TN)
        state_ref[...] = st

    return pl.pallas_call(
        body, name=name, grid=(B_HEADS, S // tg), in_specs=_gla_specs(tg),
        out_specs=[pl.BlockSpec((tg, B_DV), lambda h, i: (i, h)),
                   pl.BlockSpec((nc, 1, B_DV, B_DK), lambda h, i: (i, h, 0, 0))],
        out_shape=[jax.ShapeDtypeStruct((S, B_V), F32),
                   jax.ShapeDtypeStruct((S // B_CHUNK, B_HEADS, B_DV, B_DK), F32)],
        scratch_shapes=[pltpu.VMEM((B_DV, B_DK), F32)], compiler_params=_params(2))(
            proj, proj, proj, proj, w_gate, b_gate)


def _gla_bwd(proj, w_gate, b_gate, states, d_o, name):
    S = proj.shape[0]
    tg = GLA_ROWS
    nc = tg // B_CHUNK
    ni = S // tg

    def rev(spec):
        return pl.BlockSpec(spec.block_shape, lambda h, i, im=spec.index_map: im(h, ni - 1 - i))

    def body(q_ref, k_ref, v_ref, glr_ref, wg_ref, bg_ref, st_ref, do_ref,
             dq_ref, dk_ref, dv_ref, dz_ref, dbg_ref, dstate_ref):
        @pl.when(pl.program_id(1) == 0)
        def _():
            dstate_ref[...] = jnp.zeros_like(dstate_ref)
            dbg_ref[...] = jnp.zeros_like(dbg_ref)

        z_all, gk_all = _gla_gates(glr_ref[...], wg_ref, bg_ref)
        dst = dstate_ref[...]
        for c in range(nc - 1, -1, -1):
            rows = slice(c * B_CHUNK, (c + 1) * B_CHUNK)
            v = v_ref[rows, :]
            d_out = do_ref[rows, :]
            st = st_ref[c, 0]
            causal, bcum, bl, qt, kt, kd, a = _gla_chunk(q_ref[rows, :], k_ref[rows, :], gk_all[rows, :])
            da = jnp.where(causal, _dot(d_out, v, NT), 0.0)
            dv_ref[rows, :] = (_dot(a, d_out, TN) + _dot(kd, dst, NT)).astype(BF16)
            dqt = _dot(da, kt) + _dot(d_out, st)
            dkt = _dot(da, qt, TN)
            dkd = _dot(v, dst)
            dec = jnp.exp(bl)
            ddec = jnp.sum(dst * st, axis=0, keepdims=True)
            dst = dst * dec + _dot(d_out, qt, TN)
            dq_ref[rows, :] = (dqt * jnp.exp(bcum) * (B_DK ** -0.5)).astype(BF16)
            dk_ref[rows, :] = (dkt * jnp.exp(-bcum) + dkd * jnp.exp(bl - bcum)).astype(BF16)
            db = (dqt * qt - dkt * kt) - dkd * kd
            dbl = jnp.sum(dkd * kd, axis=0, keepdims=True) + dec * ddec
            upper = jnp.logical_not(causal) | (lax.broadcasted_iota(jnp.int32, (B_CHUNK, B_CHUNK), 0)
                                               == lax.broadcasted_iota(jnp.int32, (B_CHUNK, B_CHUNK), 1))
            dgk = jnp.dot(upper.astype(F32), db, precision=lax.Precision.HIGHEST, preferred_element_type=F32) + dbl
            dz = dgk * (1.0 / B_TAU) * jax.nn.sigmoid(-z_all[rows, :])
            dz_ref[rows, :] = dz
            dbg_ref[...] += jnp.sum(dz, axis=0, keepdims=True)
        dstate_ref[...] = dst

    qs = pl.BlockSpec((tg, B_DK), lambda h, i: (i, h))
    vs = pl.BlockSpec((tg, B_DV), lambda h, i: (i, h))
    in_specs = [rev(s) if n < 4 else s for n, s in enumerate(_gla_specs(tg))]
    in_specs += [rev(pl.BlockSpec((nc, 1, B_DV, B_DK), lambda h, i: (i, h, 0, 0))), rev(vs)]
    return pl.pallas_call(
        body, name=name, grid=(B_HEADS, ni), in_specs=in_specs,
        out_specs=[rev(qs), rev(qs), rev(vs), rev(qs), pl.BlockSpec((1, B_DK), lambda h, i: (0, h))],
        out_shape=[jax.ShapeDtypeStruct((S, B_QK), BF16), jax.ShapeDtypeStruct((S, B_QK), BF16),
                   jax.ShapeDtypeStruct((S, B_V), BF16), jax.ShapeDtypeStruct((S, B_QK), F32),
                   jax.ShapeDtypeStruct((1, B_QK), F32)],
        scratch_shapes=[pltpu.VMEM((B_DV, B_DK), F32)], compiler_params=_params(2))(
            proj, proj, proj, proj, w_gate, b_gate, states, d_o)


def _gla_out(o, proj, gnorm, name):
    S = o.shape[0]
    tr = ROW_TILE

    def body(o_ref, r_ref, g_ref, y_ref):
        for h in range(B_HEADS):
            hs = slice(h * B_DV, (h + 1) * B_DV)
            oh = o_ref[:, hs]
            rs = lax.rsqrt(jnp.mean(oh * oh, axis=-1, keepdims=True) + EPS)
            y_ref[:, hs] = (((oh * rs) * g_ref[...]) * _silu(r_ref[:, hs])).astype(BF16)

    return pl.pallas_call(
        body, name=name, grid=(S // tr,),
        in_specs=[_row(tr), pl.BlockSpec((tr, B_V), lambda i: (i, 2)), _vec(B_DV)], out_specs=_row(tr),
        out_shape=jax.ShapeDtypeStruct((S, B_V), BF16), compiler_params=_params(1))(o, proj, gnorm)


def _gla_out_bwd(o, proj, gnorm, d_y, name):
    S = o.shape[0]
    tr = ROW_TILE

    def body(o_ref, r_ref, g_ref, dy_ref, do_ref, dr_ref, dg_ref):
        @pl.when(pl.program_id(0) == 0)
        def _():
            dg_ref[...] = jnp.zeros_like(dg_ref)

        for h in range(B_HEADS):
            hs = slice(h * B_DV, (h + 1) * B_DV)
            oh, rv, dyv = o_ref[:, hs], r_ref[:, hs], dy_ref[:, hs]
            rs = lax.rsqrt(jnp.mean(oh * oh, axis=-1, keepdims=True) + EPS)
            xhat = oh * rs
            dr_ref[:, hs] = (dyv * (xhat * g_ref[...]) * _dsilu(rv)).astype(BF16)
            dn = dyv * _silu(rv)
            dg_ref[...] += jnp.sum(dn * xhat, axis=0, keepdims=True)
            dxh = dn * g_ref[...]
            do_ref[:, hs] = rs * (dxh - xhat * jnp.mean(dxh * xhat, axis=-1, keepdims=True))

    return pl.pallas_call(
        body, name=name, grid=(S // tr,),
        in_specs=[_row(tr), pl.BlockSpec((tr, B_V), lambda i: (i, 2)), _vec(B_DV), _row(tr)],
        out_specs=[_row(tr), _row(tr), _vec(B_DV)],
        out_shape=[jax.ShapeDtypeStruct((S, B_V), F32), jax.ShapeDtypeStruct((S, B_V), BF16),
                   jax.ShapeDtypeStruct((1, B_DV), F32)],
        compiler_params=_params(1))(o, proj, gnorm, d_y)


J_SH1, J_SC1, J_G1, J_SH2, J_SC2, J_G2 = range(6)


IN_A_TN = 768
UP_TN = 2 * D_FF // 4
DOWN_TK = 256


def _ffn_fwd(h, w, i, tag):
    S = h.shape[0]
    u = _matmul(h, w["w_up"], "nn", F32, f"up{tag}", shape=(S, 2 * D_FF, D_MODEL), tiles=(1024, UP_TN, D_MODEL),
                b_spec=pl.BlockSpec((None, D_MODEL, UP_TN), lambda m, j, k: (j, i, 0)))
    act = _conv_gate(u, w["conv_w"][i], w["conv_b"][i:i + 1], f"conv_gate{tag}")
    f = _matmul(act, w["w_down"], "nn", F32, f"down{tag}", shape=(S, D_MODEL, D_FF), tiles=(1024, D_MODEL, DOWN_TK),
                b_spec=pl.BlockSpec((DOWN_TK, D_MODEL), lambda m, j, k: (i * (D_FF // DOWN_TK) + k, j)))
    return u, act, f


def _ffn_bwd(dx_out, f, u, act, h, x_in, mod, w, i, tag, prev):
    S = h.shape[0]
    df, dg2 = _gate_bwd(dx_out, f, mod, J_G2, f"gate2_bwd{tag}")
    dact = _matmul(df, w["w_down"], "nt", F32, f"down_dx{tag}", shape=(S, D_FF, D_MODEL),
                   tiles=(1024, D_FF // 2, D_MODEL),
                   b_spec=pl.BlockSpec((D_FF // 2, D_MODEL), lambda m, j, k: (2 * i + j, k)))
    d_w_down = _matmul(act, df, "tn", F32, f"down_dw{tag}", shape=(D_FF, D_MODEL, S), tiles=(DOWN_TK, D_MODEL, 512),
                       o_spec=pl.BlockSpec((DOWN_TK, D_MODEL), lambda m, j, k: (i * (D_FF // DOWN_TK) + m, j)),
                       o_shape=(2 * D_FF, D_MODEL), prev=None if prev is None else prev["w_down"])
    duca, ducb, dcwa, dcwb, dcba, dcbb = _conv_gate_bwd(u, dact, w["conv_w"][i], w["conv_b"][i:i + 1],
                                                        f"conv_gate_bwd{tag}")
    du = _conv_t(duca, w["conv_w"][i], 0, None, f"conv_t_a{tag}")
    du = _conv_t(ducb, w["conv_w"][i], 1, du, f"conv_t_b{tag}")
    dh = _matmul(du, w["w_up"], "nt", F32, f"up_dx{tag}", shape=(S, D_MODEL, 2 * D_FF), tiles=(1024, D_MODEL, UP_TN),
                 b_spec=pl.BlockSpec((None, D_MODEL, UP_TN), lambda m, j, k: (k, i, 0)))
    d_w_up = _matmul(h, du, "tn", F32, f"up_dw{tag}", shape=(D_MODEL, 2 * D_FF, S), tiles=(D_MODEL, UP_TN, 512),
                     o_spec=pl.BlockSpec((None, D_MODEL, UP_TN), lambda m, j, k: (j, i, 0)),
                     o_shape=(4, 2 * D_MODEL, UP_TN), prev=None if prev is None else prev["w_up"])
    dx_in, dsc2, dsh2, dgam = _norm_mod_bwd(dh, x_in, dx_out, w["norm_ffn"][i:i + 1], mod, J_SC2,
                                            f"norm_ffn_bwd{tag}")
    grads = dict(w_down=d_w_down, w_up=d_w_up, norm_ffn=dgam,
                 conv_w=jnp.concatenate([dcwa[0:3], dcwb[0:3]], axis=1),
                 conv_b=jnp.concatenate([dcba, dcbb], axis=1))
    return dx_in, (dsh2, dsc2, dg2), grads


def _local_step(x, c, tgt, w):
    buckets = jnp.asarray(_bucket_maps())
    S = x.shape[0]
    mods = [_ada_mod(c, w["w_ada"], w["b_ada"][i:i + 1], i, f"ada_mod{i}") for i in range(2)]

    h1 = _norm_mod(x, w["norm_mix"][0:1], mods[0], J_SC1, J_SH1, "norm_mix0")
    qkv = [_matmul(h1, w["w_in_a"], "nn", BF16, f"in_a{g}", shape=(S, 3 * D_MODEL, D_MODEL),
                   tiles=(1024, IN_A_TN, D_MODEL),
                   b_spec=pl.BlockSpec((None, D_MODEL, IN_A_TN), lambda m, j, k, g=g: ((4 * g + j) // 3, 0, (4 * g + j) % 3)))
           for g in range(3)]
    bias = _bias_build(w["rel_bias"], buckets, "bias_build")
    os_, ls_ = zip(*[_attn_fwd(qkv[g], bias, g, f"attn_fwd{g}") for g in range(3)])
    omix, ltot = _mix_fwd(os_, ls_, "mix_fwd")
    y0 = _matmul(omix, w["w_out_a"], "nn", F32, "out_a")
    x1, h2 = _norm_mod(x, w["norm_ffn"][0:1], mods[0], J_SC2, J_SH2, "norm_ffn0", resid=y0, gate_mod=mods[0],
                       j_gate=J_G1)
    u0, act0, f0 = _ffn_fwd(h2, w, 0, "0")

    x2, h3 = _norm_mod(x1, w["norm_mix"][1:2], mods[1], J_SC1, J_SH1, "norm_mix1", resid=f0, gate_mod=mods[0],
                       j_gate=J_G2)
    proj = _matmul(h3, w["w_in_b"], "nn", F32, "in_b")
    o_gla, states = _gla_fwd(proj, w["w_gate_b"], w["b_gate_b"], "gla_fwd")
    on = _gla_out(o_gla, proj, w["gnorm_b"], "gla_out")
    y1 = _matmul(on, w["w_out_b"], "nn", F32, "out_b")
    x3, h4 = _norm_mod(x2, w["norm_ffn"][1:2], mods[1], J_SC2, J_SH2, "norm_ffn1", resid=y1, gate_mod=mods[1],
                       j_gate=J_G1)
    u1, act1, f1 = _ffn_fwd(h4, w, 1, "1")

    dx4, loss, d_norm_final = _final_loss(x3, f1, mods[1], J_G2, w["norm_final"], tgt, "final_loss")

    dx3, (dsh2_1, dsc2_1, dg2_1), g_ffn1 = _ffn_bwd(dx4, f1, u1, act1, h4, x3, mods[1], w, 1, "1", None)
    dy1, dg1_1 = _gate_bwd(dx3, y1, mods[1], J_G1, "gate1_bwd1")
    d_on = _matmul(dy1, w["w_out_b"], "nt", F32, "out_b_dx")
    d_w_out_b = _matmul(on, dy1, "tn", F32, "out_b_dw")
    d_ogla, d_r, d_gnorm = _gla_out_bwd(o_gla, proj, w["gnorm_b"], d_on, "gla_out_bwd")
    dq, dk, dv, dz, d_b_gate = _gla_bwd(proj, w["w_gate_b"], w["b_gate_b"], states, d_ogla, "gla_bwd")
    d_glr = _matmul(dz, w["w_gate_b"], "nt", BF16, "gate_dx")
    d_w_gate = _matmul(proj[:, 3072:3200], dz, "tn", F32, "gate_dw")
    dproj = jnp.concatenate([dq, dk, dv, d_r, d_glr], axis=1)
    dh3 = _matmul(dproj, w["w_in_b"], "nt", F32, "in_b_dx")
    d_w_in_b = _matmul(h3, dproj, "tn", F32, "in_b_dw")
    dx2, dsc1_1, dsh1_1, d_nmix1 = _norm_mod_bwd(dh3, x2, dx3, w["norm_mix"][1:2], mods[1], J_SC1, "norm_mix_bwd1")
    dmod1 = jnp.concatenate([dsh1_1, dsc1_1, dg1_1, dsh2_1, dsc2_1, dg2_1], axis=1)

    dx1, (dsh2_0, dsc2_0, dg2_0), g_ffn0 = _ffn_bwd(dx2, f0, u0, act0, h2, x1, mods[0], w, 0, "0", g_ffn1)
    dy0, dg1_0 = _gate_bwd(dx1, y0, mods[0], J_G1, "gate1_bwd0")
    d_omix = _matmul(dy0, w["w_out_a"], "nt", F32, "out_a_dx")
    d_w_out_a = _matmul(omix, dy0, "tn", F32, "out_a_dw")
    parts, dbs = [], []
    for g in range(3):
        dq_g, dk_g, dv_g, db_g = _attn_bwd(qkv[g], bias, d_omix, omix, ltot, g, f"attn_bwd{g}")
        parts += [dq_g, dk_g, dv_g]
        dbs.append(db_g)
    dqkv = jnp.concatenate(parts, axis=1)
    d_rel_bias = _bias_bwd(jnp.concatenate(dbs, axis=0), buckets, "bias_bwd")
    half_chip = 9 * D_MODEL // 8
    dh1 = _matmul(dqkv, w["w_in_a"], "nt", F32, "in_a_dx", shape=(S, D_MODEL, 9 * D_MODEL),
                  tiles=(1024, D_MODEL, half_chip),
                  b_spec=pl.BlockSpec((None, D_MODEL, half_chip), lambda m, j, k: (k // 2, j, k % 2)))
    d_w_in_a = _matmul(h1, dqkv, "tn", F32, "in_a_dw", shape=(D_MODEL, 9 * D_MODEL, S),
                       tiles=(D_MODEL, half_chip, 512),
                       o_spec=pl.BlockSpec((None, D_MODEL, half_chip), lambda m, j, k: (j // 2, m, j % 2)),
                       o_shape=(4, D_MODEL, 2 * half_chip))
    dx0, dsc1_0, dsh1_0, d_nmix0 = _norm_mod_bwd(dh1, x, dx1, w["norm_mix"][0:1], mods[0], J_SC1, "norm_mix_bwd0")
    dmod0 = jnp.concatenate([dsh1_0, dsc1_0, dg1_0, dsh2_0, dsc2_0, dg2_0], axis=1)

    grads = dict(
        w_in_a=d_w_in_a, w_out_a=d_w_out_a, rel_bias=d_rel_bias, w_in_b=d_w_in_b, w_gate_b=d_w_gate,
        b_gate_b=d_b_gate, gnorm_b=d_gnorm, w_out_b=d_w_out_b,
        norm_mix=jnp.concatenate([d_nmix0, d_nmix1], axis=0),
        norm_ffn=jnp.concatenate([g_ffn0["norm_ffn"], g_ffn1["norm_ffn"]], axis=0),
        w_ada=_ada_outer(c, jnp.stack([dmod0, dmod1]), "ada_outer"),
        b_ada=jnp.concatenate([dmod0, dmod1], axis=0),
        w_up=g_ffn0["w_up"],
        conv_w=jnp.stack([g_ffn0["conv_w"], g_ffn1["conv_w"]]),
        conv_b=jnp.concatenate([g_ffn0["conv_b"], g_ffn1["conv_b"]], axis=0),
        w_down=g_ffn0["w_down"],
        norm_final=d_norm_final)
    return loss, dx0, grads


N_CHIPS = 4
N_DEV = 8
WEIGHTS = ("w_in_a", "w_out_a", "rel_bias", "w_in_b", "w_gate_b", "b_gate_b", "gnorm_b", "w_out_b", "norm_mix",
           "norm_ffn", "w_ada", "b_ada", "w_up", "conv_w", "conv_b", "w_down", "norm_final")
SHARD_AXIS = dict(w_in_a=2, w_out_a=1, w_in_b=2, w_gate_b=2, b_gate_b=1, gnorm_b=1, w_out_b=1, w_ada=2, w_up=2,
                  conv_w=2, w_down=1)
SHARDED = tuple(n for n in WEIGHTS if n in SHARD_AXIS)
REPLICATED = tuple(n for n in WEIGHTS if n not in SHARD_AXIS)
BIG = ("w_in_a", "w_out_a", "w_in_b", "w_out_b", "w_ada", "w_up", "w_down")
SMALL = ("w_gate_b", "b_gate_b", "gnorm_b", "conv_w")
SMALL_FULL = dict(w_gate_b=(1, 16, 512), b_gate_b=(1, 512), gnorm_b=(1, 256), conv_w=(2, 3, 5632))
R_SMALL = 16
R_TINY = 72
LOSS_SLOT = 72960
R_TINY_SHARD = 40
W_IN_B_PAD = 896

COMM_VIEW = dict(
    w_in_a=((4096, 2304), 1024, 512, 512),
    w_out_a=((1024, 1024), 256, 128, 128),
    w_in_b=((4096, W_IN_B_PAD), 1024, 512, 512),
    w_out_b=((1024, 1024), 256, 128, 128),
    w_ada=((8192, 1536), 2048, 1024, 1024),
    w_up=((8192, 1408), 2048, 1024, 1024),
    w_down=((5632, 1024), 704, 2816, 704))


def _pack(arrs, rows):
    flat = jnp.concatenate([a.reshape(-1) for a in arrs])
    return jnp.pad(flat, (0, rows * LANES - flat.shape[0])).reshape(rows, LANES)


def _unpack(flat2d, shapes):
    flat = flat2d.reshape(-1)
    out, off = [], 0
    for shp in shapes:
        n = math.prod(shp)
        out.append(flat[off:off + n].reshape(shp))
        off += n
    return out


def _chip_slice(a, axis, k):
    n = a.shape[axis] // N_CHIPS
    return lax.slice_in_dim(a, k * n, (k + 1) * n, axis=axis)


def _place():
    mx, my, mc = lax.axis_index("x"), lax.axis_index("y"), lax.axis_index("c")
    chips = [(1 - mx, my), (mx, 1 - my), (1 - mx, 1 - my)]
    return mx, my, mc, chips


def _rcopy(src, dst, send_sem, recv_sem, dev):
    return pltpu.make_async_remote_copy(src_ref=src, dst_ref=dst, send_sem=send_sem, recv_sem=recv_sem,
                                        device_id=dev, device_id_type=MESH)


def _comm_call(body, name, ins, out_shapes, n_sems, n_local):
    n_in, n_out = len(ins), len(out_shapes)

    def wrapped(*refs):
        body(refs[:n_in], refs[n_in:n_in + n_out], *refs[n_in + n_out:])

    return pl.pallas_call(
        wrapped, name=name, in_specs=[HBM] * n_in, out_specs=[HBM] * n_out, out_shape=out_shapes,
        scratch_shapes=[pltpu.SemaphoreType.DMA((n_sems,)), pltpu.SemaphoreType.DMA((n_sems,)),
                        pltpu.SemaphoreType.DMA((n_local,))])(*ins)


DMA_CHUNK_BYTES = 2 * 1024 * 1024


def _rows(ref, start, size):
    return ref.at[pl.ds(pl.multiple_of(start, 16), size), :]


def _block(ref, name, k, h):
    _, bk, bh, nr = COMM_VIEW[name]
    return _rows(ref, bk * k + bh * h, nr)


def _chunks(nr, row_bytes):
    n = 1
    while nr % (2 * n) == 0 and (nr // (2 * n)) % 16 == 0 and (nr // n) * row_bytes > DMA_CHUNK_BYTES:
        n *= 2
    return [(i * (nr // n), nr // n) for i in range(n)]


def _gather_big(shards, name):
    names = BIG

    def body(x_refs, out_refs, send_sems, recv_sems, local_sems):
        mx, my, mc, chips = _place()
        chip = 2 * mx + my
        sibling = (mx, my, 1 - mc)
        local, sends = [], []
        for a, n in enumerate(names):
            nr = COMM_VIEW[n][3]
            for h in range(2):
                cp = pltpu.make_async_copy(_rows(x_refs[a], h * nr, nr), _block(out_refs[a], n, chip, h),
                                           local_sems.at[2 * a + h])
                cp.start()
                local.append(cp)
            for j, (cx, cy) in enumerate(chips):
                cp = _rcopy(_rows(x_refs[a], mc * nr, nr), _block(out_refs[a], n, chip, mc),
                            send_sems.at[6 * a + j], recv_sems.at[6 * a + j], (cx, cy, mc))
                cp.start()
                sends.append(cp)
        for a, n in enumerate(names):
            for j, (cx, cy) in enumerate(chips):
                blk = _block(out_refs[a], n, 2 * cx + cy, mc)
                _rcopy(blk, blk, send_sems.at[6 * a + j], recv_sems.at[6 * a + j], sibling).wait_recv()
                cp = _rcopy(blk, blk, send_sems.at[6 * a + 3 + j], recv_sems.at[6 * a + 3 + j], sibling)
                cp.start()
                sends.append(cp)
        for a, n in enumerate(names):
            for j, (cx, cy) in enumerate(chips):
                blk = _block(out_refs[a], n, 2 * cx + cy, 1 - mc)
                _rcopy(blk, blk, send_sems.at[6 * a + 3 + j], recv_sems.at[6 * a + 3 + j], sibling).wait_recv()
        for cp in sends:
            cp.wait_send()
        for cp in local:
            cp.wait()

    outs = _comm_call(body, name, [shards[n] for n in names],
                      [jax.ShapeDtypeStruct(COMM_VIEW[n][0], shards[n].dtype) for n in names],
                      6 * len(names), 2 * len(names))
    return dict(zip(names, outs))


def _rs_pair_exchange(views, name):
    names = BIG

    def body(g_refs, recv_refs, send_sems, recv_sems, local_sems):
        mx, my, mc, _ = _place()
        sibling = (mx, my, 1 - mc)
        for a, n in enumerate(names):
            (_, cols), _, _, nr = COMM_VIEW[n]
            for k in range(N_CHIPS):
                src = _block(g_refs[a], n, k, 1 - mc)
                for start, size in _chunks(nr, cols * 4):
                    _rcopy(src.at[pl.ds(start, size), :], recv_refs[a].at[k, pl.ds(start, size), :],
                           send_sems.at[a], recv_sems.at[a], sibling).start()
        for a in range(len(names)):
            _rcopy(recv_refs[a], recv_refs[a], send_sems.at[a], recv_sems.at[a], sibling).wait()

    outs = _comm_call(body, name, [views[n] for n in names],
                      [jax.ShapeDtypeStruct((N_CHIPS, COMM_VIEW[n][3], COMM_VIEW[n][0][1]), F32) for n in names],
                      len(names), 1)
    return dict(zip(names, outs))


def _pair_add(view, recv, c_idx, n, out_dtype, name):
    (_, cols), bk, bh, nr = COMM_VIEW[n]
    tr = _pick(math.gcd(bk, bh, nr), 256, 8)

    def body(c_ref, g_ref, r_ref, o_ref):
        o_ref[...] = (g_ref[...] + r_ref[...]).astype(o_ref.dtype)

    piece = pl.BlockSpec((None, tr, cols), lambda k, i, c_ref: (k, i, 0))
    return pl.pallas_call(
        body, name=name,
        grid_spec=pltpu.PrefetchScalarGridSpec(
            num_scalar_prefetch=1, grid=(N_CHIPS, nr // tr),
            in_specs=[pl.BlockSpec((tr, cols), lambda k, i, c_ref: ((bk * k + bh * c_ref[0]) // tr + i, 0)), piece],
            out_specs=piece),
        out_shape=jax.ShapeDtypeStruct((N_CHIPS, nr, cols), out_dtype), compiler_params=_params(2))(
            c_idx, view, recv)


def _rs_chip_exchange(q, name):
    names = BIG

    def body(q_refs, out_refs, send_sems, recv_sems, local_sems):
        mx, my, mc, chips = _place()
        chip = 2 * mx + my
        local, sends = [], []
        for a in range(len(names)):
            cp = pltpu.make_async_copy(q_refs[a].at[chip], out_refs[a].at[chip], local_sems.at[a])
            cp.start()
            local.append(cp)
            for j, (cx, cy) in enumerate(chips):
                cp = _rcopy(q_refs[a].at[2 * cx + cy], out_refs[a].at[chip], send_sems.at[3 * a + j],
                            recv_sems.at[3 * a + j], (cx, cy, mc))
                cp.start()
                sends.append(cp)
        for a in range(len(names)):
            for j, (cx, cy) in enumerate(chips):
                blk = out_refs[a].at[2 * cx + cy]
                _rcopy(blk, blk, send_sems.at[3 * a + j], recv_sems.at[3 * a + j], (cx, cy, mc)).wait_recv()
        for cp in sends:
            cp.wait_send()
        for cp in local:
            cp.wait()

    outs = _comm_call(body, name, [q[n] for n in names],
                      [jax.ShapeDtypeStruct(q[n].shape, q[n].dtype) for n in names], 3 * len(names), len(names))
    return dict(zip(names, outs))


def _rs_pair_gather(r, name):
    names = BIG

    def body(r_refs, out_refs, send_sems, recv_sems, local_sems):
        mx, my, mc, _ = _place()
        sibling = (mx, my, 1 - mc)
        local = []
        for a, n in enumerate(names):
            (_, cols), _, _, nr = COMM_VIEW[n]
            cp = pltpu.make_async_copy(r_refs[a], _rows(out_refs[a], mc * nr, nr), local_sems.at[a])
            cp.start()
            local.append(cp)
            for start, size in _chunks(nr, cols * 4):
                _rcopy(r_refs[a].at[pl.ds(start, size), :], _rows(out_refs[a], mc * nr + start, size),
                       send_sems.at[a], recv_sems.at[a], sibling).start()
        for a, n in enumerate(names):
            nr = COMM_VIEW[n][3]
            _rcopy(r_refs[a], _rows(out_refs[a], (1 - mc) * nr, nr), send_sems.at[a], recv_sems.at[a], sibling).wait()
        for cp in local:
            cp.wait()

    outs = _comm_call(body, name, [r[n] for n in names],
                      [jax.ShapeDtypeStruct((2 * COMM_VIEW[n][3], COMM_VIEW[n][0][1]), F32) for n in names],
                      len(names), len(names))
    return dict(zip(names, outs))


def _gather8(x, reduce, name):
    rows = x.shape[0]

    def body(x_ref, out_ref, *rest):
        if reduce:
            buf_ref, send_sems, recv_sems = rest
        else:
            (send_sems, recv_sems), buf_ref = rest, out_ref
        mx, my, mc, _ = _place()
        me = 4 * mx + 2 * my + mc
        buf_ref[me] = x_ref[...]
        peers = []
        for j in range(1, N_DEV):
            px = 1 - mx if j & 4 else mx
            py = 1 - my if j & 2 else my
            pc = 1 - mc if j & 1 else mc
            peers.append((px, py, pc))
        sends = [_rcopy(x_ref, buf_ref.at[me], send_sems.at[j], recv_sems.at[j], p) for j, p in enumerate(peers)]
        for cp in sends:
            cp.start()
        for j, (px, py, pc) in enumerate(peers):
            _rcopy(x_ref, buf_ref.at[4 * px + 2 * py + pc], send_sems.at[j], recv_sems.at[j], (px, py, pc)).wait_recv()
        for cp in sends:
            cp.wait_send()
        if reduce:
            acc = buf_ref[0]
            for d in range(1, N_DEV):
                acc = acc + buf_ref[d]
            out_ref[...] = acc

    vmem = pl.BlockSpec(memory_space=pltpu.VMEM)
    sems = [pltpu.SemaphoreType.DMA((N_DEV - 1,)), pltpu.SemaphoreType.DMA((N_DEV - 1,))]
    if reduce:
        out_shape = jax.ShapeDtypeStruct((rows, LANES), F32)
        scratch = [pltpu.VMEM((N_DEV, rows, LANES), F32)] + sems
    else:
        out_shape = jax.ShapeDtypeStruct((N_DEV, rows, LANES), F32)
        scratch = sems
    return pl.pallas_call(body, name=name, in_specs=[vmem], out_specs=vmem, out_shape=out_shape,
                          scratch_shapes=scratch)(x)


def _sum4(p, name):
    _, nr, cols = p.shape
    tr = _pick(nr, 256, 8)

    def body(p0, p1, p2, p3, o_ref):
        o_ref[...] = ((p0[...].astype(F32) + p1[...].astype(F32)) + p2[...].astype(F32)) + p3[...].astype(F32)

    return pl.pallas_call(
        body, name=name, grid=(nr // tr,),
        in_specs=[pl.BlockSpec((None, tr, cols), lambda i, k=k: (k, i, 0)) for k in range(N_CHIPS)],
        out_specs=_row(tr, cols), out_shape=jax.ShapeDtypeStruct((nr, cols), F32),
        compiler_params=_params(1))(p, p, p, p)


def _adamw(w, g, m, v, name):
    rows, cols = w.shape
    tr = _pick(rows, max(8, (1 << 20) // (4 * cols)), 8)

    def body(w_ref, g_ref, m_ref, v_ref, d_ref, mo_ref, vo_ref):
        gv = g_ref[...]
        mn = ADAM_B1 * m_ref[...] + (1.0 - ADAM_B1) * gv
        vn = ADAM_B2 * v_ref[...] + (1.0 - ADAM_B2) * (gv * gv)
        m_hat = mn / (1.0 - ADAM_B1 ** ADAM_STEP)
        v_hat = vn / (1.0 - ADAM_B2 ** ADAM_STEP)
        d_ref[...] = -ADAM_LR * (m_hat / (jnp.sqrt(v_hat) + ADAM_EPS) + ADAM_WD * w_ref[...])
        mo_ref[...] = mn
        vo_ref[...] = vn

    shape = jax.ShapeDtypeStruct(w.shape, F32)
    return pl.pallas_call(
        body, name=name, grid=(rows // tr,), in_specs=[_row(tr, cols)] * 4, out_specs=[_row(tr, cols)] * 3,
        out_shape=[shape] * 3, compiler_params=_params(1))(w, g, m, v)


W_IN_B_SHARD = 772


def _shard_view(n, a):
    return a.reshape(-1, a.shape[-1])


def _gather_weights(p):
    shards = {n: _shard_view(n, p[n]).astype(BF16) for n in BIG}
    shards["w_in_b"] = jnp.pad(shards["w_in_b"], ((0, 0), (0, W_IN_B_PAD - W_IN_B_SHARD)))
    big = _gather_big(shards, "gather_weights")
    small = _gather8(_pack([p[n] for n in SMALL], R_SMALL), False, "gather_small")
    pieces = [_unpack(small[2 * k], [p[n].shape for n in SMALL]) for k in range(N_CHIPS)]
    full = {n: jnp.concatenate([pieces[k][i] for k in range(N_CHIPS)], axis=SHARD_AXIS[n])
            for i, n in enumerate(SMALL)}
    wb = big["w_in_b"].reshape(N_CHIPS, D_MODEL, W_IN_B_PAD)
    wb = jnp.concatenate([wb[k, :, :W_IN_B_SHARD] for k in range(N_CHIPS)], axis=1)
    return dict(
        w_in_a=big["w_in_a"].reshape(N_CHIPS, D_MODEL, -1), w_out_a=big["w_out_a"], w_out_b=big["w_out_b"],
        w_in_b=jnp.concatenate([wb[:, :2048], wb[:, 2064:3088], wb[:, 2048:2064],
                                jnp.zeros((D_MODEL, B_IN_PAD - 3088), BF16)], axis=1),
        w_ada=big["w_ada"].reshape(N_CHIPS, 2 * D_MODEL, -1), w_up=big["w_up"].reshape(N_CHIPS, 2 * D_MODEL, -1),
        w_down=big["w_down"],
        w_gate_b=jnp.pad(full["w_gate_b"][0], ((0, 128 - B_GATE_RANK), (0, 0))),
        b_gate_b=full["b_gate_b"], gnorm_b=full["gnorm_b"], conv_w=full["conv_w"],
        rel_bias=p["rel_bias"], norm_mix=p["norm_mix"], norm_ffn=p["norm_ffn"], b_ada=p["b_ada"],
        conv_b=p["conv_b"], norm_final=p["norm_final"].reshape(1, D_MODEL))


def _grad_views(g):
    gb = g["w_in_b"]
    gb = jnp.concatenate([gb[:, :2048], gb[:, 3072:3088], gb[:, 2048:3072]], axis=1)
    gb = jnp.pad(gb.reshape(D_MODEL, N_CHIPS, W_IN_B_SHARD).transpose(1, 0, 2),
                 ((0, 0), (0, 0), (0, W_IN_B_PAD - W_IN_B_SHARD)))
    views = {n: g[n].reshape(COMM_VIEW[n][0]) for n in BIG if n != "w_in_b"}
    views["w_in_b"] = gb.reshape(COMM_VIEW["w_in_b"][0])
    return views


def _tiny_grads(g):
    out = {n: g[n] for n in REPLICATED if n != "norm_final"}
    out.update(norm_final=g["norm_final"].reshape(D_MODEL), w_gate_b=g["w_gate_b"][:B_GATE_RANK][None],
               b_gate_b=g["b_gate_b"], gnorm_b=g["gnorm_b"], conv_w=g["conv_w"])
    return out


def kernel(x, c, w_in_a, w_out_a, rel_bias, w_in_b, w_gate_b, b_gate_b, gnorm_b, w_out_b, norm_mix, norm_ffn, w_ada, b_ada, w_up, conv_w, conv_b, w_down, norm_final, loss_target, m_w_in_a, m_w_out_a, m_rel_bias, m_w_in_b, m_w_gate_b, m_b_gate_b, m_gnorm_b, m_w_out_b, m_norm_mix, m_norm_ffn, m_w_ada, m_b_ada, m_w_up, m_conv_w, m_conv_b, m_w_down, m_norm_final, v_w_in_a, v_w_out_a, v_rel_bias, v_w_in_b, v_w_gate_b, v_b_gate_b, v_gnorm_b, v_w_out_b, v_norm_mix, v_norm_ffn, v_w_ada, v_b_ada, v_w_up, v_conv_w, v_conv_b, v_w_down, v_norm_final):
    p = dict(zip(WEIGHTS, (w_in_a, w_out_a, rel_bias, w_in_b, w_gate_b, b_gate_b, gnorm_b, w_out_b, norm_mix,
                           norm_ffn, w_ada, b_ada, w_up, conv_w, conv_b, w_down, norm_final)))
    pm = dict(zip(WEIGHTS, (m_w_in_a, m_w_out_a, m_rel_bias, m_w_in_b, m_w_gate_b, m_b_gate_b, m_gnorm_b, m_w_out_b,
                            m_norm_mix, m_norm_ffn, m_w_ada, m_b_ada, m_w_up, m_conv_w, m_conv_b, m_w_down,
                            m_norm_final)))
    pv = dict(zip(WEIGHTS, (v_w_in_a, v_w_out_a, v_rel_bias, v_w_in_b, v_w_gate_b, v_b_gate_b, v_gnorm_b, v_w_out_b,
                            v_norm_mix, v_norm_ffn, v_w_ada, v_b_ada, v_w_up, v_conv_w, v_conv_b, v_w_down,
                            v_norm_final)))
    S = x.shape[1]

    chip = 2 * lax.axis_index("x") + lax.axis_index("y")
    core = lax.axis_index("c").astype(jnp.int32).reshape(1)

    w = _gather_weights(p)
    loss, dx0, grads = _local_step(x.reshape(S, D_MODEL), c, loss_target.reshape(S, D_MODEL), w)

    views = _grad_views(grads)
    recv = _rs_pair_exchange(views, "grads_pair_exchange")
    pair = {n: _pair_add(views[n], recv[n], core, n, BF16, f"grads_pair_add_{n}") for n in BIG}
    from_chips = _rs_chip_exchange(pair, "grads_chip_exchange")
    g_big = _rs_pair_gather({n: _sum4(from_chips[n], f"grads_chip_sum_{n}") for n in BIG}, "grads_pair_gather")
    g_big["w_in_b"] = g_big["w_in_b"][:, :W_IN_B_SHARD]

    tiny = _tiny_grads(grads)
    tiny_names = SMALL + REPLICATED
    tiny_full = {n: SMALL_FULL[n] if n in SMALL_FULL else p[n].shape for n in tiny_names}
    tiny_sum = _gather8(_pack([tiny[n] for n in tiny_names] + [loss[0, 0:1]], R_TINY), True, "grads_tiny_sum")
    g_tiny = dict(zip(tiny_names, _unpack(tiny_sum, [tiny_full[n] for n in tiny_names])))
    for n in SMALL:
        width = p[n].shape[SHARD_AXIS[n]]
        g_tiny[n] = lax.dynamic_slice_in_dim(g_tiny[n], chip * width, width, axis=SHARD_AXIS[n])
    total_loss = tiny_sum.reshape(-1)[LOSS_SLOT]

    out = {}
    for n in BIG:
        res = _adamw(_shard_view(n, p[n]), g_big[n], _shard_view(n, pm[n]), _shard_view(n, pv[n]), f"adamw_{n}")
        out[n] = [t.reshape(p[n].shape) for t in (g_big[n],) + tuple(res)]
    res = _adamw(*[_pack([d[n] for n in tiny_names], R_TINY_SHARD) for d in (p, g_tiny, pm, pv)], "adamw_tiny")
    unpacked = [_unpack(t, [p[n].shape for n in tiny_names]) for t in res]
    for i, n in enumerate(tiny_names):
        out[n] = [g_tiny[n]] + [u[i] for u in unpacked]

    return (total_loss, dx0.reshape(x.shape), *[out[n][0] for n in WEIGHTS], *[out[n][1] for n in WEIGHTS],
            *[out[n][2] for n in WEIGHTS], *[out[n][3] for n in WEIGHTS])
```

```python
import functools
import math

import numpy as np
import jax
import jax.numpy as jnp
from jax import lax
from jax.experimental import pallas as pl
from jax.experimental.pallas import tpu as pltpu

F32 = jnp.float32
BF16 = jnp.bfloat16
MESH = pl.DeviceIdType.MESH

D_MODEL = 1024
A_CONFIGS = ((128, 1), (512, 4), (2048, 16))
A_HEADS = 16
A_HEAD_DIM = 64
A_BLK = 128
N_BUCKETS = 32
MAX_DISTANCE = 2048
B_HEADS = 4
B_DK = 128
B_DV = 256
B_QK = 512
B_V = 1024
B_GATE_RANK = 16
B_TAU = 16.0
B_CHUNK = 64
B_IN_PAD = 3200
D_FF = 2816
EPS = 1e-6
NEG_INF = -1e30
ADAM_LR = 0.001
ADAM_B1 = 0.9
ADAM_B2 = 0.999
ADAM_EPS = 1e-08
ADAM_WD = 0.01
ADAM_STEP = 10

LANES = 1024
VMEM_LIMIT = 48 * 1024 * 1024
ROW_TILE = 256
GLA_ROWS = 512

HBM = pl.BlockSpec(memory_space=pl.ANY)


def _params(n_axes):
    return pltpu.CompilerParams(dimension_semantics=("arbitrary",) * n_axes, vmem_limit_bytes=VMEM_LIMIT)


def _pick(n, cap, mult=128):
    best = None
    for t in range(mult, min(n, cap) + 1, mult):
        if n % t == 0:
            best = t
    return n if best is None else best


def _matmul(a, b, mode, out_dtype, name, shape=None, tiles=None, b_spec=None, o_spec=None, o_shape=None, prev=None):
    dims = {"nn": (((1,), (0,)), ((), ())), "nt": NT, "tn": TN}[mode]
    if shape is None:
        if mode == "nn":
            (M, K), (_, N) = a.shape, b.shape
        elif mode == "nt":
            (M, K), (N, _) = a.shape, b.shape
        else:
            (K, M), (_, N) = a.shape, b.shape
    else:
        M, N, K = shape
    if tiles is None:
        tiles = (_pick(M, 1024, 128 if mode == "tn" else 8), _pick(N, 1536), _pick(K, 1024 if mode != "tn" else 2048))
    tm, tn, tk = tiles
    nk = K // tk
    a_spec = pl.BlockSpec((tk, tm), lambda i, j, k: (k, i)) if mode == "tn" else pl.BlockSpec(
        (tm, tk), lambda i, j, k: (i, k))
    if b_spec is None:
        b_spec = pl.BlockSpec((tn, tk), lambda i, j, k: (j, k)) if mode == "nt" else pl.BlockSpec(
            (tk, tn), lambda i, j, k: (k, j))
    if o_spec is None:
        o_spec = pl.BlockSpec((tm, tn), lambda i, j, k: (i, j))
        o_shape = (M, N)

    def body(a_ref, b_ref, *rest):
        o_ref, acc_ref = rest[-2:]
        k = pl.program_id(2)
        part = lax.dot_general(a_ref[...].astype(BF16), b_ref[...].astype(BF16), dims, preferred_element_type=F32)

        @pl.when(k == 0)
        def _():
            acc_ref[...] = part

        @pl.when(k > 0)
        def _():
            acc_ref[...] += part

        @pl.when(k == nk - 1)
        def _():
            o_ref[...] = acc_ref[...].astype(o_ref.dtype)

    ins, in_specs, aliases = [a, b], [a_spec, b_spec], {}
    if prev is not None:
        ins.append(prev)
        in_specs.append(HBM)
        aliases = {2: 0}
    return pl.pallas_call(
        body, name=name, grid=(M // tm, N // tn, nk), in_specs=in_specs, out_specs=o_spec,
        out_shape=jax.ShapeDtypeStruct(o_shape, out_dtype), scratch_shapes=[pltpu.VMEM((tm, tn), F32)],
        input_output_aliases=aliases, compiler_params=_params(3))(*ins)


def _row(tr, d=D_MODEL):
    return pl.BlockSpec((tr, d), lambda i: (i, 0))


def _vec(d=D_MODEL):
    return pl.BlockSpec((1, d), lambda i: (0, 0))


def _modspec(j):
    return pl.BlockSpec((8, D_MODEL), lambda i: (0, j))


def _silu(x):
    return x * jax.nn.sigmoid(x)


def _dsilu(x):
    s = jax.nn.sigmoid(x)
    return s * (1.0 + x * (1.0 - s))


ADA_TN = 6 * D_MODEL // 4


def _ada_mod(c, w_ada, b_ada, layer, name):
    def body(c_ref, w_ref, b_ref, o_ref):
        sc = jnp.broadcast_to(_silu(c_ref[...]), (8, D_MODEL)).astype(BF16)
        o_ref[...] = jnp.dot(sc, w_ref[...], preferred_element_type=F32) + b_ref[...]

    return pl.pallas_call(
        body, name=name, grid=(4,),
        in_specs=[_vec(), pl.BlockSpec((None, D_MODEL, ADA_TN), lambda j: (j, layer, 0)),
                  pl.BlockSpec((1, ADA_TN), lambda j: (0, j))],
        out_specs=pl.BlockSpec((8, ADA_TN), lambda j: (0, j)), out_shape=jax.ShapeDtypeStruct((8, 6 * D_MODEL), F32),
        compiler_params=_params(1))(c, w_ada, b_ada)


def _ada_outer(c, dmods, name):
    def body(c_ref, d_ref, o_ref):
        row = lax.broadcasted_iota(jnp.int32, (8, 1), 0) == 0
        a = jnp.where(row, jnp.broadcast_to(_silu(c_ref[...]), (8, D_MODEL)), 0.0).astype(BF16)
        b = jnp.where(row, jnp.broadcast_to(d_ref[...], (8, ADA_TN)), 0.0).astype(BF16)
        o_ref[...] = lax.dot_general(a, b, (((0,), (0,)), ((), ())), preferred_element_type=F32)

    return pl.pallas_call(
        body, name=name, grid=(2, 4),
        in_specs=[pl.BlockSpec((1, D_MODEL), lambda l, j: (0, 0)),
                  pl.BlockSpec((None, 1, ADA_TN), lambda l, j: (l, 0, j))],
        out_specs=pl.BlockSpec((None, D_MODEL, ADA_TN), lambda l, j: (j, l, 0)),
        out_shape=jax.ShapeDtypeStruct((4, 2 * D_MODEL, ADA_TN), F32), compiler_params=_params(2))(c, dmods)


def _norm_mod(x, gamma, mod, j_sc, j_sh, name, resid=None, gate_mod=None, j_gate=None):
    S = x.shape[0]
    tr = ROW_TILE
    has_res = resid is not None

    def body(*refs):
        if has_res:
            x_ref, y_ref, gate_ref, g_ref, sc_ref, sh_ref, xo_ref, h_ref = refs
            xn = x_ref[...] + gate_ref[0:1, :] * y_ref[...]
            xo_ref[...] = xn
        else:
            x_ref, g_ref, sc_ref, sh_ref, h_ref = refs
            xn = x_ref[...]
        r = lax.rsqrt(jnp.mean(xn * xn, axis=-1, keepdims=True) + EPS)
        n = (xn * r) * g_ref[...]
        h_ref[...] = (n * (1.0 + sc_ref[0:1, :]) + sh_ref[0:1, :]).astype(BF16)

    if has_res:
        ins = [x, resid, gate_mod, gamma, mod, mod]
        in_specs = [_row(tr), _row(tr), _modspec(j_gate), _vec(), _modspec(j_sc), _modspec(j_sh)]
        out_specs = [_row(tr), _row(tr)]
        out_shape = [jax.ShapeDtypeStruct((S, D_MODEL), F32), jax.ShapeDtypeStruct((S, D_MODEL), BF16)]
    else:
        ins = [x, gamma, mod, mod]
        in_specs = [_row(tr), _vec(), _modspec(j_sc), _modspec(j_sh)]
        out_specs = _row(tr)
        out_shape = jax.ShapeDtypeStruct((S, D_MODEL), BF16)
    return pl.pallas_call(body, name=name, grid=(S // tr,), in_specs=in_specs, out_specs=out_specs,
                          out_shape=out_shape, compiler_params=_params(1))(*ins)


def _final_loss(x, resid, mod, j_gate, gamma, tgt, name):
    S = x.shape[0]
    tr = ROW_TILE

    def body(x_ref, y_ref, gate_ref, g_ref, t_ref, dx_ref, loss_ref, dg_ref):
        @pl.when(pl.program_id(0) == 0)
        def _():
            loss_ref[...] = jnp.zeros_like(loss_ref)
            dg_ref[...] = jnp.zeros_like(dg_ref)

        xn = x_ref[...] + gate_ref[0:1, :] * y_ref[...]
        r = lax.rsqrt(jnp.mean(xn * xn, axis=-1, keepdims=True) + EPS)
        xhat = xn * r
        err = xhat * g_ref[...] - t_ref[...]
        loss_ref[...] += 0.5 * jnp.sum(jnp.mean(err * err, axis=-1, keepdims=True))
        dy = err * (1.0 / D_MODEL)
        dg_ref[...] += jnp.sum(dy * xhat, axis=0, keepdims=True)
        dxh = dy * g_ref[...]
        dx_ref[...] = r * (dxh - xhat * jnp.mean(dxh * xhat, axis=-1, keepdims=True))

    return pl.pallas_call(
        body, name=name, grid=(S // tr,),
        in_specs=[_row(tr), _row(tr), _modspec(j_gate), _vec(), _row(tr)],
        out_specs=[_row(tr), pl.BlockSpec((1, 128), lambda i: (0, 0)), _vec()],
        out_shape=[jax.ShapeDtypeStruct((S, D_MODEL), F32), jax.ShapeDtypeStruct((1, 128), F32),
                   jax.ShapeDtypeStruct((1, D_MODEL), F32)],
        compiler_params=_params(1))(x, resid, mod, gamma, tgt)


def _gate_bwd(dx, y, mod, j_gate, name):
    S = dx.shape[0]
    tr = ROW_TILE

    def body(dx_ref, y_ref, gate_ref, dy_ref, dg_ref):
        @pl.when(pl.program_id(0) == 0)
        def _():
            dg_ref[...] = jnp.zeros_like(dg_ref)

        dx_v = dx_ref[...]
        dy_ref[...] = (gate_ref[0:1, :] * dx_v).astype(BF16)
        dg_ref[...] += jnp.sum(dx_v * y_ref[...], axis=0, keepdims=True)

    return pl.pallas_call(
        body, name=name, grid=(S // tr,), in_specs=[_row(tr), _row(tr), _modspec(j_gate)],
        out_specs=[_row(tr), _vec()],
        out_shape=[jax.ShapeDtypeStruct((S, D_MODEL), BF16), jax.ShapeDtypeStruct((1, D_MODEL), F32)],
        compiler_params=_params(1))(dx, y, mod)


def _norm_mod_bwd(dh, x, dx_res, gamma, mod, j_sc, name):
    S = x.shape[0]
    tr = ROW_TILE

    def body(dh_ref, x_ref, dr_ref, g_ref, sc_ref, dx_ref, dsc_ref, dsh_ref, dg_ref):
        @pl.when(pl.program_id(0) == 0)
        def _():
            dsc_ref[...] = jnp.zeros_like(dsc_ref)
            dsh_ref[...] = jnp.zeros_like(dsh_ref)
            dg_ref[...] = jnp.zeros_like(dg_ref)

        xv = x_ref[...]
        dh_v = dh_ref[...]
        r = lax.rsqrt(jnp.mean(xv * xv, axis=-1, keepdims=True) + EPS)
        xhat = xv * r
        dsh_ref[...] += jnp.sum(dh_v, axis=0, keepdims=True)
        dsc_ref[...] += jnp.sum(dh_v * (xhat * g_ref[...]), axis=0, keepdims=True)
        dn = dh_v * (1.0 + sc_ref[0:1, :])
        dg_ref[...] += jnp.sum(dn * xhat, axis=0, keepdims=True)
        dxh = dn * g_ref[...]
        dx_ref[...] = dr_ref[...] + r * (dxh - xhat * jnp.mean(dxh * xhat, axis=-1, keepdims=True))

    vec = jax.ShapeDtypeStruct((1, D_MODEL), F32)
    return pl.pallas_call(
        body, name=name, grid=(S // tr,),
        in_specs=[_row(tr), _row(tr), _row(tr), _vec(), _modspec(j_sc)],
        out_specs=[_row(tr), _vec(), _vec(), _vec()],
        out_shape=[jax.ShapeDtypeStruct((S, D_MODEL), F32), vec, vec, vec],
        compiler_params=_params(1))(dh, x, dx_res, gamma, mod)


def _shift_down(u, halo, s):
    r = pltpu.roll(u, s, 0)
    hr = pltpu.roll(halo, s, 0)
    rid = lax.broadcasted_iota(jnp.int32, hr.shape, 0)
    top = jnp.where(rid < s, hr, r[0:8])
    return jnp.concatenate([top, r[8:]], axis=0)


def _shift_up(u, halo, s):
    n = u.shape[0]
    r = pltpu.roll(u, n - s, 0)
    hr = pltpu.roll(halo, 8 - s, 0)
    rid = lax.broadcasted_iota(jnp.int32, hr.shape, 0)
    bot = jnp.where(rid >= 8 - s, hr, r[n - 8:])
    return jnp.concatenate([r[:n - 8], bot], axis=0)


def _conv3(u, halo, w_ref, b_ref):
    u1 = _shift_down(u, halo, 1)
    u2 = _shift_down(u, halo, 2)
    return b_ref[...] + ((w_ref[0:1, :] * u2 + w_ref[1:2, :] * u1) + w_ref[2:3, :] * u), u1, u2


CONV_TC = 1408


def _conv_specs(tr, S):
    nh = D_FF // CONV_TC
    hb = tr // 8

    def cur(off):
        return pl.BlockSpec((tr, CONV_TC), lambda j, i: (i, j + off))

    def halo(off):
        return pl.BlockSpec((8, CONV_TC), lambda j, i: (jnp.maximum(i * hb - 1, 0), j + off))

    def w(off):
        return pl.BlockSpec((3, CONV_TC), lambda j, i: (0, j + off))

    def b(off):
        return pl.BlockSpec((1, CONV_TC), lambda j, i: (0, j + off))

    return nh, cur, halo, w, b


def _conv_gate(u, conv_w, conv_b, name):
    S = u.shape[0]
    tr = ROW_TILE
    nh, cur, halo, w, b = _conv_specs(tr, S)

    def body(ua_ref, ha_ref, ub_ref, hb_ref, wa_ref, wb_ref, ba_ref, bb_ref, o_ref):
        first = pl.program_id(1) == 0
        ha = jnp.where(first, 0.0, ha_ref[...])
        hbv = jnp.where(first, 0.0, hb_ref[...])
        a, _, _ = _conv3(ua_ref[...], ha, wa_ref, ba_ref)
        bb, _, _ = _conv3(ub_ref[...], hbv, wb_ref, bb_ref)
        o_ref[...] = (_silu(a) * bb).astype(BF16)

    return pl.pallas_call(
        body, name=name, grid=(nh, S // tr),
        in_specs=[cur(0), halo(0), cur(nh), halo(nh), w(0), w(nh), b(0), b(nh)],
        out_specs=pl.BlockSpec((tr, CONV_TC), lambda j, i: (i, j)),
        out_shape=jax.ShapeDtypeStruct((S, D_FF), BF16), compiler_params=_params(2))(
            u, u, u, u, conv_w, conv_w, conv_b, conv_b)


def _conv_gate_bwd(u, dact, conv_w, conv_b, name):
    S = u.shape[0]
    tr = ROW_TILE
    nh, cur, halo, w, b = _conv_specs(tr, S)

    def body(ua_ref, ha_ref, ub_ref, hb_ref, wa_ref, wb_ref, ba_ref, bb_ref, da_ref,
             dua_ref, dub_ref, dwa_ref, dwb_ref, dba_ref, dbb_ref):
        first = pl.program_id(1) == 0

        @pl.when(first)
        def _():
            for r in (dwa_ref, dwb_ref, dba_ref, dbb_ref):
                r[...] = jnp.zeros_like(r)

        ha = jnp.where(first, 0.0, ha_ref[...])
        hbv = jnp.where(first, 0.0, hb_ref[...])
        ua, ub = ua_ref[...], ub_ref[...]
        a, ua1, ua2 = _conv3(ua, ha, wa_ref, ba_ref)
        bb, ub1, ub2 = _conv3(ub, hbv, wb_ref, bb_ref)
        dact_v = da_ref[...]
        da = dact_v * bb * _dsilu(a)
        db = dact_v * _silu(a)
        dua_ref[...] = da
        dub_ref[...] = db
        for d, x0, x1, x2, dw_ref, dbias_ref in ((da, ua, ua1, ua2, dwa_ref, dba_ref),
                                                 (db, ub, ub1, ub2, dwb_ref, dbb_ref)):
            dbias_ref[...] += jnp.sum(d, axis=0, keepdims=True)
            dw_ref[0:1, :] += jnp.sum(d * x2, axis=0, keepdims=True)
            dw_ref[1:2, :] += jnp.sum(d * x1, axis=0, keepdims=True)
            dw_ref[2:3, :] += jnp.sum(d * x0, axis=0, keepdims=True)

    half = pl.BlockSpec((tr, CONV_TC), lambda j, i: (i, j))
    dw = pl.BlockSpec((8, CONV_TC), lambda j, i: (0, j))
    dbs = pl.BlockSpec((1, CONV_TC), lambda j, i: (0, j))
    f = lambda r, c: jax.ShapeDtypeStruct((r, c), F32)
    return pl.pallas_call(
        body, name=name, grid=(nh, S // tr),
        in_specs=[cur(0), halo(0), cur(nh), halo(nh), w(0), w(nh), b(0), b(nh), half],
        out_specs=[half, half, dw, dw, dbs, dbs],
        out_shape=[f(S, D_FF), f(S, D_FF), f(8, D_FF), f(8, D_FF), f(1, D_FF), f(1, D_FF)],
        compiler_params=_params(2))(u, u, u, u, conv_w, conv_w, conv_b, conv_b, dact)


def _conv_t(duc, conv_w, half, prev, name):
    S = duc.shape[0]
    tr = ROW_TILE
    nh = D_FF // CONV_TC
    hb = tr // 8
    nlast = S // 8 - 1
    nsteps = S // tr
    off = half * nh

    def body(*refs):
        if prev is None:
            d_ref, h_ref, w_ref, o_ref = refs
        else:
            d_ref, h_ref, w_ref, _, o_ref = refs
        last = pl.program_id(1) == nsteps - 1
        hv = jnp.where(last, 0.0, h_ref[...])
        d = d_ref[...]
        d1 = _shift_up(d, hv, 1)
        d2 = _shift_up(d, hv, 2)
        o_ref[...] = ((w_ref[2:3, :] * d + w_ref[1:2, :] * d1) + w_ref[0:1, :] * d2).astype(BF16)

    in_specs = [pl.BlockSpec((tr, CONV_TC), lambda j, i: (i, j)),
                pl.BlockSpec((8, CONV_TC), lambda j, i: (jnp.minimum((i + 1) * hb, nlast), j)),
                pl.BlockSpec((3, CONV_TC), lambda j, i: (0, j + off))]
    ins = [duc, duc, conv_w]
    aliases = {}
    if prev is not None:
        in_specs.append(HBM)
        ins.append(prev)
        aliases = {3: 0}
    return pl.pallas_call(
        body, name=name, grid=(nh, nsteps), in_specs=in_specs,
        out_specs=pl.BlockSpec((tr, CONV_TC), lambda j, i: (i, j + off)),
        out_shape=jax.ShapeDtypeStruct((S, 2 * D_FF), BF16), input_output_aliases=aliases,
        compiler_params=_params(2))(*ins)


def _bucket_maps():
    qi = np.arange(A_BLK)[:, None]
    ki = np.arange(2 * A_BLK)[None, :]
    steps = np.clip(qi + A_BLK - ki, 0, A_BLK)
    out = []
    max_exact = N_BUCKETS // 2
    for _, dil in A_CONFIGS:
        dist = steps * dil
        n = np.maximum(dist, max_exact).astype(np.float32)
        large = max_exact + (np.log(n / np.float32(max_exact)) / np.float32(math.log(MAX_DISTANCE / max_exact))
                             * np.float32(N_BUCKETS - max_exact)).astype(np.int32)
        large = np.minimum(large, N_BUCKETS - 1)
        out.append(np.where(dist < max_exact, dist, large))
    return np.stack(out).astype(np.int32)


def _bias_build(rel_bias, buckets, name):
    ng = len(A_CONFIGS)

    def body(t_ref, bk_ref, o_ref):
        gh = pl.program_id(0) * A_HEADS + pl.program_id(1)
        bk = bk_ref[0]
        acc = jnp.zeros((A_BLK, 2 * A_BLK), F32)
        for b in range(N_BUCKETS):
            acc = jnp.where(bk == b, t_ref[b, gh], acc)
        o_ref[0] = acc

    return pl.pallas_call(
        body, name=name, grid=(ng, A_HEADS),
        in_specs=[pl.BlockSpec(memory_space=pltpu.SMEM), pl.BlockSpec((1, A_BLK, 2 * A_BLK), lambda g, h: (g, 0, 0))],
        out_specs=pl.BlockSpec((1, A_BLK, 2 * A_BLK), lambda g, h: (g * A_HEADS + h, 0, 0)),
        out_shape=jax.ShapeDtypeStruct((ng * A_HEADS, A_BLK, 2 * A_BLK), F32),
        compiler_params=_params(2))(rel_bias, buckets)


def _bias_bwd(dbias, buckets, name):
    ng = len(A_CONFIGS)

    def body(d_ref, bk_ref, o_ref):
        gh = pl.program_id(0) * A_HEADS + pl.program_id(1)
        bk = bk_ref[0]
        d = d_ref[0]
        for b in range(N_BUCKETS):
            o_ref[b, gh] = jnp.sum(jnp.where(bk == b, d, 0.0))

    return pl.pallas_call(
        body, name=name, grid=(ng, A_HEADS),
        in_specs=[pl.BlockSpec((1, A_BLK, 2 * A_BLK), lambda g, h: (g * A_HEADS + h, 0, 0)),
                  pl.BlockSpec((1, A_BLK, 2 * A_BLK), lambda g, h: (g, 0, 0))],
        out_specs=pl.BlockSpec(memory_space=pltpu.SMEM),
        out_shape=jax.ShapeDtypeStruct((N_BUCKETS, ng * A_HEADS), F32),
        compiler_params=_params(2))(dbias, buckets)


def _attn_mask(b):
    qi = lax.broadcasted_iota(jnp.int32, (A_BLK, 2 * A_BLK), 0)
    ki = lax.broadcasted_iota(jnp.int32, (A_BLK, 2 * A_BLK), 1)
    band = (ki >= qi) & (ki <= qi + A_BLK)
    return band & ((b > 0) | (ki >= A_BLK))


def _attn_in_specs(g, dil):
    W = A_HEADS * A_HEAD_DIM

    def spec(t, prev, nb):
        def im(r, b):
            bb = jnp.minimum(b, nb - 1)
            if prev:
                bb = jnp.maximum(bb - 1, 0)
            return (bb, r * 3 + t)
        return pl.BlockSpec((A_BLK, W), im)

    return lambda nb: [spec(0, False, nb), spec(1, False, nb), spec(1, True, nb), spec(2, False, nb),
                       spec(2, True, nb)]


def _attn_fwd(qkv, bias, g, name):
    S = qkv.shape[0]
    _, dil = A_CONFIGS[g]
    L = S // dil
    nb = L // A_BLK
    W = A_HEADS * A_HEAD_DIM
    qv = qkv.reshape(L, dil * 3 * W)

    def body(q_ref, kc_ref, kp_ref, vc_ref, vp_ref, bias_ref, o_ref, l_ref):
        mask = _attn_mask(pl.program_id(1))
        for h in range(A_HEADS):
            hs = slice(h * A_HEAD_DIM, (h + 1) * A_HEAD_DIM)
            qh = q_ref[:, hs] * 0.125
            k2 = jnp.concatenate([kp_ref[:, hs], kc_ref[:, hs]], axis=0)
            v2 = jnp.concatenate([vp_ref[:, hs], vc_ref[:, hs]], axis=0)
            s = lax.dot_general(qh, k2, (((1,), (1,)), ((), ())), preferred_element_type=F32) + bias_ref[h]
            s = jnp.where(mask, s, NEG_INF)
            m = jnp.max(s, axis=-1, keepdims=True)
            p = jnp.exp(s - m)
            den = jnp.sum(p, axis=-1, keepdims=True)
            o = jnp.dot(p.astype(BF16), v2, preferred_element_type=F32) / den
            o_ref[:, hs] = o
            l_ref[:, hs] = jnp.broadcast_to(m + jnp.log(den), (A_BLK, A_HEAD_DIM))

    out_spec = pl.BlockSpec((A_BLK, W), lambda r, b: (b, r))
    o, lse = pl.pallas_call(
        body, name=name, grid=(dil, nb),
        in_specs=_attn_in_specs(g, dil)(nb) + [pl.BlockSpec((A_HEADS, A_BLK, 2 * A_BLK), lambda r, b: (g, 0, 0))],
        out_specs=[out_spec, out_spec],
        out_shape=[jax.ShapeDtypeStruct((L, dil * W), F32)] * 2,
        compiler_params=_params(2))(qv, qv, qv, qv, qv, bias)
    return o.reshape(S, W), lse.reshape(S, W)


def _mix_fwd(os, ls, name):
    S = os[0].shape[0]
    tr = ROW_TILE

    def body(o0, o1, o2, l0, l1, l2, om_ref, lt_ref):
        a, b, c = l0[...], l1[...], l2[...]
        m = jnp.maximum(jnp.maximum(a, b), c)
        ea, eb, ec = jnp.exp(a - m), jnp.exp(b - m), jnp.exp(c - m)
        z = (ea + eb) + ec
        om_ref[...] = ((ea / z) * o0[...] + (eb / z) * o1[...]) + (ec / z) * o2[...]
        lt_ref[...] = m + jnp.log(z)

    return pl.pallas_call(
        body, name=name, grid=(S // tr,), in_specs=[_row(tr)] * 6, out_specs=[_row(tr)] * 2,
        out_shape=[jax.ShapeDtypeStruct((S, D_MODEL), F32)] * 2, compiler_params=_params(1))(*os, *ls)


def _attn_bwd(qkv, bias, d_o, omix, ltot, g, name):
    S = qkv.shape[0]
    _, dil = A_CONFIGS[g]
    L = S // dil
    nb = L // A_BLK
    W = A_HEADS * A_HEAD_DIM
    qv = qkv.reshape(L, dil * 3 * W)
    view = lambda t: t.reshape(L, dil * W)

    def body(q_ref, kc_ref, kp_ref, vc_ref, vp_ref, bias_ref, do_ref, om_ref, lt_ref,
             dq_ref, dk_ref, dv_ref, db_ref, ck_ref, cv_ref):
        r, b = pl.program_id(0), pl.program_id(1)

        @pl.when((r == 0) & (b == 0))
        def _():
            db_ref[...] = jnp.zeros_like(db_ref)

        @pl.when(b == 0)
        def _():
            ck_ref[...] = jnp.zeros_like(ck_ref)
            cv_ref[...] = jnp.zeros_like(cv_ref)

        @pl.when(b < nb)
        def _():
            mask = _attn_mask(b)
            for h in range(A_HEADS):
                hs = slice(h * A_HEAD_DIM, (h + 1) * A_HEAD_DIM)
                qh = q_ref[:, hs] * 0.125
                k2 = jnp.concatenate([kp_ref[:, hs], kc_ref[:, hs]], axis=0)
                v2 = jnp.concatenate([vp_ref[:, hs], vc_ref[:, hs]], axis=0)
                s = lax.dot_general(qh, k2, (((1,), (1,)), ((), ())), preferred_element_type=F32) + bias_ref[h]
                s = jnp.where(mask, s, NEG_INF)
                wp = jnp.exp(s - lt_ref[:, h * A_HEAD_DIM:h * A_HEAD_DIM + 1])
                do_h = do_ref[:, hs]
                t_h = jnp.sum(do_h * om_ref[:, hs], axis=-1, keepdims=True)
                do_b = do_h.astype(BF16)
                dp = lax.dot_general(do_b, v2, (((1,), (1,)), ((), ())), preferred_element_type=F32)
                ds = wp * (dp - t_h)
                db_ref[h] += ds
                ds_b = ds.astype(BF16)
                dv2 = lax.dot_general(wp.astype(BF16), do_b, (((0,), (0,)), ((), ())), preferred_element_type=F32)
                dk2 = lax.dot_general(ds_b, qh, (((0,), (0,)), ((), ())), preferred_element_type=F32)
                dq_ref[:, hs] = (jnp.dot(ds_b, k2, preferred_element_type=F32) * 0.125).astype(BF16)
                dk_ref[:, hs] = (ck_ref[:, hs] + dk2[:A_BLK]).astype(BF16)
                dv_ref[:, hs] = (cv_ref[:, hs] + dv2[:A_BLK]).astype(BF16)
                ck_ref[:, hs] = dk2[A_BLK:]
                cv_ref[:, hs] = dv2[A_BLK:]

        @pl.when(b == nb)
        def _():
            dk_ref[...] = ck_ref[...].astype(BF16)
            dv_ref[...] = cv_ref[...].astype(BF16)

    act = pl.BlockSpec((A_BLK, W), lambda r, b: (jnp.minimum(b, nb - 1), r))
    lag = pl.BlockSpec((A_BLK, W), lambda r, b: (jnp.maximum(b - 1, 0), r))
    full = pl.BlockSpec((A_HEADS, A_BLK, 2 * A_BLK), lambda r, b: (0, 0, 0))
    in_specs = _attn_in_specs(g, dil)(nb) + [pl.BlockSpec((A_HEADS, A_BLK, 2 * A_BLK), lambda r, b: (g, 0, 0)),
                                             act, act, act]
    dq, dk, dv, dbias = pl.pallas_call(
        body, name=name, grid=(dil, nb + 1), in_specs=in_specs,
        out_specs=[act, lag, lag, full],
        out_shape=[jax.ShapeDtypeStruct((L, dil * W), BF16)] * 3
        + [jax.ShapeDtypeStruct((A_HEADS, A_BLK, 2 * A_BLK), F32)],
        scratch_shapes=[pltpu.VMEM((A_BLK, W), F32), pltpu.VMEM((A_BLK, W), F32)],
        compiler_params=_params(2))(qv, qv, qv, qv, qv, bias, view(d_o), view(omix), view(ltot))
    return dq.reshape(S, W), dk.reshape(S, W), dv.reshape(S, W), dbias


NT = (((1,), (1,)), ((), ()))
TN = (((0,), (0,)), ((), ()))


def _dot(a, b, dims=(((1,), (0,)), ((), ()))):
    return lax.dot_general(a.astype(BF16), b.astype(BF16), dims, preferred_element_type=F32)


def _gla_gates(glr, wg_ref, bg_ref):
    z = _dot(glr, wg_ref[...]) + bg_ref[...]
    log_sig = -(jnp.maximum(-z, 0.0) + jnp.log1p(jnp.exp(-jnp.abs(z))))
    return z, log_sig / B_TAU


def _gla_chunk(q, k, gk):
    row = lax.broadcasted_iota(jnp.int32, (B_CHUNK, B_CHUNK), 0)
    col = lax.broadcasted_iota(jnp.int32, (B_CHUNK, B_CHUNK), 1)
    causal = row >= col
    bcum = jnp.dot(causal.astype(F32), gk, precision=lax.Precision.HIGHEST, preferred_element_type=F32)
    bl = bcum[B_CHUNK - 1:B_CHUNK, :]
    qt = (q * (B_DK ** -0.5)) * jnp.exp(bcum)
    kt = k * jnp.exp(-bcum)
    kd = k * jnp.exp(bl - bcum)
    a = jnp.where(causal, _dot(qt, kt, NT), 0.0)
    return causal, bcum, bl, qt, kt, kd, a


def _gla_specs(tg):
    q = pl.BlockSpec((tg, B_DK), lambda h, i: (i, h))
    k = pl.BlockSpec((tg, B_DK), lambda h, i: (i, B_HEADS + h))
    v = pl.BlockSpec((tg, B_DV), lambda h, i: (i, B_HEADS + h))
    glr = pl.BlockSpec((tg, 128), lambda h, i: (i, 24))
    wg = pl.BlockSpec((128, B_DK), lambda h, i: (0, h))
    bg = pl.BlockSpec((1, B_DK), lambda h, i: (0, h))
    return [q, k, v, glr, wg, bg]


def _gla_fwd(proj, w_gate, b_gate, name):
    S = proj.shape[0]
    tg = GLA_ROWS
    nc = tg // B_CHUNK

    def body(q_ref, k_ref, v_ref, glr_ref, wg_ref, bg_ref, o_ref, st_ref, state_ref):
        @pl.when(pl.program_id(1) == 0)
        def _():
            state_ref[...] = jnp.zeros_like(state_ref)

        _, gk_all = _gla_gates(glr_ref[...], wg_ref, bg_ref)
        st = state_ref[...]
        for c in range(nc):
            rows = slice(c * B_CHUNK, (c + 1) * B_CHUNK)
            v = v_ref[rows, :]
            _, _, bl, qt, _, kd, a = _gla_chunk(q_ref[rows, :], k_ref[rows, :], gk_all[rows, :])
            o_ref[rows, :] = _dot(a, v) + _dot(qt, st, NT)
            st_ref[c, 0] = st
            st = st * jnp.exp(bl) + _dot(v, kd, TN)
        state_ref[...] = st

    return pl.pallas_call(
        body, name=name, grid=(B_HEADS, S // tg), in_specs=_gla_specs(tg),
        out_specs=[pl.BlockSpec((tg, B_DV), lambda h, i: (i, h)),
                   pl.BlockSpec((nc, 1, B_DV, B_DK), lambda h, i: (i, h, 0, 0))],
        out_shape=[jax.ShapeDtypeStruct((S, B_V), F32),
                   jax.ShapeDtypeStruct((S // B_CHUNK, B_HEADS, B_DV, B_DK), F32)],
        scratch_shapes=[pltpu.VMEM((B_DV, B_DK), F32)], compiler_params=_params(2))(
            proj, proj, proj, proj, w_gate, b_gate)


def _gla_bwd(proj, w_gate, b_gate, states, d_o, name):
    S = proj.shape[0]
    tg = GLA_ROWS
    nc = tg // B_CHUNK
    ni = S // tg

    def rev(spec):
        return pl.BlockSpec(spec.block_shape, lambda h, i, im=spec.index_map: im(h, ni - 1 - i))

    def body(q_ref, k_ref, v_ref, glr_ref, wg_ref, bg_ref, st_ref, do_ref,
             dq_ref, dk_ref, dv_ref, dz_ref, dbg_ref, dstate_ref):
        @pl.when(pl.program_id(1) == 0)
        def _():
            dstate_ref[...] = jnp.zeros_like(dstate_ref)
            dbg_ref[...] = jnp.zeros_like(dbg_ref)

        z_all, gk_all = _gla_gates(glr_ref[...], wg_ref, bg_ref)
        dst = dstate_ref[...]
        for c in range(nc - 1, -1, -1):
            rows = slice(c * B_CHUNK, (c + 1) * B_CHUNK)
            v = v_ref[rows, :]
            d_out = do_ref[rows, :]
            st = st_ref[c, 0]
            causal, bcum, bl, qt, kt, kd, a = _gla_chunk(q_ref[rows, :], k_ref[rows, :], gk_all[rows, :])
            da = jnp.where(causal, _dot(d_out, v, NT), 0.0)
            dv_ref[rows, :] = (_dot(a, d_out, TN) + _dot(kd, dst, NT)).astype(BF16)
            dqt = _dot(da, kt) + _dot(d_out, st)
            dkt = _dot(da, qt, TN)
            dkd = _dot(v, dst)
            dec = jnp.exp(bl)
            ddec = jnp.sum(dst * st, axis=0, keepdims=True)
            dst = dst * dec + _dot(d_out, qt, TN)
            dq_ref[rows, :] = (dqt * jnp.exp(bcum) * (B_DK ** -0.5)).astype(BF16)
            dk_ref[rows, :] = (dkt * jnp.exp(-bcum) + dkd * jnp.exp(bl - bcum)).astype(BF16)
            db = (dqt * qt - dkt * kt) - dkd * kd
            dbl = jnp.sum(dkd * kd, axis=0, keepdims=True) + dec * ddec
            upper = jnp.logical_not(causal) | (lax.broadcasted_iota(jnp.int32, (B_CHUNK, B_CHUNK), 0)
                                               == lax.broadcasted_iota(jnp.int32, (B_CHUNK, B_CHUNK), 1))
            dgk = jnp.dot(upper.astype(F32), db, precision=lax.Precision.HIGHEST, preferred_element_type=F32) + dbl
            dz = dgk * (1.0 / B_TAU) * jax.nn.sigmoid(-z_all[rows, :])
            dz_ref[rows, :] = dz
            dbg_ref[...] += jnp.sum(dz, axis=0, keepdims=True)
        dstate_ref[...] = dst

    qs = pl.BlockSpec((tg, B_DK), lambda h, i: (i, h))
    vs = pl.BlockSpec((tg, B_DV), lambda h, i: (i, h))
    in_specs = [rev(s) if n < 4 else s for n, s in enumerate(_gla_specs(tg))]
    in_specs += [rev(pl.BlockSpec((nc, 1, B_DV, B_DK), lambda h, i: (i, h, 0, 0))), rev(vs)]
    return pl.pallas_call(
        body, name=name, grid=(B_HEADS, ni), in_specs=in_specs,
        out_specs=[rev(qs), rev(qs), rev(vs), rev(qs), pl.BlockSpec((1, B_DK), lambda h, i: (0, h))],
        out_shape=[jax.ShapeDtypeStruct((S, B_QK), BF16), jax.ShapeDtypeStruct((S, B_QK), BF16),
                   jax.ShapeDtypeStruct((S, B_V), BF16), jax.ShapeDtypeStruct((S, B_QK), F32),
                   jax.ShapeDtypeStruct((1, B_QK), F32)],
        scratch_shapes=[pltpu.VMEM((B_DV, B_DK), F32)], compiler_params=_params(2))(
            proj, proj, proj, proj, w_gate, b_gate, states, d_o)


def _gla_out(o, proj, gnorm, name):
    S = o.shape[0]
    tr = ROW_TILE

    def body(o_ref, r_ref, g_ref, y_ref):
        for h in range(B_HEADS):
            hs = slice(h * B_DV, (h + 1) * B_DV)
            oh = o_ref[:, hs]
            rs = lax.rsqrt(jnp.mean(oh * oh, axis=-1, keepdims=True) + EPS)
            y_ref[:, hs] = (((oh * rs) * g_ref[...]) * _silu(r_ref[:, hs])).astype(BF16)

    return pl.pallas_call(
        body, name=name, grid=(S // tr,),
        in_specs=[_row(tr), pl.BlockSpec((tr, B_V), lambda i: (i, 2)), _vec(B_DV)], out_specs=_row(tr),
        out_shape=jax.ShapeDtypeStruct((S, B_V), BF16), compiler_params=_params(1))(o, proj, gnorm)


def _gla_out_bwd(o, proj, gnorm, d_y, name):
    S = o.shape[0]
    tr = ROW_TILE

    def body(o_ref, r_ref, g_ref, dy_ref, do_ref, dr_ref, dg_ref):
        @pl.when(pl.program_id(0) == 0)
        def _():
            dg_ref[...] = jnp.zeros_like(dg_ref)

        for h in range(B_HEADS):
            hs = slice(h * B_DV, (h + 1) * B_DV)
            oh, rv, dyv = o_ref[:, hs], r_ref[:, hs], dy_ref[:, hs]
            rs = lax.rsqrt(jnp.mean(oh * oh, axis=-1, keepdims=True) + EPS)
            xhat = oh * rs
            dr_ref[:, hs] = (dyv * (xhat * g_ref[...]) * _dsilu(rv)).astype(BF16)
            dn = dyv * _silu(rv)
            dg_ref[...] += jnp.sum(dn * xhat, axis=0, keepdims=True)
            dxh = dn * g_ref[...]
            do_ref[:, hs] = rs * (dxh - xhat * jnp.mean(dxh * xhat, axis=-1, keepdims=True))

    return pl.pallas_call(
        body, name=name, grid=(S // tr,),
        in_specs=[_row(tr), pl.BlockSpec((tr, B_V), lambda i: (i, 2)), _vec(B_DV), _row(tr)],
        out_specs=[_row(tr), _row(tr), _vec(B_DV)],
        out_shape=[jax.ShapeDtypeStruct((S, B_V), F32), jax.ShapeDtypeStruct((S, B_V), BF16),
                   jax.ShapeDtypeStruct((1, B_DV), F32)],
        compiler_params=_params(1))(o, proj, gnorm, d_y)


J_SH1, J_SC1, J_G1, J_SH2, J_SC2, J_G2 = range(6)


IN_A_TN = 768
UP_TN = 2 * D_FF // 4
TOKEN_TK = 2048


def _ffn_fwd(h, w, i, tag):
    S = h.shape[0]
    u = _matmul(h, w["w_up"], "nn", F32, f"up{tag}", shape=(S, 2 * D_FF, D_MODEL), tiles=(1024, UP_TN, D_MODEL),
                b_spec=pl.BlockSpec((None, D_MODEL, UP_TN), lambda m, j, k: (j, i, 0)))
    act = _conv_gate(u, w["conv_w"][i], w["conv_b"][i:i + 1], f"conv_gate{tag}")
    f = _matmul(act, w["w_down"], "nn", F32, f"down{tag}", shape=(S, D_MODEL, D_FF), tiles=(1024, D_MODEL, D_FF),
                b_spec=pl.BlockSpec((D_FF, D_MODEL), lambda m, j, k: (i, j)))
    return u, act, f


def _ffn_bwd(dx_out, f, u, act, h, x_in, mod, w, i, tag, prev):
    S = h.shape[0]
    df, dg2 = _gate_bwd(dx_out, f, mod, J_G2, f"gate2_bwd{tag}")
    dact = _matmul(df, w["w_down"], "nt", F32, f"down_dx{tag}", shape=(S, D_FF, D_MODEL),
                   tiles=(1024, D_FF // 2, D_MODEL),
                   b_spec=pl.BlockSpec((D_FF // 2, D_MODEL), lambda m, j, k: (2 * i + j, k)))
    d_w_down = _matmul(act, df, "tn", F32, f"down_dw{tag}", shape=(D_FF, D_MODEL, S),
                       tiles=(D_FF // 2, D_MODEL, min(S, TOKEN_TK)),
                       o_spec=pl.BlockSpec((D_FF // 2, D_MODEL), lambda m, j, k: (2 * i + m, j)),
                       o_shape=(2 * D_FF, D_MODEL), prev=None if prev is None else prev["w_down"])
    duca, ducb, dcwa, dcwb, dcba, dcbb = _conv_gate_bwd(u, dact, w["conv_w"][i], w["conv_b"][i:i + 1],
                                                        f"conv_gate_bwd{tag}")
    du = _conv_t(duca, w["conv_w"][i], 0, None, f"conv_t_a{tag}")
    du = _conv_t(ducb, w["conv_w"][i], 1, du, f"conv_t_b{tag}")
    dh = _matmul(du, w["w_up"], "nt", F32, f"up_dx{tag}", shape=(S, D_MODEL, 2 * D_FF), tiles=(1024, D_MODEL, UP_TN),
                 b_spec=pl.BlockSpec((None, D_MODEL, UP_TN), lambda m, j, k: (k, i, 0)))
    d_w_up = _matmul(h, du, "tn", F32, f"up_dw{tag}", shape=(D_MODEL, 2 * D_FF, S),
                     tiles=(D_MODEL, UP_TN, min(S, TOKEN_TK)),
                     o_spec=pl.BlockSpec((None, D_MODEL, UP_TN), lambda m, j, k: (j, i, 0)),
                     o_shape=(4, 2 * D_MODEL, UP_TN), prev=None if prev is None else prev["w_up"])
    dx_in, dsc2, dsh2, dgam = _norm_mod_bwd(dh, x_in, dx_out, w["norm_ffn"][i:i + 1], mod, J_SC2,
                                            f"norm_ffn_bwd{tag}")
    grads = dict(w_down=d_w_down, w_up=d_w_up, norm_ffn=dgam,
                 conv_w=jnp.concatenate([dcwa[0:3], dcwb[0:3]], axis=1),
                 conv_b=jnp.concatenate([dcba, dcbb], axis=1))
    return dx_in, (dsh2, dsc2, dg2), grads


def _local_step(x, c, tgt, w):
    buckets = jnp.asarray(_bucket_maps())
    S = x.shape[0]
    mods = [_ada_mod(c, w["w_ada"], w["b_ada"][i:i + 1], i, f"ada_mod{i}") for i in range(2)]

    h1 = _norm_mod(x, w["norm_mix"][0:1], mods[0], J_SC1, J_SH1, "norm_mix0")
    qkv = [_matmul(h1, w["w_in_a"], "nn", BF16, f"in_a{g}", shape=(S, 3 * D_MODEL, D_MODEL),
                   tiles=(1024, IN_A_TN, D_MODEL),
                   b_spec=pl.BlockSpec((None, D_MODEL, IN_A_TN), lambda m, j, k, g=g: ((4 * g + j) // 3, 0, (4 * g + j) % 3)))
           for g in range(3)]
    bias = _bias_build(w["rel_bias"], buckets, "bias_build")
    os_, ls_ = zip(*[_attn_fwd(qkv[g], bias, g, f"attn_fwd{g}") for g in range(3)])
    omix, ltot = _mix_fwd(os_, ls_, "mix_fwd")
    y0 = _matmul(omix, w["w_out_a"], "nn", F32, "out_a")
    x1, h2 = _norm_mod(x, w["norm_ffn"][0:1], mods[0], J_SC2, J_SH2, "norm_ffn0", resid=y0, gate_mod=mods[0],
                       j_gate=J_G1)
    u0, act0, f0 = _ffn_fwd(h2, w, 0, "0")

    x2, h3 = _norm_mod(x1, w["norm_mix"][1:2], mods[1], J_SC1, J_SH1, "norm_mix1", resid=f0, gate_mod=mods[0],
                       j_gate=J_G2)
    proj = _matmul(h3, w["w_in_b"], "nn", F32, "in_b")
    o_gla, states = _gla_fwd(proj, w["w_gate_b"], w["b_gate_b"], "gla_fwd")
    on = _gla_out(o_gla, proj, w["gnorm_b"], "gla_out")
    y1 = _matmul(on, w["w_out_b"], "nn", F32, "out_b")
    x3, h4 = _norm_mod(x2, w["norm_ffn"][1:2], mods[1], J_SC2, J_SH2, "norm_ffn1", resid=y1, gate_mod=mods[1],
                       j_gate=J_G1)
    u1, act1, f1 = _ffn_fwd(h4, w, 1, "1")

    dx4, loss, d_norm_final = _final_loss(x3, f1, mods[1], J_G2, w["norm_final"], tgt, "final_loss")

    dx3, (dsh2_1, dsc2_1, dg2_1), g_ffn1 = _ffn_bwd(dx4, f1, u1, act1, h4, x3, mods[1], w, 1, "1", None)
    dy1, dg1_1 = _gate_bwd(dx3, y1, mods[1], J_G1, "gate1_bwd1")
    d_on = _matmul(dy1, w["w_out_b"], "nt", F32, "out_b_dx")
    d_w_out_b = _matmul(on, dy1, "tn", F32, "out_b_dw")
    d_ogla, d_r, d_gnorm = _gla_out_bwd(o_gla, proj, w["gnorm_b"], d_on, "gla_out_bwd")
    dq, dk, dv, dz, d_b_gate = _gla_bwd(proj, w["w_gate_b"], w["b_gate_b"], states, d_ogla, "gla_bwd")
    d_glr = _matmul(dz, w["w_gate_b"], "nt", BF16, "gate_dx")
    d_w_gate = _matmul(proj[:, 3072:3200], dz, "tn", F32, "gate_dw")
    dproj = jnp.concatenate([dq, dk, dv, d_r, d_glr], axis=1)
    dh3 = _matmul(dproj, w["w_in_b"], "nt", F32, "in_b_dx")
    d_w_in_b = _matmul(h3, dproj, "tn", F32, "in_b_dw")
    dx2, dsc1_1, dsh1_1, d_nmix1 = _norm_mod_bwd(dh3, x2, dx3, w["norm_mix"][1:2], mods[1], J_SC1, "norm_mix_bwd1")
    dmod1 = jnp.concatenate([dsh1_1, dsc1_1, dg1_1, dsh2_1, dsc2_1, dg2_1], axis=1)

    dx1, (dsh2_0, dsc2_0, dg2_0), g_ffn0 = _ffn_bwd(dx2, f0, u0, act0, h2, x1, mods[0], w, 0, "0", g_ffn1)
    dy0, dg1_0 = _gate_bwd(dx1, y0, mods[0], J_G1, "gate1_bwd0")
    d_omix = _matmul(dy0, w["w_out_a"], "nt", F32, "out_a_dx")
    d_w_out_a = _matmul(omix, dy0, "tn", F32, "out_a_dw")
    parts, dbs = [], []
    for g in range(3):
        dq_g, dk_g, dv_g, db_g = _attn_bwd(qkv[g], bias, d_omix, omix, ltot, g, f"attn_bwd{g}")
        parts += [dq_g, dk_g, dv_g]
        dbs.append(db_g)
    dqkv = jnp.concatenate(parts, axis=1)
    d_rel_bias = _bias_bwd(jnp.concatenate(dbs, axis=0), buckets, "bias_bwd")
    half_chip = 9 * D_MODEL // 8
    dh1 = _matmul(dqkv, w["w_in_a"], "nt", F32, "in_a_dx", shape=(S, D_MODEL, 9 * D_MODEL),
                  tiles=(1024, D_MODEL, half_chip),
                  b_spec=pl.BlockSpec((None, D_MODEL, half_chip), lambda m, j, k: (k // 2, j, k % 2)))
    d_w_in_a = _matmul(h1, dqkv, "tn", F32, "in_a_dw", shape=(D_MODEL, 9 * D_MODEL, S),
                       tiles=(D_MODEL, half_chip, min(S, TOKEN_TK)),
                       o_spec=pl.BlockSpec((None, D_MODEL, half_chip), lambda m, j, k: (j // 2, m, j % 2)),
                       o_shape=(4, D_MODEL, 2 * half_chip))
    dx0, dsc1_0, dsh1_0, d_nmix0 = _norm_mod_bwd(dh1, x, dx1, w["norm_mix"][0:1], mods[0], J_SC1, "norm_mix_bwd0")
    dmod0 = jnp.concatenate([dsh1_0, dsc1_0, dg1_0, dsh2_0, dsc2_0, dg2_0], axis=1)

    grads = dict(
        w_in_a=d_w_in_a, w_out_a=d_w_out_a, rel_bias=d_rel_bias, w_in_b=d_w_in_b, w_gate_b=d_w_gate,
        b_gate_b=d_b_gate, gnorm_b=d_gnorm, w_out_b=d_w_out_b,
        norm_mix=jnp.concatenate([d_nmix0, d_nmix1], axis=0),
        norm_ffn=jnp.concatenate([g_ffn0["norm_ffn"], g_ffn1["norm_ffn"]], axis=0),
        w_ada=_ada_outer(c, jnp.stack([dmod0, dmod1]), "ada_outer"),
        b_ada=jnp.concatenate([dmod0, dmod1], axis=0),
        w_up=g_ffn0["w_up"],
        conv_w=jnp.stack([g_ffn0["conv_w"], g_ffn1["conv_w"]]),
        conv_b=jnp.concatenate([g_ffn0["conv_b"], g_ffn1["conv_b"]], axis=0),
        w_down=g_ffn0["w_down"],
        norm_final=d_norm_final)
    return loss, dx0, grads


N_CHIPS = 4
N_DEV = 8
WEIGHTS = ("w_in_a", "w_out_a", "rel_bias", "w_in_b", "w_gate_b", "b_gate_b", "gnorm_b", "w_out_b", "norm_mix",
           "norm_ffn", "w_ada", "b_ada", "w_up", "conv_w", "conv_b", "w_down", "norm_final")
SHARD_AXIS = dict(w_in_a=2, w_out_a=1, w_in_b=2, w_gate_b=2, b_gate_b=1, gnorm_b=1, w_out_b=1, w_ada=2, w_up=2,
                  conv_w=2, w_down=1)
SHARDED = tuple(n for n in WEIGHTS if n in SHARD_AXIS)
REPLICATED = tuple(n for n in WEIGHTS if n not in SHARD_AXIS)
BIG = ("w_in_a", "w_out_a", "w_in_b", "w_out_b", "w_ada", "w_up", "w_down")
SMALL = ("w_gate_b", "b_gate_b", "gnorm_b", "conv_w")
SMALL_FULL = dict(w_gate_b=(1, 16, 512), b_gate_b=(1, 512), gnorm_b=(1, 256), conv_w=(2, 3, 5632))
R_SMALL = 16
R_TINY = 72
LOSS_SLOT = 72960
R_TINY_SHARD = 40
W_IN_B_PAD = 896

COMM_VIEW = dict(
    w_in_a=((4096, 2304), 1024, 512, 512),
    w_out_a=((1024, 1024), 256, 128, 128),
    w_in_b=((4096, W_IN_B_PAD), 1024, 512, 512),
    w_out_b=((1024, 1024), 256, 128, 128),
    w_ada=((8192, 1536), 2048, 1024, 1024),
    w_up=((8192, 1408), 2048, 1024, 1024),
    w_down=((5632, 1024), 704, 2816, 704))


def _pack(arrs, rows):
    flat = jnp.concatenate([a.reshape(-1) for a in arrs])
    return jnp.pad(flat, (0, rows * LANES - flat.shape[0])).reshape(rows, LANES)


def _unpack(flat2d, shapes):
    flat = flat2d.reshape(-1)
    out, off = [], 0
    for shp in shapes:
        n = math.prod(shp)
        out.append(flat[off:off + n].reshape(shp))
        off += n
    return out


def _chip_slice(a, axis, k):
    n = a.shape[axis] // N_CHIPS
    return lax.slice_in_dim(a, k * n, (k + 1) * n, axis=axis)


def _place():
    mx, my, mc = lax.axis_index("x"), lax.axis_index("y"), lax.axis_index("c")
    chips = [(1 - mx, my), (mx, 1 - my), (1 - mx, 1 - my)]
    return mx, my, mc, chips


def _rcopy(src, dst, send_sem, recv_sem, dev):
    return pltpu.make_async_remote_copy(src_ref=src, dst_ref=dst, send_sem=send_sem, recv_sem=recv_sem,
                                        device_id=dev, device_id_type=MESH)


def _comm_call(body, name, ins, out_shapes, n_sems, in_place=False):
    n_in, n_out = len(ins), len(out_shapes)

    def wrapped(*refs):
        body(refs[:n_in], refs[n_in:n_in + n_out], *refs[n_in + n_out:])

    return pl.pallas_call(
        wrapped, name=name, in_specs=[HBM] * n_in, out_specs=[HBM] * n_out, out_shape=out_shapes,
        input_output_aliases={i: i for i in range(n_in)} if in_place else {},
        scratch_shapes=[pltpu.SemaphoreType.DMA((n_sems,)), pltpu.SemaphoreType.DMA((n_sems,))])(*ins)


DMA_CHUNK_BYTES = 2 * 1024 * 1024


def _rows(ref, start, size):
    return ref.at[pl.ds(pl.multiple_of(start, 16), size), :]


def _block(ref, name, k, h):
    _, bk, bh, nr = COMM_VIEW[name]
    return _rows(ref, bk * k + bh * h, nr)


def _chunks(nr, row_bytes):
    n = 1
    while nr % (2 * n) == 0 and (nr // (2 * n)) % 16 == 0 and (nr // n) * row_bytes > DMA_CHUNK_BYTES:
        n *= 2
    return [(i * (nr // n), nr // n) for i in range(n)]


def _gather_big(views, name):
    names = BIG

    def body(x_refs, out_refs, send_sems, recv_sems):
        mx, my, mc, chips = _place()
        chip = 2 * mx + my
        sibling = (mx, my, 1 - mc)
        sends = []
        for a, n in enumerate(names):
            for j, (cx, cy) in enumerate(chips):
                blk = _block(out_refs[a], n, chip, mc)
                cp = _rcopy(blk, blk, send_sems.at[6 * a + j], recv_sems.at[6 * a + j], (cx, cy, mc))
                cp.start()
                sends.append(cp)
        for a, n in enumerate(names):
            for j, (cx, cy) in enumerate(chips):
                blk = _block(out_refs[a], n, 2 * cx + cy, mc)
                _rcopy(blk, blk, send_sems.at[6 * a + j], recv_sems.at[6 * a + j], sibling).wait_recv()
                cp = _rcopy(blk, blk, send_sems.at[6 * a + 3 + j], recv_sems.at[6 * a + 3 + j], sibling)
                cp.start()
                sends.append(cp)
        for a, n in enumerate(names):
            for j, (cx, cy) in enumerate(chips):
                blk = _block(out_refs[a], n, 2 * cx + cy, 1 - mc)
                _rcopy(blk, blk, send_sems.at[6 * a + 3 + j], recv_sems.at[6 * a + 3 + j], sibling).wait_recv()
        for cp in sends:
            cp.wait_send()

    outs = _comm_call(body, name, [views[n] for n in names],
                      [jax.ShapeDtypeStruct(views[n].shape, views[n].dtype) for n in names], 6 * len(names),
                      in_place=True)
    return dict(zip(names, outs))


def _rs_pair_exchange(views, name):
    names = BIG

    def body(g_refs, recv_refs, send_sems, recv_sems):
        mx, my, mc, _ = _place()
        sibling = (mx, my, 1 - mc)
        for a, n in enumerate(names):
            (_, cols), _, _, nr = COMM_VIEW[n]
            for k in range(N_CHIPS):
                src = _block(g_refs[a], n, k, 1 - mc)
                for start, size in _chunks(nr, cols * 4):
                    _rcopy(src.at[pl.ds(start, size), :], recv_refs[a].at[k, pl.ds(start, size), :],
                           send_sems.at[a], recv_sems.at[a], sibling).start()
        for a in range(len(names)):
            _rcopy(recv_refs[a], recv_refs[a], send_sems.at[a], recv_sems.at[a], sibling).wait()

    outs = _comm_call(body, name, [views[n] for n in names],
                      [jax.ShapeDtypeStruct((N_CHIPS, COMM_VIEW[n][3], COMM_VIEW[n][0][1]), F32) for n in names],
                      len(names))
    return dict(zip(names, outs))


def _pair_add(view, recv, c_idx, n, out_dtype, name):
    (_, cols), bk, bh, nr = COMM_VIEW[n]
    tr = _pick(math.gcd(bk, bh, nr), 256, 8)

    def body(c_ref, g_ref, r_ref, o_ref):
        o_ref[...] = (g_ref[...] + r_ref[...]).astype(o_ref.dtype)

    piece = pl.BlockSpec((None, tr, cols), lambda k, i, c_ref: (k, i, 0))
    return pl.pallas_call(
        body, name=name,
        grid_spec=pltpu.PrefetchScalarGridSpec(
            num_scalar_prefetch=1, grid=(N_CHIPS, nr // tr),
            in_specs=[pl.BlockSpec((tr, cols), lambda k, i, c_ref: ((bk * k + bh * c_ref[0]) // tr + i, 0)), piece],
            out_specs=piece),
        out_shape=jax.ShapeDtypeStruct((N_CHIPS, nr, cols), out_dtype), compiler_params=_params(2))(
            c_idx, view, recv)


def _rs_chip_exchange(q, name):
    names = BIG

    def body(q_refs, out_refs, send_sems, recv_sems):
        mx, my, mc, chips = _place()
        chip = 2 * mx + my
        sends = []
        for a in range(len(names)):
            for j, (cx, cy) in enumerate(chips):
                cp = _rcopy(q_refs[a].at[2 * cx + cy], out_refs[a].at[chip], send_sems.at[3 * a + j],
                            recv_sems.at[3 * a + j], (cx, cy, mc))
                cp.start()
                sends.append(cp)
        for a in range(len(names)):
            for j, (cx, cy) in enumerate(chips):
                blk = out_refs[a].at[2 * cx + cy]
                _rcopy(blk, blk, send_sems.at[3 * a + j], recv_sems.at[3 * a + j], (cx, cy, mc)).wait_recv()
        for cp in sends:
            cp.wait_send()

    outs = _comm_call(body, name, [q[n] for n in names],
                      [jax.ShapeDtypeStruct(q[n].shape, q[n].dtype) for n in names], 3 * len(names))
    return dict(zip(names, outs))


def _rs_pair_gather(r, name):
    names = BIG

    def body(r_refs, out_refs, send_sems, recv_sems):
        mx, my, mc, _ = _place()
        sibling = (mx, my, 1 - mc)
        for a, n in enumerate(names):
            (_, cols), _, _, nr = COMM_VIEW[n]
            for start, size in _chunks(nr, cols * 4):
                rows = _rows(out_refs[a], mc * nr + start, size)
                _rcopy(rows, rows, send_sems.at[a], recv_sems.at[a], sibling).start()
        for a, n in enumerate(names):
            nr = COMM_VIEW[n][3]
            _rcopy(_rows(out_refs[a], mc * nr, nr), _rows(out_refs[a], (1 - mc) * nr, nr), send_sems.at[a],
                   recv_sems.at[a], sibling).wait()

    outs = _comm_call(body, name, [r[n] for n in names],
                      [jax.ShapeDtypeStruct(r[n].shape, F32) for n in names], len(names), in_place=True)
    return dict(zip(names, outs))


def _gather8(x, reduce, name):
    rows = x.shape[0]

    def body(x_ref, out_ref, *rest):
        if reduce:
            buf_ref, send_sems, recv_sems = rest
        else:
            (send_sems, recv_sems), buf_ref = rest, out_ref
        mx, my, mc, _ = _place()
        me = 4 * mx + 2 * my + mc
        buf_ref[me] = x_ref[...]
        peers = []
        for j in range(1, N_DEV):
            px = 1 - mx if j & 4 else mx
            py = 1 - my if j & 2 else my
            pc = 1 - mc if j & 1 else mc
            peers.append((px, py, pc))
        sends = [_rcopy(x_ref, buf_ref.at[me], send_sems.at[j], recv_sems.at[j], p) for j, p in enumerate(peers)]
        for cp in sends:
            cp.start()
        for j, (px, py, pc) in enumerate(peers):
            _rcopy(x_ref, buf_ref.at[4 * px + 2 * py + pc], send_sems.at[j], recv_sems.at[j], (px, py, pc)).wait_recv()
        for cp in sends:
            cp.wait_send()
        if reduce:
            acc = buf_ref[0]
            for d in range(1, N_DEV):
                acc = acc + buf_ref[d]
            out_ref[...] = acc

    vmem = pl.BlockSpec(memory_space=pltpu.VMEM)
    sems = [pltpu.SemaphoreType.DMA((N_DEV - 1,)), pltpu.SemaphoreType.DMA((N_DEV - 1,))]
    if reduce:
        out_shape = jax.ShapeDtypeStruct((rows, LANES), F32)
        scratch = [pltpu.VMEM((N_DEV, rows, LANES), F32)] + sems
    else:
        out_shape = jax.ShapeDtypeStruct((N_DEV, rows, LANES), F32)
        scratch = sems
    return pl.pallas_call(body, name=name, in_specs=[vmem], out_specs=vmem, out_shape=out_shape,
                          scratch_shapes=scratch)(x)


def _sum4(p, q, chip, core, name):
    _, nr, cols = p.shape
    tr = _pick(nr, 256, 8)

    def body(chip_ref, core_ref, p0, p1, p2, p3, own, o_ref):
        s = [jnp.where(chip_ref[0] == k, own[...], pk[...]).astype(F32) for k, pk in enumerate((p0, p1, p2, p3))]
        o_ref[...] = ((s[0] + s[1]) + s[2]) + s[3]

    return pl.pallas_call(
        body, name=name,
        grid_spec=pltpu.PrefetchScalarGridSpec(
            num_scalar_prefetch=2, grid=(nr // tr,),
            in_specs=[pl.BlockSpec((None, tr, cols), lambda i, ch, co, k=k: (jnp.where(ch[0] == k, k ^ 1, k), i, 0))
                      for k in range(N_CHIPS)]
            + [pl.BlockSpec((None, tr, cols), lambda i, ch, co: (ch[0], i, 0))],
            out_specs=pl.BlockSpec((tr, cols), lambda i, ch, co: (co[0] * (nr // tr) + i, 0))),
        out_shape=jax.ShapeDtypeStruct((2 * nr, cols), F32), compiler_params=_params(1))(chip, core, p, p, p, p, q)


def _place_shard(shard, chip, n, name):
    (rows, cols), bk, bh, nr = COMM_VIEW[n]
    tr = _pick(math.gcd(bk, bh, nr), 256, 16)

    def body(chip_ref, x_ref, o_ref):
        o_ref[...] = x_ref[...].astype(BF16)

    return pl.pallas_call(
        body, name=name,
        grid_spec=pltpu.PrefetchScalarGridSpec(
            num_scalar_prefetch=1, grid=(2, nr // tr),
            in_specs=[pl.BlockSpec((tr, cols), lambda h, i, ch: (h * (nr // tr) + i, 0))],
            out_specs=pl.BlockSpec((tr, cols), lambda h, i, ch: ((bk * ch[0] + bh * h) // tr + i, 0))),
        out_shape=jax.ShapeDtypeStruct((rows, cols), BF16), compiler_params=_params(2))(chip, shard)


def _adamw(w, g, m, v, name):
    rows, cols = w.shape
    tr = _pick(rows, max(8, (1 << 20) // (4 * cols)), 8)

    def body(w_ref, g_ref, m_ref, v_ref, d_ref, mo_ref, vo_ref):
        gv = g_ref[...]
        mn = ADAM_B1 * m_ref[...] + (1.0 - ADAM_B1) * gv
        vn = ADAM_B2 * v_ref[...] + (1.0 - ADAM_B2) * (gv * gv)
        m_hat = mn / (1.0 - ADAM_B1 ** ADAM_STEP)
        v_hat = vn / (1.0 - ADAM_B2 ** ADAM_STEP)
        d_ref[...] = -ADAM_LR * (m_hat / (jnp.sqrt(v_hat) + ADAM_EPS) + ADAM_WD * w_ref[...])
        mo_ref[...] = mn
        vo_ref[...] = vn

    shape = jax.ShapeDtypeStruct(w.shape, F32)
    return pl.pallas_call(
        body, name=name, grid=(rows // tr,), in_specs=[_row(tr, cols)] * 4, out_specs=[_row(tr, cols)] * 3,
        out_shape=[shape] * 3, compiler_params=_params(1))(w, g, m, v)


W_IN_B_SHARD = 772


def _shard_view(n, a):
    return a.reshape(-1, a.shape[-1])


def _gather_weights(p, chip):
    shards = {n: _shard_view(n, p[n]) for n in BIG}
    shards["w_in_b"] = jnp.pad(shards["w_in_b"], ((0, 0), (0, W_IN_B_PAD - W_IN_B_SHARD)))
    big = _gather_big({n: _place_shard(shards[n], chip, n, f"place_{n}") for n in BIG}, "gather_weights")
    small = _gather8(_pack([p[n] for n in SMALL], R_SMALL), False, "gather_small")
    pieces = [_unpack(small[2 * k], [p[n].shape for n in SMALL]) for k in range(N_CHIPS)]
    full = {n: jnp.concatenate([pieces[k][i] for k in range(N_CHIPS)], axis=SHARD_AXIS[n])
            for i, n in enumerate(SMALL)}
    wb = big["w_in_b"].reshape(N_CHIPS, D_MODEL, W_IN_B_PAD)
    wb = jnp.concatenate([wb[k, :, :W_IN_B_SHARD] for k in range(N_CHIPS)], axis=1)
    return dict(
        w_in_a=big["w_in_a"].reshape(N_CHIPS, D_MODEL, -1), w_out_a=big["w_out_a"], w_out_b=big["w_out_b"],
        w_in_b=jnp.concatenate([wb[:, :2048], wb[:, 2064:3088], wb[:, 2048:2064],
                                jnp.zeros((D_MODEL, B_IN_PAD - 3088), BF16)], axis=1),
        w_ada=big["w_ada"].reshape(N_CHIPS, 2 * D_MODEL, -1), w_up=big["w_up"].reshape(N_CHIPS, 2 * D_MODEL, -1),
        w_down=big["w_down"],
        w_gate_b=jnp.pad(full["w_gate_b"][0], ((0, 128 - B_GATE_RANK), (0, 0))),
        b_gate_b=full["b_gate_b"], gnorm_b=full["gnorm_b"], conv_w=full["conv_w"],
        rel_bias=p["rel_bias"], norm_mix=p["norm_mix"], norm_ffn=p["norm_ffn"], b_ada=p["b_ada"],
        conv_b=p["conv_b"], norm_final=p["norm_final"].reshape(1, D_MODEL))


def _grad_views(g):
    gb = g["w_in_b"]
    gb = jnp.concatenate([gb[:, :2048], gb[:, 3072:3088], gb[:, 2048:3072]], axis=1)
    gb = jnp.pad(gb.reshape(D_MODEL, N_CHIPS, W_IN_B_SHARD).transpose(1, 0, 2),
                 ((0, 0), (0, 0), (0, W_IN_B_PAD - W_IN_B_SHARD)))
    views = {n: g[n].reshape(COMM_VIEW[n][0]) for n in BIG if n != "w_in_b"}
    views["w_in_b"] = gb.reshape(COMM_VIEW["w_in_b"][0])
    return views


def _tiny_grads(g):
    out = {n: g[n] for n in REPLICATED if n != "norm_final"}
    out.update(norm_final=g["norm_final"].reshape(D_MODEL), w_gate_b=g["w_gate_b"][:B_GATE_RANK][None],
               b_gate_b=g["b_gate_b"], gnorm_b=g["gnorm_b"], conv_w=g["conv_w"])
    return out


def kernel(x, c, w_in_a, w_out_a, rel_bias, w_in_b, w_gate_b, b_gate_b, gnorm_b, w_out_b, norm_mix, norm_ffn, w_ada, b_ada, w_up, conv_w, conv_b, w_down, norm_final, loss_target, m_w_in_a, m_w_out_a, m_rel_bias, m_w_in_b, m_w_gate_b, m_b_gate_b, m_gnorm_b, m_w_out_b, m_norm_mix, m_norm_ffn, m_w_ada, m_b_ada, m_w_up, m_conv_w, m_conv_b, m_w_down, m_norm_final, v_w_in_a, v_w_out_a, v_rel_bias, v_w_in_b, v_w_gate_b, v_b_gate_b, v_gnorm_b, v_w_out_b, v_norm_mix, v_norm_ffn, v_w_ada, v_b_ada, v_w_up, v_conv_w, v_conv_b, v_w_down, v_norm_final):
    p = dict(zip(WEIGHTS, (w_in_a, w_out_a, rel_bias, w_in_b, w_gate_b, b_gate_b, gnorm_b, w_out_b, norm_mix,
                           norm_ffn, w_ada, b_ada, w_up, conv_w, conv_b, w_down, norm_final)))
    pm = dict(zip(WEIGHTS, (m_w_in_a, m_w_out_a, m_rel_bias, m_w_in_b, m_w_gate_b, m_b_gate_b, m_gnorm_b, m_w_out_b,
                            m_norm_mix, m_norm_ffn, m_w_ada, m_b_ada, m_w_up, m_conv_w, m_conv_b, m_w_down,
                            m_norm_final)))
    pv = dict(zip(WEIGHTS, (v_w_in_a, v_w_out_a, v_rel_bias, v_w_in_b, v_w_gate_b, v_b_gate_b, v_gnorm_b, v_w_out_b,
                            v_norm_mix, v_norm_ffn, v_w_ada, v_b_ada, v_w_up, v_conv_w, v_conv_b, v_w_down,
                            v_norm_final)))
    S = x.shape[1]

    chip = 2 * lax.axis_index("x") + lax.axis_index("y")
    core = lax.axis_index("c").astype(jnp.int32).reshape(1)
    chip_s = chip.astype(jnp.int32).reshape(1)

    w = _gather_weights(p, chip_s)
    loss, dx0, grads = _local_step(x.reshape(S, D_MODEL), c, loss_target.reshape(S, D_MODEL), w)

    views = _grad_views(grads)
    recv = _rs_pair_exchange(views, "grads_pair_exchange")
    pair = {n: _pair_add(views[n], recv[n], core, n, BF16, f"grads_pair_add_{n}") for n in BIG}
    from_chips = _rs_chip_exchange(pair, "grads_chip_exchange")
    g_big = _rs_pair_gather({n: _sum4(from_chips[n], pair[n], chip_s, core, f"grads_chip_sum_{n}") for n in BIG},
                            "grads_pair_gather")
    g_big["w_in_b"] = g_big["w_in_b"][:, :W_IN_B_SHARD]

    tiny = _tiny_grads(grads)
    tiny_names = SMALL + REPLICATED
    tiny_full = {n: SMALL_FULL[n] if n in SMALL_FULL else p[n].shape for n in tiny_names}
    tiny_sum = _gather8(_pack([tiny[n] for n in tiny_names] + [loss[0, 0:1]], R_TINY), True, "grads_tiny_sum")
    g_tiny = dict(zip(tiny_names, _unpack(tiny_sum, [tiny_full[n] for n in tiny_names])))
    for n in SMALL:
        width = p[n].shape[SHARD_AXIS[n]]
        g_tiny[n] = lax.dynamic_slice_in_dim(g_tiny[n], chip * width, width, axis=SHARD_AXIS[n])
    total_loss = tiny_sum.reshape(-1)[LOSS_SLOT]

    out = {}
    for n in BIG:
        res = _adamw(_shard_view(n, p[n]), g_big[n], _shard_view(n, pm[n]), _shard_view(n, pv[n]), f"adamw_{n}")
        out[n] = [t.reshape(p[n].shape) for t in (g_big[n],) + tuple(res)]
    res = _adamw(*[_pack([d[n] for n in tiny_names], R_TINY_SHARD) for d in (p, g_tiny, pm, pv)], "adamw_tiny")
    unpacked = [_unpack(t, [p[n].shape for n in tiny_names]) for t in res]
    for i, n in enumerate(tiny_names):
        out[n] = [g_tiny[n]] + [u[i] for u in unpacked]

    return (total_loss, dx0.reshape(x.shape), *[out[n][0] for n in WEIGHTS], *[out[n][1] for n in WEIGHTS],
            *[out[n][2] for n in WEIGHTS], *[out[n][3] for n in WEIGHTS])
```

```python
import functools
import math

import numpy as np
import jax
import jax.numpy as jnp
from jax import lax
from jax.experimental import pallas as pl
from jax.experimental.pallas import tpu as pltpu

F32 = jnp.float32
BF16 = jnp.bfloat16
MESH = pl.DeviceIdType.MESH

D_MODEL = 1024
A_CONFIGS = ((128, 1), (512, 4), (2048, 16))
A_HEADS = 16
A_HEAD_DIM = 64
A_BLK = 128
N_BUCKETS = 32
MAX_DISTANCE = 2048
B_HEADS = 4
B_DK = 128
B_DV = 256
B_QK = 512
B_V = 1024
B_GATE_RANK = 16
B_TAU = 16.0
B_CHUNK = 64
B_IN_PAD = 3200
D_FF = 2816
EPS = 1e-6
NEG_INF = -1e30
ADAM_LR = 0.001
ADAM_B1 = 0.9
ADAM_B2 = 0.999
ADAM_EPS = 1e-08
ADAM_WD = 0.01
ADAM_STEP = 10

LANES = 1024
VMEM_LIMIT = 48 * 1024 * 1024
ROW_TILE = 256
GLA_ROWS = 512

HBM = pl.BlockSpec(memory_space=pl.ANY)


def _params(n_axes):
    return pltpu.CompilerParams(dimension_semantics=("arbitrary",) * n_axes, vmem_limit_bytes=VMEM_LIMIT)


def _pick(n, cap, mult=128):
    best = None
    for t in range(mult, min(n, cap) + 1, mult):
        if n % t == 0:
            best = t
    return n if best is None else best


def _matmul(a, b, mode, out_dtype, name, shape=None, tiles=None, a_spec=None, b_spec=None, o_spec=None, o_shape=None,
            prev=None):
    dims = {"nn": (((1,), (0,)), ((), ())), "nt": NT, "tn": TN}[mode]
    if shape is None:
        if mode == "nn":
            (M, K), (_, N) = a.shape, b.shape
        elif mode == "nt":
            (M, K), (N, _) = a.shape, b.shape
        else:
            (K, M), (_, N) = a.shape, b.shape
    else:
        M, N, K = shape
    if tiles is None:
        tiles = (_pick(M, 1024, 128 if mode == "tn" else 8), _pick(N, 1536), _pick(K, 1024 if mode != "tn" else 2048))
    tm, tn, tk = tiles
    nk = K // tk
    if a_spec is None:
        a_spec = pl.BlockSpec((tk, tm), lambda i, j, k: (k, i)) if mode == "tn" else pl.BlockSpec(
            (tm, tk), lambda i, j, k: (i, k))
    if b_spec is None:
        b_spec = pl.BlockSpec((tn, tk), lambda i, j, k: (j, k)) if mode == "nt" else pl.BlockSpec(
            (tk, tn), lambda i, j, k: (k, j))
    if o_spec is None:
        o_spec = pl.BlockSpec((tm, tn), lambda i, j, k: (i, j))
        o_shape = (M, N)

    def body(a_ref, b_ref, *rest):
        o_ref, acc_ref = rest[-2:]
        k = pl.program_id(2)
        part = lax.dot_general(a_ref[...].astype(BF16), b_ref[...].astype(BF16), dims, preferred_element_type=F32)

        @pl.when(k == 0)
        def _():
            acc_ref[...] = part

        @pl.when(k > 0)
        def _():
            acc_ref[...] += part

        @pl.when(k == nk - 1)
        def _():
            o_ref[...] = acc_ref[...].astype(o_ref.dtype)

    ins, in_specs, aliases = [a, b], [a_spec, b_spec], {}
    if prev is not None:
        ins.append(prev)
        in_specs.append(HBM)
        aliases = {2: 0}
    return pl.pallas_call(
        body, name=name, grid=(M // tm, N // tn, nk), in_specs=in_specs, out_specs=o_spec,
        out_shape=jax.ShapeDtypeStruct(o_shape, out_dtype), scratch_shapes=[pltpu.VMEM((tm, tn), F32)],
        input_output_aliases=aliases, compiler_params=_params(3))(*ins)


def _row(tr, d=D_MODEL):
    return pl.BlockSpec((tr, d), lambda i: (i, 0))


def _vec(d=D_MODEL):
    return pl.BlockSpec((1, d), lambda i: (0, 0))


def _modspec(j):
    return pl.BlockSpec((8, D_MODEL), lambda i: (0, j))


def _silu(x):
    return x * jax.nn.sigmoid(x)


def _dsilu(x):
    s = jax.nn.sigmoid(x)
    return s * (1.0 + x * (1.0 - s))


ADA_TN = 6 * D_MODEL // 4


def _ada_mod(c, w_ada, b_ada, layer, name):
    def body(c_ref, w_ref, b_ref, o_ref):
        sc = jnp.broadcast_to(_silu(c_ref[...]), (8, D_MODEL)).astype(BF16)
        o_ref[...] = jnp.dot(sc, w_ref[...], preferred_element_type=F32) + b_ref[...]

    return pl.pallas_call(
        body, name=name, grid=(4,),
        in_specs=[_vec(), pl.BlockSpec((None, D_MODEL, ADA_TN), lambda j: (j, layer, 0)),
                  pl.BlockSpec((1, ADA_TN), lambda j: (0, j))],
        out_specs=pl.BlockSpec((8, ADA_TN), lambda j: (0, j)), out_shape=jax.ShapeDtypeStruct((8, 6 * D_MODEL), F32),
        compiler_params=_params(1))(c, w_ada, b_ada)


def _ada_outer(c, dmods, name):
    def body(c_ref, d_ref, o_ref):
        row = lax.broadcasted_iota(jnp.int32, (8, 1), 0) == 0
        a = jnp.where(row, jnp.broadcast_to(_silu(c_ref[...]), (8, D_MODEL)), 0.0).astype(BF16)
        b = jnp.where(row, jnp.broadcast_to(d_ref[...], (8, ADA_TN)), 0.0).astype(BF16)
        o_ref[...] = lax.dot_general(a, b, (((0,), (0,)), ((), ())), preferred_element_type=F32)

    return pl.pallas_call(
        body, name=name, grid=(2, 4),
        in_specs=[pl.BlockSpec((1, D_MODEL), lambda l, j: (0, 0)),
                  pl.BlockSpec((None, 1, ADA_TN), lambda l, j: (l, 0, j))],
        out_specs=pl.BlockSpec((None, D_MODEL, ADA_TN), lambda l, j: (j, l, 0)),
        out_shape=jax.ShapeDtypeStruct((4, 2 * D_MODEL, ADA_TN), F32), compiler_params=_params(2))(c, dmods)


def _view_spec(tr, d, width=D_MODEL):
    return pl.BlockSpec((tr // d, d * width), lambda i: (i, 0))


def _view_shape(S, d, dtype, width=D_MODEL):
    return jax.ShapeDtypeStruct((S // d, d * width), dtype)


LANE_TILE = 128
N_LANE_TILES = D_MODEL // LANE_TILE


def _token_scratch(tr):
    return pltpu.VMEM((N_LANE_TILES, tr, LANE_TILE), F32)


def _scratch_put(scr_ref, val):
    for c in range(N_LANE_TILES):
        scr_ref[c] = val[:, c * LANE_TILE:(c + 1) * LANE_TILE]


def _scratch_get(scr_ref):
    return jnp.concatenate([scr_ref[c] for c in range(N_LANE_TILES)], axis=1)


def _store_view(scr_ref, out_ref, d):
    n = scr_ref.shape[1] // d
    for r in range(d):
        for c in range(N_LANE_TILES):
            lo = r * D_MODEL + c * LANE_TILE
            out_ref[:, lo:lo + LANE_TILE] = scr_ref.at[c][pl.ds(r, n, stride=d), :].astype(out_ref.dtype)


def _load_view(view_ref, scr_ref, d):
    n = scr_ref.shape[1] // d
    for r in range(d):
        for c in range(N_LANE_TILES):
            lo = r * D_MODEL + c * LANE_TILE
            scr_ref.at[c][pl.ds(r, n, stride=d), :] = view_ref[:, lo:lo + LANE_TILE].astype(F32)


def _norm_mod(x, gamma, mod, j_sc, j_sh, name, resid=None, gate_mod=None, j_gate=None, views=False):
    S = x.shape[0]
    tr = ROW_TILE
    has_res = resid is not None

    def body(*refs):
        if has_res:
            x_ref, y_ref, gate_ref, g_ref, sc_ref, sh_ref, xo_ref, h_ref = refs
            xn = x_ref[...] + gate_ref[0:1, :] * y_ref[...]
            xo_ref[...] = xn
        elif views:
            x_ref, g_ref, sc_ref, sh_ref, h_ref, h4_ref, h16_ref, scr_ref = refs
            xn = x_ref[...]
        else:
            x_ref, g_ref, sc_ref, sh_ref, h_ref = refs
            xn = x_ref[...]
        r = lax.rsqrt(jnp.mean(xn * xn, axis=-1, keepdims=True) + EPS)
        n = (xn * r) * g_ref[...]
        h = n * (1.0 + sc_ref[0:1, :]) + sh_ref[0:1, :]
        h_ref[...] = h.astype(BF16)
        if views:
            _scratch_put(scr_ref, h)
            _store_view(scr_ref, h4_ref, 4)
            _store_view(scr_ref, h16_ref, 16)

    scratch = []
    if has_res:
        ins = [x, resid, gate_mod, gamma, mod, mod]
        in_specs = [_row(tr), _row(tr), _modspec(j_gate), _vec(), _modspec(j_sc), _modspec(j_sh)]
        out_specs = [_row(tr), _row(tr)]
        out_shape = [jax.ShapeDtypeStruct((S, D_MODEL), F32), jax.ShapeDtypeStruct((S, D_MODEL), BF16)]
    else:
        ins = [x, gamma, mod, mod]
        in_specs = [_row(tr), _vec(), _modspec(j_sc), _modspec(j_sh)]
        out_specs = _row(tr)
        out_shape = jax.ShapeDtypeStruct((S, D_MODEL), BF16)
        if views:
            out_specs = [_row(tr), _view_spec(tr, 4), _view_spec(tr, 16)]
            out_shape = [out_shape, _view_shape(S, 4, BF16), _view_shape(S, 16, BF16)]
            scratch = [_token_scratch(tr)]
    return pl.pallas_call(body, name=name, grid=(S // tr,), in_specs=in_specs, out_specs=out_specs,
                          out_shape=out_shape, scratch_shapes=scratch, compiler_params=_params(1))(*ins)


def _final_loss(x, resid, mod, j_gate, gamma, tgt, name):
    S = x.shape[0]
    tr = ROW_TILE

    def body(x_ref, y_ref, gate_ref, g_ref, t_ref, dx_ref, loss_ref, dg_ref):
        @pl.when(pl.program_id(0) == 0)
        def _():
            loss_ref[...] = jnp.zeros_like(loss_ref)
            dg_ref[...] = jnp.zeros_like(dg_ref)

        xn = x_ref[...] + gate_ref[0:1, :] * y_ref[...]
        r = lax.rsqrt(jnp.mean(xn * xn, axis=-1, keepdims=True) + EPS)
        xhat = xn * r
        err = xhat * g_ref[...] - t_ref[...]
        loss_ref[...] += 0.5 * jnp.sum(jnp.mean(err * err, axis=-1, keepdims=True))
        dy = err * (1.0 / D_MODEL)
        dg_ref[...] += jnp.sum(dy * xhat, axis=0, keepdims=True)
        dxh = dy * g_ref[...]
        dx_ref[...] = r * (dxh - xhat * jnp.mean(dxh * xhat, axis=-1, keepdims=True))

    return pl.pallas_call(
        body, name=name, grid=(S // tr,),
        in_specs=[_row(tr), _row(tr), _modspec(j_gate), _vec(), _row(tr)],
        out_specs=[_row(tr), pl.BlockSpec((1, 128), lambda i: (0, 0)), _vec()],
        out_shape=[jax.ShapeDtypeStruct((S, D_MODEL), F32), jax.ShapeDtypeStruct((1, 128), F32),
                   jax.ShapeDtypeStruct((1, D_MODEL), F32)],
        compiler_params=_params(1))(x, resid, mod, gamma, tgt)


def _gate_bwd(dx, y, mod, j_gate, name):
    S = dx.shape[0]
    tr = ROW_TILE

    def body(dx_ref, y_ref, gate_ref, dy_ref, dg_ref):
        @pl.when(pl.program_id(0) == 0)
        def _():
            dg_ref[...] = jnp.zeros_like(dg_ref)

        dx_v = dx_ref[...]
        dy_ref[...] = (gate_ref[0:1, :] * dx_v).astype(BF16)
        dg_ref[...] += jnp.sum(dx_v * y_ref[...], axis=0, keepdims=True)

    return pl.pallas_call(
        body, name=name, grid=(S // tr,), in_specs=[_row(tr), _row(tr), _modspec(j_gate)],
        out_specs=[_row(tr), _vec()],
        out_shape=[jax.ShapeDtypeStruct((S, D_MODEL), BF16), jax.ShapeDtypeStruct((1, D_MODEL), F32)],
        compiler_params=_params(1))(dx, y, mod)


def _norm_mod_bwd(dh, x, dx_res, gamma, mod, j_sc, name, dh_views=None):
    S = x.shape[0]
    tr = ROW_TILE
    n_views = 0 if dh_views is None else 2

    def body(dh_ref, *refs):
        x_ref, dr_ref, g_ref, sc_ref, dx_ref, dsc_ref, dsh_ref, dg_ref = refs[n_views:n_views + 8]

        @pl.when(pl.program_id(0) == 0)
        def _():
            dsc_ref[...] = jnp.zeros_like(dsc_ref)
            dsh_ref[...] = jnp.zeros_like(dsh_ref)
            dg_ref[...] = jnp.zeros_like(dg_ref)

        xv = x_ref[...]
        dh_v = dh_ref[...]
        if n_views:
            scr_ref = refs[-1]
            for view_ref, d in zip(refs[:2], (4, 16)):
                _load_view(view_ref, scr_ref, d)
                dh_v = dh_v + _scratch_get(scr_ref)
        r = lax.rsqrt(jnp.mean(xv * xv, axis=-1, keepdims=True) + EPS)
        xhat = xv * r
        dsh_ref[...] += jnp.sum(dh_v, axis=0, keepdims=True)
        dsc_ref[...] += jnp.sum(dh_v * (xhat * g_ref[...]), axis=0, keepdims=True)
        dn = dh_v * (1.0 + sc_ref[0:1, :])
        dg_ref[...] += jnp.sum(dn * xhat, axis=0, keepdims=True)
        dxh = dn * g_ref[...]
        dx_ref[...] = dr_ref[...] + r * (dxh - xhat * jnp.mean(dxh * xhat, axis=-1, keepdims=True))

    vec = jax.ShapeDtypeStruct((1, D_MODEL), F32)
    views = [] if dh_views is None else list(dh_views)
    view_specs = [_view_spec(tr, 4), _view_spec(tr, 16)] if views else []
    return pl.pallas_call(
        body, name=name, grid=(S // tr,),
        in_specs=[_row(tr)] + view_specs + [_row(tr), _row(tr), _vec(), _modspec(j_sc)],
        out_specs=[_row(tr), _vec(), _vec(), _vec()],
        out_shape=[jax.ShapeDtypeStruct((S, D_MODEL), F32), vec, vec, vec],
        scratch_shapes=[_token_scratch(tr)] if views else [],
        compiler_params=_params(1))(dh, *views, x, dx_res, gamma, mod)


def _shift_down(u, halo, s):
    r = pltpu.roll(u, s, 0)
    hr = pltpu.roll(halo, s, 0)
    rid = lax.broadcasted_iota(jnp.int32, hr.shape, 0)
    top = jnp.where(rid < s, hr, r[0:8])
    return jnp.concatenate([top, r[8:]], axis=0)


def _shift_up(u, halo, s):
    n = u.shape[0]
    r = pltpu.roll(u, n - s, 0)
    hr = pltpu.roll(halo, 8 - s, 0)
    rid = lax.broadcasted_iota(jnp.int32, hr.shape, 0)
    bot = jnp.where(rid >= 8 - s, hr, r[n - 8:])
    return jnp.concatenate([r[:n - 8], bot], axis=0)


def _conv3(u, halo, w_ref, b_ref):
    u1 = _shift_down(u, halo, 1)
    u2 = _shift_down(u, halo, 2)
    return b_ref[...] + ((w_ref[0:1, :] * u2 + w_ref[1:2, :] * u1) + w_ref[2:3, :] * u), u1, u2


CONV_TC = 1408


def _conv_specs(tr, S):
    nh = D_FF // CONV_TC
    hb = tr // 8

    def cur(off):
        return pl.BlockSpec((tr, CONV_TC), lambda j, i: (i, j + off))

    def halo(off):
        return pl.BlockSpec((8, CONV_TC), lambda j, i: (jnp.maximum(i * hb - 1, 0), j + off))

    def w(off):
        return pl.BlockSpec((3, CONV_TC), lambda j, i: (0, j + off))

    def b(off):
        return pl.BlockSpec((1, CONV_TC), lambda j, i: (0, j + off))

    return nh, cur, halo, w, b


def _conv_gate(u, conv_w, conv_b, name):
    S = u.shape[0]
    tr = ROW_TILE
    nh, cur, halo, w, b = _conv_specs(tr, S)

    def body(ua_ref, ha_ref, ub_ref, hb_ref, wa_ref, wb_ref, ba_ref, bb_ref, o_ref):
        first = pl.program_id(1) == 0
        ha = jnp.where(first, 0.0, ha_ref[...])
        hbv = jnp.where(first, 0.0, hb_ref[...])
        a, _, _ = _conv3(ua_ref[...], ha, wa_ref, ba_ref)
        bb, _, _ = _conv3(ub_ref[...], hbv, wb_ref, bb_ref)
        o_ref[...] = (_silu(a) * bb).astype(BF16)

    return pl.pallas_call(
        body, name=name, grid=(nh, S // tr),
        in_specs=[cur(0), halo(0), cur(nh), halo(nh), w(0), w(nh), b(0), b(nh)],
        out_specs=pl.BlockSpec((tr, CONV_TC), lambda j, i: (i, j)),
        out_shape=jax.ShapeDtypeStruct((S, D_FF), BF16), compiler_params=_params(2))(
            u, u, u, u, conv_w, conv_w, conv_b, conv_b)


def _conv_gate_bwd(u, dact, conv_w, conv_b, name):
    S = u.shape[0]
    tr = ROW_TILE
    nh, cur, halo, w, b = _conv_specs(tr, S)

    def body(ua_ref, ha_ref, ub_ref, hb_ref, wa_ref, wb_ref, ba_ref, bb_ref, da_ref,
             dua_ref, dub_ref, dwa_ref, dwb_ref, dba_ref, dbb_ref):
        first = pl.program_id(1) == 0

        @pl.when(first)
        def _():
            for r in (dwa_ref, dwb_ref, dba_ref, dbb_ref):
                r[...] = jnp.zeros_like(r)

        ha = jnp.where(first, 0.0, ha_ref[...])
        hbv = jnp.where(first, 0.0, hb_ref[...])
        ua, ub = ua_ref[...], ub_ref[...]
        a, ua1, ua2 = _conv3(ua, ha, wa_ref, ba_ref)
        bb, ub1, ub2 = _conv3(ub, hbv, wb_ref, bb_ref)
        dact_v = da_ref[...]
        da = dact_v * bb * _dsilu(a)
        db = dact_v * _silu(a)
        dua_ref[...] = da
        dub_ref[...] = db
        for d, x0, x1, x2, dw_ref, dbias_ref in ((da, ua, ua1, ua2, dwa_ref, dba_ref),
                                                 (db, ub, ub1, ub2, dwb_ref, dbb_ref)):
            dbias_ref[...] += jnp.sum(d, axis=0, keepdims=True)
            dw_ref[0:1, :] += jnp.sum(d * x2, axis=0, keepdims=True)
            dw_ref[1:2, :] += jnp.sum(d * x1, axis=0, keepdims=True)
            dw_ref[2:3, :] += jnp.sum(d * x0, axis=0, keepdims=True)

    half = pl.BlockSpec((tr, CONV_TC), lambda j, i: (i, j))
    dw = pl.BlockSpec((8, CONV_TC), lambda j, i: (0, j))
    dbs = pl.BlockSpec((1, CONV_TC), lambda j, i: (0, j))
    f = lambda r, c: jax.ShapeDtypeStruct((r, c), F32)
    return pl.pallas_call(
        body, name=name, grid=(nh, S // tr),
        in_specs=[cur(0), halo(0), cur(nh), halo(nh), w(0), w(nh), b(0), b(nh), half],
        out_specs=[half, half, dw, dw, dbs, dbs],
        out_shape=[f(S, D_FF), f(S, D_FF), f(8, D_FF), f(8, D_FF), f(1, D_FF), f(1, D_FF)],
        compiler_params=_params(2))(u, u, u, u, conv_w, conv_w, conv_b, conv_b, dact)


def _conv_t(duc, conv_w, half, prev, name):
    S = duc.shape[0]
    tr = ROW_TILE
    nh = D_FF // CONV_TC
    hb = tr // 8
    nlast = S // 8 - 1
    nsteps = S // tr
    off = half * nh

    def body(*refs):
        if prev is None:
            d_ref, h_ref, w_ref, o_ref = refs
        else:
            d_ref, h_ref, w_ref, _, o_ref = refs
        last = pl.program_id(1) == nsteps - 1
        hv = jnp.where(last, 0.0, h_ref[...])
        d = d_ref[...]
        d1 = _shift_up(d, hv, 1)
        d2 = _shift_up(d, hv, 2)
        o_ref[...] = ((w_ref[2:3, :] * d + w_ref[1:2, :] * d1) + w_ref[0:1, :] * d2).astype(BF16)

    in_specs = [pl.BlockSpec((tr, CONV_TC), lambda j, i: (i, j)),
                pl.BlockSpec((8, CONV_TC), lambda j, i: (jnp.minimum((i + 1) * hb, nlast), j)),
                pl.BlockSpec((3, CONV_TC), lambda j, i: (0, j + off))]
    ins = [duc, duc, conv_w]
    aliases = {}
    if prev is not None:
        in_specs.append(HBM)
        ins.append(prev)
        aliases = {3: 0}
    return pl.pallas_call(
        body, name=name, grid=(nh, nsteps), in_specs=in_specs,
        out_specs=pl.BlockSpec((tr, CONV_TC), lambda j, i: (i, j + off)),
        out_shape=jax.ShapeDtypeStruct((S, 2 * D_FF), BF16), input_output_aliases=aliases,
        compiler_params=_params(2))(*ins)


def _bucket_maps():
    qi = np.arange(A_BLK)[:, None]
    ki = np.arange(2 * A_BLK)[None, :]
    steps = np.clip(qi + A_BLK - ki, 0, A_BLK)
    out = []
    max_exact = N_BUCKETS // 2
    for _, dil in A_CONFIGS:
        dist = steps * dil
        n = np.maximum(dist, max_exact).astype(np.float32)
        large = max_exact + (np.log(n / np.float32(max_exact)) / np.float32(math.log(MAX_DISTANCE / max_exact))
                             * np.float32(N_BUCKETS - max_exact)).astype(np.int32)
        large = np.minimum(large, N_BUCKETS - 1)
        out.append(np.where(dist < max_exact, dist, large))
    return np.stack(out).astype(np.int32)


def _bias_build(rel_bias, buckets, name):
    ng = len(A_CONFIGS)

    def body(t_ref, bk_ref, o_ref):
        gh = pl.program_id(0) * A_HEADS + pl.program_id(1)
        bk = bk_ref[0]
        acc = jnp.zeros((A_BLK, 2 * A_BLK), F32)
        for b in range(N_BUCKETS):
            acc = jnp.where(bk == b, t_ref[b, gh], acc)
        o_ref[0] = acc

    return pl.pallas_call(
        body, name=name, grid=(ng, A_HEADS),
        in_specs=[pl.BlockSpec(memory_space=pltpu.SMEM), pl.BlockSpec((1, A_BLK, 2 * A_BLK), lambda g, h: (g, 0, 0))],
        out_specs=pl.BlockSpec((1, A_BLK, 2 * A_BLK), lambda g, h: (g * A_HEADS + h, 0, 0)),
        out_shape=jax.ShapeDtypeStruct((ng * A_HEADS, A_BLK, 2 * A_BLK), F32),
        compiler_params=_params(2))(rel_bias, buckets)


def _bias_bwd(dbias, buckets, name):
    ng = len(A_CONFIGS)

    def body(d_ref, bk_ref, o_ref):
        gh = pl.program_id(0) * A_HEADS + pl.program_id(1)
        bk = bk_ref[0]
        d = d_ref[0]
        for b in range(N_BUCKETS):
            o_ref[b, gh] = jnp.sum(jnp.where(bk == b, d, 0.0))

    return pl.pallas_call(
        body, name=name, grid=(ng, A_HEADS),
        in_specs=[pl.BlockSpec((1, A_BLK, 2 * A_BLK), lambda g, h: (g * A_HEADS + h, 0, 0)),
                  pl.BlockSpec((1, A_BLK, 2 * A_BLK), lambda g, h: (g, 0, 0))],
        out_specs=pl.BlockSpec(memory_space=pltpu.SMEM),
        out_shape=jax.ShapeDtypeStruct((N_BUCKETS, ng * A_HEADS), F32),
        compiler_params=_params(2))(dbias, buckets)


def _attn_mask(b):
    qi = lax.broadcasted_iota(jnp.int32, (A_BLK, 2 * A_BLK), 0)
    ki = lax.broadcasted_iota(jnp.int32, (A_BLK, 2 * A_BLK), 1)
    band = (ki >= qi) & (ki <= qi + A_BLK)
    return band & ((b > 0) | (ki >= A_BLK))


def _attn_in_specs(g, dil):
    W = A_HEADS * A_HEAD_DIM

    def spec(t, prev, nb):
        def im(r, b):
            bb = jnp.minimum(b, nb - 1)
            if prev:
                bb = jnp.maximum(bb - 1, 0)
            return (bb, r * 3 + t)
        return pl.BlockSpec((A_BLK, W), im)

    return lambda nb: [spec(0, False, nb), spec(1, False, nb), spec(1, True, nb), spec(2, False, nb),
                       spec(2, True, nb)]


def _attn_fwd(qv, bias, g, name):
    _, dil = A_CONFIGS[g]
    L = qv.shape[0]
    nb = L // A_BLK
    W = A_HEADS * A_HEAD_DIM

    def body(q_ref, kc_ref, kp_ref, vc_ref, vp_ref, bias_ref, o_ref, l_ref):
        mask = _attn_mask(pl.program_id(1))
        for h in range(A_HEADS):
            hs = slice(h * A_HEAD_DIM, (h + 1) * A_HEAD_DIM)
            qh = q_ref[:, hs] * 0.125
            k2 = jnp.concatenate([kp_ref[:, hs], kc_ref[:, hs]], axis=0)
            v2 = jnp.concatenate([vp_ref[:, hs], vc_ref[:, hs]], axis=0)
            s = lax.dot_general(qh, k2, (((1,), (1,)), ((), ())), preferred_element_type=F32) + bias_ref[h]
            s = jnp.where(mask, s, NEG_INF)
            m = jnp.max(s, axis=-1, keepdims=True)
            p = jnp.exp(s - m)
            den = jnp.sum(p, axis=-1, keepdims=True)
            o = jnp.dot(p.astype(BF16), v2, preferred_element_type=F32) / den
            o_ref[:, hs] = o
            l_ref[:, hs] = jnp.broadcast_to(m + jnp.log(den), (A_BLK, A_HEAD_DIM))

    out_spec = pl.BlockSpec((A_BLK, W), lambda r, b: (b, r))
    return pl.pallas_call(
        body, name=name, grid=(dil, nb),
        in_specs=_attn_in_specs(g, dil)(nb) + [pl.BlockSpec((A_HEADS, A_BLK, 2 * A_BLK), lambda r, b: (g, 0, 0))],
        out_specs=[out_spec, out_spec],
        out_shape=[jax.ShapeDtypeStruct((L, dil * W), F32)] * 2,
        compiler_params=_params(2))(qv, qv, qv, qv, qv, bias)


def _mix_fwd(os, ls, name):
    S = os[0].shape[0]
    tr = ROW_TILE

    def body(o0, o1, o2, l0, l1, l2, om_ref, om4_ref, om16_ref, lt_ref, lt4_ref, lt16_ref, s_o1, s_o2, s_l1, s_l2):
        for view_ref, scr_ref, d in ((o1, s_o1, 4), (o2, s_o2, 16), (l1, s_l1, 4), (l2, s_l2, 16)):
            _load_view(view_ref, scr_ref, d)
        a, b, c = l0[...], _scratch_get(s_l1), _scratch_get(s_l2)
        m = jnp.maximum(jnp.maximum(a, b), c)
        ea, eb, ec = jnp.exp(a - m), jnp.exp(b - m), jnp.exp(c - m)
        z = (ea + eb) + ec
        om = ((ea / z) * o0[...] + (eb / z) * _scratch_get(s_o1)) + (ec / z) * _scratch_get(s_o2)
        lt = m + jnp.log(z)
        om_ref[...] = om
        lt_ref[...] = lt
        _scratch_put(s_o1, om)
        _scratch_put(s_l1, lt)
        for scr_ref, v4_ref, v16_ref in ((s_o1, om4_ref, om16_ref), (s_l1, lt4_ref, lt16_ref)):
            _store_view(scr_ref, v4_ref, 4)
            _store_view(scr_ref, v16_ref, 16)

    ins = [_row(tr), _view_spec(tr, 4), _view_spec(tr, 16)]
    outs = [jax.ShapeDtypeStruct((S, D_MODEL), F32), _view_shape(S, 4, F32), _view_shape(S, 16, F32)]
    return pl.pallas_call(
        body, name=name, grid=(S // tr,), in_specs=ins * 2, out_specs=ins * 2, out_shape=outs * 2,
        scratch_shapes=[_token_scratch(tr)] * 4, compiler_params=_params(1))(*os, *ls)


def _to_views(x, name):
    S = x.shape[0]
    tr = ROW_TILE

    def body(x_ref, v4_ref, v16_ref, scr_ref):
        _scratch_put(scr_ref, x_ref[...])
        _store_view(scr_ref, v4_ref, 4)
        _store_view(scr_ref, v16_ref, 16)

    return pl.pallas_call(
        body, name=name, grid=(S // tr,), in_specs=[_row(tr)], out_specs=[_view_spec(tr, 4), _view_spec(tr, 16)],
        out_shape=[_view_shape(S, 4, F32), _view_shape(S, 16, F32)], scratch_shapes=[_token_scratch(tr)],
        compiler_params=_params(1))(x)


def _attn_bwd(qv, bias, d_o, omix, ltot, g, name):
    _, dil = A_CONFIGS[g]
    L = qv.shape[0]
    nb = L // A_BLK
    W = A_HEADS * A_HEAD_DIM

    def body(q_ref, kc_ref, kp_ref, vc_ref, vp_ref, bias_ref, do_ref, om_ref, lt_ref,
             dqkv_ref, db_ref, cq_ref, ck_ref, cv_ref):
        r, b = pl.program_id(0), pl.program_id(1)
        dq_ref, dk_ref, dv_ref = (dqkv_ref.at[:, t * W:(t + 1) * W] for t in range(3))

        @pl.when((r == 0) & (b == 0))
        def _():
            db_ref[...] = jnp.zeros_like(db_ref)

        @pl.when(b == 0)
        def _():
            cq_ref[...] = jnp.zeros_like(cq_ref)
            ck_ref[...] = jnp.zeros_like(ck_ref)
            cv_ref[...] = jnp.zeros_like(cv_ref)

        dq_ref[...] = cq_ref[...]

        @pl.when(b < nb)
        def _():
            mask = _attn_mask(b)
            for h in range(A_HEADS):
                hs = slice(h * A_HEAD_DIM, (h + 1) * A_HEAD_DIM)
                qh = q_ref[:, hs] * 0.125
                k2 = jnp.concatenate([kp_ref[:, hs], kc_ref[:, hs]], axis=0)
                v2 = jnp.concatenate([vp_ref[:, hs], vc_ref[:, hs]], axis=0)
                s = lax.dot_general(qh, k2, (((1,), (1,)), ((), ())), preferred_element_type=F32) + bias_ref[h]
                s = jnp.where(mask, s, NEG_INF)
                wp = jnp.exp(s - lt_ref[:, h * A_HEAD_DIM:h * A_HEAD_DIM + 1])
                do_h = do_ref[:, hs]
                t_h = jnp.sum(do_h * om_ref[:, hs], axis=-1, keepdims=True)
                do_b = do_h.astype(BF16)
                dp = lax.dot_general(do_b, v2, (((1,), (1,)), ((), ())), preferred_element_type=F32)
                ds = wp * (dp - t_h)
                db_ref[h] += ds
                ds_b = ds.astype(BF16)
                dv2 = lax.dot_general(wp.astype(BF16), do_b, (((0,), (0,)), ((), ())), preferred_element_type=F32)
                dk2 = lax.dot_general(ds_b, qh, (((0,), (0,)), ((), ())), preferred_element_type=F32)
                cq_ref[:, hs] = (jnp.dot(ds_b, k2, preferred_element_type=F32) * 0.125).astype(BF16)
                dk_ref[:, hs] = (ck_ref[:, hs] + dk2[:A_BLK]).astype(BF16)
                dv_ref[:, hs] = (cv_ref[:, hs] + dv2[:A_BLK]).astype(BF16)
                ck_ref[:, hs] = dk2[A_BLK:]
                cv_ref[:, hs] = dv2[A_BLK:]

        @pl.when(b == nb)
        def _():
            dk_ref[...] = ck_ref[...].astype(BF16)
            dv_ref[...] = cv_ref[...].astype(BF16)

    act = pl.BlockSpec((A_BLK, W), lambda r, b: (jnp.minimum(b, nb - 1), r))
    lag = pl.BlockSpec((A_BLK, 3 * W), lambda r, b: (jnp.maximum(b - 1, 0), r))
    full = pl.BlockSpec((A_HEADS, A_BLK, 2 * A_BLK), lambda r, b: (0, 0, 0))
    in_specs = _attn_in_specs(g, dil)(nb) + [pl.BlockSpec((A_HEADS, A_BLK, 2 * A_BLK), lambda r, b: (g, 0, 0)),
                                             act, act, act]
    return pl.pallas_call(
        body, name=name, grid=(dil, nb + 1), in_specs=in_specs, out_specs=[lag, full],
        out_shape=[jax.ShapeDtypeStruct((L, dil * 3 * W), BF16),
                   jax.ShapeDtypeStruct((A_HEADS, A_BLK, 2 * A_BLK), F32)],
        scratch_shapes=[pltpu.VMEM((A_BLK, W), BF16), pltpu.VMEM((A_BLK, W), F32), pltpu.VMEM((A_BLK, W), F32)],
        compiler_params=_params(2))(qv, qv, qv, qv, qv, bias, d_o, omix, ltot)


NT = (((1,), (1,)), ((), ()))
TN = (((0,), (0,)), ((), ()))


def _dot(a, b, dims=(((1,), (0,)), ((), ()))):
    return lax.dot_general(a.astype(BF16), b.astype(BF16), dims, preferred_element_type=F32)


def _gla_gates(glr, wg_ref, bg_ref):
    z = _dot(glr, wg_ref[...]) + bg_ref[...]
    log_sig = -(jnp.maximum(-z, 0.0) + jnp.log1p(jnp.exp(-jnp.abs(z))))
    return z, log_sig / B_TAU


def _gla_chunk(q, k, gk):
    row = lax.broadcasted_iota(jnp.int32, (B_CHUNK, B_CHUNK), 0)
    col = lax.broadcasted_iota(jnp.int32, (B_CHUNK, B_CHUNK), 1)
    causal = row >= col
    bcum = jnp.dot(causal.astype(F32), gk, precision=lax.Precision.HIGHEST, preferred_element_type=F32)
    bl = bcum[B_CHUNK - 1:B_CHUNK, :]
    qt = (q * (B_DK ** -0.5)) * jnp.exp(bcum)
    kt = k * jnp.exp(-bcum)
    kd = k * jnp.exp(bl - bcum)
    a = jnp.where(causal, _dot(qt, kt, NT), 0.0)
    return causal, bcum, bl, qt, kt, kd, a


def _gla_specs(tg):
    q = pl.BlockSpec((tg, B_DK), lambda h, i: (i, h))
    k = pl.BlockSpec((tg, B_DK), lambda h, i: (i, B_HEADS + h))
    v = pl.BlockSpec((tg, B_DV), lambda h, i: (i, B_HEADS + h))
    glr = pl.BlockSpec((tg, 128), lambda h, i: (i, 24))
    wg = pl.BlockSpec((128, B_DK), lambda h, i: (0, h))
    bg = pl.BlockSpec((1, B_DK), lambda h, i: (0, h))
    return [q, k, v, glr, wg, bg]


def _gla_fwd(proj, w_gate, b_gate, name):
    S = proj.shape[0]
    tg = GLA_ROWS
    nc = tg // B_CHUNK

    def body(q_ref, k_ref, v_ref, glr_ref, wg_ref, bg_ref, o_ref, st_ref, state_ref):
        @pl.when(pl.program_id(1) == 0)
        def _():
            state_ref[...] = jnp.zeros_like(state_ref)

        _, gk_all = _gla_gates(glr_ref[...], wg_ref, bg_ref)
        st = state_ref[...]
        for c in range(nc):
            rows = slice(c * B_CHUNK, (c + 1) * B_CHUNK)
            v = v_ref[rows, :]
            _, _, bl, qt, _, kd, a = _gla_chunk(q_ref[rows, :], k_ref[rows, :], gk_all[rows, :])
            o_ref[rows, :] = _dot(a, v) + _dot(qt, st, NT)
            st_ref[c, 0] = st
            st = st * jnp.exp(bl) + _dot(v, kd, TN)
        state_ref[...] = st

    return pl.pallas_call(
        body, name=name, grid=(B_HEADS, S // tg), in_specs=_gla_specs(tg),
        out_specs=[pl.BlockSpec((tg, B_DV), lambda h, i: (i, h)),
                   pl.BlockSpec((nc, 1, B_DV, B_DK), lambda h, i: (i, h, 0, 0))],
        out_shape=[jax.ShapeDtypeStruct((S, B_V), F32),
                   jax.ShapeDtypeStruct((S // B_CHUNK, B_HEADS, B_DV, B_DK), F32)],
        scratch_shapes=[pltpu.VMEM((B_DV, B_DK), F32)], compiler_params=_params(2))(
            proj, proj, proj, proj, w_gate, b_gate)


def _gla_bwd(proj, w_gate, b_gate, states, d_o, name):
    S = proj.shape[0]
    tg = GLA_ROWS
    nc = tg // B_CHUNK
    ni = S // tg

    def rev(spec):
        return pl.BlockSpec(spec.block_shape, lambda h, i, im=spec.index_map: im(h, ni - 1 - i))

    def body(q_ref, k_ref, v_ref, glr_ref, wg_ref, bg_ref, st_ref, do_ref,
             dq_ref, dk_ref, dv_ref, dz_ref, dbg_ref, dstate_ref):
        @pl.when(pl.program_id(1) == 0)
        def _():
            dstate_ref[...] = jnp.zeros_like(dstate_ref)
            dbg_ref[...] = jnp.zeros_like(dbg_ref)

        z_all, gk_all = _gla_gates(glr_ref[...], wg_ref, bg_ref)
        dst = dstate_ref[...]
        for c in range(nc - 1, -1, -1):
            rows = slice(c * B_CHUNK, (c + 1) * B_CHUNK)
            v = v_ref[rows, :]
            d_out = do_ref[rows, :]
            st = st_ref[c, 0]
            causal, bcum, bl, qt, kt, kd, a = _gla_chunk(q_ref[rows, :], k_ref[rows, :], gk_all[rows, :])
            da = jnp.where(causal, _dot(d_out, v, NT), 0.0)
            dv_ref[rows, :] = (_dot(a, d_out, TN) + _dot(kd, dst, NT)).astype(BF16)
            dqt = _dot(da, kt) + _dot(d_out, st)
            dkt = _dot(da, qt, TN)
            dkd = _dot(v, dst)
            dec = jnp.exp(bl)
            ddec = jnp.sum(dst * st, axis=0, keepdims=True)
            dst = dst * dec + _dot(d_out, qt, TN)
            dq_ref[rows, :] = (dqt * jnp.exp(bcum) * (B_DK ** -0.5)).astype(BF16)
            dk_ref[rows, :] = (dkt * jnp.exp(-bcum) + dkd * jnp.exp(bl - bcum)).astype(BF16)
            db = (dqt * qt - dkt * kt) - dkd * kd
            dbl = jnp.sum(dkd * kd, axis=0, keepdims=True) + dec * ddec
            upper = jnp.logical_not(causal) | (lax.broadcasted_iota(jnp.int32, (B_CHUNK, B_CHUNK), 0)
                                               == lax.broadcasted_iota(jnp.int32, (B_CHUNK, B_CHUNK), 1))
            dgk = jnp.dot(upper.astype(F32), db, precision=lax.Precision.HIGHEST, preferred_element_type=F32) + dbl
            dz = dgk * (1.0 / B_TAU) * jax.nn.sigmoid(-z_all[rows, :])
            dz_ref[rows, :] = dz
            dbg_ref[...] += jnp.sum(dz, axis=0, keepdims=True)
        dstate_ref[...] = dst

    qs = pl.BlockSpec((tg, B_DK), lambda h, i: (i, h))
    vs = pl.BlockSpec((tg, B_DV), lambda h, i: (i, h))
    in_specs = [rev(s) if n < 4 else s for n, s in enumerate(_gla_specs(tg))]
    in_specs += [rev(pl.BlockSpec((nc, 1, B_DV, B_DK), lambda h, i: (i, h, 0, 0))), rev(vs)]
    return pl.pallas_call(
        body, name=name, grid=(B_HEADS, ni), in_specs=in_specs,
        out_specs=[rev(qs), rev(qs), rev(vs), rev(qs), pl.BlockSpec((1, B_DK), lambda h, i: (0, h))],
        out_shape=[jax.ShapeDtypeStruct((S, B_QK), BF16), jax.ShapeDtypeStruct((S, B_QK), BF16),
                   jax.ShapeDtypeStruct((S, B_V), BF16), jax.ShapeDtypeStruct((S, B_QK), F32),
                   jax.ShapeDtypeStruct((1, B_QK), F32)],
        scratch_shapes=[pltpu.VMEM((B_DV, B_DK), F32)], compiler_params=_params(2))(
            proj, proj, proj, proj, w_gate, b_gate, states, d_o)


def _gla_out(o, proj, gnorm, name):
    S = o.shape[0]
    tr = ROW_TILE

    def body(o_ref, r_ref, g_ref, y_ref):
        for h in range(B_HEADS):
            hs = slice(h * B_DV, (h + 1) * B_DV)
            oh = o_ref[:, hs]
            rs = lax.rsqrt(jnp.mean(oh * oh, axis=-1, keepdims=True) + EPS)
            y_ref[:, hs] = (((oh * rs) * g_ref[...]) * _silu(r_ref[:, hs])).astype(BF16)

    return pl.pallas_call(
        body, name=name, grid=(S // tr,),
        in_specs=[_row(tr), pl.BlockSpec((tr, B_V), lambda i: (i, 2)), _vec(B_DV)], out_specs=_row(tr),
        out_shape=jax.ShapeDtypeStruct((S, B_V), BF16), compiler_params=_params(1))(o, proj, gnorm)


def _gla_out_bwd(o, proj, gnorm, d_y, name):
    S = o.shape[0]
    tr = ROW_TILE

    def body(o_ref, r_ref, g_ref, dy_ref, do_ref, dr_ref, dg_ref):
        @pl.when(pl.program_id(0) == 0)
        def _():
            dg_ref[...] = jnp.zeros_like(dg_ref)

        for h in range(B_HEADS):
            hs = slice(h * B_DV, (h + 1) * B_DV)
            oh, rv, dyv = o_ref[:, hs], r_ref[:, hs], dy_ref[:, hs]
            rs = lax.rsqrt(jnp.mean(oh * oh, axis=-1, keepdims=True) + EPS)
            xhat = oh * rs
            dr_ref[:, hs] = (dyv * (xhat * g_ref[...]) * _dsilu(rv)).astype(BF16)
            dn = dyv * _silu(rv)
            dg_ref[...] += jnp.sum(dn * xhat, axis=0, keepdims=True)
            dxh = dn * g_ref[...]
            do_ref[:, hs] = rs * (dxh - xhat * jnp.mean(dxh * xhat, axis=-1, keepdims=True))

    return pl.pallas_call(
        body, name=name, grid=(S // tr,),
        in_specs=[_row(tr), pl.BlockSpec((tr, B_V), lambda i: (i, 2)), _vec(B_DV), _row(tr)],
        out_specs=[_row(tr), _row(tr), _vec(B_DV)],
        out_shape=[jax.ShapeDtypeStruct((S, B_V), F32), jax.ShapeDtypeStruct((S, B_V), BF16),
                   jax.ShapeDtypeStruct((1, B_DV), F32)],
        compiler_params=_params(1))(o, proj, gnorm, d_y)


J_SH1, J_SC1, J_G1, J_SH2, J_SC2, J_G2 = range(6)


IN_A_TN = 768
UP_TN = 2 * D_FF // 4
TOKEN_TK = 2048


def _ffn_fwd(h, w, i, tag):
    S = h.shape[0]
    u = _matmul(h, w["w_up"], "nn", F32, f"up{tag}", shape=(S, 2 * D_FF, D_MODEL), tiles=(1024, UP_TN, D_MODEL),
                b_spec=pl.BlockSpec((None, D_MODEL, UP_TN), lambda m, j, k: (j, i, 0)))
    act = _conv_gate(u, w["conv_w"][i], w["conv_b"][i:i + 1], f"conv_gate{tag}")
    f = _matmul(act, w["w_down"], "nn", F32, f"down{tag}", shape=(S, D_MODEL, D_FF), tiles=(1024, D_MODEL, D_FF),
                b_spec=pl.BlockSpec((D_FF, D_MODEL), lambda m, j, k: (i, j)))
    return u, act, f


def _ffn_bwd(dx_out, f, u, act, h, x_in, mod, w, i, tag, prev):
    S = h.shape[0]
    df, dg2 = _gate_bwd(dx_out, f, mod, J_G2, f"gate2_bwd{tag}")
    dact = _matmul(df, w["w_down"], "nt", F32, f"down_dx{tag}", shape=(S, D_FF, D_MODEL),
                   tiles=(1024, D_FF // 2, D_MODEL),
                   b_spec=pl.BlockSpec((D_FF // 2, D_MODEL), lambda m, j, k: (2 * i + j, k)))
    d_w_down = _matmul(act, df, "tn", F32, f"down_dw{tag}", shape=(D_FF, D_MODEL, S),
                       tiles=(D_FF // 2, D_MODEL, min(S, TOKEN_TK)),
                       o_spec=pl.BlockSpec((D_FF // 2, D_MODEL), lambda m, j, k: (2 * i + m, j)),
                       o_shape=(2 * D_FF, D_MODEL), prev=None if prev is None else prev["w_down"])
    duca, ducb, dcwa, dcwb, dcba, dcbb = _conv_gate_bwd(u, dact, w["conv_w"][i], w["conv_b"][i:i + 1],
                                                        f"conv_gate_bwd{tag}")
    du = _conv_t(duca, w["conv_w"][i], 0, None, f"conv_t_a{tag}")
    du = _conv_t(ducb, w["conv_w"][i], 1, du, f"conv_t_b{tag}")
    dh = _matmul(du, w["w_up"], "nt", F32, f"up_dx{tag}", shape=(S, D_MODEL, 2 * D_FF), tiles=(1024, D_MODEL, UP_TN),
                 b_spec=pl.BlockSpec((None, D_MODEL, UP_TN), lambda m, j, k: (k, i, 0)))
    d_w_up = _matmul(h, du, "tn", F32, f"up_dw{tag}", shape=(D_MODEL, 2 * D_FF, S),
                     tiles=(D_MODEL, UP_TN, min(S, TOKEN_TK)),
                     o_spec=pl.BlockSpec((None, D_MODEL, UP_TN), lambda m, j, k: (j, i, 0)),
                     o_shape=(4, 2 * D_MODEL, UP_TN), prev=None if prev is None else prev["w_up"])
    dx_in, dsc2, dsh2, dgam = _norm_mod_bwd(dh, x_in, dx_out, w["norm_ffn"][i:i + 1], mod, J_SC2,
                                            f"norm_ffn_bwd{tag}")
    grads = dict(w_down=d_w_down, w_up=d_w_up, norm_ffn=dgam,
                 conv_w=jnp.concatenate([dcwa[0:3], dcwb[0:3]], axis=1),
                 conv_b=jnp.concatenate([dcba, dcbb], axis=1))
    return dx_in, (dsh2, dsc2, dg2), grads


def _local_step(x, c, tgt, w):
    buckets = jnp.asarray(_bucket_maps())
    S = x.shape[0]
    mods = [_ada_mod(c, w["w_ada"], w["b_ada"][i:i + 1], i, f"ada_mod{i}") for i in range(2)]

    h1 = _norm_mod(x, w["norm_mix"][0:1], mods[0], J_SC1, J_SH1, "norm_mix0", views=True)
    geo = []
    for g, (_, dil) in enumerate(A_CONFIGS):
        tm = min(1024, S // dil)
        geo.append((dil, tm, S // dil // tm))
    w_cols = [pl.BlockSpec((None, D_MODEL, IN_A_TN), lambda m, j, k, g=g: ((4 * g + j) // 3, 0, (4 * g + j) % 3))
              for g in range(3)]
    qkv = [_matmul(h1[g], w["w_in_a"], "nn", BF16, f"in_a{g}", shape=(S, 3 * D_MODEL, D_MODEL),
                   tiles=(tm, IN_A_TN, D_MODEL),
                   a_spec=pl.BlockSpec((tm, D_MODEL), lambda m, j, k, n=n_i: (m % n, m // n)), b_spec=w_cols[g],
                   o_spec=pl.BlockSpec((tm, IN_A_TN), lambda m, j, k, n=n_i: (m % n, (m // n) * 4 + j)),
                   o_shape=(S // dil, dil * 3 * D_MODEL))
           for g, (dil, tm, n_i) in enumerate(geo)]
    bias = _bias_build(w["rel_bias"], buckets, "bias_build")
    os_, ls_ = zip(*[_attn_fwd(qkv[g], bias, g, f"attn_fwd{g}") for g in range(3)])
    omix, omix4, omix16, ltot, ltot4, ltot16 = _mix_fwd(os_, ls_, "mix_fwd")
    y0 = _matmul(omix, w["w_out_a"], "nn", F32, "out_a")
    x1, h2 = _norm_mod(x, w["norm_ffn"][0:1], mods[0], J_SC2, J_SH2, "norm_ffn0", resid=y0, gate_mod=mods[0],
                       j_gate=J_G1)
    u0, act0, f0 = _ffn_fwd(h2, w, 0, "0")

    x2, h3 = _norm_mod(x1, w["norm_mix"][1:2], mods[1], J_SC1, J_SH1, "norm_mix1", resid=f0, gate_mod=mods[0],
                       j_gate=J_G2)
    proj = _matmul(h3, w["w_in_b"], "nn", F32, "in_b")
    o_gla, states = _gla_fwd(proj, w["w_gate_b"], w["b_gate_b"], "gla_fwd")
    on = _gla_out(o_gla, proj, w["gnorm_b"], "gla_out")
    y1 = _matmul(on, w["w_out_b"], "nn", F32, "out_b")
    x3, h4 = _norm_mod(x2, w["norm_ffn"][1:2], mods[1], J_SC2, J_SH2, "norm_ffn1", resid=y1, gate_mod=mods[1],
                       j_gate=J_G1)
    u1, act1, f1 = _ffn_fwd(h4, w, 1, "1")

    dx4, loss, d_norm_final = _final_loss(x3, f1, mods[1], J_G2, w["norm_final"], tgt, "final_loss")

    dx3, (dsh2_1, dsc2_1, dg2_1), g_ffn1 = _ffn_bwd(dx4, f1, u1, act1, h4, x3, mods[1], w, 1, "1", None)
    dy1, dg1_1 = _gate_bwd(dx3, y1, mods[1], J_G1, "gate1_bwd1")
    d_on = _matmul(dy1, w["w_out_b"], "nt", F32, "out_b_dx")
    d_w_out_b = _matmul(on, dy1, "tn", F32, "out_b_dw")
    d_ogla, d_r, d_gnorm = _gla_out_bwd(o_gla, proj, w["gnorm_b"], d_on, "gla_out_bwd")
    dq, dk, dv, dz, d_b_gate = _gla_bwd(proj, w["w_gate_b"], w["b_gate_b"], states, d_ogla, "gla_bwd")
    d_glr = _matmul(dz, w["w_gate_b"], "nt", BF16, "gate_dx")
    d_w_gate = _matmul(proj[:, 3072:3200], dz, "tn", F32, "gate_dw")
    dproj = jnp.concatenate([dq, dk, dv, d_r, d_glr], axis=1)
    dh3 = _matmul(dproj, w["w_in_b"], "nt", F32, "in_b_dx")
    d_w_in_b = _matmul(h3, dproj, "tn", F32, "in_b_dw")
    dx2, dsc1_1, dsh1_1, d_nmix1 = _norm_mod_bwd(dh3, x2, dx3, w["norm_mix"][1:2], mods[1], J_SC1, "norm_mix_bwd1")
    dmod1 = jnp.concatenate([dsh1_1, dsc1_1, dg1_1, dsh2_1, dsc2_1, dg2_1], axis=1)

    dx1, (dsh2_0, dsc2_0, dg2_0), g_ffn0 = _ffn_bwd(dx2, f0, u0, act0, h2, x1, mods[0], w, 0, "0", g_ffn1)
    dy0, dg1_0 = _gate_bwd(dx1, y0, mods[0], J_G1, "gate1_bwd0")
    d_omix = _matmul(dy0, w["w_out_a"], "nt", F32, "out_a_dx")
    d_w_out_a = _matmul(omix, dy0, "tn", F32, "out_a_dw")
    d_omix4, d_omix16 = _to_views(d_omix, "d_omix_views")
    d_omix_v, omix_v, ltot_v = (d_omix, d_omix4, d_omix16), (omix, omix4, omix16), (ltot, ltot4, ltot16)
    dqkv, dbs = zip(*[_attn_bwd(qkv[g], bias, d_omix_v[g], omix_v[g], ltot_v[g], g, f"attn_bwd{g}")
                      for g in range(3)])
    d_rel_bias = _bias_bwd(jnp.concatenate(dbs, axis=0), buckets, "bias_bwd")
    dh1, d_w_in_a = [], None
    for g, (dil, tm, n_i) in enumerate(geo):
        dh1.append(_matmul(
            dqkv[g], w["w_in_a"], "nt", F32, f"in_a_dx{g}", shape=(S, D_MODEL, 3 * D_MODEL),
            tiles=(tm, D_MODEL, IN_A_TN),
            a_spec=pl.BlockSpec((tm, IN_A_TN), lambda m, j, k, n=n_i: (m % n, (m // n) * 4 + k)),
            b_spec=pl.BlockSpec((None, D_MODEL, IN_A_TN), lambda m, j, k, g=g: ((4 * g + k) // 3, j, (4 * g + k) % 3)),
            o_spec=pl.BlockSpec((tm, D_MODEL), lambda m, j, k, n=n_i: (m % n, m // n)),
            o_shape=(S // dil, dil * D_MODEL)))
        tk = min(TOKEN_TK, S // dil)
        n_k = S // dil // tk
        d_w_in_a = _matmul(
            h1[g], dqkv[g], "tn", F32, f"in_a_dw{g}", shape=(D_MODEL, 3 * D_MODEL, S), tiles=(D_MODEL, IN_A_TN, tk),
            a_spec=pl.BlockSpec((tk, D_MODEL), lambda m, j, k, n=n_k: (k % n, k // n)),
            b_spec=pl.BlockSpec((tk, IN_A_TN), lambda m, j, k, n=n_k: (k % n, (k // n) * 4 + j)),
            o_spec=pl.BlockSpec((None, D_MODEL, IN_A_TN), lambda m, j, k, g=g: ((4 * g + j) // 3, m, (4 * g + j) % 3)),
            o_shape=(4, D_MODEL, 9 * D_MODEL // 4), prev=d_w_in_a)
    dx0, dsc1_0, dsh1_0, d_nmix0 = _norm_mod_bwd(dh1[0], x, dx1, w["norm_mix"][0:1], mods[0], J_SC1, "norm_mix_bwd0",
                                                 dh_views=dh1[1:])
    dmod0 = jnp.concatenate([dsh1_0, dsc1_0, dg1_0, dsh2_0, dsc2_0, dg2_0], axis=1)

    grads = dict(
        w_in_a=d_w_in_a, w_out_a=d_w_out_a, rel_bias=d_rel_bias, w_in_b=d_w_in_b, w_gate_b=d_w_gate,
        b_gate_b=d_b_gate, gnorm_b=d_gnorm, w_out_b=d_w_out_b,
        norm_mix=jnp.concatenate([d_nmix0, d_nmix1], axis=0),
        norm_ffn=jnp.concatenate([g_ffn0["norm_ffn"], g_ffn1["norm_ffn"]], axis=0),
        w_ada=_ada_outer(c, jnp.stack([dmod0, dmod1]), "ada_outer"),
        b_ada=jnp.concatenate([dmod0, dmod1], axis=0),
        w_up=g_ffn0["w_up"],
        conv_w=jnp.stack([g_ffn0["conv_w"], g_ffn1["conv_w"]]),
        conv_b=jnp.concatenate([g_ffn0["conv_b"], g_ffn1["conv_b"]], axis=0),
        w_down=g_ffn0["w_down"],
        norm_final=d_norm_final)
    return loss, dx0, grads


N_CHIPS = 4
N_DEV = 8
WEIGHTS = ("w_in_a", "w_out_a", "rel_bias", "w_in_b", "w_gate_b", "b_gate_b", "gnorm_b", "w_out_b", "norm_mix",
           "norm_ffn", "w_ada", "b_ada", "w_up", "conv_w", "conv_b", "w_down", "norm_final")
SHARD_AXIS = dict(w_in_a=2, w_out_a=1, w_in_b=2, w_gate_b=2, b_gate_b=1, gnorm_b=1, w_out_b=1, w_ada=2, w_up=2,
                  conv_w=2, w_down=1)
SHARDED = tuple(n for n in WEIGHTS if n in SHARD_AXIS)
REPLICATED = tuple(n for n in WEIGHTS if n not in SHARD_AXIS)
BIG = ("w_in_a", "w_out_a", "w_in_b", "w_out_b", "w_ada", "w_up", "w_down")
SMALL = ("w_gate_b", "b_gate_b", "gnorm_b", "conv_w")
SMALL_FULL = dict(w_gate_b=(1, 16, 512), b_gate_b=(1, 512), gnorm_b=(1, 256), conv_w=(2, 3, 5632))
R_SMALL = 16
R_TINY = 72
LOSS_SLOT = 72960
R_TINY_SHARD = 40
W_IN_B_PAD = 896

COMM_VIEW = dict(
    w_in_a=((4096, 2304), 1024, 512, 512),
    w_out_a=((1024, 1024), 256, 128, 128),
    w_in_b=((4096, W_IN_B_PAD), 1024, 512, 512),
    w_out_b=((1024, 1024), 256, 128, 128),
    w_ada=((8192, 1536), 2048, 1024, 1024),
    w_up=((8192, 1408), 2048, 1024, 1024),
    w_down=((5632, 1024), 704, 2816, 704))


def _pack(arrs, rows):
    flat = jnp.concatenate([a.reshape(-1) for a in arrs])
    return jnp.pad(flat, (0, rows * LANES - flat.shape[0])).reshape(rows, LANES)


def _unpack(flat2d, shapes):
    flat = flat2d.reshape(-1)
    out, off = [], 0
    for shp in shapes:
        n = math.prod(shp)
        out.append(flat[off:off + n].reshape(shp))
        off += n
    return out


def _chip_slice(a, axis, k):
    n = a.shape[axis] // N_CHIPS
    return lax.slice_in_dim(a, k * n, (k + 1) * n, axis=axis)


def _place():
    mx, my, mc = lax.axis_index("x"), lax.axis_index("y"), lax.axis_index("c")
    chips = [(1 - mx, my), (mx, 1 - my), (1 - mx, 1 - my)]
    return mx, my, mc, chips


def _rcopy(src, dst, send_sem, recv_sem, dev):
    return pltpu.make_async_remote_copy(src_ref=src, dst_ref=dst, send_sem=send_sem, recv_sem=recv_sem,
                                        device_id=dev, device_id_type=MESH)


def _comm_call(body, name, ins, out_shapes, n_sems, in_place=False):
    n_in, n_out = len(ins), len(out_shapes)

    def wrapped(*refs):
        body(refs[:n_in], refs[n_in:n_in + n_out], *refs[n_in + n_out:])

    return pl.pallas_call(
        wrapped, name=name, in_specs=[HBM] * n_in, out_specs=[HBM] * n_out, out_shape=out_shapes,
        input_output_aliases={i: i for i in range(n_in)} if in_place else {},
        scratch_shapes=[pltpu.SemaphoreType.DMA((n_sems,)), pltpu.SemaphoreType.DMA((n_sems,))])(*ins)


DMA_CHUNK_BYTES = 2 * 1024 * 1024


def _rows(ref, start, size):
    return ref.at[pl.ds(pl.multiple_of(start, 16), size), :]


def _block(ref, name, k, h):
    _, bk, bh, nr = COMM_VIEW[name]
    return _rows(ref, bk * k + bh * h, nr)


def _chunks(nr, row_bytes):
    n = 1
    while nr % (2 * n) == 0 and (nr // (2 * n)) % 16 == 0 and (nr // n) * row_bytes > DMA_CHUNK_BYTES:
        n *= 2
    return [(i * (nr // n), nr // n) for i in range(n)]


def _gather_big(views, name):
    names = BIG

    def body(x_refs, out_refs, send_sems, recv_sems):
        mx, my, mc, chips = _place()
        chip = 2 * mx + my
        sibling = (mx, my, 1 - mc)
        sends = []
        for a, n in enumerate(names):
            for j, (cx, cy) in enumerate(chips):
                blk = _block(out_refs[a], n, chip, mc)
                cp = _rcopy(blk, blk, send_sems.at[6 * a + j], recv_sems.at[6 * a + j], (cx, cy, mc))
                cp.start()
                sends.append(cp)
        for a, n in enumerate(names):
            for j, (cx, cy) in enumerate(chips):
                blk = _block(out_refs[a], n, 2 * cx + cy, mc)
                _rcopy(blk, blk, send_sems.at[6 * a + j], recv_sems.at[6 * a + j], sibling).wait_recv()
                cp = _rcopy(blk, blk, send_sems.at[6 * a + 3 + j], recv_sems.at[6 * a + 3 + j], sibling)
                cp.start()
                sends.append(cp)
        for a, n in enumerate(names):
            for j, (cx, cy) in enumerate(chips):
                blk = _block(out_refs[a], n, 2 * cx + cy, 1 - mc)
                _rcopy(blk, blk, send_sems.at[6 * a + 3 + j], recv_sems.at[6 * a + 3 + j], sibling).wait_recv()
        for cp in sends:
            cp.wait_send()

    outs = _comm_call(body, name, [views[n] for n in names],
                      [jax.ShapeDtypeStruct(views[n].shape, views[n].dtype) for n in names], 6 * len(names),
                      in_place=True)
    return dict(zip(names, outs))


def _rs_pair_exchange(views, name):
    names = BIG

    def body(g_refs, recv_refs, send_sems, recv_sems):
        mx, my, mc, _ = _place()
        sibling = (mx, my, 1 - mc)
        for a, n in enumerate(names):
            (_, cols), _, _, nr = COMM_VIEW[n]
            for k in range(N_CHIPS):
                src = _block(g_refs[a], n, k, 1 - mc)
                for start, size in _chunks(nr, cols * 4):
                    _rcopy(src.at[pl.ds(start, size), :], recv_refs[a].at[k, pl.ds(start, size), :],
                           send_sems.at[a], recv_sems.at[a], sibling).start()
        for a in range(len(names)):
            _rcopy(recv_refs[a], recv_refs[a], send_sems.at[a], recv_sems.at[a], sibling).wait()

    outs = _comm_call(body, name, [views[n] for n in names],
                      [jax.ShapeDtypeStruct((N_CHIPS, COMM_VIEW[n][3], COMM_VIEW[n][0][1]), F32) for n in names],
                      len(names))
    return dict(zip(names, outs))


def _pair_add(view, recv, c_idx, n, out_dtype, name):
    (_, cols), bk, bh, nr = COMM_VIEW[n]
    tr = _pick(math.gcd(bk, bh, nr), 256, 8)

    def body(c_ref, g_ref, r_ref, o_ref):
        o_ref[...] = (g_ref[...] + r_ref[...]).astype(o_ref.dtype)

    piece = pl.BlockSpec((None, tr, cols), lambda k, i, c_ref: (k, i, 0))
    return pl.pallas_call(
        body, name=name,
        grid_spec=pltpu.PrefetchScalarGridSpec(
            num_scalar_prefetch=1, grid=(N_CHIPS, nr // tr),
            in_specs=[pl.BlockSpec((tr, cols), lambda k, i, c_ref: ((bk * k + bh * c_ref[0]) // tr + i, 0)), piece],
            out_specs=piece),
        out_shape=jax.ShapeDtypeStruct((N_CHIPS, nr, cols), out_dtype), compiler_params=_params(2))(
            c_idx, view, recv)


def _rs_chip_exchange(q, name):
    names = BIG

    def body(q_refs, out_refs, send_sems, recv_sems):
        mx, my, mc, chips = _place()
        chip = 2 * mx + my
        sends = []
        for a in range(len(names)):
            for j, (cx, cy) in enumerate(chips):
                cp = _rcopy(q_refs[a].at[2 * cx + cy], out_refs[a].at[chip], send_sems.at[3 * a + j],
                            recv_sems.at[3 * a + j], (cx, cy, mc))
                cp.start()
                sends.append(cp)
        for a in range(len(names)):
            for j, (cx, cy) in enumerate(chips):
                blk = out_refs[a].at[2 * cx + cy]
                _rcopy(blk, blk, send_sems.at[3 * a + j], recv_sems.at[3 * a + j], (cx, cy, mc)).wait_recv()
        for cp in sends:
            cp.wait_send()

    outs = _comm_call(body, name, [q[n] for n in names],
                      [jax.ShapeDtypeStruct(q[n].shape, q[n].dtype) for n in names], 3 * len(names))
    return dict(zip(names, outs))


def _rs_pair_gather(r, name):
    names = BIG

    def body(r_refs, out_refs, send_sems, recv_sems):
        mx, my, mc, _ = _place()
        sibling = (mx, my, 1 - mc)
        for a, n in enumerate(names):
            (_, cols), _, _, nr = COMM_VIEW[n]
            for start, size in _chunks(nr, cols * 4):
                rows = _rows(out_refs[a], mc * nr + start, size)
                _rcopy(rows, rows, send_sems.at[a], recv_sems.at[a], sibling).start()
        for a, n in enumerate(names):
            nr = COMM_VIEW[n][3]
            _rcopy(_rows(out_refs[a], mc * nr, nr), _rows(out_refs[a], (1 - mc) * nr, nr), send_sems.at[a],
                   recv_sems.at[a], sibling).wait()

    outs = _comm_call(body, name, [r[n] for n in names],
                      [jax.ShapeDtypeStruct(r[n].shape, F32) for n in names], len(names), in_place=True)
    return dict(zip(names, outs))


def _gather8(x, reduce, name):
    rows = x.shape[0]

    def body(x_ref, out_ref, *rest):
        if reduce:
            buf_ref, send_sems, recv_sems = rest
        else:
            (send_sems, recv_sems), buf_ref = rest, out_ref
        mx, my, mc, _ = _place()
        me = 4 * mx + 2 * my + mc
        buf_ref[me] = x_ref[...]
        peers = []
        for j in range(1, N_DEV):
            px = 1 - mx if j & 4 else mx
            py = 1 - my if j & 2 else my
            pc = 1 - mc if j & 1 else mc
            peers.append((px, py, pc))
        sends = [_rcopy(x_ref, buf_ref.at[me], send_sems.at[j], recv_sems.at[j], p) for j, p in enumerate(peers)]
        for cp in sends:
            cp.start()
        for j, (px, py, pc) in enumerate(peers):
            _rcopy(x_ref, buf_ref.at[4 * px + 2 * py + pc], send_sems.at[j], recv_sems.at[j], (px, py, pc)).wait_recv()
        for cp in sends:
            cp.wait_send()
        if reduce:
            acc = buf_ref[0]
            for d in range(1, N_DEV):
                acc = acc + buf_ref[d]
            out_ref[...] = acc

    vmem = pl.BlockSpec(memory_space=pltpu.VMEM)
    sems = [pltpu.SemaphoreType.DMA((N_DEV - 1,)), pltpu.SemaphoreType.DMA((N_DEV - 1,))]
    if reduce:
        out_shape = jax.ShapeDtypeStruct((rows, LANES), F32)
        scratch = [pltpu.VMEM((N_DEV, rows, LANES), F32)] + sems
    else:
        out_shape = jax.ShapeDtypeStruct((N_DEV, rows, LANES), F32)
        scratch = sems
    return pl.pallas_call(body, name=name, in_specs=[vmem], out_specs=vmem, out_shape=out_shape,
                          scratch_shapes=scratch)(x)


def _sum4(p, q, chip, core, name):
    _, nr, cols = p.shape
    tr = _pick(nr, 256, 8)

    def body(chip_ref, core_ref, p0, p1, p2, p3, own, o_ref):
        s = [jnp.where(chip_ref[0] == k, own[...], pk[...]).astype(F32) for k, pk in enumerate((p0, p1, p2, p3))]
        o_ref[...] = ((s[0] + s[1]) + s[2]) + s[3]

    return pl.pallas_call(
        body, name=name,
        grid_spec=pltpu.PrefetchScalarGridSpec(
            num_scalar_prefetch=2, grid=(nr // tr,),
            in_specs=[pl.BlockSpec((None, tr, cols), lambda i, ch, co, k=k: (jnp.where(ch[0] == k, k ^ 1, k), i, 0))
                      for k in range(N_CHIPS)]
            + [pl.BlockSpec((None, tr, cols), lambda i, ch, co: (ch[0], i, 0))],
            out_specs=pl.BlockSpec((tr, cols), lambda i, ch, co: (co[0] * (nr // tr) + i, 0))),
        out_shape=jax.ShapeDtypeStruct((2 * nr, cols), F32), compiler_params=_params(1))(chip, core, p, p, p, p, q)


def _place_shard(shard, chip, n, name):
    (rows, cols), bk, bh, nr = COMM_VIEW[n]
    tr = _pick(math.gcd(bk, bh, nr), 256, 16)

    def body(chip_ref, x_ref, o_ref):
        o_ref[...] = x_ref[...].astype(BF16)

    return pl.pallas_call(
        body, name=name,
        grid_spec=pltpu.PrefetchScalarGridSpec(
            num_scalar_prefetch=1, grid=(2, nr // tr),
            in_specs=[pl.BlockSpec((tr, cols), lambda h, i, ch: (h * (nr // tr) + i, 0))],
            out_specs=pl.BlockSpec((tr, cols), lambda h, i, ch: ((bk * ch[0] + bh * h) // tr + i, 0))),
        out_shape=jax.ShapeDtypeStruct((rows, cols), BF16), compiler_params=_params(2))(chip, shard)


def _adamw(w, g, m, v, name):
    rows, cols = w.shape
    tr = _pick(rows, max(8, (1 << 20) // (4 * cols)), 8)

    def body(w_ref, g_ref, m_ref, v_ref, d_ref, mo_ref, vo_ref):
        gv = g_ref[...]
        mn = ADAM_B1 * m_ref[...] + (1.0 - ADAM_B1) * gv
        vn = ADAM_B2 * v_ref[...] + (1.0 - ADAM_B2) * (gv * gv)
        m_hat = mn / (1.0 - ADAM_B1 ** ADAM_STEP)
        v_hat = vn / (1.0 - ADAM_B2 ** ADAM_STEP)
        d_ref[...] = -ADAM_LR * (m_hat / (jnp.sqrt(v_hat) + ADAM_EPS) + ADAM_WD * w_ref[...])
        mo_ref[...] = mn
        vo_ref[...] = vn

    shape = jax.ShapeDtypeStruct(w.shape, F32)
    return pl.pallas_call(
        body, name=name, grid=(rows // tr,), in_specs=[_row(tr, cols)] * 4, out_specs=[_row(tr, cols)] * 3,
        out_shape=[shape] * 3, compiler_params=_params(1))(w, g, m, v)


W_IN_B_SHARD = 772


def _shard_view(n, a):
    return a.reshape(-1, a.shape[-1])


def _gather_weights(p, chip):
    shards = {n: _shard_view(n, p[n]) for n in BIG}
    shards["w_in_b"] = jnp.pad(shards["w_in_b"], ((0, 0), (0, W_IN_B_PAD - W_IN_B_SHARD)))
    big = _gather_big({n: _place_shard(shards[n], chip, n, f"place_{n}") for n in BIG}, "gather_weights")
    small = _gather8(_pack([p[n] for n in SMALL], R_SMALL), False, "gather_small")
    pieces = [_unpack(small[2 * k], [p[n].shape for n in SMALL]) for k in range(N_CHIPS)]
    full = {n: jnp.concatenate([pieces[k][i] for k in range(N_CHIPS)], axis=SHARD_AXIS[n])
            for i, n in enumerate(SMALL)}
    wb = big["w_in_b"].reshape(N_CHIPS, D_MODEL, W_IN_B_PAD)
    wb = jnp.concatenate([wb[k, :, :W_IN_B_SHARD] for k in range(N_CHIPS)], axis=1)
    return dict(
        w_in_a=big["w_in_a"].reshape(N_CHIPS, D_MODEL, -1), w_out_a=big["w_out_a"], w_out_b=big["w_out_b"],
        w_in_b=jnp.concatenate([wb[:, :2048], wb[:, 2064:3088], wb[:, 2048:2064],
                                jnp.zeros((D_MODEL, B_IN_PAD - 3088), BF16)], axis=1),
        w_ada=big["w_ada"].reshape(N_CHIPS, 2 * D_MODEL, -1), w_up=big["w_up"].reshape(N_CHIPS, 2 * D_MODEL, -1),
        w_down=big["w_down"],
        w_gate_b=jnp.pad(full["w_gate_b"][0], ((0, 128 - B_GATE_RANK), (0, 0))),
        b_gate_b=full["b_gate_b"], gnorm_b=full["gnorm_b"], conv_w=full["conv_w"],
        rel_bias=p["rel_bias"], norm_mix=p["norm_mix"], norm_ffn=p["norm_ffn"], b_ada=p["b_ada"],
        conv_b=p["conv_b"], norm_final=p["norm_final"].reshape(1, D_MODEL))


def _grad_views(g):
    gb = g["w_in_b"]
    gb = jnp.concatenate([gb[:, :2048], gb[:, 3072:3088], gb[:, 2048:3072]], axis=1)
    gb = jnp.pad(gb.reshape(D_MODEL, N_CHIPS, W_IN_B_SHARD).transpose(1, 0, 2),
                 ((0, 0), (0, 0), (0, W_IN_B_PAD - W_IN_B_SHARD)))
    views = {n: g[n].reshape(COMM_VIEW[n][0]) for n in BIG if n != "w_in_b"}
    views["w_in_b"] = gb.reshape(COMM_VIEW["w_in_b"][0])
    return views


def _tiny_grads(g):
    out = {n: g[n] for n in REPLICATED if n != "norm_final"}
    out.update(norm_final=g["norm_final"].reshape(D_MODEL), w_gate_b=g["w_gate_b"][:B_GATE_RANK][None],
               b_gate_b=g["b_gate_b"], gnorm_b=g["gnorm_b"], conv_w=g["conv_w"])
    return out


def kernel(x, c, w_in_a, w_out_a, rel_bias, w_in_b, w_gate_b, b_gate_b, gnorm_b, w_out_b, norm_mix, norm_ffn, w_ada, b_ada, w_up, conv_w, conv_b, w_down, norm_final, loss_target, m_w_in_a, m_w_out_a, m_rel_bias, m_w_in_b, m_w_gate_b, m_b_gate_b, m_gnorm_b, m_w_out_b, m_norm_mix, m_norm_ffn, m_w_ada, m_b_ada, m_w_up, m_conv_w, m_conv_b, m_w_down, m_norm_final, v_w_in_a, v_w_out_a, v_rel_bias, v_w_in_b, v_w_gate_b, v_b_gate_b, v_gnorm_b, v_w_out_b, v_norm_mix, v_norm_ffn, v_w_ada, v_b_ada, v_w_up, v_conv_w, v_conv_b, v_w_down, v_norm_final):
    p = dict(zip(WEIGHTS, (w_in_a, w_out_a, rel_bias, w_in_b, w_gate_b, b_gate_b, gnorm_b, w_out_b, norm_mix,
                           norm_ffn, w_ada, b_ada, w_up, conv_w, conv_b, w_down, norm_final)))
    pm = dict(zip(WEIGHTS, (m_w_in_a, m_w_out_a, m_rel_bias, m_w_in_b, m_w_gate_b, m_b_gate_b, m_gnorm_b, m_w_out_b,
                            m_norm_mix, m_norm_ffn, m_w_ada, m_b_ada, m_w_up, m_conv_w, m_conv_b, m_w_down,
                            m_norm_final)))
    pv = dict(zip(WEIGHTS, (v_w_in_a, v_w_out_a, v_rel_bias, v_w_in_b, v_w_gate_b, v_b_gate_b, v_gnorm_b, v_w_out_b,
                            v_norm_mix, v_norm_ffn, v_w_ada, v_b_ada, v_w_up, v_conv_w, v_conv_b, v_w_down,
                            v_norm_final)))
    S = x.shape[1]

    chip = 2 * lax.axis_index("x") + lax.axis_index("y")
    core = lax.axis_index("c").astype(jnp.int32).reshape(1)
    chip_s = chip.astype(jnp.int32).reshape(1)

    w = _gather_weights(p, chip_s)
    loss, dx0, grads = _local_step(x.reshape(S, D_MODEL), c, loss_target.reshape(S, D_MODEL), w)

    views = _grad_views(grads)
    recv = _rs_pair_exchange(views, "grads_pair_exchange")
    pair = {n: _pair_add(views[n], recv[n], core, n, BF16, f"grads_pair_add_{n}") for n in BIG}
    from_chips = _rs_chip_exchange(pair, "grads_chip_exchange")
    g_big = _rs_pair_gather({n: _sum4(from_chips[n], pair[n], chip_s, core, f"grads_chip_sum_{n}") for n in BIG},
                            "grads_pair_gather")
    g_big["w_in_b"] = g_big["w_in_b"][:, :W_IN_B_SHARD]

    tiny = _tiny_grads(grads)
    tiny_names = SMALL + REPLICATED
    tiny_full = {n: SMALL_FULL[n] if n in SMALL_FULL else p[n].shape for n in tiny_names}
    tiny_sum = _gather8(_pack([tiny[n] for n in tiny_names] + [loss[0, 0:1]], R_TINY), True, "grads_tiny_sum")
    g_tiny = dict(zip(tiny_names, _unpack(tiny_sum, [tiny_full[n] for n in tiny_names])))
    for n in SMALL:
        width = p[n].shape[SHARD_AXIS[n]]
        g_tiny[n] = lax.dynamic_slice_in_dim(g_tiny[n], chip * width, width, axis=SHARD_AXIS[n])
    total_loss = tiny_sum.reshape(-1)[LOSS_SLOT]

    out = {}
    for n in BIG:
        res = _adamw(_shard_view(n, p[n]), g_big[n], _shard_view(n, pm[n]), _shard_view(n, pv[n]), f"adamw_{n}")
        out[n] = [t.reshape(p[n].shape) for t in (g_big[n],) + tuple(res)]
    res = _adamw(*[_pack([d[n] for n in tiny_names], R_TINY_SHARD) for d in (p, g_tiny, pm, pv)], "adamw_tiny")
    unpacked = [_unpack(t, [p[n].shape for n in tiny_names]) for t in res]
    for i, n in enumerate(tiny_names):
        out[n] = [g_tiny[n]] + [u[i] for u in unpacked]

    return (total_loss, dx0.reshape(x.shape), *[out[n][0] for n in WEIGHTS], *[out[n][1] for n in WEIGHTS],
            *[out[n][2] for n in WEIGHTS], *[out[n][3] for n in WEIGHTS])
```

```python
import functools
import math

import numpy as np
import jax
import jax.numpy as jnp
from jax import lax
from jax.experimental import pallas as pl
from jax.experimental.pallas import tpu as pltpu

F32 = jnp.float32
BF16 = jnp.bfloat16
MESH = pl.DeviceIdType.MESH

D_MODEL = 1024
A_CONFIGS = ((128, 1), (512, 4), (2048, 16))
A_HEADS = 16
A_HEAD_DIM = 64
A_BLK = 128
N_BUCKETS = 32
MAX_DISTANCE = 2048
B_HEADS = 4
B_DK = 128
B_DV = 256
B_QK = 512
B_V = 1024
B_GATE_RANK = 16
B_TAU = 16.0
B_CHUNK = 64
B_IN_PAD = 3200
D_FF = 2816
EPS = 1e-6
NEG_INF = -1e30
ADAM_LR = 0.001
ADAM_B1 = 0.9
ADAM_B2 = 0.999
ADAM_EPS = 1e-08
ADAM_WD = 0.01
ADAM_STEP = 10

LANES = 1024
VMEM_LIMIT = 48 * 1024 * 1024
ROW_TILE = 256
GLA_ROWS = 512

HBM = pl.BlockSpec(memory_space=pl.ANY)


def _params(n_axes):
    return pltpu.CompilerParams(dimension_semantics=("arbitrary",) * n_axes, vmem_limit_bytes=VMEM_LIMIT)


def _pick(n, cap, mult=128):
    best = None
    for t in range(mult, min(n, cap) + 1, mult):
        if n % t == 0:
            best = t
    return n if best is None else best


def _matmul(a, b, mode, out_dtype, name, shape=None, tiles=None, a_spec=None, b_spec=None, o_spec=None, o_shape=None,
            prev=None):
    dims = {"nn": (((1,), (0,)), ((), ())), "nt": NT, "tn": TN}[mode]
    if shape is None:
        if mode == "nn":
            (M, K), (_, N) = a.shape, b.shape
        elif mode == "nt":
            (M, K), (N, _) = a.shape, b.shape
        else:
            (K, M), (_, N) = a.shape, b.shape
    else:
        M, N, K = shape
    if tiles is None:
        tiles = (_pick(M, 1024, 128 if mode == "tn" else 8), _pick(N, 1536), _pick(K, 1024 if mode != "tn" else 2048))
    tm, tn, tk = tiles
    nk = K // tk
    if a_spec is None:
        a_spec = pl.BlockSpec((tk, tm), lambda i, j, k: (k, i)) if mode == "tn" else pl.BlockSpec(
            (tm, tk), lambda i, j, k: (i, k))
    if b_spec is None:
        b_spec = pl.BlockSpec((tn, tk), lambda i, j, k: (j, k)) if mode == "nt" else pl.BlockSpec(
            (tk, tn), lambda i, j, k: (k, j))
    if o_spec is None:
        o_spec = pl.BlockSpec((tm, tn), lambda i, j, k: (i, j))
        o_shape = (M, N)

    def body(a_ref, b_ref, *rest):
        o_ref, acc_ref = rest[-2:]
        k = pl.program_id(2)
        part = lax.dot_general(a_ref[...].astype(BF16), b_ref[...].astype(BF16), dims, preferred_element_type=F32)

        @pl.when(k == 0)
        def _():
            acc_ref[...] = part

        @pl.when(k > 0)
        def _():
            acc_ref[...] += part

        @pl.when(k == nk - 1)
        def _():
            o_ref[...] = acc_ref[...].astype(o_ref.dtype)

    ins, in_specs, aliases = [a, b], [a_spec, b_spec], {}
    if prev is not None:
        ins.append(prev)
        in_specs.append(HBM)
        aliases = {2: 0}
    return pl.pallas_call(
        body, name=name, grid=(M // tm, N // tn, nk), in_specs=in_specs, out_specs=o_spec,
        out_shape=jax.ShapeDtypeStruct(o_shape, out_dtype), scratch_shapes=[pltpu.VMEM((tm, tn), F32)],
        input_output_aliases=aliases, compiler_params=_params(3))(*ins)


def _row(tr, d=D_MODEL):
    return pl.BlockSpec((tr, d), lambda i: (i, 0))


def _vec(d=D_MODEL):
    return pl.BlockSpec((1, d), lambda i: (0, 0))


def _modspec(j):
    return pl.BlockSpec((8, D_MODEL), lambda i: (0, j))


def _silu(x):
    return x * jax.nn.sigmoid(x)


def _dsilu(x):
    s = jax.nn.sigmoid(x)
    return s * (1.0 + x * (1.0 - s))


ADA_TN = 6 * D_MODEL // 4


def _ada_mod(c, w_ada, b_ada, layer, name):
    def body(c_ref, w_ref, b_ref, o_ref):
        sc = jnp.broadcast_to(_silu(c_ref[...]), (8, D_MODEL)).astype(BF16)
        o_ref[...] = jnp.dot(sc, w_ref[...], preferred_element_type=F32) + b_ref[...]

    return pl.pallas_call(
        body, name=name, grid=(4,),
        in_specs=[_vec(), pl.BlockSpec((None, D_MODEL, ADA_TN), lambda j: (j, layer, 0)),
                  pl.BlockSpec((1, ADA_TN), lambda j: (0, j))],
        out_specs=pl.BlockSpec((8, ADA_TN), lambda j: (0, j)), out_shape=jax.ShapeDtypeStruct((8, 6 * D_MODEL), F32),
        compiler_params=_params(1))(c, w_ada, b_ada)


def _ada_outer(c, dmods, name):
    def body(c_ref, d_ref, o_ref):
        row = lax.broadcasted_iota(jnp.int32, (8, 1), 0) == 0
        a = jnp.where(row, jnp.broadcast_to(_silu(c_ref[...]), (8, D_MODEL)), 0.0).astype(BF16)
        b = jnp.where(row, jnp.broadcast_to(d_ref[...], (8, ADA_TN)), 0.0).astype(BF16)
        o_ref[...] = lax.dot_general(a, b, (((0,), (0,)), ((), ())), preferred_element_type=F32)

    return pl.pallas_call(
        body, name=name, grid=(2, 4),
        in_specs=[pl.BlockSpec((1, D_MODEL), lambda l, j: (0, 0)),
                  pl.BlockSpec((None, 1, ADA_TN), lambda l, j: (l, 0, j))],
        out_specs=pl.BlockSpec((None, D_MODEL, ADA_TN), lambda l, j: (j, l, 0)),
        out_shape=jax.ShapeDtypeStruct((4, 2 * D_MODEL, ADA_TN), F32), compiler_params=_params(2))(c, dmods)


def _view_spec(tr, d, width=D_MODEL):
    return pl.BlockSpec((tr // d, d * width), lambda i: (i, 0))


def _view_shape(S, d, dtype, width=D_MODEL):
    return jax.ShapeDtypeStruct((S // d, d * width), dtype)


LANE_TILE = 128
N_LANE_TILES = D_MODEL // LANE_TILE


def _token_scratch(tr):
    return pltpu.VMEM((N_LANE_TILES, tr, LANE_TILE), F32)


def _scratch_put(scr_ref, val):
    for c in range(N_LANE_TILES):
        scr_ref[c] = val[:, c * LANE_TILE:(c + 1) * LANE_TILE]


def _scratch_get(scr_ref):
    return jnp.concatenate([scr_ref[c] for c in range(N_LANE_TILES)], axis=1)


def _store_view(scr_ref, out_ref, d):
    n = scr_ref.shape[1] // d
    for r in range(d):
        for c in range(N_LANE_TILES):
            lo = r * D_MODEL + c * LANE_TILE
            out_ref[:, lo:lo + LANE_TILE] = scr_ref.at[c][pl.ds(r, n, stride=d), :].astype(out_ref.dtype)


def _load_view(view_ref, scr_ref, d):
    n = scr_ref.shape[1] // d
    for r in range(d):
        for c in range(N_LANE_TILES):
            lo = r * D_MODEL + c * LANE_TILE
            scr_ref.at[c][pl.ds(r, n, stride=d), :] = view_ref[:, lo:lo + LANE_TILE].astype(F32)


def _norm_mod(x, gamma, mod, j_sc, j_sh, name, resid=None, gate_mod=None, j_gate=None, views=False):
    S = x.shape[0]
    tr = ROW_TILE
    has_res = resid is not None

    def body(*refs):
        if has_res:
            x_ref, y_ref, gate_ref, g_ref, sc_ref, sh_ref, xo_ref, h_ref = refs
            xn = x_ref[...] + gate_ref[0:1, :] * y_ref[...]
            xo_ref[...] = xn
        elif views:
            x_ref, g_ref, sc_ref, sh_ref, h_ref, h4_ref, h16_ref, scr_ref = refs
            xn = x_ref[...]
        else:
            x_ref, g_ref, sc_ref, sh_ref, h_ref = refs
            xn = x_ref[...]
        r = lax.rsqrt(jnp.mean(xn * xn, axis=-1, keepdims=True) + EPS)
        n = (xn * r) * g_ref[...]
        h = n * (1.0 + sc_ref[0:1, :]) + sh_ref[0:1, :]
        h_ref[...] = h.astype(BF16)
        if views:
            _scratch_put(scr_ref, h)
            _store_view(scr_ref, h4_ref, 4)
            _store_view(scr_ref, h16_ref, 16)

    scratch = []
    if has_res:
        ins = [x, resid, gate_mod, gamma, mod, mod]
        in_specs = [_row(tr), _row(tr), _modspec(j_gate), _vec(), _modspec(j_sc), _modspec(j_sh)]
        out_specs = [_row(tr), _row(tr)]
        out_shape = [jax.ShapeDtypeStruct((S, D_MODEL), F32), jax.ShapeDtypeStruct((S, D_MODEL), BF16)]
    else:
        ins = [x, gamma, mod, mod]
        in_specs = [_row(tr), _vec(), _modspec(j_sc), _modspec(j_sh)]
        out_specs = _row(tr)
        out_shape = jax.ShapeDtypeStruct((S, D_MODEL), BF16)
        if views:
            out_specs = [_row(tr), _view_spec(tr, 4), _view_spec(tr, 16)]
            out_shape = [out_shape, _view_shape(S, 4, BF16), _view_shape(S, 16, BF16)]
            scratch = [_token_scratch(tr)]
    return pl.pallas_call(body, name=name, grid=(S // tr,), in_specs=in_specs, out_specs=out_specs,
                          out_shape=out_shape, scratch_shapes=scratch, compiler_params=_params(1))(*ins)


def _final_loss(x, resid, mod, j_gate, gamma, tgt, name):
    S = x.shape[0]
    tr = ROW_TILE

    def body(x_ref, y_ref, gate_ref, g_ref, t_ref, dx_ref, loss_ref, dg_ref):
        @pl.when(pl.program_id(0) == 0)
        def _():
            loss_ref[...] = jnp.zeros_like(loss_ref)
            dg_ref[...] = jnp.zeros_like(dg_ref)

        xn = x_ref[...] + gate_ref[0:1, :] * y_ref[...]
        r = lax.rsqrt(jnp.mean(xn * xn, axis=-1, keepdims=True) + EPS)
        xhat = xn * r
        err = xhat * g_ref[...] - t_ref[...]
        loss_ref[...] += 0.5 * jnp.sum(jnp.mean(err * err, axis=-1, keepdims=True))
        dy = err * (1.0 / D_MODEL)
        dg_ref[...] += jnp.sum(dy * xhat, axis=0, keepdims=True)
        dxh = dy * g_ref[...]
        dx_ref[...] = r * (dxh - xhat * jnp.mean(dxh * xhat, axis=-1, keepdims=True))

    return pl.pallas_call(
        body, name=name, grid=(S // tr,),
        in_specs=[_row(tr), _row(tr), _modspec(j_gate), _vec(), _row(tr)],
        out_specs=[_row(tr), pl.BlockSpec((1, 128), lambda i: (0, 0)), _vec()],
        out_shape=[jax.ShapeDtypeStruct((S, D_MODEL), F32), jax.ShapeDtypeStruct((1, 128), F32),
                   jax.ShapeDtypeStruct((1, D_MODEL), F32)],
        compiler_params=_params(1))(x, resid, mod, gamma, tgt)


def _gate_bwd(dx, y, mod, j_gate, name):
    S = dx.shape[0]
    tr = ROW_TILE

    def body(dx_ref, y_ref, gate_ref, dy_ref, dg_ref):
        @pl.when(pl.program_id(0) == 0)
        def _():
            dg_ref[...] = jnp.zeros_like(dg_ref)

        dx_v = dx_ref[...]
        dy_ref[...] = (gate_ref[0:1, :] * dx_v).astype(BF16)
        dg_ref[...] += jnp.sum(dx_v * y_ref[...], axis=0, keepdims=True)

    return pl.pallas_call(
        body, name=name, grid=(S // tr,), in_specs=[_row(tr), _row(tr), _modspec(j_gate)],
        out_specs=[_row(tr), _vec()],
        out_shape=[jax.ShapeDtypeStruct((S, D_MODEL), BF16), jax.ShapeDtypeStruct((1, D_MODEL), F32)],
        compiler_params=_params(1))(dx, y, mod)


def _norm_mod_bwd(dh, x, dx_res, gamma, mod, j_sc, name, dh_views=None):
    S = x.shape[0]
    tr = ROW_TILE
    n_views = 0 if dh_views is None else 2

    def body(dh_ref, *refs):
        x_ref, dr_ref, g_ref, sc_ref, dx_ref, dsc_ref, dsh_ref, dg_ref = refs[n_views:n_views + 8]

        @pl.when(pl.program_id(0) == 0)
        def _():
            dsc_ref[...] = jnp.zeros_like(dsc_ref)
            dsh_ref[...] = jnp.zeros_like(dsh_ref)
            dg_ref[...] = jnp.zeros_like(dg_ref)

        xv = x_ref[...]
        dh_v = dh_ref[...]
        if n_views:
            scr_ref = refs[-1]
            for view_ref, d in zip(refs[:2], (4, 16)):
                _load_view(view_ref, scr_ref, d)
                dh_v = dh_v + _scratch_get(scr_ref)
        r = lax.rsqrt(jnp.mean(xv * xv, axis=-1, keepdims=True) + EPS)
        xhat = xv * r
        dsh_ref[...] += jnp.sum(dh_v, axis=0, keepdims=True)
        dsc_ref[...] += jnp.sum(dh_v * (xhat * g_ref[...]), axis=0, keepdims=True)
        dn = dh_v * (1.0 + sc_ref[0:1, :])
        dg_ref[...] += jnp.sum(dn * xhat, axis=0, keepdims=True)
        dxh = dn * g_ref[...]
        dx_ref[...] = dr_ref[...] + r * (dxh - xhat * jnp.mean(dxh * xhat, axis=-1, keepdims=True))

    vec = jax.ShapeDtypeStruct((1, D_MODEL), F32)
    views = [] if dh_views is None else list(dh_views)
    view_specs = [_view_spec(tr, 4), _view_spec(tr, 16)] if views else []
    return pl.pallas_call(
        body, name=name, grid=(S // tr,),
        in_specs=[_row(tr)] + view_specs + [_row(tr), _row(tr), _vec(), _modspec(j_sc)],
        out_specs=[_row(tr), _vec(), _vec(), _vec()],
        out_shape=[jax.ShapeDtypeStruct((S, D_MODEL), F32), vec, vec, vec],
        scratch_shapes=[_token_scratch(tr)] if views else [],
        compiler_params=_params(1))(dh, *views, x, dx_res, gamma, mod)


def _shift_down(u, halo, s):
    r = pltpu.roll(u, s, 0)
    hr = pltpu.roll(halo, s, 0)
    rid = lax.broadcasted_iota(jnp.int32, hr.shape, 0)
    top = jnp.where(rid < s, hr, r[0:8])
    return jnp.concatenate([top, r[8:]], axis=0)


def _shift_up(u, halo, s):
    n = u.shape[0]
    r = pltpu.roll(u, n - s, 0)
    hr = pltpu.roll(halo, 8 - s, 0)
    rid = lax.broadcasted_iota(jnp.int32, hr.shape, 0)
    bot = jnp.where(rid >= 8 - s, hr, r[n - 8:])
    return jnp.concatenate([r[:n - 8], bot], axis=0)


def _conv3(u, halo, w_ref, b_ref):
    u1 = _shift_down(u, halo, 1)
    u2 = _shift_down(u, halo, 2)
    return b_ref[...] + ((w_ref[0:1, :] * u2 + w_ref[1:2, :] * u1) + w_ref[2:3, :] * u), u1, u2


CONV_TC = 1408


def _conv_specs(tr, S):
    nh = D_FF // CONV_TC
    hb = tr // 8

    def cur(off):
        return pl.BlockSpec((tr, CONV_TC), lambda j, i: (i, j + off))

    def halo(off):
        return pl.BlockSpec((8, CONV_TC), lambda j, i: (jnp.maximum(i * hb - 1, 0), j + off))

    def w(off):
        return pl.BlockSpec((3, CONV_TC), lambda j, i: (0, j + off))

    def b(off):
        return pl.BlockSpec((1, CONV_TC), lambda j, i: (0, j + off))

    return nh, cur, halo, w, b


def _conv_gate(u, conv_w, conv_b, name):
    S = u.shape[0]
    tr = ROW_TILE
    nh, cur, halo, w, b = _conv_specs(tr, S)

    def body(ua_ref, ha_ref, ub_ref, hb_ref, wa_ref, wb_ref, ba_ref, bb_ref, o_ref):
        first = pl.program_id(1) == 0
        ha = jnp.where(first, 0.0, ha_ref[...])
        hbv = jnp.where(first, 0.0, hb_ref[...])
        a, _, _ = _conv3(ua_ref[...], ha, wa_ref, ba_ref)
        bb, _, _ = _conv3(ub_ref[...], hbv, wb_ref, bb_ref)
        o_ref[...] = (_silu(a) * bb).astype(BF16)

    return pl.pallas_call(
        body, name=name, grid=(nh, S // tr),
        in_specs=[cur(0), halo(0), cur(nh), halo(nh), w(0), w(nh), b(0), b(nh)],
        out_specs=pl.BlockSpec((tr, CONV_TC), lambda j, i: (i, j)),
        out_shape=jax.ShapeDtypeStruct((S, D_FF), BF16), compiler_params=_params(2))(
            u, u, u, u, conv_w, conv_w, conv_b, conv_b)


def _conv_gate_bwd(u, dact, conv_w, conv_b, name):
    S = u.shape[0]
    tr = ROW_TILE
    nh, cur, halo, w, b = _conv_specs(tr, S)

    def body(ua_ref, ha_ref, ub_ref, hb_ref, wa_ref, wb_ref, ba_ref, bb_ref, da_ref,
             dua_ref, dub_ref, dwa_ref, dwb_ref, dba_ref, dbb_ref):
        first = pl.program_id(1) == 0

        @pl.when(first)
        def _():
            for r in (dwa_ref, dwb_ref, dba_ref, dbb_ref):
                r[...] = jnp.zeros_like(r)

        ha = jnp.where(first, 0.0, ha_ref[...])
        hbv = jnp.where(first, 0.0, hb_ref[...])
        ua, ub = ua_ref[...], ub_ref[...]
        a, ua1, ua2 = _conv3(ua, ha, wa_ref, ba_ref)
        bb, ub1, ub2 = _conv3(ub, hbv, wb_ref, bb_ref)
        dact_v = da_ref[...]
        da = dact_v * bb * _dsilu(a)
        db = dact_v * _silu(a)
        dua_ref[...] = da
        dub_ref[...] = db
        for d, x0, x1, x2, dw_ref, dbias_ref in ((da, ua, ua1, ua2, dwa_ref, dba_ref),
                                                 (db, ub, ub1, ub2, dwb_ref, dbb_ref)):
            dbias_ref[...] += jnp.sum(d, axis=0, keepdims=True)
            dw_ref[0:1, :] += jnp.sum(d * x2, axis=0, keepdims=True)
            dw_ref[1:2, :] += jnp.sum(d * x1, axis=0, keepdims=True)
            dw_ref[2:3, :] += jnp.sum(d * x0, axis=0, keepdims=True)

    half = pl.BlockSpec((tr, CONV_TC), lambda j, i: (i, j))
    dw = pl.BlockSpec((8, CONV_TC), lambda j, i: (0, j))
    dbs = pl.BlockSpec((1, CONV_TC), lambda j, i: (0, j))
    f = lambda r, c: jax.ShapeDtypeStruct((r, c), F32)
    return pl.pallas_call(
        body, name=name, grid=(nh, S // tr),
        in_specs=[cur(0), halo(0), cur(nh), halo(nh), w(0), w(nh), b(0), b(nh), half],
        out_specs=[half, half, dw, dw, dbs, dbs],
        out_shape=[f(S, D_FF), f(S, D_FF), f(8, D_FF), f(8, D_FF), f(1, D_FF), f(1, D_FF)],
        compiler_params=_params(2))(u, u, u, u, conv_w, conv_w, conv_b, conv_b, dact)


def _conv_t(duc, conv_w, half, prev, name):
    S = duc.shape[0]
    tr = ROW_TILE
    nh = D_FF // CONV_TC
    hb = tr // 8
    nlast = S // 8 - 1
    nsteps = S // tr
    off = half * nh

    def body(*refs):
        if prev is None:
            d_ref, h_ref, w_ref, o_ref = refs
        else:
            d_ref, h_ref, w_ref, _, o_ref = refs
        last = pl.program_id(1) == nsteps - 1
        hv = jnp.where(last, 0.0, h_ref[...])
        d = d_ref[...]
        d1 = _shift_up(d, hv, 1)
        d2 = _shift_up(d, hv, 2)
        o_ref[...] = ((w_ref[2:3, :] * d + w_ref[1:2, :] * d1) + w_ref[0:1, :] * d2).astype(BF16)

    in_specs = [pl.BlockSpec((tr, CONV_TC), lambda j, i: (i, j)),
                pl.BlockSpec((8, CONV_TC), lambda j, i: (jnp.minimum((i + 1) * hb, nlast), j)),
                pl.BlockSpec((3, CONV_TC), lambda j, i: (0, j + off))]
    ins = [duc, duc, conv_w]
    aliases = {}
    if prev is not None:
        in_specs.append(HBM)
        ins.append(prev)
        aliases = {3: 0}
    return pl.pallas_call(
        body, name=name, grid=(nh, nsteps), in_specs=in_specs,
        out_specs=pl.BlockSpec((tr, CONV_TC), lambda j, i: (i, j + off)),
        out_shape=jax.ShapeDtypeStruct((S, 2 * D_FF), BF16), input_output_aliases=aliases,
        compiler_params=_params(2))(*ins)


def _bucket_maps():
    qi = np.arange(A_BLK)[:, None]
    ki = np.arange(2 * A_BLK)[None, :]
    steps = np.clip(qi + A_BLK - ki, 0, A_BLK)
    out = []
    max_exact = N_BUCKETS // 2
    for _, dil in A_CONFIGS:
        dist = steps * dil
        n = np.maximum(dist, max_exact).astype(np.float32)
        large = max_exact + (np.log(n / np.float32(max_exact)) / np.float32(math.log(MAX_DISTANCE / max_exact))
                             * np.float32(N_BUCKETS - max_exact)).astype(np.int32)
        large = np.minimum(large, N_BUCKETS - 1)
        out.append(np.where(dist < max_exact, dist, large))
    return np.stack(out).astype(np.int32)


def _bias_build(rel_bias, buckets, name):
    ng = len(A_CONFIGS)

    def body(t_ref, bk_ref, o_ref):
        gh = pl.program_id(0) * A_HEADS + pl.program_id(1)
        bk = bk_ref[0]
        acc = jnp.zeros((A_BLK, 2 * A_BLK), F32)
        for b in range(N_BUCKETS):
            acc = jnp.where(bk == b, t_ref[b, gh], acc)
        o_ref[0] = acc

    return pl.pallas_call(
        body, name=name, grid=(ng, A_HEADS),
        in_specs=[pl.BlockSpec(memory_space=pltpu.SMEM), pl.BlockSpec((1, A_BLK, 2 * A_BLK), lambda g, h: (g, 0, 0))],
        out_specs=pl.BlockSpec((1, A_BLK, 2 * A_BLK), lambda g, h: (g * A_HEADS + h, 0, 0)),
        out_shape=jax.ShapeDtypeStruct((ng * A_HEADS, A_BLK, 2 * A_BLK), F32),
        compiler_params=_params(2))(rel_bias, buckets)


def _bias_bwd(dbias, buckets, name):
    ng = len(A_CONFIGS)

    def body(d_ref, bk_ref, o_ref):
        gh = pl.program_id(0) * A_HEADS + pl.program_id(1)
        bk = bk_ref[0]
        d = d_ref[0]
        for b in range(N_BUCKETS):
            o_ref[b, gh] = jnp.sum(jnp.where(bk == b, d, 0.0))

    return pl.pallas_call(
        body, name=name, grid=(ng, A_HEADS),
        in_specs=[pl.BlockSpec((1, A_BLK, 2 * A_BLK), lambda g, h: (g * A_HEADS + h, 0, 0)),
                  pl.BlockSpec((1, A_BLK, 2 * A_BLK), lambda g, h: (g, 0, 0))],
        out_specs=pl.BlockSpec(memory_space=pltpu.SMEM),
        out_shape=jax.ShapeDtypeStruct((N_BUCKETS, ng * A_HEADS), F32),
        compiler_params=_params(2))(dbias, buckets)


def _attn_mask(b):
    qi = lax.broadcasted_iota(jnp.int32, (A_BLK, 2 * A_BLK), 0)
    ki = lax.broadcasted_iota(jnp.int32, (A_BLK, 2 * A_BLK), 1)
    band = (ki >= qi) & (ki <= qi + A_BLK)
    return band & ((b > 0) | (ki >= A_BLK))


def _first_head_lanes():
    return lax.broadcasted_iota(jnp.int32, (A_BLK, 2 * A_HEAD_DIM), 1) < A_HEAD_DIM


def _attn_in_specs(g, dil):
    W = A_HEADS * A_HEAD_DIM

    def spec(t, prev, nb):
        def im(r, b):
            bb = jnp.minimum(b, nb - 1)
            if prev:
                bb = jnp.maximum(bb - 1, 0)
            return (bb, r * 3 + t)
        return pl.BlockSpec((A_BLK, W), im)

    return lambda nb: [spec(0, False, nb), spec(1, False, nb), spec(1, True, nb), spec(2, False, nb),
                       spec(2, True, nb)]


def _attn_fwd(qv, bias, g, name):
    _, dil = A_CONFIGS[g]
    L = qv.shape[0]
    nb = L // A_BLK
    W = A_HEADS * A_HEAD_DIM

    def body(q_ref, kc_ref, kp_ref, vc_ref, vp_ref, bias_ref, o_ref, l_ref):
        mask = _attn_mask(pl.program_id(1))
        first = _first_head_lanes()
        for j in range(A_HEADS // 2):
            ps = slice(j * 2 * A_HEAD_DIM, (j + 1) * 2 * A_HEAD_DIM)
            q2 = q_ref[:, ps] * 0.125
            k2 = jnp.concatenate([kp_ref[:, ps], kc_ref[:, ps]], axis=0)
            v2 = jnp.concatenate([vp_ref[:, ps], vc_ref[:, ps]], axis=0)
            o_pair, l_pair = [], []
            for e in range(2):
                qh = jnp.where(first if e == 0 else ~first, q2, jnp.zeros_like(q2))
                s = lax.dot_general(qh, k2, NT, preferred_element_type=F32) + bias_ref[2 * j + e]
                s = jnp.where(mask, s, NEG_INF)
                m = jnp.max(s, axis=-1, keepdims=True)
                p = jnp.exp(s - m)
                den = jnp.sum(p, axis=-1, keepdims=True)
                o_pair.append(jnp.dot(p.astype(BF16), v2, preferred_element_type=F32) / den)
                l_pair.append(m + jnp.log(den))
            o_ref[:, ps] = jnp.where(first, o_pair[0], o_pair[1])
            l_ref[:, ps] = jnp.where(first, l_pair[0], l_pair[1])

    out_spec = pl.BlockSpec((A_BLK, W), lambda r, b: (b, r))
    return pl.pallas_call(
        body, name=name, grid=(dil, nb),
        in_specs=_attn_in_specs(g, dil)(nb) + [pl.BlockSpec((A_HEADS, A_BLK, 2 * A_BLK), lambda r, b: (g, 0, 0))],
        out_specs=[out_spec, out_spec],
        out_shape=[jax.ShapeDtypeStruct((L, dil * W), F32)] * 2,
        compiler_params=_params(2))(qv, qv, qv, qv, qv, bias)


def _mix_fwd(os, ls, name):
    S = os[0].shape[0]
    tr = ROW_TILE

    def body(o0, o1, o2, l0, l1, l2, om_ref, om4_ref, om16_ref, lt_ref, lt4_ref, lt16_ref, s_o1, s_o2, s_l1, s_l2):
        for view_ref, scr_ref, d in ((o1, s_o1, 4), (o2, s_o2, 16), (l1, s_l1, 4), (l2, s_l2, 16)):
            _load_view(view_ref, scr_ref, d)
        a, b, c = l0[...], _scratch_get(s_l1), _scratch_get(s_l2)
        m = jnp.maximum(jnp.maximum(a, b), c)
        ea, eb, ec = jnp.exp(a - m), jnp.exp(b - m), jnp.exp(c - m)
        z = (ea + eb) + ec
        om = ((ea / z) * o0[...] + (eb / z) * _scratch_get(s_o1)) + (ec / z) * _scratch_get(s_o2)
        lt = m + jnp.log(z)
        om_ref[...] = om
        lt_ref[...] = lt
        _scratch_put(s_o1, om)
        _scratch_put(s_l1, lt)
        for scr_ref, v4_ref, v16_ref in ((s_o1, om4_ref, om16_ref), (s_l1, lt4_ref, lt16_ref)):
            _store_view(scr_ref, v4_ref, 4)
            _store_view(scr_ref, v16_ref, 16)

    ins = [_row(tr), _view_spec(tr, 4), _view_spec(tr, 16)]
    outs = [jax.ShapeDtypeStruct((S, D_MODEL), F32), _view_shape(S, 4, F32), _view_shape(S, 16, F32)]
    return pl.pallas_call(
        body, name=name, grid=(S // tr,), in_specs=ins * 2, out_specs=ins * 2, out_shape=outs * 2,
        scratch_shapes=[_token_scratch(tr)] * 4, compiler_params=_params(1))(*os, *ls)


def _to_views(x, name):
    S = x.shape[0]
    tr = ROW_TILE

    def body(x_ref, v4_ref, v16_ref, scr_ref):
        _scratch_put(scr_ref, x_ref[...])
        _store_view(scr_ref, v4_ref, 4)
        _store_view(scr_ref, v16_ref, 16)

    return pl.pallas_call(
        body, name=name, grid=(S // tr,), in_specs=[_row(tr)], out_specs=[_view_spec(tr, 4), _view_spec(tr, 16)],
        out_shape=[_view_shape(S, 4, F32), _view_shape(S, 16, F32)], scratch_shapes=[_token_scratch(tr)],
        compiler_params=_params(1))(x)


def _attn_bwd(qv, bias, d_o, omix, ltot, g, name):
    _, dil = A_CONFIGS[g]
    L = qv.shape[0]
    nb = L // A_BLK
    W = A_HEADS * A_HEAD_DIM

    def body(q_ref, kc_ref, kp_ref, vc_ref, vp_ref, bias_ref, do_ref, om_ref, lt_ref,
             dqkv_ref, db_ref, cq_ref, ck_ref, cv_ref):
        r, b = pl.program_id(0), pl.program_id(1)
        dq_ref, dk_ref, dv_ref = (dqkv_ref.at[:, t * W:(t + 1) * W] for t in range(3))

        @pl.when((r == 0) & (b == 0))
        def _():
            db_ref[...] = jnp.zeros_like(db_ref)

        @pl.when(b == 0)
        def _():
            cq_ref[...] = jnp.zeros_like(cq_ref)
            ck_ref[...] = jnp.zeros_like(ck_ref)
            cv_ref[...] = jnp.zeros_like(cv_ref)

        dq_ref[...] = cq_ref[...]

        @pl.when(b < nb)
        def _():
            mask = _attn_mask(b)
            first = _first_head_lanes()
            for j in range(A_HEADS // 2):
                ps = slice(j * 2 * A_HEAD_DIM, (j + 1) * 2 * A_HEAD_DIM)
                q2 = q_ref[:, ps] * 0.125
                k2 = jnp.concatenate([kp_ref[:, ps], kc_ref[:, ps]], axis=0)
                v2 = jnp.concatenate([vp_ref[:, ps], vc_ref[:, ps]], axis=0)
                do2, om2 = do_ref[:, ps], om_ref[:, ps]
                dq_pair, dk2, dv2 = [], None, None
                for e in range(2):
                    mine = first if e == 0 else ~first
                    h = 2 * j + e
                    qh = jnp.where(mine, q2, jnp.zeros_like(q2))
                    s = lax.dot_general(qh, k2, NT, preferred_element_type=F32) + bias_ref[h]
                    s = jnp.where(mask, s, NEG_INF)
                    wp = jnp.exp(s - lt_ref[:, h * A_HEAD_DIM:h * A_HEAD_DIM + 1])
                    do_h = jnp.where(mine, do2, 0.0)
                    t_h = jnp.sum(do_h * om2, axis=-1, keepdims=True)
                    do_b = do_h.astype(BF16)
                    dp = lax.dot_general(do_b, v2, NT, preferred_element_type=F32)
                    ds = wp * (dp - t_h)
                    db_ref[h] += ds
                    ds_b = ds.astype(BF16)
                    dv_e = lax.dot_general(wp.astype(BF16), do_b, TN, preferred_element_type=F32)
                    dk_e = lax.dot_general(ds_b, qh, TN, preferred_element_type=F32)
                    dv2 = dv_e if dv2 is None else dv2 + dv_e
                    dk2 = dk_e if dk2 is None else dk2 + dk_e
                    dq_pair.append(jnp.dot(ds_b, k2, preferred_element_type=F32))
                cq_ref[:, ps] = (jnp.where(first, dq_pair[0], dq_pair[1]) * 0.125).astype(BF16)
                dk_ref[:, ps] = (ck_ref[:, ps] + dk2[:A_BLK]).astype(BF16)
                dv_ref[:, ps] = (cv_ref[:, ps] + dv2[:A_BLK]).astype(BF16)
                ck_ref[:, ps] = dk2[A_BLK:]
                cv_ref[:, ps] = dv2[A_BLK:]

        @pl.when(b == nb)
        def _():
            dk_ref[...] = ck_ref[...].astype(BF16)
            dv_ref[...] = cv_ref[...].astype(BF16)

    act = pl.BlockSpec((A_BLK, W), lambda r, b: (jnp.minimum(b, nb - 1), r))
    lag = pl.BlockSpec((A_BLK, 3 * W), lambda r, b: (jnp.maximum(b - 1, 0), r))
    full = pl.BlockSpec((A_HEADS, A_BLK, 2 * A_BLK), lambda r, b: (0, 0, 0))
    in_specs = _attn_in_specs(g, dil)(nb) + [pl.BlockSpec((A_HEADS, A_BLK, 2 * A_BLK), lambda r, b: (g, 0, 0)),
                                             act, act, act]
    return pl.pallas_call(
        body, name=name, grid=(dil, nb + 1), in_specs=in_specs, out_specs=[lag, full],
        out_shape=[jax.ShapeDtypeStruct((L, dil * 3 * W), BF16),
                   jax.ShapeDtypeStruct((A_HEADS, A_BLK, 2 * A_BLK), F32)],
        scratch_shapes=[pltpu.VMEM((A_BLK, W), BF16), pltpu.VMEM((A_BLK, W), F32), pltpu.VMEM((A_BLK, W), F32)],
        compiler_params=_params(2))(qv, qv, qv, qv, qv, bias, d_o, omix, ltot)


NT = (((1,), (1,)), ((), ()))
TN = (((0,), (0,)), ((), ()))


def _dot(a, b, dims=(((1,), (0,)), ((), ()))):
    return lax.dot_general(a.astype(BF16), b.astype(BF16), dims, preferred_element_type=F32)


def _gla_gates(glr, wg_ref, bg_ref):
    z = _dot(glr, wg_ref[...]) + bg_ref[...]
    log_sig = -(jnp.maximum(-z, 0.0) + jnp.log1p(jnp.exp(-jnp.abs(z))))
    return z, log_sig / B_TAU


def _gla_chunk(q, k, gk):
    row = lax.broadcasted_iota(jnp.int32, (B_CHUNK, B_CHUNK), 0)
    col = lax.broadcasted_iota(jnp.int32, (B_CHUNK, B_CHUNK), 1)
    causal = row >= col
    bcum = jnp.dot(causal.astype(F32), gk, precision=lax.Precision.HIGHEST, preferred_element_type=F32)
    bl = bcum[B_CHUNK - 1:B_CHUNK, :]
    qt = (q * (B_DK ** -0.5)) * jnp.exp(bcum)
    kt = k * jnp.exp(-bcum)
    kd = k * jnp.exp(bl - bcum)
    a = jnp.where(causal, _dot(qt, kt, NT), 0.0)
    return causal, bcum, bl, qt, kt, kd, a


def _gla_specs(tg):
    q = pl.BlockSpec((tg, B_DK), lambda h, i: (i, h))
    k = pl.BlockSpec((tg, B_DK), lambda h, i: (i, B_HEADS + h))
    v = pl.BlockSpec((tg, B_DV), lambda h, i: (i, B_HEADS + h))
    glr = pl.BlockSpec((tg, 128), lambda h, i: (i, 24))
    wg = pl.BlockSpec((128, B_DK), lambda h, i: (0, h))
    bg = pl.BlockSpec((1, B_DK), lambda h, i: (0, h))
    return [q, k, v, glr, wg, bg]


def _gla_fwd(proj, w_gate, b_gate, name):
    S = proj.shape[0]
    tg = GLA_ROWS
    nc = tg // B_CHUNK

    def body(q_ref, k_ref, v_ref, glr_ref, wg_ref, bg_ref, o_ref, st_ref, state_ref):
        @pl.when(pl.program_id(1) == 0)
        def _():
            state_ref[...] = jnp.zeros_like(state_ref)

        _, gk_all = _gla_gates(glr_ref[...], wg_ref, bg_ref)
        st = state_ref[...]
        for c in range(nc):
            rows = slice(c * B_CHUNK, (c + 1) * B_CHUNK)
            v = v_ref[rows, :]
            _, _, bl, qt, _, kd, a = _gla_chunk(q_ref[rows, :], k_ref[rows, :], gk_all[rows, :])
            o_ref[rows, :] = _dot(a, v) + _dot(qt, st, NT)
            st_ref[c, 0] = st
            st = st * jnp.exp(bl) + _dot(v, kd, TN)
        state_ref[...] = st

    return pl.pallas_call(
        body, name=name, grid=(B_HEADS, S // tg), in_specs=_gla_specs(tg),
        out_specs=[pl.BlockSpec((tg, B_DV), lambda h, i: (i, h)),
                   pl.BlockSpec((nc, 1, B_DV, B_DK), lambda h, i: (i, h, 0, 0))],
        out_shape=[jax.ShapeDtypeStruct((S, B_V), F32),
                   jax.ShapeDtypeStruct((S // B_CHUNK, B_HEADS, B_DV, B_DK), F32)],
        scratch_shapes=[pltpu.VMEM((B_DV, B_DK), F32)], compiler_params=_params(2))(
            proj, proj, proj, proj, w_gate, b_gate)


def _gla_bwd(proj, w_gate, b_gate, states, d_o, name):
    S = proj.shape[0]
    tg = GLA_ROWS
    nc = tg // B_CHUNK
    ni = S // tg

    def rev(spec):
        return pl.BlockSpec(spec.block_shape, lambda h, i, im=spec.index_map: im(h, ni - 1 - i))

    def body(q_ref, k_ref, v_ref, glr_ref, wg_ref, bg_ref, st_ref, do_ref,
             dq_ref, dk_ref, dv_ref, dz_ref, dbg_ref, dstate_ref):
        @pl.when(pl.program_id(1) == 0)
        def _():
            dstate_ref[...] = jnp.zeros_like(dstate_ref)
            dbg_ref[...] = jnp.zeros_like(dbg_ref)

        z_all, gk_all = _gla_gates(glr_ref[...], wg_ref, bg_ref)
        dst = dstate_ref[...]
        for c in range(nc - 1, -1, -1):
            rows = slice(c * B_CHUNK, (c + 1) * B_CHUNK)
            v = v_ref[rows, :]
            d_out = do_ref[rows, :]
            st = st_ref[c, 0]
            causal, bcum, bl, qt, kt, kd, a = _gla_chunk(q_ref[rows, :], k_ref[rows, :], gk_all[rows, :])
            da = jnp.where(causal, _dot(d_out, v, NT), 0.0)
            dv_ref[rows, :] = (_dot(a, d_out, TN) + _dot(kd, dst, NT)).astype(BF16)
            dqt = _dot(da, kt) + _dot(d_out, st)
            dkt = _dot(da, qt, TN)
            dkd = _dot(v, dst)
            dec = jnp.exp(bl)
            ddec = jnp.sum(dst * st, axis=0, keepdims=True)
            dst = dst * dec + _dot(d_out, qt, TN)
            dq_ref[rows, :] = (dqt * jnp.exp(bcum) * (B_DK ** -0.5)).astype(BF16)
            dk_ref[rows, :] = (dkt * jnp.exp(-bcum) + dkd * jnp.exp(bl - bcum)).astype(BF16)
            db = (dqt * qt - dkt * kt) - dkd * kd
            dbl = jnp.sum(dkd * kd, axis=0, keepdims=True) + dec * ddec
            upper = jnp.logical_not(causal) | (lax.broadcasted_iota(jnp.int32, (B_CHUNK, B_CHUNK), 0)
                                               == lax.broadcasted_iota(jnp.int32, (B_CHUNK, B_CHUNK), 1))
            dgk = jnp.dot(upper.astype(F32), db, precision=lax.Precision.HIGHEST, preferred_element_type=F32) + dbl
            dz = dgk * (1.0 / B_TAU) * jax.nn.sigmoid(-z_all[rows, :])
            dz_ref[rows, :] = dz
            dbg_ref[...] += jnp.sum(dz, axis=0, keepdims=True)
        dstate_ref[...] = dst

    qs = pl.BlockSpec((tg, B_DK), lambda h, i: (i, h))
    vs = pl.BlockSpec((tg, B_DV), lambda h, i: (i, h))
    in_specs = [rev(s) if n < 4 else s for n, s in enumerate(_gla_specs(tg))]
    in_specs += [rev(pl.BlockSpec((nc, 1, B_DV, B_DK), lambda h, i: (i, h, 0, 0))), rev(vs)]
    return pl.pallas_call(
        body, name=name, grid=(B_HEADS, ni), in_specs=in_specs,
        out_specs=[rev(qs), rev(qs), rev(vs), rev(qs), pl.BlockSpec((1, B_DK), lambda h, i: (0, h))],
        out_shape=[jax.ShapeDtypeStruct((S, B_QK), BF16), jax.ShapeDtypeStruct((S, B_QK), BF16),
                   jax.ShapeDtypeStruct((S, B_V), BF16), jax.ShapeDtypeStruct((S, B_QK), F32),
                   jax.ShapeDtypeStruct((1, B_QK), F32)],
        scratch_shapes=[pltpu.VMEM((B_DV, B_DK), F32)], compiler_params=_params(2))(
            proj, proj, proj, proj, w_gate, b_gate, states, d_o)


def _gla_out(o, proj, gnorm, name):
    S = o.shape[0]
    tr = ROW_TILE

    def body(o_ref, r_ref, g_ref, y_ref):
        for h in range(B_HEADS):
            hs = slice(h * B_DV, (h + 1) * B_DV)
            oh = o_ref[:, hs]
            rs = lax.rsqrt(jnp.mean(oh * oh, axis=-1, keepdims=True) + EPS)
            y_ref[:, hs] = (((oh * rs) * g_ref[...]) * _silu(r_ref[:, hs])).astype(BF16)

    return pl.pallas_call(
        body, name=name, grid=(S // tr,),
        in_specs=[_row(tr), pl.BlockSpec((tr, B_V), lambda i: (i, 2)), _vec(B_DV)], out_specs=_row(tr),
        out_shape=jax.ShapeDtypeStruct((S, B_V), BF16), compiler_params=_params(1))(o, proj, gnorm)


def _gla_out_bwd(o, proj, gnorm, d_y, name):
    S = o.shape[0]
    tr = ROW_TILE

    def body(o_ref, r_ref, g_ref, dy_ref, do_ref, dr_ref, dg_ref):
        @pl.when(pl.program_id(0) == 0)
        def _():
            dg_ref[...] = jnp.zeros_like(dg_ref)

        for h in range(B_HEADS):
            hs = slice(h * B_DV, (h + 1) * B_DV)
            oh, rv, dyv = o_ref[:, hs], r_ref[:, hs], dy_ref[:, hs]
            rs = lax.rsqrt(jnp.mean(oh * oh, axis=-1, keepdims=True) + EPS)
            xhat = oh * rs
            dr_ref[:, hs] = (dyv * (xhat * g_ref[...]) * _dsilu(rv)).astype(BF16)
            dn = dyv * _silu(rv)
            dg_ref[...] += jnp.sum(dn * xhat, axis=0, keepdims=True)
            dxh = dn * g_ref[...]
            do_ref[:, hs] = rs * (dxh - xhat * jnp.mean(dxh * xhat, axis=-1, keepdims=True))

    return pl.pallas_call(
        body, name=name, grid=(S // tr,),
        in_specs=[_row(tr), pl.BlockSpec((tr, B_V), lambda i: (i, 2)), _vec(B_DV), _row(tr)],
        out_specs=[_row(tr), _row(tr), _vec(B_DV)],
        out_shape=[jax.ShapeDtypeStruct((S, B_V), F32), jax.ShapeDtypeStruct((S, B_V), BF16),
                   jax.ShapeDtypeStruct((1, B_DV), F32)],
        compiler_params=_params(1))(o, proj, gnorm, d_y)


J_SH1, J_SC1, J_G1, J_SH2, J_SC2, J_G2 = range(6)


IN_A_TN = 768
UP_TN = 2 * D_FF // 4
TOKEN_TK = 2048


def _ffn_fwd(h, w, i, tag):
    S = h.shape[0]
    u = _matmul(h, w["w_up"], "nn", F32, f"up{tag}", shape=(S, 2 * D_FF, D_MODEL), tiles=(1024, UP_TN, D_MODEL),
                b_spec=pl.BlockSpec((None, D_MODEL, UP_TN), lambda m, j, k: (j, i, 0)))
    act = _conv_gate(u, w["conv_w"][i], w["conv_b"][i:i + 1], f"conv_gate{tag}")
    f = _matmul(act, w["w_down"], "nn", F32, f"down{tag}", shape=(S, D_MODEL, D_FF), tiles=(1024, D_MODEL, D_FF),
                b_spec=pl.BlockSpec((D_FF, D_MODEL), lambda m, j, k: (i, j)))
    return u, act, f


def _ffn_bwd(dx_out, f, u, act, h, x_in, mod, w, i, tag, prev):
    S = h.shape[0]
    df, dg2 = _gate_bwd(dx_out, f, mod, J_G2, f"gate2_bwd{tag}")
    dact = _matmul(df, w["w_down"], "nt", F32, f"down_dx{tag}", shape=(S, D_FF, D_MODEL),
                   tiles=(1024, D_FF // 2, D_MODEL),
                   b_spec=pl.BlockSpec((D_FF // 2, D_MODEL), lambda m, j, k: (2 * i + j, k)))
    d_w_down = _matmul(act, df, "tn", F32, f"down_dw{tag}", shape=(D_FF, D_MODEL, S),
                       tiles=(D_FF // 2, D_MODEL, min(S, TOKEN_TK)),
                       o_spec=pl.BlockSpec((D_FF // 2, D_MODEL), lambda m, j, k: (2 * i + m, j)),
                       o_shape=(2 * D_FF, D_MODEL), prev=None if prev is None else prev["w_down"])
    duca, ducb, dcwa, dcwb, dcba, dcbb = _conv_gate_bwd(u, dact, w["conv_w"][i], w["conv_b"][i:i + 1],
                                                        f"conv_gate_bwd{tag}")
    du = _conv_t(duca, w["conv_w"][i], 0, None, f"conv_t_a{tag}")
    du = _conv_t(ducb, w["conv_w"][i], 1, du, f"conv_t_b{tag}")
    dh = _matmul(du, w["w_up"], "nt", F32, f"up_dx{tag}", shape=(S, D_MODEL, 2 * D_FF), tiles=(1024, D_MODEL, UP_TN),
                 b_spec=pl.BlockSpec((None, D_MODEL, UP_TN), lambda m, j, k: (k, i, 0)))
    d_w_up = _matmul(h, du, "tn", F32, f"up_dw{tag}", shape=(D_MODEL, 2 * D_FF, S),
                     tiles=(D_MODEL, UP_TN, min(S, TOKEN_TK)),
                     o_spec=pl.BlockSpec((None, D_MODEL, UP_TN), lambda m, j, k: (j, i, 0)),
                     o_shape=(4, 2 * D_MODEL, UP_TN), prev=None if prev is None else prev["w_up"])
    dx_in, dsc2, dsh2, dgam = _norm_mod_bwd(dh, x_in, dx_out, w["norm_ffn"][i:i + 1], mod, J_SC2,
                                            f"norm_ffn_bwd{tag}")
    grads = dict(w_down=d_w_down, w_up=d_w_up, norm_ffn=dgam,
                 conv_w=jnp.concatenate([dcwa[0:3], dcwb[0:3]], axis=1),
                 conv_b=jnp.concatenate([dcba, dcbb], axis=1))
    return dx_in, (dsh2, dsc2, dg2), grads


def _local_step(x, c, tgt, w):
    buckets = jnp.asarray(_bucket_maps())
    S = x.shape[0]
    mods = [_ada_mod(c, w["w_ada"], w["b_ada"][i:i + 1], i, f"ada_mod{i}") for i in range(2)]

    h1 = _norm_mod(x, w["norm_mix"][0:1], mods[0], J_SC1, J_SH1, "norm_mix0", views=True)
    geo = []
    for g, (_, dil) in enumerate(A_CONFIGS):
        tm = min(1024, S // dil)
        geo.append((dil, tm, S // dil // tm))
    w_cols = [pl.BlockSpec((None, D_MODEL, IN_A_TN), lambda m, j, k, g=g: ((4 * g + j) // 3, 0, (4 * g + j) % 3))
              for g in range(3)]
    qkv = [_matmul(h1[g], w["w_in_a"], "nn", BF16, f"in_a{g}", shape=(S, 3 * D_MODEL, D_MODEL),
                   tiles=(tm, IN_A_TN, D_MODEL),
                   a_spec=pl.BlockSpec((tm, D_MODEL), lambda m, j, k, n=n_i: (m % n, m // n)), b_spec=w_cols[g],
                   o_spec=pl.BlockSpec((tm, IN_A_TN), lambda m, j, k, n=n_i: (m % n, (m // n) * 4 + j)),
                   o_shape=(S // dil, dil * 3 * D_MODEL))
           for g, (dil, tm, n_i) in enumerate(geo)]
    bias = _bias_build(w["rel_bias"], buckets, "bias_build")
    os_, ls_ = zip(*[_attn_fwd(qkv[g], bias, g, f"attn_fwd{g}") for g in range(3)])
    omix, omix4, omix16, ltot, ltot4, ltot16 = _mix_fwd(os_, ls_, "mix_fwd")
    y0 = _matmul(omix, w["w_out_a"], "nn", F32, "out_a")
    x1, h2 = _norm_mod(x, w["norm_ffn"][0:1], mods[0], J_SC2, J_SH2, "norm_ffn0", resid=y0, gate_mod=mods[0],
                       j_gate=J_G1)
    u0, act0, f0 = _ffn_fwd(h2, w, 0, "0")

    x2, h3 = _norm_mod(x1, w["norm_mix"][1:2], mods[1], J_SC1, J_SH1, "norm_mix1", resid=f0, gate_mod=mods[0],
                       j_gate=J_G2)
    proj = _matmul(h3, w["w_in_b"], "nn", F32, "in_b")
    o_gla, states = _gla_fwd(proj, w["w_gate_b"], w["b_gate_b"], "gla_fwd")
    on = _gla_out(o_gla, proj, w["gnorm_b"], "gla_out")
    y1 = _matmul(on, w["w_out_b"], "nn", F32, "out_b")
    x3, h4 = _norm_mod(x2, w["norm_ffn"][1:2], mods[1], J_SC2, J_SH2, "norm_ffn1", resid=y1, gate_mod=mods[1],
                       j_gate=J_G1)
    u1, act1, f1 = _ffn_fwd(h4, w, 1, "1")

    dx4, loss, d_norm_final = _final_loss(x3, f1, mods[1], J_G2, w["norm_final"], tgt, "final_loss")

    dx3, (dsh2_1, dsc2_1, dg2_1), g_ffn1 = _ffn_bwd(dx4, f1, u1, act1, h4, x3, mods[1], w, 1, "1", None)
    dy1, dg1_1 = _gate_bwd(dx3, y1, mods[1], J_G1, "gate1_bwd1")
    d_on = _matmul(dy1, w["w_out_b"], "nt", F32, "out_b_dx")
    d_w_out_b = _matmul(on, dy1, "tn", F32, "out_b_dw")
    d_ogla, d_r, d_gnorm = _gla_out_bwd(o_gla, proj, w["gnorm_b"], d_on, "gla_out_bwd")
    dq, dk, dv, dz, d_b_gate = _gla_bwd(proj, w["w_gate_b"], w["b_gate_b"], states, d_ogla, "gla_bwd")
    d_glr = _matmul(dz, w["w_gate_b"], "nt", BF16, "gate_dx")
    d_w_gate = _matmul(proj[:, 3072:3200], dz, "tn", F32, "gate_dw")
    dproj = jnp.concatenate([dq, dk, dv, d_r, d_glr], axis=1)
    dh3 = _matmul(dproj, w["w_in_b"], "nt", F32, "in_b_dx")
    d_w_in_b = _matmul(h3, dproj, "tn", F32, "in_b_dw")
    dx2, dsc1_1, dsh1_1, d_nmix1 = _norm_mod_bwd(dh3, x2, dx3, w["norm_mix"][1:2], mods[1], J_SC1, "norm_mix_bwd1")
    dmod1 = jnp.concatenate([dsh1_1, dsc1_1, dg1_1, dsh2_1, dsc2_1, dg2_1], axis=1)

    dx1, (dsh2_0, dsc2_0, dg2_0), g_ffn0 = _ffn_bwd(dx2, f0, u0, act0, h2, x1, mods[0], w, 0, "0", g_ffn1)
    dy0, dg1_0 = _gate_bwd(dx1, y0, mods[0], J_G1, "gate1_bwd0")
    d_omix = _matmul(dy0, w["w_out_a"], "nt", F32, "out_a_dx")
    d_w_out_a = _matmul(omix, dy0, "tn", F32, "out_a_dw")
    d_omix4, d_omix16 = _to_views(d_omix, "d_omix_views")
    d_omix_v, omix_v, ltot_v = (d_omix, d_omix4, d_omix16), (omix, omix4, omix16), (ltot, ltot4, ltot16)
    dqkv, dbs = zip(*[_attn_bwd(qkv[g], bias, d_omix_v[g], omix_v[g], ltot_v[g], g, f"attn_bwd{g}")
                      for g in range(3)])
    d_rel_bias = _bias_bwd(jnp.concatenate(dbs, axis=0), buckets, "bias_bwd")
    dh1, d_w_in_a = [], None
    for g, (dil, tm, n_i) in enumerate(geo):
        dh1.append(_matmul(
            dqkv[g], w["w_in_a"], "nt", F32, f"in_a_dx{g}", shape=(S, D_MODEL, 3 * D_MODEL),
            tiles=(tm, D_MODEL, IN_A_TN),
            a_spec=pl.BlockSpec((tm, IN_A_TN), lambda m, j, k, n=n_i: (m % n, (m // n) * 4 + k)),
            b_spec=pl.BlockSpec((None, D_MODEL, IN_A_TN), lambda m, j, k, g=g: ((4 * g + k) // 3, j, (4 * g + k) % 3)),
            o_spec=pl.BlockSpec((tm, D_MODEL), lambda m, j, k, n=n_i: (m % n, m // n)),
            o_shape=(S // dil, dil * D_MODEL)))
        tk = min(TOKEN_TK, S // dil)
        n_k = S // dil // tk
        d_w_in_a = _matmul(
            h1[g], dqkv[g], "tn", F32, f"in_a_dw{g}", shape=(D_MODEL, 3 * D_MODEL, S), tiles=(D_MODEL, IN_A_TN, tk),
            a_spec=pl.BlockSpec((tk, D_MODEL), lambda m, j, k, n=n_k: (k % n, k // n)),
            b_spec=pl.BlockSpec((tk, IN_A_TN), lambda m, j, k, n=n_k: (k % n, (k // n) * 4 + j)),
            o_spec=pl.BlockSpec((None, D_MODEL, IN_A_TN), lambda m, j, k, g=g: ((4 * g + j) // 3, m, (4 * g + j) % 3)),
            o_shape=(4, D_MODEL, 9 * D_MODEL // 4), prev=d_w_in_a)
    dx0, dsc1_0, dsh1_0, d_nmix0 = _norm_mod_bwd(dh1[0], x, dx1, w["norm_mix"][0:1], mods[0], J_SC1, "norm_mix_bwd0",
                                                 dh_views=dh1[1:])
    dmod0 = jnp.concatenate([dsh1_0, dsc1_0, dg1_0, dsh2_0, dsc2_0, dg2_0], axis=1)

    grads = dict(
        w_in_a=d_w_in_a, w_out_a=d_w_out_a, rel_bias=d_rel_bias, w_in_b=d_w_in_b, w_gate_b=d_w_gate,
        b_gate_b=d_b_gate, gnorm_b=d_gnorm, w_out_b=d_w_out_b,
        norm_mix=jnp.concatenate([d_nmix0, d_nmix1], axis=0),
        norm_ffn=jnp.concatenate([g_ffn0["norm_ffn"], g_ffn1["norm_ffn"]], axis=0),
        w_ada=_ada_outer(c, jnp.stack([dmod0, dmod1]), "ada_outer"),
        b_ada=jnp.concatenate([dmod0, dmod1], axis=0),
        w_up=g_ffn0["w_up"],
        conv_w=jnp.stack([g_ffn0["conv_w"], g_ffn1["conv_w"]]),
        conv_b=jnp.concatenate([g_ffn0["conv_b"], g_ffn1["conv_b"]], axis=0),
        w_down=g_ffn0["w_down"],
        norm_final=d_norm_final)
    return loss, dx0, grads


N_CHIPS = 4
N_DEV = 8
WEIGHTS = ("w_in_a", "w_out_a", "rel_bias", "w_in_b", "w_gate_b", "b_gate_b", "gnorm_b", "w_out_b", "norm_mix",
           "norm_ffn", "w_ada", "b_ada", "w_up", "conv_w", "conv_b", "w_down", "norm_final")
SHARD_AXIS = dict(w_in_a=2, w_out_a=1, w_in_b=2, w_gate_b=2, b_gate_b=1, gnorm_b=1, w_out_b=1, w_ada=2, w_up=2,
                  conv_w=2, w_down=1)
SHARDED = tuple(n for n in WEIGHTS if n in SHARD_AXIS)
REPLICATED = tuple(n for n in WEIGHTS if n not in SHARD_AXIS)
BIG = ("w_in_a", "w_out_a", "w_in_b", "w_out_b", "w_ada", "w_up", "w_down")
SMALL = ("w_gate_b", "b_gate_b", "gnorm_b", "conv_w")
SMALL_FULL = dict(w_gate_b=(1, 16, 512), b_gate_b=(1, 512), gnorm_b=(1, 256), conv_w=(2, 3, 5632))
R_SMALL = 16
R_TINY = 72
LOSS_SLOT = 72960
R_TINY_SHARD = 40
W_IN_B_PAD = 896

COMM_VIEW = dict(
    w_in_a=((4096, 2304), 1024, 512, 512),
    w_out_a=((1024, 1024), 256, 128, 128),
    w_in_b=((4096, W_IN_B_PAD), 1024, 512, 512),
    w_out_b=((1024, 1024), 256, 128, 128),
    w_ada=((8192, 1536), 2048, 1024, 1024),
    w_up=((8192, 1408), 2048, 1024, 1024),
    w_down=((5632, 1024), 704, 2816, 704))


def _pack(arrs, rows):
    flat = jnp.concatenate([a.reshape(-1) for a in arrs])
    return jnp.pad(flat, (0, rows * LANES - flat.shape[0])).reshape(rows, LANES)


def _unpack(flat2d, shapes):
    flat = flat2d.reshape(-1)
    out, off = [], 0
    for shp in shapes:
        n = math.prod(shp)
        out.append(flat[off:off + n].reshape(shp))
        off += n
    return out


def _chip_slice(a, axis, k):
    n = a.shape[axis] // N_CHIPS
    return lax.slice_in_dim(a, k * n, (k + 1) * n, axis=axis)


def _place():
    mx, my, mc = lax.axis_index("x"), lax.axis_index("y"), lax.axis_index("c")
    chips = [(1 - mx, my), (mx, 1 - my), (1 - mx, 1 - my)]
    return mx, my, mc, chips


def _rcopy(src, dst, send_sem, recv_sem, dev):
    return pltpu.make_async_remote_copy(src_ref=src, dst_ref=dst, send_sem=send_sem, recv_sem=recv_sem,
                                        device_id=dev, device_id_type=MESH)


def _comm_call(body, name, ins, out_shapes, n_sems, in_place=False):
    n_in, n_out = len(ins), len(out_shapes)

    def wrapped(*refs):
        body(refs[:n_in], refs[n_in:n_in + n_out], *refs[n_in + n_out:])

    return pl.pallas_call(
        wrapped, name=name, in_specs=[HBM] * n_in, out_specs=[HBM] * n_out, out_shape=out_shapes,
        input_output_aliases={i: i for i in range(n_in)} if in_place else {},
        scratch_shapes=[pltpu.SemaphoreType.DMA((n_sems,)), pltpu.SemaphoreType.DMA((n_sems,))])(*ins)


DMA_CHUNK_BYTES = 2 * 1024 * 1024


def _rows(ref, start, size):
    return ref.at[pl.ds(pl.multiple_of(start, 16), size), :]


def _block(ref, name, k, h):
    _, bk, bh, nr = COMM_VIEW[name]
    return _rows(ref, bk * k + bh * h, nr)


def _chunks(nr, row_bytes):
    n = 1
    while nr % (2 * n) == 0 and (nr // (2 * n)) % 16 == 0 and (nr // n) * row_bytes > DMA_CHUNK_BYTES:
        n *= 2
    return [(i * (nr // n), nr // n) for i in range(n)]


def _gather_big(views, name):
    names = BIG

    def body(x_refs, out_refs, send_sems, recv_sems):
        mx, my, mc, chips = _place()
        chip = 2 * mx + my
        sibling = (mx, my, 1 - mc)
        sends = []
        for a, n in enumerate(names):
            for j, (cx, cy) in enumerate(chips):
                blk = _block(out_refs[a], n, chip, mc)
                cp = _rcopy(blk, blk, send_sems.at[6 * a + j], recv_sems.at[6 * a + j], (cx, cy, mc))
                cp.start()
                sends.append(cp)
        for a, n in enumerate(names):
            for j, (cx, cy) in enumerate(chips):
                blk = _block(out_refs[a], n, 2 * cx + cy, mc)
                _rcopy(blk, blk, send_sems.at[6 * a + j], recv_sems.at[6 * a + j], sibling).wait_recv()
                cp = _rcopy(blk, blk, send_sems.at[6 * a + 3 + j], recv_sems.at[6 * a + 3 + j], sibling)
                cp.start()
                sends.append(cp)
        for a, n in enumerate(names):
            for j, (cx, cy) in enumerate(chips):
                blk = _block(out_refs[a], n, 2 * cx + cy, 1 - mc)
                _rcopy(blk, blk, send_sems.at[6 * a + 3 + j], recv_sems.at[6 * a + 3 + j], sibling).wait_recv()
        for cp in sends:
            cp.wait_send()

    outs = _comm_call(body, name, [views[n] for n in names],
                      [jax.ShapeDtypeStruct(views[n].shape, views[n].dtype) for n in names], 6 * len(names),
                      in_place=True)
    return dict(zip(names, outs))


def _rs_pair_exchange(views, name):
    names = BIG

    def body(g_refs, recv_refs, send_sems, recv_sems):
        mx, my, mc, _ = _place()
        sibling = (mx, my, 1 - mc)
        for a, n in enumerate(names):
            (_, cols), _, _, nr = COMM_VIEW[n]
            for k in range(N_CHIPS):
                src = _block(g_refs[a], n, k, 1 - mc)
                for start, size in _chunks(nr, cols * 4):
                    _rcopy(src.at[pl.ds(start, size), :], recv_refs[a].at[k, pl.ds(start, size), :],
                           send_sems.at[a], recv_sems.at[a], sibling).start()
        for a in range(len(names)):
            _rcopy(recv_refs[a], recv_refs[a], send_sems.at[a], recv_sems.at[a], sibling).wait()

    outs = _comm_call(body, name, [views[n] for n in names],
                      [jax.ShapeDtypeStruct((N_CHIPS, COMM_VIEW[n][3], COMM_VIEW[n][0][1]), F32) for n in names],
                      len(names))
    return dict(zip(names, outs))


def _pair_add(view, recv, c_idx, n, out_dtype, name):
    (_, cols), bk, bh, nr = COMM_VIEW[n]
    tr = _pick(math.gcd(bk, bh, nr), 256, 8)

    def body(c_ref, g_ref, r_ref, o_ref):
        o_ref[...] = (g_ref[...] + r_ref[...]).astype(o_ref.dtype)

    piece = pl.BlockSpec((None, tr, cols), lambda k, i, c_ref: (k, i, 0))
    return pl.pallas_call(
        body, name=name,
        grid_spec=pltpu.PrefetchScalarGridSpec(
            num_scalar_prefetch=1, grid=(N_CHIPS, nr // tr),
            in_specs=[pl.BlockSpec((tr, cols), lambda k, i, c_ref: ((bk * k + bh * c_ref[0]) // tr + i, 0)), piece],
            out_specs=piece),
        out_shape=jax.ShapeDtypeStruct((N_CHIPS, nr, cols), out_dtype), compiler_params=_params(2))(
            c_idx, view, recv)


def _rs_chip_exchange(q, name):
    names = BIG

    def body(q_refs, out_refs, send_sems, recv_sems):
        mx, my, mc, chips = _place()
        chip = 2 * mx + my
        sends = []
        for a in range(len(names)):
            for j, (cx, cy) in enumerate(chips):
                cp = _rcopy(q_refs[a].at[2 * cx + cy], out_refs[a].at[chip], send_sems.at[3 * a + j],
                            recv_sems.at[3 * a + j], (cx, cy, mc))
                cp.start()
                sends.append(cp)
        for a in range(len(names)):
            for j, (cx, cy) in enumerate(chips):
                blk = out_refs[a].at[2 * cx + cy]
                _rcopy(blk, blk, send_sems.at[3 * a + j], recv_sems.at[3 * a + j], (cx, cy, mc)).wait_recv()
        for cp in sends:
            cp.wait_send()

    outs = _comm_call(body, name, [q[n] for n in names],
                      [jax.ShapeDtypeStruct(q[n].shape, q[n].dtype) for n in names], 3 * len(names))
    return dict(zip(names, outs))


def _rs_pair_gather(r, name):
    names = BIG

    def body(r_refs, out_refs, send_sems, recv_sems):
        mx, my, mc, _ = _place()
        sibling = (mx, my, 1 - mc)
        for a, n in enumerate(names):
            (_, cols), _, _, nr = COMM_VIEW[n]
            for start, size in _chunks(nr, cols * 4):
                rows = _rows(out_refs[a], mc * nr + start, size)
                _rcopy(rows, rows, send_sems.at[a], recv_sems.at[a], sibling).start()
        for a, n in enumerate(names):
            nr = COMM_VIEW[n][3]
            _rcopy(_rows(out_refs[a], mc * nr, nr), _rows(out_refs[a], (1 - mc) * nr, nr), send_sems.at[a],
                   recv_sems.at[a], sibling).wait()

    outs = _comm_call(body, name, [r[n] for n in names],
                      [jax.ShapeDtypeStruct(r[n].shape, F32) for n in names], len(names), in_place=True)
    return dict(zip(names, outs))


def _gather8(x, reduce, name):
    rows = x.shape[0]

    def body(x_ref, out_ref, *rest):
        if reduce:
            buf_ref, send_sems, recv_sems = rest
        else:
            (send_sems, recv_sems), buf_ref = rest, out_ref
        mx, my, mc, _ = _place()
        me = 4 * mx + 2 * my + mc
        buf_ref[me] = x_ref[...]
        peers = []
        for j in range(1, N_DEV):
            px = 1 - mx if j & 4 else mx
            py = 1 - my if j & 2 else my
            pc = 1 - mc if j & 1 else mc
            peers.append((px, py, pc))
        sends = [_rcopy(x_ref, buf_ref.at[me], send_sems.at[j], recv_sems.at[j], p) for j, p in enumerate(peers)]
        for cp in sends:
            cp.start()
        for j, (px, py, pc) in enumerate(peers):
            _rcopy(x_ref, buf_ref.at[4 * px + 2 * py + pc], send_sems.at[j], recv_sems.at[j], (px, py, pc)).wait_recv()
        for cp in sends:
            cp.wait_send()
        if reduce:
            acc = buf_ref[0]
            for d in range(1, N_DEV):
                acc = acc + buf_ref[d]
            out_ref[...] = acc

    vmem = pl.BlockSpec(memory_space=pltpu.VMEM)
    sems = [pltpu.SemaphoreType.DMA((N_DEV - 1,)), pltpu.SemaphoreType.DMA((N_DEV - 1,))]
    if reduce:
        out_shape = jax.ShapeDtypeStruct((rows, LANES), F32)
        scratch = [pltpu.VMEM((N_DEV, rows, LANES), F32)] + sems
    else:
        out_shape = jax.ShapeDtypeStruct((N_DEV, rows, LANES), F32)
        scratch = sems
    return pl.pallas_call(body, name=name, in_specs=[vmem], out_specs=vmem, out_shape=out_shape,
                          scratch_shapes=scratch)(x)


def _sum4(p, q, chip, core, name):
    _, nr, cols = p.shape
    tr = _pick(nr, 256, 8)

    def body(chip_ref, core_ref, p0, p1, p2, p3, own, o_ref):
        s = [jnp.where(chip_ref[0] == k, own[...], pk[...]).astype(F32) for k, pk in enumerate((p0, p1, p2, p3))]
        o_ref[...] = ((s[0] + s[1]) + s[2]) + s[3]

    return pl.pallas_call(
        body, name=name,
        grid_spec=pltpu.PrefetchScalarGridSpec(
            num_scalar_prefetch=2, grid=(nr // tr,),
            in_specs=[pl.BlockSpec((None, tr, cols), lambda i, ch, co, k=k: (jnp.where(ch[0] == k, k ^ 1, k), i, 0))
                      for k in range(N_CHIPS)]
            + [pl.BlockSpec((None, tr, cols), lambda i, ch, co: (ch[0], i, 0))],
            out_specs=pl.BlockSpec((tr, cols), lambda i, ch, co: (co[0] * (nr // tr) + i, 0))),
        out_shape=jax.ShapeDtypeStruct((2 * nr, cols), F32), compiler_params=_params(1))(chip, core, p, p, p, p, q)


def _place_shard(shard, chip, n, name):
    (rows, cols), bk, bh, nr = COMM_VIEW[n]
    tr = _pick(math.gcd(bk, bh, nr), 256, 16)

    def body(chip_ref, x_ref, o_ref):
        o_ref[...] = x_ref[...].astype(BF16)

    return pl.pallas_call(
        body, name=name,
        grid_spec=pltpu.PrefetchScalarGridSpec(
            num_scalar_prefetch=1, grid=(2, nr // tr),
            in_specs=[pl.BlockSpec((tr, cols), lambda h, i, ch: (h * (nr // tr) + i, 0))],
            out_specs=pl.BlockSpec((tr, cols), lambda h, i, ch: ((bk * ch[0] + bh * h) // tr + i, 0))),
        out_shape=jax.ShapeDtypeStruct((rows, cols), BF16), compiler_params=_params(2))(chip, shard)


def _adamw(w, g, m, v, name):
    rows, cols = w.shape
    tr = _pick(rows, max(8, (1 << 20) // (4 * cols)), 8)

    def body(w_ref, g_ref, m_ref, v_ref, d_ref, mo_ref, vo_ref):
        gv = g_ref[...]
        mn = ADAM_B1 * m_ref[...] + (1.0 - ADAM_B1) * gv
        vn = ADAM_B2 * v_ref[...] + (1.0 - ADAM_B2) * (gv * gv)
        m_hat = mn / (1.0 - ADAM_B1 ** ADAM_STEP)
        v_hat = vn / (1.0 - ADAM_B2 ** ADAM_STEP)
        d_ref[...] = -ADAM_LR * (m_hat / (jnp.sqrt(v_hat) + ADAM_EPS) + ADAM_WD * w_ref[...])
        mo_ref[...] = mn
        vo_ref[...] = vn

    shape = jax.ShapeDtypeStruct(w.shape, F32)
    return pl.pallas_call(
        body, name=name, grid=(rows // tr,), in_specs=[_row(tr, cols)] * 4, out_specs=[_row(tr, cols)] * 3,
        out_shape=[shape] * 3, compiler_params=_params(1))(w, g, m, v)


W_IN_B_SHARD = 772


def _shard_view(n, a):
    return a.reshape(-1, a.shape[-1])


def _gather_weights(p, chip):
    shards = {n: _shard_view(n, p[n]) for n in BIG}
    shards["w_in_b"] = jnp.pad(shards["w_in_b"], ((0, 0), (0, W_IN_B_PAD - W_IN_B_SHARD)))
    big = _gather_big({n: _place_shard(shards[n], chip, n, f"place_{n}") for n in BIG}, "gather_weights")
    small = _gather8(_pack([p[n] for n in SMALL], R_SMALL), False, "gather_small")
    pieces = [_unpack(small[2 * k], [p[n].shape for n in SMALL]) for k in range(N_CHIPS)]
    full = {n: jnp.concatenate([pieces[k][i] for k in range(N_CHIPS)], axis=SHARD_AXIS[n])
            for i, n in enumerate(SMALL)}
    wb = big["w_in_b"].reshape(N_CHIPS, D_MODEL, W_IN_B_PAD)
    wb = jnp.concatenate([wb[k, :, :W_IN_B_SHARD] for k in range(N_CHIPS)], axis=1)
    return dict(
        w_in_a=big["w_in_a"].reshape(N_CHIPS, D_MODEL, -1), w_out_a=big["w_out_a"], w_out_b=big["w_out_b"],
        w_in_b=jnp.concatenate([wb[:, :2048], wb[:, 2064:3088], wb[:, 2048:2064],
                                jnp.zeros((D_MODEL, B_IN_PAD - 3088), BF16)], axis=1),
        w_ada=big["w_ada"].reshape(N_CHIPS, 2 * D_MODEL, -1), w_up=big["w_up"].reshape(N_CHIPS, 2 * D_MODEL, -1),
        w_down=big["w_down"],
        w_gate_b=jnp.pad(full["w_gate_b"][0], ((0, 128 - B_GATE_RANK), (0, 0))),
        b_gate_b=full["b_gate_b"], gnorm_b=full["gnorm_b"], conv_w=full["conv_w"],
        rel_bias=p["rel_bias"], norm_mix=p["norm_mix"], norm_ffn=p["norm_ffn"], b_ada=p["b_ada"],
        conv_b=p["conv_b"], norm_final=p["norm_final"].reshape(1, D_MODEL))


def _grad_views(g):
    gb = g["w_in_b"]
    gb = jnp.concatenate([gb[:, :2048], gb[:, 3072:3088], gb[:, 2048:3072]], axis=1)
    gb = jnp.stack([jnp.pad(gb[:, k * W_IN_B_SHARD:(k + 1) * W_IN_B_SHARD], ((0, 0), (0, W_IN_B_PAD - W_IN_B_SHARD)))
                    for k in range(N_CHIPS)])
    views = {n: g[n].reshape(COMM_VIEW[n][0]) for n in BIG if n != "w_in_b"}
    views["w_in_b"] = gb.reshape(COMM_VIEW["w_in_b"][0])
    return views


def _tiny_grads(g):
    out = {n: g[n] for n in REPLICATED if n != "norm_final"}
    out.update(norm_final=g["norm_final"].reshape(D_MODEL), w_gate_b=g["w_gate_b"][:B_GATE_RANK][None],
               b_gate_b=g["b_gate_b"], gnorm_b=g["gnorm_b"], conv_w=g["conv_w"])
    return out


def kernel(x, c, w_in_a, w_out_a, rel_bias, w_in_b, w_gate_b, b_gate_b, gnorm_b, w_out_b, norm_mix, norm_ffn, w_ada, b_ada, w_up, conv_w, conv_b, w_down, norm_final, loss_target, m_w_in_a, m_w_out_a, m_rel_bias, m_w_in_b, m_w_gate_b, m_b_gate_b, m_gnorm_b, m_w_out_b, m_norm_mix, m_norm_ffn, m_w_ada, m_b_ada, m_w_up, m_conv_w, m_conv_b, m_w_down, m_norm_final, v_w_in_a, v_w_out_a, v_rel_bias, v_w_in_b, v_w_gate_b, v_b_gate_b, v_gnorm_b, v_w_out_b, v_norm_mix, v_norm_ffn, v_w_ada, v_b_ada, v_w_up, v_conv_w, v_conv_b, v_w_down, v_norm_final):
    p = dict(zip(WEIGHTS, (w_in_a, w_out_a, rel_bias, w_in_b, w_gate_b, b_gate_b, gnorm_b, w_out_b, norm_mix,
                           norm_ffn, w_ada, b_ada, w_up, conv_w, conv_b, w_down, norm_final)))
    pm = dict(zip(WEIGHTS, (m_w_in_a, m_w_out_a, m_rel_bias, m_w_in_b, m_w_gate_b, m_b_gate_b, m_gnorm_b, m_w_out_b,
                            m_norm_mix, m_norm_ffn, m_w_ada, m_b_ada, m_w_up, m_conv_w, m_conv_b, m_w_down,
                            m_norm_final)))
    pv = dict(zip(WEIGHTS, (v_w_in_a, v_w_out_a, v_rel_bias, v_w_in_b, v_w_gate_b, v_b_gate_b, v_gnorm_b, v_w_out_b,
                            v_norm_mix, v_norm_ffn, v_w_ada, v_b_ada, v_w_up, v_conv_w, v_conv_b, v_w_down,
                            v_norm_final)))
    S = x.shape[1]

    chip = 2 * lax.axis_index("x") + lax.axis_index("y")
    core = lax.axis_index("c").astype(jnp.int32).reshape(1)
    chip_s = chip.astype(jnp.int32).reshape(1)

    w = _gather_weights(p, chip_s)
    loss, dx0, grads = _local_step(x.reshape(S, D_MODEL), c, loss_target.reshape(S, D_MODEL), w)

    views = _grad_views(grads)
    recv = _rs_pair_exchange(views, "grads_pair_exchange")
    pair = {n: _pair_add(views[n], recv[n], core, n, BF16, f"grads_pair_add_{n}") for n in BIG}
    from_chips = _rs_chip_exchange(pair, "grads_chip_exchange")
    g_big = _rs_pair_gather({n: _sum4(from_chips[n], pair[n], chip_s, core, f"grads_chip_sum_{n}") for n in BIG},
                            "grads_pair_gather")
    g_big["w_in_b"] = g_big["w_in_b"][:, :W_IN_B_SHARD]

    tiny = _tiny_grads(grads)
    tiny_names = SMALL + REPLICATED
    tiny_full = {n: SMALL_FULL[n] if n in SMALL_FULL else p[n].shape for n in tiny_names}
    tiny_sum = _gather8(_pack([tiny[n] for n in tiny_names] + [loss[0, 0:1]], R_TINY), True, "grads_tiny_sum")
    g_tiny = dict(zip(tiny_names, _unpack(tiny_sum, [tiny_full[n] for n in tiny_names])))
    for n in SMALL:
        width = p[n].shape[SHARD_AXIS[n]]
        g_tiny[n] = lax.dynamic_slice_in_dim(g_tiny[n], chip * width, width, axis=SHARD_AXIS[n])
    total_loss = tiny_sum.reshape(-1)[LOSS_SLOT]

    out = {}
    for n in BIG:
        res = _adamw(_shard_view(n, p[n]), g_big[n], _shard_view(n, pm[n]), _shard_view(n, pv[n]), f"adamw_{n}")
        out[n] = [t.reshape(p[n].shape) for t in (g_big[n],) + tuple(res)]
    res = _adamw(*[_pack([d[n] for n in tiny_names], R_TINY_SHARD) for d in (p, g_tiny, pm, pv)], "adamw_tiny")
    unpacked = [_unpack(t, [p[n].shape for n in tiny_names]) for t in res]
    for i, n in enumerate(tiny_names):
        out[n] = [g_tiny[n]] + [u[i] for u in unpacked]

    return (total_loss, dx0.reshape(x.shape), *[out[n][0] for n in WEIGHTS], *[out[n][1] for n in WEIGHTS],
            *[out[n][2] for n in WEIGHTS], *[out[n][3] for n in WEIGHTS])
```

```python
import functools
import math

import numpy as np
import jax
import jax.numpy as jnp
from jax import lax
from jax.experimental import pallas as pl
from jax.experimental.pallas import tpu as pltpu

F32 = jnp.float32
BF16 = jnp.bfloat16
MESH = pl.DeviceIdType.MESH

D_MODEL = 1024
A_CONFIGS = ((128, 1), (512, 4), (2048, 16))
A_HEADS = 16
A_HEAD_DIM = 64
A_BLK = 128
N_BUCKETS = 32
MAX_DISTANCE = 2048
B_HEADS = 4
B_DK = 128
B_DV = 256
B_QK = 512
B_V = 1024
B_GATE_RANK = 16
B_TAU = 16.0
B_CHUNK = 64
B_IN_PAD = 3200
D_FF = 2816
EPS = 1e-6
NEG_INF = -1e30
ADAM_LR = 0.001
ADAM_B1 = 0.9
ADAM_B2 = 0.999
ADAM_EPS = 1e-08
ADAM_WD = 0.01
ADAM_STEP = 10

LANES = 1024
VMEM_LIMIT = 48 * 1024 * 1024
ROW_TILE = 256
GLA_ROWS = 512

HBM = pl.BlockSpec(memory_space=pl.ANY)


def _params(n_axes):
    return pltpu.CompilerParams(dimension_semantics=("arbitrary",) * n_axes, vmem_limit_bytes=VMEM_LIMIT)


def _pick(n, cap, mult=128):
    best = None
    for t in range(mult, min(n, cap) + 1, mult):
        if n % t == 0:
            best = t
    return n if best is None else best


def _matmul(a, b, mode, out_dtype, name, shape=None, tiles=None, a_spec=None, b_spec=None, o_spec=None, o_shape=None,
            prev=None, add=None):
    dims = {"nn": (((1,), (0,)), ((), ())), "nt": NT, "tn": TN}[mode]
    if shape is None:
        if mode == "nn":
            (M, K), (_, N) = a.shape, b.shape
        elif mode == "nt":
            (M, K), (N, _) = a.shape, b.shape
        else:
            (K, M), (_, N) = a.shape, b.shape
    else:
        M, N, K = shape
    if tiles is None:
        tiles = (_pick(M, 1024, 128 if mode == "tn" else 8), _pick(N, 1536), _pick(K, 1024 if mode != "tn" else 2048))
    tm, tn, tk = tiles
    nk = K // tk
    if a_spec is None:
        a_spec = pl.BlockSpec((tk, tm), lambda i, j, k: (k, i)) if mode == "tn" else pl.BlockSpec(
            (tm, tk), lambda i, j, k: (i, k))
    if b_spec is None:
        b_spec = pl.BlockSpec((tn, tk), lambda i, j, k: (j, k)) if mode == "nt" else pl.BlockSpec(
            (tk, tn), lambda i, j, k: (k, j))
    if o_spec is None:
        o_spec = pl.BlockSpec((tm, tn), lambda i, j, k: (i, j))
        o_shape = (M, N)

    has_add = add is not None

    def body(a_ref, b_ref, *rest):
        part = lax.dot_general(a_ref[...].astype(BF16), b_ref[...].astype(BF16), dims, preferred_element_type=F32)

        def finish(total):
            if has_add:
                total = total + rest[0][...]
            return total.astype(out_dtype)

        if nk == 1:
            rest[-1][...] = finish(part)
            return
        o_ref, acc_ref = rest[-2:]
        k = pl.program_id(2)

        @pl.when(k == 0)
        def _():
            acc_ref[...] = part

        @pl.when(k > 0)
        def _():
            acc_ref[...] += part

        @pl.when(k == nk - 1)
        def _():
            o_ref[...] = finish(acc_ref[...])

    ins, in_specs, aliases = [a, b], [a_spec, b_spec], {}
    if has_add:
        ins.append(add)
        in_specs.append(o_spec)
    if prev is not None:
        aliases = {len(ins): 0}
        ins.append(prev)
        in_specs.append(HBM)
    return pl.pallas_call(
        body, name=name, grid=(M // tm, N // tn, nk), in_specs=in_specs, out_specs=o_spec,
        out_shape=jax.ShapeDtypeStruct(o_shape, out_dtype),
        scratch_shapes=[pltpu.VMEM((tm, tn), F32)] if nk > 1 else [],
        input_output_aliases=aliases, compiler_params=_params(3))(*ins)


def _row(tr, d=D_MODEL):
    return pl.BlockSpec((tr, d), lambda i: (i, 0))


def _vec(d=D_MODEL):
    return pl.BlockSpec((1, d), lambda i: (0, 0))


def _modspec(j):
    return pl.BlockSpec((8, D_MODEL), lambda i: (0, j))


def _silu(x):
    return x * jax.nn.sigmoid(x)


def _dsilu(x):
    s = jax.nn.sigmoid(x)
    return s * (1.0 + x * (1.0 - s))


ADA_TN = 6 * D_MODEL // 4


def _ada_mod(c, w_ada, b_ada, layer, name):
    def body(c_ref, w_ref, b_ref, o_ref):
        sc = jnp.broadcast_to(_silu(c_ref[...]), (8, D_MODEL)).astype(BF16)
        o_ref[...] = jnp.dot(sc, w_ref[...], preferred_element_type=F32) + b_ref[...]

    return pl.pallas_call(
        body, name=name, grid=(4,),
        in_specs=[_vec(), pl.BlockSpec((None, D_MODEL, ADA_TN), lambda j: (j, layer, 0)),
                  pl.BlockSpec((1, ADA_TN), lambda j: (0, j))],
        out_specs=pl.BlockSpec((8, ADA_TN), lambda j: (0, j)), out_shape=jax.ShapeDtypeStruct((8, 6 * D_MODEL), F32),
        compiler_params=_params(1))(c, w_ada, b_ada)


def _ada_outer(c, dmods, name):
    def body(c_ref, d_ref, o_ref):
        row = lax.broadcasted_iota(jnp.int32, (8, 1), 0) == 0
        a = jnp.where(row, jnp.broadcast_to(_silu(c_ref[...]), (8, D_MODEL)), 0.0).astype(BF16)
        b = jnp.where(row, jnp.broadcast_to(d_ref[...], (8, ADA_TN)), 0.0).astype(BF16)
        o_ref[...] = lax.dot_general(a, b, (((0,), (0,)), ((), ())), preferred_element_type=F32)

    return pl.pallas_call(
        body, name=name, grid=(2, 4),
        in_specs=[pl.BlockSpec((1, D_MODEL), lambda l, j: (0, 0)),
                  pl.BlockSpec((None, 1, ADA_TN), lambda l, j: (l, 0, j))],
        out_specs=pl.BlockSpec((None, D_MODEL, ADA_TN), lambda l, j: (j, l, 0)),
        out_shape=jax.ShapeDtypeStruct((4, 2 * D_MODEL, ADA_TN), F32), compiler_params=_params(2))(c, dmods)


def _view_spec(tr, d, width=D_MODEL):
    return pl.BlockSpec((tr // d, d * width), lambda i: (i, 0))


def _view_shape(S, d, dtype, width=D_MODEL):
    return jax.ShapeDtypeStruct((S // d, d * width), dtype)


LANE_TILE = 128
N_LANE_TILES = D_MODEL // LANE_TILE


def _token_scratch(tr):
    return pltpu.VMEM((N_LANE_TILES, tr, LANE_TILE), F32)


def _scratch_put(scr_ref, val):
    for c in range(N_LANE_TILES):
        scr_ref[c] = val[:, c * LANE_TILE:(c + 1) * LANE_TILE]


def _scratch_get(scr_ref):
    return jnp.concatenate([scr_ref[c] for c in range(N_LANE_TILES)], axis=1)


def _store_view(scr_ref, out_ref, d):
    n = scr_ref.shape[1] // d
    for r in range(d):
        for c in range(N_LANE_TILES):
            lo = r * D_MODEL + c * LANE_TILE
            out_ref[:, lo:lo + LANE_TILE] = scr_ref.at[c][pl.ds(r, n, stride=d), :].astype(out_ref.dtype)


def _load_view(view_ref, scr_ref, d):
    n = scr_ref.shape[1] // d
    for r in range(d):
        for c in range(N_LANE_TILES):
            lo = r * D_MODEL + c * LANE_TILE
            scr_ref.at[c][pl.ds(r, n, stride=d), :] = view_ref[:, lo:lo + LANE_TILE].astype(F32)


def _norm_mod(x, gamma, mod, j_sc, j_sh, name, resid=None, gate_mod=None, j_gate=None, views=False):
    S = x.shape[0]
    tr = ROW_TILE
    has_res = resid is not None

    def body(*refs):
        if has_res:
            x_ref, y_ref, gate_ref, g_ref, sc_ref, sh_ref, xo_ref, h_ref = refs
            xn = x_ref[...] + gate_ref[0:1, :] * y_ref[...]
            xo_ref[...] = xn
        elif views:
            x_ref, g_ref, sc_ref, sh_ref, h_ref, h4_ref, h16_ref, scr_ref = refs
            xn = x_ref[...]
        else:
            x_ref, g_ref, sc_ref, sh_ref, h_ref = refs
            xn = x_ref[...]
        r = lax.rsqrt(jnp.mean(xn * xn, axis=-1, keepdims=True) + EPS)
        n = (xn * r) * g_ref[...]
        h = n * (1.0 + sc_ref[0:1, :]) + sh_ref[0:1, :]
        h_ref[...] = h.astype(BF16)
        if views:
            _scratch_put(scr_ref, h)
            _store_view(scr_ref, h4_ref, 4)
            _store_view(scr_ref, h16_ref, 16)

    scratch = []
    if has_res:
        ins = [x, resid, gate_mod, gamma, mod, mod]
        in_specs = [_row(tr), _row(tr), _modspec(j_gate), _vec(), _modspec(j_sc), _modspec(j_sh)]
        out_specs = [_row(tr), _row(tr)]
        out_shape = [jax.ShapeDtypeStruct((S, D_MODEL), F32), jax.ShapeDtypeStruct((S, D_MODEL), BF16)]
    else:
        ins = [x, gamma, mod, mod]
        in_specs = [_row(tr), _vec(), _modspec(j_sc), _modspec(j_sh)]
        out_specs = _row(tr)
        out_shape = jax.ShapeDtypeStruct((S, D_MODEL), BF16)
        if views:
            out_specs = [_row(tr), _view_spec(tr, 4), _view_spec(tr, 16)]
            out_shape = [out_shape, _view_shape(S, 4, BF16), _view_shape(S, 16, BF16)]
            scratch = [_token_scratch(tr)]
    return pl.pallas_call(body, name=name, grid=(S // tr,), in_specs=in_specs, out_specs=out_specs,
                          out_shape=out_shape, scratch_shapes=scratch, compiler_params=_params(1))(*ins)


def _final_loss(x, resid, mod, j_gate, gamma, tgt, name):
    S = x.shape[0]
    tr = ROW_TILE

    def body(x_ref, y_ref, gate_ref, g_ref, t_ref, dx_ref, loss_ref, dg_ref, dy_ref, dgate_ref):
        @pl.when(pl.program_id(0) == 0)
        def _():
            loss_ref[...] = jnp.zeros_like(loss_ref)
            dg_ref[...] = jnp.zeros_like(dg_ref)
            dgate_ref[...] = jnp.zeros_like(dgate_ref)

        yv = y_ref[...]
        xn = x_ref[...] + gate_ref[0:1, :] * yv
        r = lax.rsqrt(jnp.mean(xn * xn, axis=-1, keepdims=True) + EPS)
        xhat = xn * r
        err = xhat * g_ref[...] - t_ref[...]
        loss_ref[...] += 0.5 * jnp.sum(jnp.mean(err * err, axis=-1, keepdims=True))
        dy = err * (1.0 / D_MODEL)
        dg_ref[...] += jnp.sum(dy * xhat, axis=0, keepdims=True)
        dxh = dy * g_ref[...]
        dx = r * (dxh - xhat * jnp.mean(dxh * xhat, axis=-1, keepdims=True))
        dx_ref[...] = dx
        dy_ref[...] = (gate_ref[0:1, :] * dx).astype(BF16)
        dgate_ref[...] += jnp.sum(dx * yv, axis=0, keepdims=True)

    vec = jax.ShapeDtypeStruct((1, D_MODEL), F32)
    return pl.pallas_call(
        body, name=name, grid=(S // tr,),
        in_specs=[_row(tr), _row(tr), _modspec(j_gate), _vec(), _row(tr)],
        out_specs=[_row(tr), pl.BlockSpec((1, 128), lambda i: (0, 0)), _vec(), _row(tr), _vec()],
        out_shape=[jax.ShapeDtypeStruct((S, D_MODEL), F32), jax.ShapeDtypeStruct((1, 128), F32), vec,
                   jax.ShapeDtypeStruct((S, D_MODEL), BF16), vec],
        compiler_params=_params(1))(x, resid, mod, gamma, tgt)


def _norm_mod_bwd(dh, x, dx_res, gamma, mod, j_sc, name, dh_views=None, branch=None):
    S = x.shape[0]
    tr = ROW_TILE
    n_views = 0 if dh_views is None else 2
    n_branch = 0 if branch is None else 2

    def body(dh_ref, *refs):
        x_ref, dr_ref, g_ref, sc_ref = refs[n_views:n_views + 4]
        dx_ref, dsc_ref, dsh_ref, dg_ref = refs[n_views + 4 + n_branch:n_views + 8 + n_branch]

        @pl.when(pl.program_id(0) == 0)
        def _():
            dsc_ref[...] = jnp.zeros_like(dsc_ref)
            dsh_ref[...] = jnp.zeros_like(dsh_ref)
            dg_ref[...] = jnp.zeros_like(dg_ref)

        xv = x_ref[...]
        dh_v = dh_ref[...]
        if n_views:
            scr_ref = refs[-1]
            for view_ref, d in zip(refs[:2], (4, 16)):
                _load_view(view_ref, scr_ref, d)
                dh_v = dh_v + _scratch_get(scr_ref)
        r = lax.rsqrt(jnp.mean(xv * xv, axis=-1, keepdims=True) + EPS)
        xhat = xv * r
        dsh_ref[...] += jnp.sum(dh_v, axis=0, keepdims=True)
        dsc_ref[...] += jnp.sum(dh_v * (xhat * g_ref[...]), axis=0, keepdims=True)
        dn = dh_v * (1.0 + sc_ref[0:1, :])
        dg_ref[...] += jnp.sum(dn * xhat, axis=0, keepdims=True)
        dxh = dn * g_ref[...]
        dx = dr_ref[...] + r * (dxh - xhat * jnp.mean(dxh * xhat, axis=-1, keepdims=True))
        dx_ref[...] = dx
        if n_branch:
            y_ref, gate_ref = refs[n_views + 4:n_views + 6]
            dy_ref, dgate_ref = refs[n_views + 10:n_views + 12]

            @pl.when(pl.program_id(0) == 0)
            def _():
                dgate_ref[...] = jnp.zeros_like(dgate_ref)

            dy_ref[...] = (gate_ref[0:1, :] * dx).astype(BF16)
            dgate_ref[...] += jnp.sum(dx * y_ref[...], axis=0, keepdims=True)

    vec = jax.ShapeDtypeStruct((1, D_MODEL), F32)
    views = [] if dh_views is None else list(dh_views)
    view_specs = [_view_spec(tr, 4), _view_spec(tr, 16)] if views else []
    ins = [dh, *views, x, dx_res, gamma, mod]
    in_specs = [_row(tr)] + view_specs + [_row(tr), _row(tr), _vec(), _modspec(j_sc)]
    out_specs = [_row(tr), _vec(), _vec(), _vec()]
    out_shape = [jax.ShapeDtypeStruct((S, D_MODEL), F32), vec, vec, vec]
    if branch is not None:
        y, gate_mod, j_gate = branch
        ins += [y, gate_mod]
        in_specs += [_row(tr), _modspec(j_gate)]
        out_specs += [_row(tr), _vec()]
        out_shape += [jax.ShapeDtypeStruct((S, D_MODEL), BF16), vec]
    return pl.pallas_call(
        body, name=name, grid=(S // tr,), in_specs=in_specs, out_specs=out_specs, out_shape=out_shape,
        scratch_shapes=[_token_scratch(tr)] if views else [], compiler_params=_params(1))(*ins)


def _shift_down(u, halo, s):
    r = pltpu.roll(u, s, 0)
    hr = pltpu.roll(halo, s, 0)
    rid = lax.broadcasted_iota(jnp.int32, hr.shape, 0)
    top = jnp.where(rid < s, hr, r[0:8])
    return jnp.concatenate([top, r[8:]], axis=0)


def _conv3(u, halo, w_ref, b_ref):
    u1 = _shift_down(u, halo, 1)
    u2 = _shift_down(u, halo, 2)
    return b_ref[...] + ((w_ref[0:1, :] * u2 + w_ref[1:2, :] * u1) + w_ref[2:3, :] * u), u1, u2


CONV_TC = 1408


def _conv_specs(tr, S):
    nh = D_FF // CONV_TC
    hb = tr // 8

    def cur(off):
        return pl.BlockSpec((tr, CONV_TC), lambda j, i: (i, j + off))

    def halo(off):
        return pl.BlockSpec((8, CONV_TC), lambda j, i: (jnp.maximum(i * hb - 1, 0), j + off))

    def w(off):
        return pl.BlockSpec((3, CONV_TC), lambda j, i: (0, j + off))

    def b(off):
        return pl.BlockSpec((1, CONV_TC), lambda j, i: (0, j + off))

    return nh, cur, halo, w, b


def _conv_gate(u, conv_w, conv_b, name):
    S = u.shape[0]
    tr = ROW_TILE
    nh, cur, halo, w, b = _conv_specs(tr, S)

    def body(ua_ref, ha_ref, ub_ref, hb_ref, wa_ref, wb_ref, ba_ref, bb_ref, o_ref):
        first = pl.program_id(1) == 0
        ha = jnp.where(first, 0.0, ha_ref[...])
        hbv = jnp.where(first, 0.0, hb_ref[...])
        a, _, _ = _conv3(ua_ref[...], ha, wa_ref, ba_ref)
        bb, _, _ = _conv3(ub_ref[...], hbv, wb_ref, bb_ref)
        o_ref[...] = (_silu(a) * bb).astype(BF16)

    return pl.pallas_call(
        body, name=name, grid=(nh, S // tr),
        in_specs=[cur(0), halo(0), cur(nh), halo(nh), w(0), w(nh), b(0), b(nh)],
        out_specs=pl.BlockSpec((tr, CONV_TC), lambda j, i: (i, j)),
        out_shape=jax.ShapeDtypeStruct((S, D_FF), BF16), compiler_params=_params(2))(
            u, u, u, u, conv_w, conv_w, conv_b, conv_b)


def _conv_gate_bwd(u, dact, conv_w, conv_b, name):
    S = u.shape[0]
    tr = ROW_TILE
    nh, cur, halo, w, b = _conv_specs(tr, S)
    hb = tr // 8
    nlast = S // 8 - 1
    nsteps = S // tr

    def after(off):
        return pl.BlockSpec((8, CONV_TC), lambda j, i: (jnp.minimum((i + 1) * hb, nlast), j + off))

    def body(ua_ref, ha_ref, na_ref, ub_ref, hb_ref, nb_ref, wa_ref, wb_ref, ba_ref, bb_ref, da_ref, dn_ref,
             dua_ref, dub_ref, dwa_ref, dwb_ref, dba_ref, dbb_ref):
        first = pl.program_id(1) == 0
        last = pl.program_id(1) == nsteps - 1

        @pl.when(first)
        def _():
            for r in (dwa_ref, dwb_ref, dba_ref, dbb_ref):
                r[...] = jnp.zeros_like(r)

        ha = jnp.where(first, 0.0, ha_ref[...])
        hbv = jnp.where(first, 0.0, hb_ref[...])
        ua = jnp.concatenate([ua_ref[...], na_ref[...]], axis=0)
        ub = jnp.concatenate([ub_ref[...], nb_ref[...]], axis=0)
        a, ua1, ua2 = _conv3(ua, ha, wa_ref, ba_ref)
        bb, ub1, ub2 = _conv3(ub, hbv, wb_ref, bb_ref)
        dact_v = jnp.concatenate([da_ref[...], jnp.where(last, 0.0, dn_ref[...])], axis=0)
        da = dact_v * bb * _dsilu(a)
        db = dact_v * _silu(a)
        n = tr + 8
        for d, x0, x1, x2, w_ref, du_ref, dw_ref, dbias_ref in (
                (da, ua, ua1, ua2, wa_ref, dua_ref, dwa_ref, dba_ref),
                (db, ub, ub1, ub2, wb_ref, dub_ref, dwb_ref, dbb_ref)):
            d1 = pltpu.roll(d, n - 1, 0)
            d2 = pltpu.roll(d, n - 2, 0)
            du_ref[...] = ((w_ref[2:3, :] * d + w_ref[1:2, :] * d1) + w_ref[0:1, :] * d2)[:tr].astype(BF16)
            dt = d[:tr]
            dbias_ref[...] += jnp.sum(dt, axis=0, keepdims=True)
            dw_ref[0:1, :] += jnp.sum(dt * x2[:tr], axis=0, keepdims=True)
            dw_ref[1:2, :] += jnp.sum(dt * x1[:tr], axis=0, keepdims=True)
            dw_ref[2:3, :] += jnp.sum(dt * x0[:tr], axis=0, keepdims=True)

    half = pl.BlockSpec((tr, CONV_TC), lambda j, i: (i, j))
    half_after = pl.BlockSpec((8, CONV_TC), lambda j, i: (jnp.minimum((i + 1) * hb, nlast), j))
    dw = pl.BlockSpec((8, CONV_TC), lambda j, i: (0, j))
    dbs = pl.BlockSpec((1, CONV_TC), lambda j, i: (0, j))
    f = lambda r, c: jax.ShapeDtypeStruct((r, c), F32)
    du = jax.ShapeDtypeStruct((S, D_FF), BF16)
    return pl.pallas_call(
        body, name=name, grid=(nh, nsteps),
        in_specs=[cur(0), halo(0), after(0), cur(nh), halo(nh), after(nh), w(0), w(nh), b(0), b(nh), half,
                  half_after],
        out_specs=[half, half, dw, dw, dbs, dbs],
        out_shape=[du, du, f(8, D_FF), f(8, D_FF), f(1, D_FF), f(1, D_FF)],
        compiler_params=_params(2))(u, u, u, u, u, u, conv_w, conv_w, conv_b, conv_b, dact, dact)


def _bucket_maps():
    qi = np.arange(A_BLK)[:, None]
    ki = np.arange(2 * A_BLK)[None, :]
    steps = np.clip(qi + A_BLK - ki, 0, A_BLK)
    out = []
    max_exact = N_BUCKETS // 2
    for _, dil in A_CONFIGS:
        dist = steps * dil
        n = np.maximum(dist, max_exact).astype(np.float32)
        large = max_exact + (np.log(n / np.float32(max_exact)) / np.float32(math.log(MAX_DISTANCE / max_exact))
                             * np.float32(N_BUCKETS - max_exact)).astype(np.int32)
        large = np.minimum(large, N_BUCKETS - 1)
        out.append(np.where(dist < max_exact, dist, large))
    return np.stack(out).astype(np.int32)


def _bias_build(rel_bias, buckets, name):
    ng = len(A_CONFIGS)

    def body(t_ref, bk_ref, o_ref):
        gh = pl.program_id(0) * A_HEADS + pl.program_id(1)
        bk = bk_ref[0]
        acc = jnp.zeros((A_BLK, 2 * A_BLK), F32)
        for b in range(N_BUCKETS):
            acc = jnp.where(bk == b, t_ref[b, gh], acc)
        o_ref[0] = acc

    return pl.pallas_call(
        body, name=name, grid=(ng, A_HEADS),
        in_specs=[pl.BlockSpec(memory_space=pltpu.SMEM), pl.BlockSpec((1, A_BLK, 2 * A_BLK), lambda g, h: (g, 0, 0))],
        out_specs=pl.BlockSpec((1, A_BLK, 2 * A_BLK), lambda g, h: (g * A_HEADS + h, 0, 0)),
        out_shape=jax.ShapeDtypeStruct((ng * A_HEADS, A_BLK, 2 * A_BLK), F32),
        compiler_params=_params(2))(rel_bias, buckets)


def _bias_bwd(dbias, buckets, name):
    ng = len(A_CONFIGS)

    def body(d_ref, bk_ref, o_ref):
        gh = pl.program_id(0) * A_HEADS + pl.program_id(1)
        bk = bk_ref[0]
        d = d_ref[0]
        for b in range(N_BUCKETS):
            o_ref[b, gh] = jnp.sum(jnp.where(bk == b, d, 0.0))

    return pl.pallas_call(
        body, name=name, grid=(ng, A_HEADS),
        in_specs=[pl.BlockSpec((1, A_BLK, 2 * A_BLK), lambda g, h: (g * A_HEADS + h, 0, 0)),
                  pl.BlockSpec((1, A_BLK, 2 * A_BLK), lambda g, h: (g, 0, 0))],
        out_specs=pl.BlockSpec(memory_space=pltpu.SMEM),
        out_shape=jax.ShapeDtypeStruct((N_BUCKETS, ng * A_HEADS), F32),
        compiler_params=_params(2))(dbias, buckets)


def _attn_mask(b):
    qi = lax.broadcasted_iota(jnp.int32, (A_BLK, 2 * A_BLK), 0)
    ki = lax.broadcasted_iota(jnp.int32, (A_BLK, 2 * A_BLK), 1)
    band = (ki >= qi) & (ki <= qi + A_BLK)
    return band & ((b > 0) | (ki >= A_BLK))


def _first_head_lanes():
    return lax.broadcasted_iota(jnp.int32, (A_BLK, 2 * A_HEAD_DIM), 1) < A_HEAD_DIM


def _attn_in_specs(g, dil):
    W = A_HEADS * A_HEAD_DIM

    def spec(t, prev, nb):
        def im(r, b):
            bb = jnp.minimum(b, nb - 1)
            if prev:
                bb = jnp.maximum(bb - 1, 0)
            return (bb, r * 3 + t)
        return pl.BlockSpec((A_BLK, W), im)

    return lambda nb: [spec(0, False, nb), spec(1, False, nb), spec(1, True, nb), spec(2, False, nb),
                       spec(2, True, nb)]


def _attn_fwd(qv, bias, g, name):
    _, dil = A_CONFIGS[g]
    L = qv.shape[0]
    nb = L // A_BLK
    W = A_HEADS * A_HEAD_DIM

    def body(q_ref, kc_ref, kp_ref, vc_ref, vp_ref, bias_ref, o_ref, l_ref):
        mask = _attn_mask(pl.program_id(1))
        first = _first_head_lanes()
        for j in range(A_HEADS // 2):
            ps = slice(j * 2 * A_HEAD_DIM, (j + 1) * 2 * A_HEAD_DIM)
            q2 = q_ref[:, ps] * 0.125
            k2 = jnp.concatenate([kp_ref[:, ps], kc_ref[:, ps]], axis=0)
            v2 = jnp.concatenate([vp_ref[:, ps], vc_ref[:, ps]], axis=0)
            o_pair, l_pair = [], []
            for e in range(2):
                qh = jnp.where(first if e == 0 else ~first, q2, jnp.zeros_like(q2))
                s = lax.dot_general(qh, k2, NT, preferred_element_type=F32) + bias_ref[2 * j + e]
                s = jnp.where(mask, s, NEG_INF)
                m = jnp.max(s, axis=-1, keepdims=True)
                p = jnp.exp(s - m)
                den = jnp.sum(p, axis=-1, keepdims=True)
                o_pair.append(jnp.dot(p.astype(BF16), v2, preferred_element_type=F32) / den)
                l_pair.append(m + jnp.log(den))
            o_ref[:, ps] = jnp.where(first, o_pair[0], o_pair[1])
            l_ref[:, ps] = jnp.where(first, l_pair[0], l_pair[1])

    out_spec = pl.BlockSpec((A_BLK, W), lambda r, b: (b, r))
    return pl.pallas_call(
        body, name=name, grid=(dil, nb),
        in_specs=_attn_in_specs(g, dil)(nb) + [pl.BlockSpec((A_HEADS, A_BLK, 2 * A_BLK), lambda r, b: (g, 0, 0))],
        out_specs=[out_spec, out_spec],
        out_shape=[jax.ShapeDtypeStruct((L, dil * W), F32)] * 2,
        compiler_params=_params(2))(qv, qv, qv, qv, qv, bias)


def _mix_fwd(os, ls, name):
    S = os[0].shape[0]
    tr = ROW_TILE

    def body(o0, o1, o2, l0, l1, l2, om_ref, om4_ref, om16_ref, lt_ref, lt4_ref, lt16_ref, s_o1, s_o2, s_l1, s_l2):
        for view_ref, scr_ref, d in ((o1, s_o1, 4), (o2, s_o2, 16), (l1, s_l1, 4), (l2, s_l2, 16)):
            _load_view(view_ref, scr_ref, d)
        a, b, c = l0[...], _scratch_get(s_l1), _scratch_get(s_l2)
        m = jnp.maximum(jnp.maximum(a, b), c)
        ea, eb, ec = jnp.exp(a - m), jnp.exp(b - m), jnp.exp(c - m)
        z = (ea + eb) + ec
        om = ((ea / z) * o0[...] + (eb / z) * _scratch_get(s_o1)) + (ec / z) * _scratch_get(s_o2)
        lt = m + jnp.log(z)
        om_ref[...] = om
        lt_ref[...] = lt
        _scratch_put(s_o1, om)
        _scratch_put(s_l1, lt)
        for scr_ref, v4_ref, v16_ref in ((s_o1, om4_ref, om16_ref), (s_l1, lt4_ref, lt16_ref)):
            _store_view(scr_ref, v4_ref, 4)
            _store_view(scr_ref, v16_ref, 16)

    ins = [_row(tr), _view_spec(tr, 4), _view_spec(tr, 16)]
    outs = [jax.ShapeDtypeStruct((S, D_MODEL), F32), _view_shape(S, 4, F32), _view_shape(S, 16, F32)]
    return pl.pallas_call(
        body, name=name, grid=(S // tr,), in_specs=ins * 2, out_specs=ins * 2, out_shape=outs * 2,
        scratch_shapes=[_token_scratch(tr)] * 4, compiler_params=_params(1))(*os, *ls)


def _to_views(x, name):
    S = x.shape[0]
    tr = ROW_TILE

    def body(x_ref, v4_ref, v16_ref, scr_ref):
        _scratch_put(scr_ref, x_ref[...])
        _store_view(scr_ref, v4_ref, 4)
        _store_view(scr_ref, v16_ref, 16)

    return pl.pallas_call(
        body, name=name, grid=(S // tr,), in_specs=[_row(tr)], out_specs=[_view_spec(tr, 4), _view_spec(tr, 16)],
        out_shape=[_view_shape(S, 4, F32), _view_shape(S, 16, F32)], scratch_shapes=[_token_scratch(tr)],
        compiler_params=_params(1))(x)


def _attn_bwd(qv, bias, d_o, omix, ltot, g, name):
    _, dil = A_CONFIGS[g]
    L = qv.shape[0]
    nb = L // A_BLK
    W = A_HEADS * A_HEAD_DIM

    def body(q_ref, kc_ref, kp_ref, vc_ref, vp_ref, bias_ref, do_ref, om_ref, lt_ref,
             dqkv_ref, db_ref, cq_ref, ck_ref, cv_ref):
        r, b = pl.program_id(0), pl.program_id(1)
        dq_ref, dk_ref, dv_ref = (dqkv_ref.at[:, t * W:(t + 1) * W] for t in range(3))

        @pl.when((r == 0) & (b == 0))
        def _():
            db_ref[...] = jnp.zeros_like(db_ref)

        @pl.when(b == 0)
        def _():
            cq_ref[...] = jnp.zeros_like(cq_ref)
            ck_ref[...] = jnp.zeros_like(ck_ref)
            cv_ref[...] = jnp.zeros_like(cv_ref)

        dq_ref[...] = cq_ref[...]

        @pl.when(b < nb)
        def _():
            mask = _attn_mask(b)
            first = _first_head_lanes()
            for j in range(A_HEADS // 2):
                ps = slice(j * 2 * A_HEAD_DIM, (j + 1) * 2 * A_HEAD_DIM)
                q2 = q_ref[:, ps] * 0.125
                k2 = jnp.concatenate([kp_ref[:, ps], kc_ref[:, ps]], axis=0)
                v2 = jnp.concatenate([vp_ref[:, ps], vc_ref[:, ps]], axis=0)
                do2, om2 = do_ref[:, ps], om_ref[:, ps]
                dq_pair, dk2, dv2 = [], None, None
                for e in range(2):
                    mine = first if e == 0 else ~first
                    h = 2 * j + e
                    qh = jnp.where(mine, q2, jnp.zeros_like(q2))
                    s = lax.dot_general(qh, k2, NT, preferred_element_type=F32) + bias_ref[h]
                    s = jnp.where(mask, s, NEG_INF)
                    wp = jnp.exp(s - lt_ref[:, h * A_HEAD_DIM:h * A_HEAD_DIM + 1])
                    do_h = jnp.where(mine, do2, 0.0)
                    t_h = jnp.sum(do_h * om2, axis=-1, keepdims=True)
                    do_b = do_h.astype(BF16)
                    dp = lax.dot_general(do_b, v2, NT, preferred_element_type=F32)
                    ds = wp * (dp - t_h)
                    db_ref[h] += ds
                    ds_b = ds.astype(BF16)
                    dv_e = lax.dot_general(wp.astype(BF16), do_b, TN, preferred_element_type=F32)
                    dk_e = lax.dot_general(ds_b, qh, TN, preferred_element_type=F32)
                    dv2 = dv_e if dv2 is None else dv2 + dv_e
                    dk2 = dk_e if dk2 is None else dk2 + dk_e
                    dq_pair.append(jnp.dot(ds_b, k2, preferred_element_type=F32))
                cq_ref[:, ps] = (jnp.where(first, dq_pair[0], dq_pair[1]) * 0.125).astype(BF16)
                dk_ref[:, ps] = (ck_ref[:, ps] + dk2[:A_BLK]).astype(BF16)
                dv_ref[:, ps] = (cv_ref[:, ps] + dv2[:A_BLK]).astype(BF16)
                ck_ref[:, ps] = dk2[A_BLK:]
                cv_ref[:, ps] = dv2[A_BLK:]

        @pl.when(b == nb)
        def _():
            dk_ref[...] = ck_ref[...].astype(BF16)
            dv_ref[...] = cv_ref[...].astype(BF16)

    act = pl.BlockSpec((A_BLK, W), lambda r, b: (jnp.minimum(b, nb - 1), r))
    lag = pl.BlockSpec((A_BLK, 3 * W), lambda r, b: (jnp.maximum(b - 1, 0), r))
    full = pl.BlockSpec((A_HEADS, A_BLK, 2 * A_BLK), lambda r, b: (0, 0, 0))
    in_specs = _attn_in_specs(g, dil)(nb) + [pl.BlockSpec((A_HEADS, A_BLK, 2 * A_BLK), lambda r, b: (g, 0, 0)),
                                             act, act, act]
    return pl.pallas_call(
        body, name=name, grid=(dil, nb + 1), in_specs=in_specs, out_specs=[lag, full],
        out_shape=[jax.ShapeDtypeStruct((L, dil * 3 * W), BF16),
                   jax.ShapeDtypeStruct((A_HEADS, A_BLK, 2 * A_BLK), F32)],
        scratch_shapes=[pltpu.VMEM((A_BLK, W), BF16), pltpu.VMEM((A_BLK, W), F32), pltpu.VMEM((A_BLK, W), F32)],
        compiler_params=_params(2))(qv, qv, qv, qv, qv, bias, d_o, omix, ltot)


NT = (((1,), (1,)), ((), ()))
TN = (((0,), (0,)), ((), ()))


def _dot(a, b, dims=(((1,), (0,)), ((), ()))):
    return lax.dot_general(a.astype(BF16), b.astype(BF16), dims, preferred_element_type=F32)


def _gla_gates(glr, wg_ref, bg_ref):
    z = _dot(glr, wg_ref[...]) + bg_ref[...]
    log_sig = -(jnp.maximum(-z, 0.0) + jnp.log1p(jnp.exp(-jnp.abs(z))))
    return z, log_sig / B_TAU


def _gla_chunk(q, k, gk):
    row = lax.broadcasted_iota(jnp.int32, (B_CHUNK, B_CHUNK), 0)
    col = lax.broadcasted_iota(jnp.int32, (B_CHUNK, B_CHUNK), 1)
    causal = row >= col
    bcum = jnp.dot(causal.astype(F32), gk, precision=lax.Precision.HIGHEST, preferred_element_type=F32)
    bl = bcum[B_CHUNK - 1:B_CHUNK, :]
    qt = (q * (B_DK ** -0.5)) * jnp.exp(bcum)
    kt = k * jnp.exp(-bcum)
    kd = k * jnp.exp(bl - bcum)
    a = jnp.where(causal, _dot(qt, kt, NT), 0.0)
    return causal, bcum, bl, qt, kt, kd, a


def _gla_specs(tg):
    q = pl.BlockSpec((tg, B_DK), lambda h, i: (i, h))
    k = pl.BlockSpec((tg, B_DK), lambda h, i: (i, B_HEADS + h))
    v = pl.BlockSpec((tg, B_DV), lambda h, i: (i, B_HEADS + h))
    glr = pl.BlockSpec((tg, 128), lambda h, i: (i, 24))
    wg = pl.BlockSpec((128, B_DK), lambda h, i: (0, h))
    bg = pl.BlockSpec((1, B_DK), lambda h, i: (0, h))
    return [q, k, v, glr, wg, bg]


def _gla_fwd(proj, w_gate, b_gate, name):
    S = proj.shape[0]
    tg = GLA_ROWS
    nc = tg // B_CHUNK

    def body(q_ref, k_ref, v_ref, glr_ref, wg_ref, bg_ref, o_ref, st_ref, state_ref):
        @pl.when(pl.program_id(1) == 0)
        def _():
            state_ref[...] = jnp.zeros_like(state_ref)

        _, gk_all = _gla_gates(glr_ref[...], wg_ref, bg_ref)
        st = state_ref[...]
        for c in range(nc):
            rows = slice(c * B_CHUNK, (c + 1) * B_CHUNK)
            v = v_ref[rows, :]
            _, _, bl, qt, _, kd, a = _gla_chunk(q_ref[rows, :], k_ref[rows, :], gk_all[rows, :])
            o_ref[rows, :] = _dot(a, v) + _dot(qt, st, NT)
            st_ref[c, 0] = st
            st = st * jnp.exp(bl) + _dot(v, kd, TN)
        state_ref[...] = st

    return pl.pallas_call(
        body, name=name, grid=(B_HEADS, S // tg), in_specs=_gla_specs(tg),
        out_specs=[pl.BlockSpec((tg, B_DV), lambda h, i: (i, h)),
                   pl.BlockSpec((nc, 1, B_DV, B_DK), lambda h, i: (i, h, 0, 0))],
        out_shape=[jax.ShapeDtypeStruct((S, B_V), F32),
                   jax.ShapeDtypeStruct((S // B_CHUNK, B_HEADS, B_DV, B_DK), F32)],
        scratch_shapes=[pltpu.VMEM((B_DV, B_DK), F32)], compiler_params=_params(2))(
            proj, proj, proj, proj, w_gate, b_gate)


def _gla_bwd(proj, w_gate, b_gate, states, d_o, name):
    S = proj.shape[0]
    tg = GLA_ROWS
    nc = tg // B_CHUNK
    ni = S // tg

    def rev(spec):
        return pl.BlockSpec(spec.block_shape, lambda h, i, im=spec.index_map: im(h, ni - 1 - i))

    def body(q_ref, k_ref, v_ref, glr_ref, wg_ref, bg_ref, st_ref, do_ref,
             dq_ref, dk_ref, dv_ref, dz_ref, dbg_ref, dstate_ref):
        @pl.when(pl.program_id(1) == 0)
        def _():
            dstate_ref[...] = jnp.zeros_like(dstate_ref)
            dbg_ref[...] = jnp.zeros_like(dbg_ref)

        z_all, gk_all = _gla_gates(glr_ref[...], wg_ref, bg_ref)
        dst = dstate_ref[...]
        for c in range(nc - 1, -1, -1):
            rows = slice(c * B_CHUNK, (c + 1) * B_CHUNK)
            v = v_ref[rows, :]
            d_out = do_ref[rows, :]
            st = st_ref[c, 0]
            causal, bcum, bl, qt, kt, kd, a = _gla_chunk(q_ref[rows, :], k_ref[rows, :], gk_all[rows, :])
            da = jnp.where(causal, _dot(d_out, v, NT), 0.0)
            dv_ref[rows, :] = (_dot(a, d_out, TN) + _dot(kd, dst, NT)).astype(BF16)
            dqt = _dot(da, kt) + _dot(d_out, st)
            dkt = _dot(da, qt, TN)
            dkd = _dot(v, dst)
            dec = jnp.exp(bl)
            ddec = jnp.sum(dst * st, axis=0, keepdims=True)
            dst = dst * dec + _dot(d_out, qt, TN)
            dq_ref[rows, :] = (dqt * jnp.exp(bcum) * (B_DK ** -0.5)).astype(BF16)
            dk_ref[rows, :] = (dkt * jnp.exp(-bcum) + dkd * jnp.exp(bl - bcum)).astype(BF16)
            db = (dqt * qt - dkt * kt) - dkd * kd
            dbl = jnp.sum(dkd * kd, axis=0, keepdims=True) + dec * ddec
            upper = jnp.logical_not(causal) | (lax.broadcasted_iota(jnp.int32, (B_CHUNK, B_CHUNK), 0)
                                               == lax.broadcasted_iota(jnp.int32, (B_CHUNK, B_CHUNK), 1))
            dgk = jnp.dot(upper.astype(F32), db, precision=lax.Precision.HIGHEST, preferred_element_type=F32) + dbl
            dz = dgk * (1.0 / B_TAU) * jax.nn.sigmoid(-z_all[rows, :])
            dz_ref[rows, :] = dz
            dbg_ref[...] += jnp.sum(dz, axis=0, keepdims=True)
        dstate_ref[...] = dst

    qs = pl.BlockSpec((tg, B_DK), lambda h, i: (i, h))
    vs = pl.BlockSpec((tg, B_DV), lambda h, i: (i, h))
    in_specs = [rev(s) if n < 4 else s for n, s in enumerate(_gla_specs(tg))]
    in_specs += [rev(pl.BlockSpec((nc, 1, B_DV, B_DK), lambda h, i: (i, h, 0, 0))), rev(vs)]
    return pl.pallas_call(
        body, name=name, grid=(B_HEADS, ni), in_specs=in_specs,
        out_specs=[rev(qs), rev(qs), rev(vs), rev(qs), pl.BlockSpec((1, B_DK), lambda h, i: (0, h))],
        out_shape=[jax.ShapeDtypeStruct((S, B_QK), BF16), jax.ShapeDtypeStruct((S, B_QK), BF16),
                   jax.ShapeDtypeStruct((S, B_V), BF16), jax.ShapeDtypeStruct((S, B_QK), F32),
                   jax.ShapeDtypeStruct((1, B_QK), F32)],
        scratch_shapes=[pltpu.VMEM((B_DV, B_DK), F32)], compiler_params=_params(2))(
            proj, proj, proj, proj, w_gate, b_gate, states, d_o)


def _gla_out(o, proj, gnorm, name):
    S = o.shape[0]
    tr = ROW_TILE

    def body(o_ref, r_ref, g_ref, y_ref):
        for h in range(B_HEADS):
            hs = slice(h * B_DV, (h + 1) * B_DV)
            oh = o_ref[:, hs]
            rs = lax.rsqrt(jnp.mean(oh * oh, axis=-1, keepdims=True) + EPS)
            y_ref[:, hs] = (((oh * rs) * g_ref[...]) * _silu(r_ref[:, hs])).astype(BF16)

    return pl.pallas_call(
        body, name=name, grid=(S // tr,),
        in_specs=[_row(tr), pl.BlockSpec((tr, B_V), lambda i: (i, 2)), _vec(B_DV)], out_specs=_row(tr),
        out_shape=jax.ShapeDtypeStruct((S, B_V), BF16), compiler_params=_params(1))(o, proj, gnorm)


def _gla_out_bwd(o, proj, gnorm, d_y, name):
    S = o.shape[0]
    tr = ROW_TILE

    def body(o_ref, r_ref, g_ref, dy_ref, do_ref, dr_ref, dg_ref):
        @pl.when(pl.program_id(0) == 0)
        def _():
            dg_ref[...] = jnp.zeros_like(dg_ref)

        for h in range(B_HEADS):
            hs = slice(h * B_DV, (h + 1) * B_DV)
            oh, rv, dyv = o_ref[:, hs], r_ref[:, hs], dy_ref[:, hs]
            rs = lax.rsqrt(jnp.mean(oh * oh, axis=-1, keepdims=True) + EPS)
            xhat = oh * rs
            dr_ref[:, hs] = (dyv * (xhat * g_ref[...]) * _dsilu(rv)).astype(BF16)
            dn = dyv * _silu(rv)
            dg_ref[...] += jnp.sum(dn * xhat, axis=0, keepdims=True)
            dxh = dn * g_ref[...]
            do_ref[:, hs] = rs * (dxh - xhat * jnp.mean(dxh * xhat, axis=-1, keepdims=True))

    return pl.pallas_call(
        body, name=name, grid=(S // tr,),
        in_specs=[_row(tr), pl.BlockSpec((tr, B_V), lambda i: (i, 2)), _vec(B_DV), _row(tr)],
        out_specs=[_row(tr), _row(tr), _vec(B_DV)],
        out_shape=[jax.ShapeDtypeStruct((S, B_V), F32), jax.ShapeDtypeStruct((S, B_V), BF16),
                   jax.ShapeDtypeStruct((1, B_DV), F32)],
        compiler_params=_params(1))(o, proj, gnorm, d_y)


J_SH1, J_SC1, J_G1, J_SH2, J_SC2, J_G2 = range(6)


IN_A_TN = 768
UP_TN = 2 * D_FF // 4
TOKEN_TK = 2048


def _ffn_fwd(h, w, i, tag):
    S = h.shape[0]
    u = _matmul(h, w["w_up"], "nn", F32, f"up{tag}", shape=(S, 2 * D_FF, D_MODEL), tiles=(1024, UP_TN, D_MODEL),
                b_spec=pl.BlockSpec((None, D_MODEL, UP_TN), lambda m, j, k: (j, i, 0)))
    act = _conv_gate(u, w["conv_w"][i], w["conv_b"][i:i + 1], f"conv_gate{tag}")
    f = _matmul(act, w["w_down"], "nn", F32, f"down{tag}", shape=(S, D_MODEL, D_FF), tiles=(1024, D_MODEL, D_FF),
                b_spec=pl.BlockSpec((D_FF, D_MODEL), lambda m, j, k: (i, j)))
    return u, act, f


def _ffn_bwd(dx_out, df, u, act, h, x_in, mod, w, i, tag, prev, branch):
    S = h.shape[0]
    dact = _matmul(df, w["w_down"], "nt", F32, f"down_dx{tag}", shape=(S, D_FF, D_MODEL),
                   tiles=(1024, D_FF // 2, D_MODEL),
                   b_spec=pl.BlockSpec((D_FF // 2, D_MODEL), lambda m, j, k: (2 * i + j, k)))
    d_w_down = _matmul(act, df, "tn", F32, f"down_dw{tag}", shape=(D_FF, D_MODEL, S),
                       tiles=(D_FF // 2, D_MODEL, min(S, TOKEN_TK)),
                       o_spec=pl.BlockSpec((D_FF // 2, D_MODEL), lambda m, j, k: (2 * i + m, j)),
                       o_shape=(2 * D_FF, D_MODEL), prev=None if prev is None else prev["w_down"])
    du_a, du_b, dcwa, dcwb, dcba, dcbb = _conv_gate_bwd(u, dact, w["conv_w"][i], w["conv_b"][i:i + 1],
                                                        f"conv_gate_bwd{tag}")
    dh, d_w_up = None, None if prev is None else prev["w_up"]
    for half, du in enumerate((du_a, du_b)):
        dh = _matmul(du, w["w_up"], "nt", F32, f"up_dx{tag}{'ab'[half]}", shape=(S, D_MODEL, D_FF),
                     tiles=(1024, D_MODEL, UP_TN), add=dh,
                     b_spec=pl.BlockSpec((None, D_MODEL, UP_TN), lambda m, j, k, half=half: (2 * half + k, i, 0)))
        d_w_up = _matmul(h, du, "tn", F32, f"up_dw{tag}{'ab'[half]}", shape=(D_MODEL, D_FF, S),
                         tiles=(D_MODEL, UP_TN, min(S, TOKEN_TK)),
                         o_spec=pl.BlockSpec((None, D_MODEL, UP_TN), lambda m, j, k, half=half: (2 * half + j, i, 0)),
                         o_shape=(4, 2 * D_MODEL, UP_TN), prev=d_w_up)
    res = _norm_mod_bwd(dh, x_in, dx_out, w["norm_ffn"][i:i + 1], mod, J_SC2, f"norm_ffn_bwd{tag}", branch=branch)
    dx_in, dsc2, dsh2, dgam = res[:4]
    grads = dict(w_down=d_w_down, w_up=d_w_up, norm_ffn=dgam,
                 conv_w=jnp.concatenate([dcwa[0:3], dcwb[0:3]], axis=1),
                 conv_b=jnp.concatenate([dcba, dcbb], axis=1))
    return dx_in, res[4:], (dsh2, dsc2), grads


def _local_step(x, c, tgt, w):
    buckets = jnp.asarray(_bucket_maps())
    S = x.shape[0]
    mods = [_ada_mod(c, w["w_ada"], w["b_ada"][i:i + 1], i, f"ada_mod{i}") for i in range(2)]

    h1 = _norm_mod(x, w["norm_mix"][0:1], mods[0], J_SC1, J_SH1, "norm_mix0", views=True)
    geo = []
    for g, (_, dil) in enumerate(A_CONFIGS):
        tm = min(1024, S // dil)
        geo.append((dil, tm, S // dil // tm))
    w_cols = [pl.BlockSpec((None, D_MODEL, IN_A_TN), lambda m, j, k, g=g: ((4 * g + j) // 3, 0, (4 * g + j) % 3))
              for g in range(3)]
    qkv = [_matmul(h1[g], w["w_in_a"], "nn", BF16, f"in_a{g}", shape=(S, 3 * D_MODEL, D_MODEL),
                   tiles=(tm, IN_A_TN, D_MODEL),
                   a_spec=pl.BlockSpec((tm, D_MODEL), lambda m, j, k, n=n_i: (m % n, m // n)), b_spec=w_cols[g],
                   o_spec=pl.BlockSpec((tm, IN_A_TN), lambda m, j, k, n=n_i: (m % n, (m // n) * 4 + j)),
                   o_shape=(S // dil, dil * 3 * D_MODEL))
           for g, (dil, tm, n_i) in enumerate(geo)]
    bias = _bias_build(w["rel_bias"], buckets, "bias_build")
    os_, ls_ = zip(*[_attn_fwd(qkv[g], bias, g, f"attn_fwd{g}") for g in range(3)])
    omix, omix4, omix16, ltot, ltot4, ltot16 = _mix_fwd(os_, ls_, "mix_fwd")
    y0 = _matmul(omix, w["w_out_a"], "nn", F32, "out_a")
    x1, h2 = _norm_mod(x, w["norm_ffn"][0:1], mods[0], J_SC2, J_SH2, "norm_ffn0", resid=y0, gate_mod=mods[0],
                       j_gate=J_G1)
    u0, act0, f0 = _ffn_fwd(h2, w, 0, "0")

    x2, h3 = _norm_mod(x1, w["norm_mix"][1:2], mods[1], J_SC1, J_SH1, "norm_mix1", resid=f0, gate_mod=mods[0],
                       j_gate=J_G2)
    proj = _matmul(h3, w["w_in_b"], "nn", F32, "in_b", tiles=(512, B_IN_PAD, D_MODEL))
    o_gla, states = _gla_fwd(proj, w["w_gate_b"], w["b_gate_b"], "gla_fwd")
    on = _gla_out(o_gla, proj, w["gnorm_b"], "gla_out")
    y1 = _matmul(on, w["w_out_b"], "nn", F32, "out_b")
    x3, h4 = _norm_mod(x2, w["norm_ffn"][1:2], mods[1], J_SC2, J_SH2, "norm_ffn1", resid=y1, gate_mod=mods[1],
                       j_gate=J_G1)
    u1, act1, f1 = _ffn_fwd(h4, w, 1, "1")

    dx4, loss, d_norm_final, df1, dg2_1 = _final_loss(x3, f1, mods[1], J_G2, w["norm_final"], tgt, "final_loss")

    dx3, (dy1, dg1_1), (dsh2_1, dsc2_1), g_ffn1 = _ffn_bwd(dx4, df1, u1, act1, h4, x3, mods[1], w, 1, "1", None,
                                                           (y1, mods[1], J_G1))
    d_on = _matmul(dy1, w["w_out_b"], "nt", F32, "out_b_dx")
    d_w_out_b = _matmul(on, dy1, "tn", F32, "out_b_dw")
    d_ogla, d_r, d_gnorm = _gla_out_bwd(o_gla, proj, w["gnorm_b"], d_on, "gla_out_bwd")
    dq, dk, dv, dz, d_b_gate = _gla_bwd(proj, w["w_gate_b"], w["b_gate_b"], states, d_ogla, "gla_bwd")
    d_glr = _matmul(dz, w["w_gate_b"], "nt", BF16, "gate_dx")
    d_w_gate = _matmul(proj[:, 3072:3200], dz, "tn", F32, "gate_dw")
    dproj = jnp.concatenate([dq, dk, dv, d_r, d_glr], axis=1)
    dh3 = _matmul(dproj, w["w_in_b"], "nt", F32, "in_b_dx", tiles=(1024, D_MODEL, B_IN_PAD))
    d_w_in_b = _matmul(h3, dproj, "tn", F32, "in_b_dw")
    dx2, dsc1_1, dsh1_1, d_nmix1, df0, dg2_0 = _norm_mod_bwd(
        dh3, x2, dx3, w["norm_mix"][1:2], mods[1], J_SC1, "norm_mix_bwd1", branch=(f0, mods[0], J_G2))
    dmod1 = jnp.concatenate([dsh1_1, dsc1_1, dg1_1, dsh2_1, dsc2_1, dg2_1], axis=1)

    dx1, (dy0, dg1_0), (dsh2_0, dsc2_0), g_ffn0 = _ffn_bwd(dx2, df0, u0, act0, h2, x1, mods[0], w, 0, "0", g_ffn1,
                                                           (y0, mods[0], J_G1))
    d_omix = _matmul(dy0, w["w_out_a"], "nt", F32, "out_a_dx")
    d_w_out_a = _matmul(omix, dy0, "tn", F32, "out_a_dw")
    d_omix4, d_omix16 = _to_views(d_omix, "d_omix_views")
    d_omix_v, omix_v, ltot_v = (d_omix, d_omix4, d_omix16), (omix, omix4, omix16), (ltot, ltot4, ltot16)
    dqkv, dbs = zip(*[_attn_bwd(qkv[g], bias, d_omix_v[g], omix_v[g], ltot_v[g], g, f"attn_bwd{g}")
                      for g in range(3)])
    d_rel_bias = _bias_bwd(jnp.concatenate(dbs, axis=0), buckets, "bias_bwd")
    dh1, d_w_in_a = [], None
    for g, (dil, tm, n_i) in enumerate(geo):
        dh1.append(_matmul(
            dqkv[g], w["w_in_a"], "nt", F32, f"in_a_dx{g}", shape=(S, D_MODEL, 3 * D_MODEL),
            tiles=(tm, D_MODEL, IN_A_TN),
            a_spec=pl.BlockSpec((tm, IN_A_TN), lambda m, j, k, n=n_i: (m % n, (m // n) * 4 + k)),
            b_spec=pl.BlockSpec((None, D_MODEL, IN_A_TN), lambda m, j, k, g=g: ((4 * g + k) // 3, j, (4 * g + k) % 3)),
            o_spec=pl.BlockSpec((tm, D_MODEL), lambda m, j, k, n=n_i: (m % n, m // n)),
            o_shape=(S // dil, dil * D_MODEL)))
        tk = min(TOKEN_TK, S // dil)
        n_k = S // dil // tk
        d_w_in_a = _matmul(
            h1[g], dqkv[g], "tn", F32, f"in_a_dw{g}", shape=(D_MODEL, 3 * D_MODEL, S), tiles=(D_MODEL, IN_A_TN, tk),
            a_spec=pl.BlockSpec((tk, D_MODEL), lambda m, j, k, n=n_k: (k % n, k // n)),
            b_spec=pl.BlockSpec((tk, IN_A_TN), lambda m, j, k, n=n_k: (k % n, (k // n) * 4 + j)),
            o_spec=pl.BlockSpec((None, D_MODEL, IN_A_TN), lambda m, j, k, g=g: ((4 * g + j) // 3, m, (4 * g + j) % 3)),
            o_shape=(4, D_MODEL, 9 * D_MODEL // 4), prev=d_w_in_a)
    dx0, dsc1_0, dsh1_0, d_nmix0 = _norm_mod_bwd(dh1[0], x, dx1, w["norm_mix"][0:1], mods[0], J_SC1, "norm_mix_bwd0",
                                                 dh_views=dh1[1:])
    dmod0 = jnp.concatenate([dsh1_0, dsc1_0, dg1_0, dsh2_0, dsc2_0, dg2_0], axis=1)

    grads = dict(
        w_in_a=d_w_in_a, w_out_a=d_w_out_a, rel_bias=d_rel_bias, w_in_b=d_w_in_b, w_gate_b=d_w_gate,
        b_gate_b=d_b_gate, gnorm_b=d_gnorm, w_out_b=d_w_out_b,
        norm_mix=jnp.concatenate([d_nmix0, d_nmix1], axis=0),
        norm_ffn=jnp.concatenate([g_ffn0["norm_ffn"], g_ffn1["norm_ffn"]], axis=0),
        w_ada=_ada_outer(c, jnp.stack([dmod0, dmod1]), "ada_outer"),
        b_ada=jnp.concatenate([dmod0, dmod1], axis=0),
        w_up=g_ffn0["w_up"],
        conv_w=jnp.stack([g_ffn0["conv_w"], g_ffn1["conv_w"]]),
        conv_b=jnp.concatenate([g_ffn0["conv_b"], g_ffn1["conv_b"]], axis=0),
        w_down=g_ffn0["w_down"],
        norm_final=d_norm_final)
    return loss, dx0, grads


N_CHIPS = 4
N_DEV = 8
WEIGHTS = ("w_in_a", "w_out_a", "rel_bias", "w_in_b", "w_gate_b", "b_gate_b", "gnorm_b", "w_out_b", "norm_mix",
           "norm_ffn", "w_ada", "b_ada", "w_up", "conv_w", "conv_b", "w_down", "norm_final")
SHARD_AXIS = dict(w_in_a=2, w_out_a=1, w_in_b=2, w_gate_b=2, b_gate_b=1, gnorm_b=1, w_out_b=1, w_ada=2, w_up=2,
                  conv_w=2, w_down=1)
SHARDED = tuple(n for n in WEIGHTS if n in SHARD_AXIS)
REPLICATED = tuple(n for n in WEIGHTS if n not in SHARD_AXIS)
BIG = ("w_in_a", "w_out_a", "w_in_b", "w_out_b", "w_ada", "w_up", "w_down")
SMALL = ("w_gate_b", "b_gate_b", "gnorm_b", "conv_w")
SMALL_FULL = dict(w_gate_b=(1, 16, 512), b_gate_b=(1, 512), gnorm_b=(1, 256), conv_w=(2, 3, 5632))
R_SMALL = 16
R_TINY = 72
LOSS_SLOT = 72960
R_TINY_SHARD = 40
W_IN_B_PAD = 896

COMM_VIEW = dict(
    w_in_a=((4096, 2304), 1024, 512, 512),
    w_out_a=((1024, 1024), 256, 128, 128),
    w_in_b=((4096, W_IN_B_PAD), 1024, 512, 512),
    w_out_b=((1024, 1024), 256, 128, 128),
    w_ada=((8192, 1536), 2048, 1024, 1024),
    w_up=((8192, 1408), 2048, 1024, 1024),
    w_down=((5632, 1024), 704, 2816, 704))


def _pack(arrs, rows):
    flat = jnp.concatenate([a.reshape(-1) for a in arrs])
    return jnp.pad(flat, (0, rows * LANES - flat.shape[0])).reshape(rows, LANES)


def _unpack(flat2d, shapes):
    flat = flat2d.reshape(-1)
    out, off = [], 0
    for shp in shapes:
        n = math.prod(shp)
        out.append(flat[off:off + n].reshape(shp))
        off += n
    return out


def _chip_slice(a, axis, k):
    n = a.shape[axis] // N_CHIPS
    return lax.slice_in_dim(a, k * n, (k + 1) * n, axis=axis)


def _place():
    mx, my, mc = lax.axis_index("x"), lax.axis_index("y"), lax.axis_index("c")
    chips = [(1 - mx, my), (mx, 1 - my), (1 - mx, 1 - my)]
    return mx, my, mc, chips


def _rcopy(src, dst, send_sem, recv_sem, dev):
    return pltpu.make_async_remote_copy(src_ref=src, dst_ref=dst, send_sem=send_sem, recv_sem=recv_sem,
                                        device_id=dev, device_id_type=MESH)


def _comm_call(body, name, ins, out_shapes, n_sems, in_place=False):
    n_in, n_out = len(ins), len(out_shapes)

    def wrapped(*refs):
        body(refs[:n_in], refs[n_in:n_in + n_out], *refs[n_in + n_out:])

    return pl.pallas_call(
        wrapped, name=name, in_specs=[HBM] * n_in, out_specs=[HBM] * n_out, out_shape=out_shapes,
        input_output_aliases={i: i for i in range(n_in)} if in_place else {},
        scratch_shapes=[pltpu.SemaphoreType.DMA((n_sems,)), pltpu.SemaphoreType.DMA((n_sems,))])(*ins)


DMA_CHUNK_BYTES = 2 * 1024 * 1024


def _rows(ref, start, size):
    return ref.at[pl.ds(pl.multiple_of(start, 16), size), :]


def _block(ref, name, k, h):
    _, bk, bh, nr = COMM_VIEW[name]
    return _rows(ref, bk * k + bh * h, nr)


def _chunks(nr, row_bytes):
    n = 1
    while nr % (2 * n) == 0 and (nr // (2 * n)) % 16 == 0 and (nr // n) * row_bytes > DMA_CHUNK_BYTES:
        n *= 2
    return [(i * (nr // n), nr // n) for i in range(n)]


def _gather_big(views, name):
    names = BIG

    def body(x_refs, out_refs, send_sems, recv_sems):
        mx, my, mc, chips = _place()
        chip = 2 * mx + my
        sibling = (mx, my, 1 - mc)
        sends = []
        for a, n in enumerate(names):
            for j, (cx, cy) in enumerate(chips):
                blk = _block(out_refs[a], n, chip, mc)
                cp = _rcopy(blk, blk, send_sems.at[6 * a + j], recv_sems.at[6 * a + j], (cx, cy, mc))
                cp.start()
                sends.append(cp)
        for a, n in enumerate(names):
            for j, (cx, cy) in enumerate(chips):
                blk = _block(out_refs[a], n, 2 * cx + cy, mc)
                _rcopy(blk, blk, send_sems.at[6 * a + j], recv_sems.at[6 * a + j], sibling).wait_recv()
                cp = _rcopy(blk, blk, send_sems.at[6 * a + 3 + j], recv_sems.at[6 * a + 3 + j], sibling)
                cp.start()
                sends.append(cp)
        for a, n in enumerate(names):
            for j, (cx, cy) in enumerate(chips):
                blk = _block(out_refs[a], n, 2 * cx + cy, 1 - mc)
                _rcopy(blk, blk, send_sems.at[6 * a + 3 + j], recv_sems.at[6 * a + 3 + j], sibling).wait_recv()
        for cp in sends:
            cp.wait_send()

    outs = _comm_call(body, name, [views[n] for n in names],
                      [jax.ShapeDtypeStruct(views[n].shape, views[n].dtype) for n in names], 6 * len(names),
                      in_place=True)
    return dict(zip(names, outs))


def _rs_pair_exchange(views, name):
    names = BIG

    def body(g_refs, recv_refs, send_sems, recv_sems):
        mx, my, mc, _ = _place()
        sibling = (mx, my, 1 - mc)
        for a, n in enumerate(names):
            (_, cols), _, _, nr = COMM_VIEW[n]
            for k in range(N_CHIPS):
                src = _block(g_refs[a], n, k, 1 - mc)
                for start, size in _chunks(nr, cols * 4):
                    _rcopy(src.at[pl.ds(start, size), :], recv_refs[a].at[k, pl.ds(start, size), :],
                           send_sems.at[a], recv_sems.at[a], sibling).start()
        for a in range(len(names)):
            _rcopy(recv_refs[a], recv_refs[a], send_sems.at[a], recv_sems.at[a], sibling).wait()

    outs = _comm_call(body, name, [views[n] for n in names],
                      [jax.ShapeDtypeStruct((N_CHIPS, COMM_VIEW[n][3], COMM_VIEW[n][0][1]), F32) for n in names],
                      len(names))
    return dict(zip(names, outs))


def _pair_add(view, recv, c_idx, n, out_dtype, name):
    (_, cols), bk, bh, nr = COMM_VIEW[n]
    tr = _pick(math.gcd(bk, bh, nr), 256, 8)

    def body(c_ref, g_ref, r_ref, o_ref):
        o_ref[...] = (g_ref[...] + r_ref[...]).astype(o_ref.dtype)

    piece = pl.BlockSpec((None, tr, cols), lambda k, i, c_ref: (k, i, 0))
    return pl.pallas_call(
        body, name=name,
        grid_spec=pltpu.PrefetchScalarGridSpec(
            num_scalar_prefetch=1, grid=(N_CHIPS, nr // tr),
            in_specs=[pl.BlockSpec((tr, cols), lambda k, i, c_ref: ((bk * k + bh * c_ref[0]) // tr + i, 0)), piece],
            out_specs=piece),
        out_shape=jax.ShapeDtypeStruct((N_CHIPS, nr, cols), out_dtype), compiler_params=_params(2))(
            c_idx, view, recv)


def _rs_chip_exchange(q, name):
    names = BIG

    def body(q_refs, out_refs, send_sems, recv_sems):
        mx, my, mc, chips = _place()
        chip = 2 * mx + my
        sends = []
        for a in range(len(names)):
            for j, (cx, cy) in enumerate(chips):
                cp = _rcopy(q_refs[a].at[2 * cx + cy], out_refs[a].at[chip], send_sems.at[3 * a + j],
                            recv_sems.at[3 * a + j], (cx, cy, mc))
                cp.start()
                sends.append(cp)
        for a in range(len(names)):
            for j, (cx, cy) in enumerate(chips):
                blk = out_refs[a].at[2 * cx + cy]
                _rcopy(blk, blk, send_sems.at[3 * a + j], recv_sems.at[3 * a + j], (cx, cy, mc)).wait_recv()
        for cp in sends:
            cp.wait_send()

    outs = _comm_call(body, name, [q[n] for n in names],
                      [jax.ShapeDtypeStruct(q[n].shape, q[n].dtype) for n in names], 3 * len(names))
    return dict(zip(names, outs))


def _rs_pair_gather(r, name):
    names = BIG

    def body(r_refs, out_refs, send_sems, recv_sems):
        mx, my, mc, _ = _place()
        sibling = (mx, my, 1 - mc)
        for a, n in enumerate(names):
            (_, cols), _, _, nr = COMM_VIEW[n]
            for start, size in _chunks(nr, cols * 4):
                rows = _rows(out_refs[a], mc * nr + start, size)
                _rcopy(rows, rows, send_sems.at[a], recv_sems.at[a], sibling).start()
        for a, n in enumerate(names):
            nr = COMM_VIEW[n][3]
            _rcopy(_rows(out_refs[a], mc * nr, nr), _rows(out_refs[a], (1 - mc) * nr, nr), send_sems.at[a],
                   recv_sems.at[a], sibling).wait()

    outs = _comm_call(body, name, [r[n] for n in names],
                      [jax.ShapeDtypeStruct(r[n].shape, F32) for n in names], len(names), in_place=True)
    return dict(zip(names, outs))


def _gather8(x, reduce, name):
    rows = x.shape[0]

    def body(x_ref, out_ref, *rest):
        if reduce:
            buf_ref, send_sems, recv_sems = rest
        else:
            (send_sems, recv_sems), buf_ref = rest, out_ref
        mx, my, mc, _ = _place()
        me = 4 * mx + 2 * my + mc
        buf_ref[me] = x_ref[...]
        peers = []
        for j in range(1, N_DEV):
            px = 1 - mx if j & 4 else mx
            py = 1 - my if j & 2 else my
            pc = 1 - mc if j & 1 else mc
            peers.append((px, py, pc))
        sends = [_rcopy(x_ref, buf_ref.at[me], send_sems.at[j], recv_sems.at[j], p) for j, p in enumerate(peers)]
        for cp in sends:
            cp.start()
        for j, (px, py, pc) in enumerate(peers):
            _rcopy(x_ref, buf_ref.at[4 * px + 2 * py + pc], send_sems.at[j], recv_sems.at[j], (px, py, pc)).wait_recv()
        for cp in sends:
            cp.wait_send()
        if reduce:
            acc = buf_ref[0]
            for d in range(1, N_DEV):
                acc = acc + buf_ref[d]
            out_ref[...] = acc

    vmem = pl.BlockSpec(memory_space=pltpu.VMEM)
    sems = [pltpu.SemaphoreType.DMA((N_DEV - 1,)), pltpu.SemaphoreType.DMA((N_DEV - 1,))]
    if reduce:
        out_shape = jax.ShapeDtypeStruct((rows, LANES), F32)
        scratch = [pltpu.VMEM((N_DEV, rows, LANES), F32)] + sems
    else:
        out_shape = jax.ShapeDtypeStruct((N_DEV, rows, LANES), F32)
        scratch = sems
    return pl.pallas_call(body, name=name, in_specs=[vmem], out_specs=vmem, out_shape=out_shape,
                          scratch_shapes=scratch)(x)


def _sum4(p, q, chip, core, name):
    _, nr, cols = p.shape
    tr = _pick(nr, 256, 8)

    def body(chip_ref, core_ref, p0, p1, p2, p3, own, o_ref):
        s = [jnp.where(chip_ref[0] == k, own[...], pk[...]).astype(F32) for k, pk in enumerate((p0, p1, p2, p3))]
        o_ref[...] = ((s[0] + s[1]) + s[2]) + s[3]

    return pl.pallas_call(
        body, name=name,
        grid_spec=pltpu.PrefetchScalarGridSpec(
            num_scalar_prefetch=2, grid=(nr // tr,),
            in_specs=[pl.BlockSpec((None, tr, cols), lambda i, ch, co, k=k: (jnp.where(ch[0] == k, k ^ 1, k), i, 0))
                      for k in range(N_CHIPS)]
            + [pl.BlockSpec((None, tr, cols), lambda i, ch, co: (ch[0], i, 0))],
            out_specs=pl.BlockSpec((tr, cols), lambda i, ch, co: (co[0] * (nr // tr) + i, 0))),
        out_shape=jax.ShapeDtypeStruct((2 * nr, cols), F32), compiler_params=_params(1))(chip, core, p, p, p, p, q)


def _place_shard(shard, chip, n, name):
    (rows, cols), bk, bh, nr = COMM_VIEW[n]
    tr = _pick(math.gcd(bk, bh, nr), 256, 16)

    def body(chip_ref, x_ref, o_ref):
        o_ref[...] = x_ref[...].astype(BF16)

    return pl.pallas_call(
        body, name=name,
        grid_spec=pltpu.PrefetchScalarGridSpec(
            num_scalar_prefetch=1, grid=(2, nr // tr),
            in_specs=[pl.BlockSpec((tr, cols), lambda h, i, ch: (h * (nr // tr) + i, 0))],
            out_specs=pl.BlockSpec((tr, cols), lambda h, i, ch: ((bk * ch[0] + bh * h) // tr + i, 0))),
        out_shape=jax.ShapeDtypeStruct((rows, cols), BF16), compiler_params=_params(2))(chip, shard)


def _adamw(w, g, m, v, name):
    rows, cols = w.shape
    tr = _pick(rows, max(8, (1 << 20) // (4 * cols)), 8)

    def body(w_ref, g_ref, m_ref, v_ref, d_ref, mo_ref, vo_ref):
        gv = g_ref[...]
        mn = ADAM_B1 * m_ref[...] + (1.0 - ADAM_B1) * gv
        vn = ADAM_B2 * v_ref[...] + (1.0 - ADAM_B2) * (gv * gv)
        m_hat = mn / (1.0 - ADAM_B1 ** ADAM_STEP)
        v_hat = vn / (1.0 - ADAM_B2 ** ADAM_STEP)
        d_ref[...] = -ADAM_LR * (m_hat / (jnp.sqrt(v_hat) + ADAM_EPS) + ADAM_WD * w_ref[...])
        mo_ref[...] = mn
        vo_ref[...] = vn

    shape = jax.ShapeDtypeStruct(w.shape, F32)
    return pl.pallas_call(
        body, name=name, grid=(rows // tr,), in_specs=[_row(tr, cols)] * 4, out_specs=[_row(tr, cols)] * 3,
        out_shape=[shape] * 3, compiler_params=_params(1))(w, g, m, v)


W_IN_B_SHARD = 772


def _shard_view(n, a):
    return a.reshape(-1, a.shape[-1])


def _gather_weights(p, chip):
    shards = {n: _shard_view(n, p[n]) for n in BIG}
    shards["w_in_b"] = jnp.pad(shards["w_in_b"], ((0, 0), (0, W_IN_B_PAD - W_IN_B_SHARD)))
    big = _gather_big({n: _place_shard(shards[n], chip, n, f"place_{n}") for n in BIG}, "gather_weights")
    small = _gather8(_pack([p[n] for n in SMALL], R_SMALL), False, "gather_small")
    pieces = [_unpack(small[2 * k], [p[n].shape for n in SMALL]) for k in range(N_CHIPS)]
    full = {n: jnp.concatenate([pieces[k][i] for k in range(N_CHIPS)], axis=SHARD_AXIS[n])
            for i, n in enumerate(SMALL)}
    wb = big["w_in_b"].reshape(N_CHIPS, D_MODEL, W_IN_B_PAD)
    wb = jnp.concatenate([wb[k, :, :W_IN_B_SHARD] for k in range(N_CHIPS)], axis=1)
    return dict(
        w_in_a=big["w_in_a"].reshape(N_CHIPS, D_MODEL, -1), w_out_a=big["w_out_a"], w_out_b=big["w_out_b"],
        w_in_b=jnp.concatenate([wb[:, :2048], wb[:, 2064:3088], wb[:, 2048:2064],
                                jnp.zeros((D_MODEL, B_IN_PAD - 3088), BF16)], axis=1),
        w_ada=big["w_ada"].reshape(N_CHIPS, 2 * D_MODEL, -1), w_up=big["w_up"].reshape(N_CHIPS, 2 * D_MODEL, -1),
        w_down=big["w_down"],
        w_gate_b=jnp.pad(full["w_gate_b"][0], ((0, 128 - B_GATE_RANK), (0, 0))),
        b_gate_b=full["b_gate_b"], gnorm_b=full["gnorm_b"], conv_w=full["conv_w"],
        rel_bias=p["rel_bias"], norm_mix=p["norm_mix"], norm_ffn=p["norm_ffn"], b_ada=p["b_ada"],
        conv_b=p["conv_b"], norm_final=p["norm_final"].reshape(1, D_MODEL))


def _grad_views(g):
    gb = g["w_in_b"]
    gb = jnp.concatenate([gb[:, :2048], gb[:, 3072:3088], gb[:, 2048:3072]], axis=1)
    gb = jnp.stack([jnp.pad(gb[:, k * W_IN_B_SHARD:(k + 1) * W_IN_B_SHARD], ((0, 0), (0, W_IN_B_PAD - W_IN_B_SHARD)))
                    for k in range(N_CHIPS)])
    views = {n: g[n].reshape(COMM_VIEW[n][0]) for n in BIG if n != "w_in_b"}
    views["w_in_b"] = gb.reshape(COMM_VIEW["w_in_b"][0])
    return views


def _tiny_grads(g):
    out = {n: g[n] for n in REPLICATED if n != "norm_final"}
    out.update(norm_final=g["norm_final"].reshape(D_MODEL), w_gate_b=g["w_gate_b"][:B_GATE_RANK][None],
               b_gate_b=g["b_gate_b"], gnorm_b=g["gnorm_b"], conv_w=g["conv_w"])
    return out


def kernel(x, c, w_in_a, w_out_a, rel_bias, w_in_b, w_gate_b, b_gate_b, gnorm_b, w_out_b, norm_mix, norm_ffn, w_ada, b_ada, w_up, conv_w, conv_b, w_down, norm_final, loss_target, m_w_in_a, m_w_out_a, m_rel_bias, m_w_in_b, m_w_gate_b, m_b_gate_b, m_gnorm_b, m_w_out_b, m_norm_mix, m_norm_ffn, m_w_ada, m_b_ada, m_w_up, m_conv_w, m_conv_b, m_w_down, m_norm_final, v_w_in_a, v_w_out_a, v_rel_bias, v_w_in_b, v_w_gate_b, v_b_gate_b, v_gnorm_b, v_w_out_b, v_norm_mix, v_norm_ffn, v_w_ada, v_b_ada, v_w_up, v_conv_w, v_conv_b, v_w_down, v_norm_final):
    p = dict(zip(WEIGHTS, (w_in_a, w_out_a, rel_bias, w_in_b, w_gate_b, b_gate_b, gnorm_b, w_out_b, norm_mix,
                           norm_ffn, w_ada, b_ada, w_up, conv_w, conv_b, w_down, norm_final)))
    pm = dict(zip(WEIGHTS, (m_w_in_a, m_w_out_a, m_rel_bias, m_w_in_b, m_w_gate_b, m_b_gate_b, m_gnorm_b, m_w_out_b,
                            m_norm_mix, m_norm_ffn, m_w_ada, m_b_ada, m_w_up, m_conv_w, m_conv_b, m_w_down,
                            m_norm_final)))
    pv = dict(zip(WEIGHTS, (v_w_in_a, v_w_out_a, v_rel_bias, v_w_in_b, v_w_gate_b, v_b_gate_b, v_gnorm_b, v_w_out_b,
                            v_norm_mix, v_norm_ffn, v_w_ada, v_b_ada, v_w_up, v_conv_w, v_conv_b, v_w_down,
                            v_norm_final)))
    S = x.shape[1]

    chip = 2 * lax.axis_index("x") + lax.axis_index("y")
    core = lax.axis_index("c").astype(jnp.int32).reshape(1)
    chip_s = chip.astype(jnp.int32).reshape(1)

    w = _gather_weights(p, chip_s)
    loss, dx0, grads = _local_step(x.reshape(S, D_MODEL), c, loss_target.reshape(S, D_MODEL), w)

    views = _grad_views(grads)
    recv = _rs_pair_exchange(views, "grads_pair_exchange")
    pair = {n: _pair_add(views[n], recv[n], core, n, BF16, f"grads_pair_add_{n}") for n in BIG}
    from_chips = _rs_chip_exchange(pair, "grads_chip_exchange")
    g_big = _rs_pair_gather({n: _sum4(from_chips[n], pair[n], chip_s, core, f"grads_chip_sum_{n}") for n in BIG},
                            "grads_pair_gather")
    g_big["w_in_b"] = g_big["w_in_b"][:, :W_IN_B_SHARD]

    tiny = _tiny_grads(grads)
    tiny_names = SMALL + REPLICATED
    tiny_full = {n: SMALL_FULL[n] if n in SMALL_FULL else p[n].shape for n in tiny_names}
    tiny_sum = _gather8(_pack([tiny[n] for n in tiny_names] + [loss[0, 0:1]], R_TINY), True, "grads_tiny_sum")
    g_tiny = dict(zip(tiny_names, _unpack(tiny_sum, [tiny_full[n] for n in tiny_names])))
    for n in SMALL:
        width = p[n].shape[SHARD_AXIS[n]]
        g_tiny[n] = lax.dynamic_slice_in_dim(g_tiny[n], chip * width, width, axis=SHARD_AXIS[n])
    total_loss = tiny_sum.reshape(-1)[LOSS_SLOT]

    out = {}
    for n in BIG:
        res = _adamw(_shard_view(n, p[n]), g_big[n], _shard_view(n, pm[n]), _shard_view(n, pv[n]), f"adamw_{n}")
        out[n] = [t.reshape(p[n].shape) for t in (g_big[n],) + tuple(res)]
    res = _adamw(*[_pack([d[n] for n in tiny_names], R_TINY_SHARD) for d in (p, g_tiny, pm, pv)], "adamw_tiny")
    unpacked = [_unpack(t, [p[n].shape for n in tiny_names]) for t in res]
    for i, n in enumerate(tiny_names):
        out[n] = [g_tiny[n]] + [u[i] for u in unpacked]

    return (total_loss, dx0.reshape(x.shape), *[out[n][0] for n in WEIGHTS], *[out[n][1] for n in WEIGHTS],
            *[out[n][2] for n in WEIGHTS], *[out[n][3] for n in WEIGHTS])
```

```python
import functools
import math

import numpy as np
import jax
import jax.numpy as jnp
from jax import lax
from jax.experimental import pallas as pl
from jax.experimental.pallas import tpu as pltpu

F32 = jnp.float32
BF16 = jnp.bfloat16
MESH = pl.DeviceIdType.MESH

D_MODEL = 1024
A_CONFIGS = ((128, 1), (512, 4), (2048, 16))
A_HEADS = 16
A_HEAD_DIM = 64
A_BLK = 128
N_BUCKETS = 32
MAX_DISTANCE = 2048
B_HEADS = 4
B_DK = 128
B_DV = 256
B_QK = 512
B_V = 1024
B_GATE_RANK = 16
B_TAU = 16.0
B_CHUNK = 64
B_IN_PAD = 3200
D_FF = 2816
EPS = 1e-6
NEG_INF = -1e30
ADAM_LR = 0.001
ADAM_B1 = 0.9
ADAM_B2 = 0.999
ADAM_EPS = 1e-08
ADAM_WD = 0.01
ADAM_STEP = 10

LANES = 1024
VMEM_LIMIT = 48 * 1024 * 1024
ROW_TILE = 256
GLA_ROWS = 512

HBM = pl.BlockSpec(memory_space=pl.ANY)


def _params(n_axes):
    return pltpu.CompilerParams(dimension_semantics=("arbitrary",) * n_axes, vmem_limit_bytes=VMEM_LIMIT)


def _pick(n, cap, mult=128):
    best = None
    for t in range(mult, min(n, cap) + 1, mult):
        if n % t == 0:
            best = t
    return n if best is None else best


def _matmul(a, b, mode, out_dtype, name, shape=None, tiles=None, a_spec=None, b_spec=None, o_spec=None, o_shape=None,
            prev=None, add=None):
    dims = {"nn": (((1,), (0,)), ((), ())), "nt": NT, "tn": TN}[mode]
    if shape is None:
        if mode == "nn":
            (M, K), (_, N) = a.shape, b.shape
        elif mode == "nt":
            (M, K), (N, _) = a.shape, b.shape
        else:
            (K, M), (_, N) = a.shape, b.shape
    else:
        M, N, K = shape
    if tiles is None:
        tiles = (_pick(M, 1024, 128 if mode == "tn" else 8), _pick(N, 1536), _pick(K, 1024 if mode != "tn" else 2048))
    tm, tn, tk = tiles
    nk = K // tk
    if a_spec is None:
        a_spec = pl.BlockSpec((tk, tm), lambda i, j, k: (k, i)) if mode == "tn" else pl.BlockSpec(
            (tm, tk), lambda i, j, k: (i, k))
    if b_spec is None:
        b_spec = pl.BlockSpec((tn, tk), lambda i, j, k: (j, k)) if mode == "nt" else pl.BlockSpec(
            (tk, tn), lambda i, j, k: (k, j))
    if o_spec is None:
        o_spec = pl.BlockSpec((tm, tn), lambda i, j, k: (i, j))
        o_shape = (M, N)

    has_add = add is not None

    def body(a_ref, b_ref, *rest):
        part = lax.dot_general(a_ref[...].astype(BF16), b_ref[...].astype(BF16), dims, preferred_element_type=F32)

        def finish(total):
            if has_add:
                total = total + rest[0][...]
            return total.astype(out_dtype)

        if nk == 1:
            rest[-1][...] = finish(part)
            return
        o_ref, acc_ref = rest[-2:]
        k = pl.program_id(2)

        @pl.when(k == 0)
        def _():
            acc_ref[...] = part

        @pl.when(k > 0)
        def _():
            acc_ref[...] += part

        @pl.when(k == nk - 1)
        def _():
            o_ref[...] = finish(acc_ref[...])

    ins, in_specs, aliases = [a, b], [a_spec, b_spec], {}
    if has_add:
        ins.append(add)
        in_specs.append(o_spec)
    if prev is not None:
        aliases = {len(ins): 0}
        ins.append(prev)
        in_specs.append(HBM)
    return pl.pallas_call(
        body, name=name, grid=(M // tm, N // tn, nk), in_specs=in_specs, out_specs=o_spec,
        out_shape=jax.ShapeDtypeStruct(o_shape, out_dtype),
        scratch_shapes=[pltpu.VMEM((tm, tn), F32)] if nk > 1 else [],
        input_output_aliases=aliases, compiler_params=_params(3))(*ins)


def _row(tr, d=D_MODEL):
    return pl.BlockSpec((tr, d), lambda i: (i, 0))


def _vec(d=D_MODEL):
    return pl.BlockSpec((1, d), lambda i: (0, 0))


def _modspec(j):
    return pl.BlockSpec((8, D_MODEL), lambda i: (0, j))


def _silu(x):
    return x * jax.nn.sigmoid(x)


def _dsilu(x):
    s = jax.nn.sigmoid(x)
    return s * (1.0 + x * (1.0 - s))


ADA_TN = 6 * D_MODEL // 4


def _ada_mod_shard(c_all, w_ada, b_ada, name):
    def body(c_ref, w_ref, b_ref, o_ref):
        o_ref[...] = _dot(_silu(c_ref[...]), w_ref[...]) + b_ref[...]

    return pl.pallas_call(
        body, name=name, grid=(2,),
        in_specs=[pl.BlockSpec((N_DEV, D_MODEL), lambda l: (0, 0)),
                  pl.BlockSpec((None, D_MODEL, ADA_TN), lambda l: (l, 0, 0)),
                  pl.BlockSpec((None, 1, ADA_TN), lambda l: (l, 0, 0))],
        out_specs=pl.BlockSpec((None, N_DEV, ADA_TN), lambda l: (l, 0, 0)),
        out_shape=jax.ShapeDtypeStruct((2, N_DEV, ADA_TN), F32), compiler_params=_params(1))(c_all, w_ada, b_ada)


def _ada_grad_shard(c_all, dmod, name):
    def body(c_ref, d_ref, o_ref):
        o_ref[...] = _dot(_silu(c_ref[...]), d_ref[...], TN)

    return pl.pallas_call(
        body, name=name, grid=(2,),
        in_specs=[pl.BlockSpec((N_DEV, D_MODEL), lambda l: (0, 0)),
                  pl.BlockSpec((None, N_DEV, ADA_TN), lambda l: (l, 0, 0))],
        out_specs=pl.BlockSpec((None, D_MODEL, ADA_TN), lambda l: (l, 0, 0)),
        out_shape=jax.ShapeDtypeStruct((2, D_MODEL, ADA_TN), F32), compiler_params=_params(1))(c_all, dmod)


def _view_spec(tr, d, width=D_MODEL):
    return pl.BlockSpec((tr // d, d * width), lambda i: (i, 0))


def _view_shape(S, d, dtype, width=D_MODEL):
    return jax.ShapeDtypeStruct((S // d, d * width), dtype)


LANE_TILE = 128
N_LANE_TILES = D_MODEL // LANE_TILE


def _token_scratch(tr):
    return pltpu.VMEM((N_LANE_TILES, tr, LANE_TILE), F32)


def _scratch_put(scr_ref, val):
    for c in range(N_LANE_TILES):
        scr_ref[c] = val[:, c * LANE_TILE:(c + 1) * LANE_TILE]


def _scratch_get(scr_ref):
    return jnp.concatenate([scr_ref[c] for c in range(N_LANE_TILES)], axis=1)


def _store_view(scr_ref, out_ref, d):
    n = scr_ref.shape[1] // d
    for r in range(d):
        for c in range(N_LANE_TILES):
            lo = r * D_MODEL + c * LANE_TILE
            out_ref[:, lo:lo + LANE_TILE] = scr_ref.at[c][pl.ds(r, n, stride=d), :].astype(out_ref.dtype)


def _load_view(view_ref, scr_ref, d):
    n = scr_ref.shape[1] // d
    for r in range(d):
        for c in range(N_LANE_TILES):
            lo = r * D_MODEL + c * LANE_TILE
            scr_ref.at[c][pl.ds(r, n, stride=d), :] = view_ref[:, lo:lo + LANE_TILE].astype(F32)


def _norm_mod(x, gamma, mod, j_sc, j_sh, name, resid=None, gate_mod=None, j_gate=None, views=False):
    S = x.shape[0]
    tr = ROW_TILE
    has_res = resid is not None

    def body(*refs):
        if has_res:
            x_ref, y_ref, gate_ref, g_ref, sc_ref, sh_ref, xo_ref, h_ref = refs
            xn = x_ref[...] + gate_ref[0:1, :] * y_ref[...]
            xo_ref[...] = xn
        elif views:
            x_ref, g_ref, sc_ref, sh_ref, h_ref, h4_ref, h16_ref, scr_ref = refs
            xn = x_ref[...]
        else:
            x_ref, g_ref, sc_ref, sh_ref, h_ref = refs
            xn = x_ref[...]
        r = lax.rsqrt(jnp.mean(xn * xn, axis=-1, keepdims=True) + EPS)
        n = (xn * r) * g_ref[...]
        h = n * (1.0 + sc_ref[0:1, :]) + sh_ref[0:1, :]
        h_ref[...] = h.astype(BF16)
        if views:
            _scratch_put(scr_ref, h)
            _store_view(scr_ref, h4_ref, 4)
            _store_view(scr_ref, h16_ref, 16)

    scratch = []
    if has_res:
        ins = [x, resid, gate_mod, gamma, mod, mod]
        in_specs = [_row(tr), _row(tr), _modspec(j_gate), _vec(), _modspec(j_sc), _modspec(j_sh)]
        out_specs = [_row(tr), _row(tr)]
        out_shape = [jax.ShapeDtypeStruct((S, D_MODEL), F32), jax.ShapeDtypeStruct((S, D_MODEL), BF16)]
    else:
        ins = [x, gamma, mod, mod]
        in_specs = [_row(tr), _vec(), _modspec(j_sc), _modspec(j_sh)]
        out_specs = _row(tr)
        out_shape = jax.ShapeDtypeStruct((S, D_MODEL), BF16)
        if views:
            out_specs = [_row(tr), _view_spec(tr, 4), _view_spec(tr, 16)]
            out_shape = [out_shape, _view_shape(S, 4, BF16), _view_shape(S, 16, BF16)]
            scratch = [_token_scratch(tr)]
    return pl.pallas_call(body, name=name, grid=(S // tr,), in_specs=in_specs, out_specs=out_specs,
                          out_shape=out_shape, scratch_shapes=scratch, compiler_params=_params(1))(*ins)


def _final_loss(x, resid, mod, j_gate, gamma, tgt, name):
    S = x.shape[0]
    tr = ROW_TILE

    def body(x_ref, y_ref, gate_ref, g_ref, t_ref, dx_ref, loss_ref, dg_ref, dy_ref, dgate_ref):
        @pl.when(pl.program_id(0) == 0)
        def _():
            loss_ref[...] = jnp.zeros_like(loss_ref)
            dg_ref[...] = jnp.zeros_like(dg_ref)
            dgate_ref[...] = jnp.zeros_like(dgate_ref)

        yv = y_ref[...]
        xn = x_ref[...] + gate_ref[0:1, :] * yv
        r = lax.rsqrt(jnp.mean(xn * xn, axis=-1, keepdims=True) + EPS)
        xhat = xn * r
        err = xhat * g_ref[...] - t_ref[...]
        loss_ref[...] += 0.5 * jnp.sum(jnp.mean(err * err, axis=-1, keepdims=True))
        dy = err * (1.0 / D_MODEL)
        dg_ref[...] += jnp.sum(dy * xhat, axis=0, keepdims=True)
        dxh = dy * g_ref[...]
        dx = r * (dxh - xhat * jnp.mean(dxh * xhat, axis=-1, keepdims=True))
        dx_ref[...] = dx
        dy_ref[...] = (gate_ref[0:1, :] * dx).astype(BF16)
        dgate_ref[...] += jnp.sum(dx * yv, axis=0, keepdims=True)

    vec = jax.ShapeDtypeStruct((1, D_MODEL), F32)
    return pl.pallas_call(
        body, name=name, grid=(S // tr,),
        in_specs=[_row(tr), _row(tr), _modspec(j_gate), _vec(), _row(tr)],
        out_specs=[_row(tr), pl.BlockSpec((1, 128), lambda i: (0, 0)), _vec(), _row(tr), _vec()],
        out_shape=[jax.ShapeDtypeStruct((S, D_MODEL), F32), jax.ShapeDtypeStruct((1, 128), F32), vec,
                   jax.ShapeDtypeStruct((S, D_MODEL), BF16), vec],
        compiler_params=_params(1))(x, resid, mod, gamma, tgt)


def _norm_mod_bwd(dh, x, dx_res, gamma, mod, j_sc, name, dh_views=None, branch=None):
    S = x.shape[0]
    tr = ROW_TILE
    n_views = 0 if dh_views is None else 2
    n_branch = 0 if branch is None else 2

    def body(dh_ref, *refs):
        x_ref, dr_ref, g_ref, sc_ref = refs[n_views:n_views + 4]
        dx_ref, dsc_ref, dsh_ref, dg_ref = refs[n_views + 4 + n_branch:n_views + 8 + n_branch]

        @pl.when(pl.program_id(0) == 0)
        def _():
            dsc_ref[...] = jnp.zeros_like(dsc_ref)
            dsh_ref[...] = jnp.zeros_like(dsh_ref)
            dg_ref[...] = jnp.zeros_like(dg_ref)

        xv = x_ref[...]
        dh_v = dh_ref[...]
        if n_views:
            scr_ref = refs[-1]
            for view_ref, d in zip(refs[:2], (4, 16)):
                _load_view(view_ref, scr_ref, d)
                dh_v = dh_v + _scratch_get(scr_ref)
        r = lax.rsqrt(jnp.mean(xv * xv, axis=-1, keepdims=True) + EPS)
        xhat = xv * r
        dsh_ref[...] += jnp.sum(dh_v, axis=0, keepdims=True)
        dsc_ref[...] += jnp.sum(dh_v * (xhat * g_ref[...]), axis=0, keepdims=True)
        dn = dh_v * (1.0 + sc_ref[0:1, :])
        dg_ref[...] += jnp.sum(dn * xhat, axis=0, keepdims=True)
        dxh = dn * g_ref[...]
        dx = dr_ref[...] + r * (dxh - xhat * jnp.mean(dxh * xhat, axis=-1, keepdims=True))
        dx_ref[...] = dx
        if n_branch:
            y_ref, gate_ref = refs[n_views + 4:n_views + 6]
            dy_ref, dgate_ref = refs[n_views + 10:n_views + 12]

            @pl.when(pl.program_id(0) == 0)
            def _():
                dgate_ref[...] = jnp.zeros_like(dgate_ref)

            dy_ref[...] = (gate_ref[0:1, :] * dx).astype(BF16)
            dgate_ref[...] += jnp.sum(dx * y_ref[...], axis=0, keepdims=True)

    vec = jax.ShapeDtypeStruct((1, D_MODEL), F32)
    views = [] if dh_views is None else list(dh_views)
    view_specs = [_view_spec(tr, 4), _view_spec(tr, 16)] if views else []
    ins = [dh, *views, x, dx_res, gamma, mod]
    in_specs = [_row(tr)] + view_specs + [_row(tr), _row(tr), _vec(), _modspec(j_sc)]
    out_specs = [_row(tr), _vec(), _vec(), _vec()]
    out_shape = [jax.ShapeDtypeStruct((S, D_MODEL), F32), vec, vec, vec]
    if branch is not None:
        y, gate_mod, j_gate = branch
        ins += [y, gate_mod]
        in_specs += [_row(tr), _modspec(j_gate)]
        out_specs += [_row(tr), _vec()]
        out_shape += [jax.ShapeDtypeStruct((S, D_MODEL), BF16), vec]
    return pl.pallas_call(
        body, name=name, grid=(S // tr,), in_specs=in_specs, out_specs=out_specs, out_shape=out_shape,
        scratch_shapes=[_token_scratch(tr)] if views else [], compiler_params=_params(1))(*ins)


def _shift_down(u, halo, s):
    r = pltpu.roll(u, s, 0)
    hr = pltpu.roll(halo, s, 0)
    rid = lax.broadcasted_iota(jnp.int32, hr.shape, 0)
    top = jnp.where(rid < s, hr, r[0:8])
    return jnp.concatenate([top, r[8:]], axis=0)


def _conv3(u, halo, w_ref, b_ref):
    u1 = _shift_down(u, halo, 1)
    u2 = _shift_down(u, halo, 2)
    return b_ref[...] + ((w_ref[0:1, :] * u2 + w_ref[1:2, :] * u1) + w_ref[2:3, :] * u), u1, u2


CONV_TC = 1408


def _conv_specs(tr, S):
    nh = D_FF // CONV_TC
    hb = tr // 8

    def cur(off):
        return pl.BlockSpec((tr, CONV_TC), lambda j, i: (i, j + off))

    def halo(off):
        return pl.BlockSpec((8, CONV_TC), lambda j, i: (jnp.maximum(i * hb - 1, 0), j + off))

    def w(off):
        return pl.BlockSpec((3, CONV_TC), lambda j, i: (0, j + off))

    def b(off):
        return pl.BlockSpec((1, CONV_TC), lambda j, i: (0, j + off))

    return nh, cur, halo, w, b


def _conv_gate(u, conv_w, conv_b, name):
    S = u.shape[0]
    tr = ROW_TILE
    nh, cur, halo, w, b = _conv_specs(tr, S)

    def body(ua_ref, ha_ref, ub_ref, hb_ref, wa_ref, wb_ref, ba_ref, bb_ref, o_ref):
        first = pl.program_id(1) == 0
        ha = jnp.where(first, 0.0, ha_ref[...])
        hbv = jnp.where(first, 0.0, hb_ref[...])
        a, _, _ = _conv3(ua_ref[...], ha, wa_ref, ba_ref)
        bb, _, _ = _conv3(ub_ref[...], hbv, wb_ref, bb_ref)
        o_ref[...] = (_silu(a) * bb).astype(BF16)

    return pl.pallas_call(
        body, name=name, grid=(nh, S // tr),
        in_specs=[cur(0), halo(0), cur(nh), halo(nh), w(0), w(nh), b(0), b(nh)],
        out_specs=pl.BlockSpec((tr, CONV_TC), lambda j, i: (i, j)),
        out_shape=jax.ShapeDtypeStruct((S, D_FF), BF16), compiler_params=_params(2))(
            u, u, u, u, conv_w, conv_w, conv_b, conv_b)


def _conv_gate_bwd(u, dact, conv_w, conv_b, name):
    S = u.shape[0]
    tr = ROW_TILE
    nh, cur, halo, w, b = _conv_specs(tr, S)
    hb = tr // 8
    nlast = S // 8 - 1
    nsteps = S // tr

    def after(off):
        return pl.BlockSpec((8, CONV_TC), lambda j, i: (jnp.minimum((i + 1) * hb, nlast), j + off))

    def body(ua_ref, ha_ref, na_ref, ub_ref, hb_ref, nb_ref, wa_ref, wb_ref, ba_ref, bb_ref, da_ref, dn_ref,
             dua_ref, dub_ref, dwa_ref, dwb_ref, dba_ref, dbb_ref):
        first = pl.program_id(1) == 0
        last = pl.program_id(1) == nsteps - 1

        @pl.when(first)
        def _():
            for r in (dwa_ref, dwb_ref, dba_ref, dbb_ref):
                r[...] = jnp.zeros_like(r)

        ha = jnp.where(first, 0.0, ha_ref[...])
        hbv = jnp.where(first, 0.0, hb_ref[...])
        ua = jnp.concatenate([ua_ref[...], na_ref[...]], axis=0)
        ub = jnp.concatenate([ub_ref[...], nb_ref[...]], axis=0)
        a, ua1, ua2 = _conv3(ua, ha, wa_ref, ba_ref)
        bb, ub1, ub2 = _conv3(ub, hbv, wb_ref, bb_ref)
        dact_v = jnp.concatenate([da_ref[...], jnp.where(last, 0.0, dn_ref[...])], axis=0)
        da = dact_v * bb * _dsilu(a)
        db = dact_v * _silu(a)
        n = tr + 8
        for d, x0, x1, x2, w_ref, du_ref, dw_ref, dbias_ref in (
                (da, ua, ua1, ua2, wa_ref, dua_ref, dwa_ref, dba_ref),
                (db, ub, ub1, ub2, wb_ref, dub_ref, dwb_ref, dbb_ref)):
            d1 = pltpu.roll(d, n - 1, 0)
            d2 = pltpu.roll(d, n - 2, 0)
            du_ref[...] = ((w_ref[2:3, :] * d + w_ref[1:2, :] * d1) + w_ref[0:1, :] * d2)[:tr].astype(BF16)
            dt = d[:tr]
            dbias_ref[...] += jnp.sum(dt, axis=0, keepdims=True)
            dw_ref[0:1, :] += jnp.sum(dt * x2[:tr], axis=0, keepdims=True)
            dw_ref[1:2, :] += jnp.sum(dt * x1[:tr], axis=0, keepdims=True)
            dw_ref[2:3, :] += jnp.sum(dt * x0[:tr], axis=0, keepdims=True)

    half = pl.BlockSpec((tr, CONV_TC), lambda j, i: (i, j))
    half_after = pl.BlockSpec((8, CONV_TC), lambda j, i: (jnp.minimum((i + 1) * hb, nlast), j))
    dw = pl.BlockSpec((8, CONV_TC), lambda j, i: (0, j))
    dbs = pl.BlockSpec((1, CONV_TC), lambda j, i: (0, j))
    f = lambda r, c: jax.ShapeDtypeStruct((r, c), F32)
    du = jax.ShapeDtypeStruct((S, D_FF), BF16)
    return pl.pallas_call(
        body, name=name, grid=(nh, nsteps),
        in_specs=[cur(0), halo(0), after(0), cur(nh), halo(nh), after(nh), w(0), w(nh), b(0), b(nh), half,
                  half_after],
        out_specs=[half, half, dw, dw, dbs, dbs],
        out_shape=[du, du, f(8, D_FF), f(8, D_FF), f(1, D_FF), f(1, D_FF)],
        compiler_params=_params(2))(u, u, u, u, u, u, conv_w, conv_w, conv_b, conv_b, dact, dact)


def _bucket_maps():
    qi = np.arange(A_BLK)[:, None]
    ki = np.arange(2 * A_BLK)[None, :]
    steps = np.clip(qi + A_BLK - ki, 0, A_BLK)
    out = []
    max_exact = N_BUCKETS // 2
    for _, dil in A_CONFIGS:
        dist = steps * dil
        n = np.maximum(dist, max_exact).astype(np.float32)
        large = max_exact + (np.log(n / np.float32(max_exact)) / np.float32(math.log(MAX_DISTANCE / max_exact))
                             * np.float32(N_BUCKETS - max_exact)).astype(np.int32)
        large = np.minimum(large, N_BUCKETS - 1)
        out.append(np.where(dist < max_exact, dist, large))
    return np.stack(out).astype(np.int32)


def _bias_build(rel_bias, buckets, name):
    ng = len(A_CONFIGS)

    def body(t_ref, bk_ref, o_ref):
        gh = pl.program_id(0) * A_HEADS + pl.program_id(1)
        bk = bk_ref[0]
        acc = jnp.zeros((A_BLK, 2 * A_BLK), F32)
        for b in range(N_BUCKETS):
            acc = jnp.where(bk == b, t_ref[b, gh], acc)
        o_ref[0] = acc

    return pl.pallas_call(
        body, name=name, grid=(ng, A_HEADS),
        in_specs=[pl.BlockSpec(memory_space=pltpu.SMEM), pl.BlockSpec((1, A_BLK, 2 * A_BLK), lambda g, h: (g, 0, 0))],
        out_specs=pl.BlockSpec((1, A_BLK, 2 * A_BLK), lambda g, h: (g * A_HEADS + h, 0, 0)),
        out_shape=jax.ShapeDtypeStruct((ng * A_HEADS, A_BLK, 2 * A_BLK), F32),
        compiler_params=_params(2))(rel_bias, buckets)


def _bias_bwd(dbias, buckets, name):
    ng = len(A_CONFIGS)

    def body(d_ref, bk_ref, o_ref):
        gh = pl.program_id(0) * A_HEADS + pl.program_id(1)
        bk = bk_ref[0]
        d = d_ref[0]
        for b in range(N_BUCKETS):
            o_ref[b, gh] = jnp.sum(jnp.where(bk == b, d, 0.0))

    return pl.pallas_call(
        body, name=name, grid=(ng, A_HEADS),
        in_specs=[pl.BlockSpec((1, A_BLK, 2 * A_BLK), lambda g, h: (g * A_HEADS + h, 0, 0)),
                  pl.BlockSpec((1, A_BLK, 2 * A_BLK), lambda g, h: (g, 0, 0))],
        out_specs=pl.BlockSpec(memory_space=pltpu.SMEM),
        out_shape=jax.ShapeDtypeStruct((N_BUCKETS, ng * A_HEADS), F32),
        compiler_params=_params(2))(dbias, buckets)


def _attn_mask(b):
    qi = lax.broadcasted_iota(jnp.int32, (A_BLK, 2 * A_BLK), 0)
    ki = lax.broadcasted_iota(jnp.int32, (A_BLK, 2 * A_BLK), 1)
    band = (ki >= qi) & (ki <= qi + A_BLK)
    return band & ((b > 0) | (ki >= A_BLK))


def _first_head_lanes():
    return lax.broadcasted_iota(jnp.int32, (A_BLK, 2 * A_HEAD_DIM), 1) < A_HEAD_DIM


def _attn_in_specs(g, dil):
    W = A_HEADS * A_HEAD_DIM

    def spec(t, prev, nb):
        def im(r, b):
            bb = jnp.minimum(b, nb - 1)
            if prev:
                bb = jnp.maximum(bb - 1, 0)
            return (bb, r * 3 + t)
        return pl.BlockSpec((A_BLK, W), im)

    return lambda nb: [spec(0, False, nb), spec(1, False, nb), spec(1, True, nb), spec(2, False, nb),
                       spec(2, True, nb)]


def _attn_fwd(qv, bias, g, name):
    _, dil = A_CONFIGS[g]
    L = qv.shape[0]
    nb = L // A_BLK
    W = A_HEADS * A_HEAD_DIM

    def body(q_ref, kc_ref, kp_ref, vc_ref, vp_ref, bias_ref, o_ref, l_ref):
        mask = _attn_mask(pl.program_id(1))
        first = _first_head_lanes()
        for j in range(A_HEADS // 2):
            ps = slice(j * 2 * A_HEAD_DIM, (j + 1) * 2 * A_HEAD_DIM)
            q2 = q_ref[:, ps] * 0.125
            k2 = jnp.concatenate([kp_ref[:, ps], kc_ref[:, ps]], axis=0)
            v2 = jnp.concatenate([vp_ref[:, ps], vc_ref[:, ps]], axis=0)
            o_pair, l_pair = [], []
            for e in range(2):
                qh = jnp.where(first if e == 0 else ~first, q2, jnp.zeros_like(q2))
                s = lax.dot_general(qh, k2, NT, preferred_element_type=F32) + bias_ref[2 * j + e]
                s = jnp.where(mask, s, NEG_INF)
                m = jnp.max(s, axis=-1, keepdims=True)
                p = jnp.exp(s - m)
                den = jnp.sum(p, axis=-1, keepdims=True)
                o_pair.append(jnp.dot(p.astype(BF16), v2, preferred_element_type=F32) / den)
                l_pair.append(m + jnp.log(den))
            o_ref[:, ps] = jnp.where(first, o_pair[0], o_pair[1])
            l_ref[:, ps] = jnp.where(first, l_pair[0], l_pair[1])

    out_spec = pl.BlockSpec((A_BLK, W), lambda r, b: (b, r))
    return pl.pallas_call(
        body, name=name, grid=(dil, nb),
        in_specs=_attn_in_specs(g, dil)(nb) + [pl.BlockSpec((A_HEADS, A_BLK, 2 * A_BLK), lambda r, b: (g, 0, 0))],
        out_specs=[out_spec, out_spec],
        out_shape=[jax.ShapeDtypeStruct((L, dil * W), F32)] * 2,
        compiler_params=_params(2))(qv, qv, qv, qv, qv, bias)


def _mix_fwd(os, ls, name):
    S = os[0].shape[0]
    tr = ROW_TILE

    def body(o0, o1, o2, l0, l1, l2, om_ref, om4_ref, om16_ref, lt_ref, lt4_ref, lt16_ref, s_o1, s_o2, s_l1, s_l2):
        for view_ref, scr_ref, d in ((o1, s_o1, 4), (o2, s_o2, 16), (l1, s_l1, 4), (l2, s_l2, 16)):
            _load_view(view_ref, scr_ref, d)
        a, b, c = l0[...], _scratch_get(s_l1), _scratch_get(s_l2)
        m = jnp.maximum(jnp.maximum(a, b), c)
        ea, eb, ec = jnp.exp(a - m), jnp.exp(b - m), jnp.exp(c - m)
        z = (ea + eb) + ec
        om = ((ea / z) * o0[...] + (eb / z) * _scratch_get(s_o1)) + (ec / z) * _scratch_get(s_o2)
        lt = m + jnp.log(z)
        om_ref[...] = om
        lt_ref[...] = lt
        _scratch_put(s_o1, om)
        _scratch_put(s_l1, lt)
        for scr_ref, v4_ref, v16_ref in ((s_o1, om4_ref, om16_ref), (s_l1, lt4_ref, lt16_ref)):
            _store_view(scr_ref, v4_ref, 4)
            _store_view(scr_ref, v16_ref, 16)

    ins = [_row(tr), _view_spec(tr, 4), _view_spec(tr, 16)]
    outs = [jax.ShapeDtypeStruct((S, D_MODEL), F32), _view_shape(S, 4, F32), _view_shape(S, 16, F32)]
    return pl.pallas_call(
        body, name=name, grid=(S // tr,), in_specs=ins * 2, out_specs=ins * 2, out_shape=outs * 2,
        scratch_shapes=[_token_scratch(tr)] * 4, compiler_params=_params(1))(*os, *ls)


def _to_views(x, name):
    S = x.shape[0]
    tr = ROW_TILE

    def body(x_ref, v4_ref, v16_ref, scr_ref):
        _scratch_put(scr_ref, x_ref[...])
        _store_view(scr_ref, v4_ref, 4)
        _store_view(scr_ref, v16_ref, 16)

    return pl.pallas_call(
        body, name=name, grid=(S // tr,), in_specs=[_row(tr)], out_specs=[_view_spec(tr, 4), _view_spec(tr, 16)],
        out_shape=[_view_shape(S, 4, F32), _view_shape(S, 16, F32)], scratch_shapes=[_token_scratch(tr)],
        compiler_params=_params(1))(x)


def _attn_bwd(qv, bias, d_o, omix, ltot, g, name):
    _, dil = A_CONFIGS[g]
    L = qv.shape[0]
    nb = L // A_BLK
    W = A_HEADS * A_HEAD_DIM

    def body(q_ref, kc_ref, kp_ref, vc_ref, vp_ref, bias_ref, do_ref, om_ref, lt_ref,
             dqkv_ref, db_ref, cq_ref, ck_ref, cv_ref):
        r, b = pl.program_id(0), pl.program_id(1)
        dq_ref, dk_ref, dv_ref = (dqkv_ref.at[:, t * W:(t + 1) * W] for t in range(3))

        @pl.when((r == 0) & (b == 0))
        def _():
            db_ref[...] = jnp.zeros_like(db_ref)

        @pl.when(b == 0)
        def _():
            cq_ref[...] = jnp.zeros_like(cq_ref)
            ck_ref[...] = jnp.zeros_like(ck_ref)
            cv_ref[...] = jnp.zeros_like(cv_ref)

        dq_ref[...] = cq_ref[...]

        @pl.when(b < nb)
        def _():
            mask = _attn_mask(b)
            first = _first_head_lanes()
            for j in range(A_HEADS // 2):
                ps = slice(j * 2 * A_HEAD_DIM, (j + 1) * 2 * A_HEAD_DIM)
                q2 = q_ref[:, ps] * 0.125
                k2 = jnp.concatenate([kp_ref[:, ps], kc_ref[:, ps]], axis=0)
                v2 = jnp.concatenate([vp_ref[:, ps], vc_ref[:, ps]], axis=0)
                do2, om2 = do_ref[:, ps], om_ref[:, ps]
                dq_pair, dk2, dv2 = [], None, None
                for e in range(2):
                    mine = first if e == 0 else ~first
                    h = 2 * j + e
                    qh = jnp.where(mine, q2, jnp.zeros_like(q2))
                    s = lax.dot_general(qh, k2, NT, preferred_element_type=F32) + bias_ref[h]
                    s = jnp.where(mask, s, NEG_INF)
                    wp = jnp.exp(s - lt_ref[:, h * A_HEAD_DIM:h * A_HEAD_DIM + 1])
                    do_h = jnp.where(mine, do2, 0.0)
                    t_h = jnp.sum(do_h * om2, axis=-1, keepdims=True)
                    do_b = do_h.astype(BF16)
                    dp = lax.dot_general(do_b, v2, NT, preferred_element_type=F32)
                    ds = wp * (dp - t_h)
                    db_ref[h] += ds
                    ds_b = ds.astype(BF16)
                    dv_e = lax.dot_general(wp.astype(BF16), do_b, TN, preferred_element_type=F32)
                    dk_e = lax.dot_general(ds_b, qh, TN, preferred_element_type=F32)
                    dv2 = dv_e if dv2 is None else dv2 + dv_e
                    dk2 = dk_e if dk2 is None else dk2 + dk_e
                    dq_pair.append(jnp.dot(ds_b, k2, preferred_element_type=F32))
                cq_ref[:, ps] = (jnp.where(first, dq_pair[0], dq_pair[1]) * 0.125).astype(BF16)
                dk_ref[:, ps] = (ck_ref[:, ps] + dk2[:A_BLK]).astype(BF16)
                dv_ref[:, ps] = (cv_ref[:, ps] + dv2[:A_BLK]).astype(BF16)
                ck_ref[:, ps] = dk2[A_BLK:]
                cv_ref[:, ps] = dv2[A_BLK:]

        @pl.when(b == nb)
        def _():
            dk_ref[...] = ck_ref[...].astype(BF16)
            dv_ref[...] = cv_ref[...].astype(BF16)

    act = pl.BlockSpec((A_BLK, W), lambda r, b: (jnp.minimum(b, nb - 1), r))
    lag = pl.BlockSpec((A_BLK, 3 * W), lambda r, b: (jnp.maximum(b - 1, 0), r))
    full = pl.BlockSpec((A_HEADS, A_BLK, 2 * A_BLK), lambda r, b: (0, 0, 0))
    in_specs = _attn_in_specs(g, dil)(nb) + [pl.BlockSpec((A_HEADS, A_BLK, 2 * A_BLK), lambda r, b: (g, 0, 0)),
                                             act, act, act]
    return pl.pallas_call(
        body, name=name, grid=(dil, nb + 1), in_specs=in_specs, out_specs=[lag, full],
        out_shape=[jax.ShapeDtypeStruct((L, dil * 3 * W), BF16),
                   jax.ShapeDtypeStruct((A_HEADS, A_BLK, 2 * A_BLK), F32)],
        scratch_shapes=[pltpu.VMEM((A_BLK, W), BF16), pltpu.VMEM((A_BLK, W), F32), pltpu.VMEM((A_BLK, W), F32)],
        compiler_params=_params(2))(qv, qv, qv, qv, qv, bias, d_o, omix, ltot)


NT = (((1,), (1,)), ((), ()))
TN = (((0,), (0,)), ((), ()))


def _dot(a, b, dims=(((1,), (0,)), ((), ()))):
    return lax.dot_general(a.astype(BF16), b.astype(BF16), dims, preferred_element_type=F32)


def _gla_gates(glr, wg_ref, bg_ref):
    z = _dot(glr, wg_ref[...]) + bg_ref[...]
    log_sig = -(jnp.maximum(-z, 0.0) + jnp.log1p(jnp.exp(-jnp.abs(z))))
    return z, log_sig / B_TAU


def _gla_chunk(q, k, gk):
    row = lax.broadcasted_iota(jnp.int32, (B_CHUNK, B_CHUNK), 0)
    col = lax.broadcasted_iota(jnp.int32, (B_CHUNK, B_CHUNK), 1)
    causal = row >= col
    bcum = jnp.dot(causal.astype(F32), gk, precision=lax.Precision.HIGHEST, preferred_element_type=F32)
    bl = bcum[B_CHUNK - 1:B_CHUNK, :]
    qt = (q * (B_DK ** -0.5)) * jnp.exp(bcum)
    kt = k * jnp.exp(-bcum)
    kd = k * jnp.exp(bl - bcum)
    a = jnp.where(causal, _dot(qt, kt, NT), 0.0)
    return causal, bcum, bl, qt, kt, kd, a


def _gla_specs(tg):
    q = pl.BlockSpec((tg, B_DK), lambda h, i: (i, h))
    k = pl.BlockSpec((tg, B_DK), lambda h, i: (i, B_HEADS + h))
    v = pl.BlockSpec((tg, B_DV), lambda h, i: (i, B_HEADS + h))
    glr = pl.BlockSpec((tg, 128), lambda h, i: (i, 24))
    wg = pl.BlockSpec((128, B_DK), lambda h, i: (0, h))
    bg = pl.BlockSpec((1, B_DK), lambda h, i: (0, h))
    return [q, k, v, glr, wg, bg]


def _gla_fwd(proj, w_gate, b_gate, name):
    S = proj.shape[0]
    tg = GLA_ROWS
    nc = tg // B_CHUNK

    def body(q_ref, k_ref, v_ref, glr_ref, wg_ref, bg_ref, o_ref, st_ref, state_ref):
        @pl.when(pl.program_id(1) == 0)
        def _():
            state_ref[...] = jnp.zeros_like(state_ref)

        _, gk_all = _gla_gates(glr_ref[...], wg_ref, bg_ref)
        st = state_ref[...]
        for c in range(nc):
            rows = slice(c * B_CHUNK, (c + 1) * B_CHUNK)
            v = v_ref[rows, :]
            _, _, bl, qt, _, kd, a = _gla_chunk(q_ref[rows, :], k_ref[rows, :], gk_all[rows, :])
            o_ref[rows, :] = _dot(a, v) + _dot(qt, st, NT)
            st_ref[c, 0] = st
            st = st * jnp.exp(bl) + _dot(v, kd, TN)
        state_ref[...] = st

    return pl.pallas_call(
        body, name=name, grid=(B_HEADS, S // tg), in_specs=_gla_specs(tg),
        out_specs=[pl.BlockSpec((tg, B_DV), lambda h, i: (i, h)),
                   pl.BlockSpec((nc, 1, B_DV, B_DK), lambda h, i: (i, h, 0, 0))],
        out_shape=[jax.ShapeDtypeStruct((S, B_V), F32),
                   jax.ShapeDtypeStruct((S // B_CHUNK, B_HEADS, B_DV, B_DK), F32)],
        scratch_shapes=[pltpu.VMEM((B_DV, B_DK), F32)], compiler_params=_params(2))(
            proj, proj, proj, proj, w_gate, b_gate)


def _gla_bwd(proj, w_gate, b_gate, states, d_o, name):
    S = proj.shape[0]
    tg = GLA_ROWS
    nc = tg // B_CHUNK
    ni = S // tg

    def rev(spec):
        return pl.BlockSpec(spec.block_shape, lambda h, i, im=spec.index_map: im(h, ni - 1 - i))

    def body(q_ref, k_ref, v_ref, glr_ref, wg_ref, bg_ref, st_ref, do_ref,
             dq_ref, dk_ref, dv_ref, dz_ref, dbg_ref, dstate_ref):
        @pl.when(pl.program_id(1) == 0)
        def _():
            dstate_ref[...] = jnp.zeros_like(dstate_ref)
            dbg_ref[...] = jnp.zeros_like(dbg_ref)

        z_all, gk_all = _gla_gates(glr_ref[...], wg_ref, bg_ref)
        dst = dstate_ref[...]
        for c in range(nc - 1, -1, -1):
            rows = slice(c * B_CHUNK, (c + 1) * B_CHUNK)
            v = v_ref[rows, :]
            d_out = do_ref[rows, :]
            st = st_ref[c, 0]
            causal, bcum, bl, qt, kt, kd, a = _gla_chunk(q_ref[rows, :], k_ref[rows, :], gk_all[rows, :])
            da = jnp.where(causal, _dot(d_out, v, NT), 0.0)
            dv_ref[rows, :] = (_dot(a, d_out, TN) + _dot(kd, dst, NT)).astype(BF16)
            dqt = _dot(da, kt) + _dot(d_out, st)
            dkt = _dot(da, qt, TN)
            dkd = _dot(v, dst)
            dec = jnp.exp(bl)
            ddec = jnp.sum(dst * st, axis=0, keepdims=True)
            dst = dst * dec + _dot(d_out, qt, TN)
            dq_ref[rows, :] = (dqt * jnp.exp(bcum) * (B_DK ** -0.5)).astype(BF16)
            dk_ref[rows, :] = (dkt * jnp.exp(-bcum) + dkd * jnp.exp(bl - bcum)).astype(BF16)
            db = (dqt * qt - dkt * kt) - dkd * kd
            dbl = jnp.sum(dkd * kd, axis=0, keepdims=True) + dec * ddec
            upper = jnp.logical_not(causal) | (lax.broadcasted_iota(jnp.int32, (B_CHUNK, B_CHUNK), 0)
                                               == lax.broadcasted_iota(jnp.int32, (B_CHUNK, B_CHUNK), 1))
            dgk = jnp.dot(upper.astype(F32), db, precision=lax.Precision.HIGHEST, preferred_element_type=F32) + dbl
            dz = dgk * (1.0 / B_TAU) * jax.nn.sigmoid(-z_all[rows, :])
            dz_ref[rows, :] = dz
            dbg_ref[...] += jnp.sum(dz, axis=0, keepdims=True)
        dstate_ref[...] = dst

    qs = pl.BlockSpec((tg, B_DK), lambda h, i: (i, h))
    vs = pl.BlockSpec((tg, B_DV), lambda h, i: (i, h))
    in_specs = [rev(s) if n < 4 else s for n, s in enumerate(_gla_specs(tg))]
    in_specs += [rev(pl.BlockSpec((nc, 1, B_DV, B_DK), lambda h, i: (i, h, 0, 0))), rev(vs)]
    return pl.pallas_call(
        body, name=name, grid=(B_HEADS, ni), in_specs=in_specs,
        out_specs=[rev(qs), rev(qs), rev(vs), rev(qs), pl.BlockSpec((1, B_DK), lambda h, i: (0, h))],
        out_shape=[jax.ShapeDtypeStruct((S, B_QK), BF16), jax.ShapeDtypeStruct((S, B_QK), BF16),
                   jax.ShapeDtypeStruct((S, B_V), BF16), jax.ShapeDtypeStruct((S, B_QK), F32),
                   jax.ShapeDtypeStruct((1, B_QK), F32)],
        scratch_shapes=[pltpu.VMEM((B_DV, B_DK), F32)], compiler_params=_params(2))(
            proj, proj, proj, proj, w_gate, b_gate, states, d_o)


def _gla_out(o, proj, gnorm, name):
    S = o.shape[0]
    tr = ROW_TILE

    def body(o_ref, r_ref, g_ref, y_ref):
        for h in range(B_HEADS):
            hs = slice(h * B_DV, (h + 1) * B_DV)
            oh = o_ref[:, hs]
            rs = lax.rsqrt(jnp.mean(oh * oh, axis=-1, keepdims=True) + EPS)
            y_ref[:, hs] = (((oh * rs) * g_ref[...]) * _silu(r_ref[:, hs])).astype(BF16)

    return pl.pallas_call(
        body, name=name, grid=(S // tr,),
        in_specs=[_row(tr), pl.BlockSpec((tr, B_V), lambda i: (i, 2)), _vec(B_DV)], out_specs=_row(tr),
        out_shape=jax.ShapeDtypeStruct((S, B_V), BF16), compiler_params=_params(1))(o, proj, gnorm)


def _gla_out_bwd(o, proj, gnorm, d_y, name):
    S = o.shape[0]
    tr = ROW_TILE

    def body(o_ref, r_ref, g_ref, dy_ref, do_ref, dr_ref, dg_ref):
        @pl.when(pl.program_id(0) == 0)
        def _():
            dg_ref[...] = jnp.zeros_like(dg_ref)

        for h in range(B_HEADS):
            hs = slice(h * B_DV, (h + 1) * B_DV)
            oh, rv, dyv = o_ref[:, hs], r_ref[:, hs], dy_ref[:, hs]
            rs = lax.rsqrt(jnp.mean(oh * oh, axis=-1, keepdims=True) + EPS)
            xhat = oh * rs
            dr_ref[:, hs] = (dyv * (xhat * g_ref[...]) * _dsilu(rv)).astype(BF16)
            dn = dyv * _silu(rv)
            dg_ref[...] += jnp.sum(dn * xhat, axis=0, keepdims=True)
            dxh = dn * g_ref[...]
            do_ref[:, hs] = rs * (dxh - xhat * jnp.mean(dxh * xhat, axis=-1, keepdims=True))

    return pl.pallas_call(
        body, name=name, grid=(S // tr,),
        in_specs=[_row(tr), pl.BlockSpec((tr, B_V), lambda i: (i, 2)), _vec(B_DV), _row(tr)],
        out_specs=[_row(tr), _row(tr), _vec(B_DV)],
        out_shape=[jax.ShapeDtypeStruct((S, B_V), F32), jax.ShapeDtypeStruct((S, B_V), BF16),
                   jax.ShapeDtypeStruct((1, B_DV), F32)],
        compiler_params=_params(1))(o, proj, gnorm, d_y)


J_SH1, J_SC1, J_G1, J_SH2, J_SC2, J_G2 = range(6)


IN_A_TN = 768
UP_TN = 2 * D_FF // 4
TOKEN_TK = 2048


def _ffn_fwd(h, w, i, tag):
    S = h.shape[0]
    u = _matmul(h, w["w_up"], "nn", F32, f"up{tag}", shape=(S, 2 * D_FF, D_MODEL), tiles=(1024, UP_TN, D_MODEL),
                b_spec=pl.BlockSpec((None, D_MODEL, UP_TN), lambda m, j, k: (j, i, 0)))
    act = _conv_gate(u, w["conv_w"][i], w["conv_b"][i:i + 1], f"conv_gate{tag}")
    f = _matmul(act, w["w_down"], "nn", F32, f"down{tag}", shape=(S, D_MODEL, D_FF), tiles=(1024, D_MODEL, D_FF),
                b_spec=pl.BlockSpec((D_FF, D_MODEL), lambda m, j, k: (i, j)))
    return u, act, f


def _ffn_bwd(dx_out, df, u, act, h, x_in, mod, w, i, tag, prev, branch):
    S = h.shape[0]
    dact = _matmul(df, w["w_down"], "nt", F32, f"down_dx{tag}", shape=(S, D_FF, D_MODEL),
                   tiles=(1024, D_FF // 2, D_MODEL),
                   b_spec=pl.BlockSpec((D_FF // 2, D_MODEL), lambda m, j, k: (2 * i + j, k)))
    d_w_down = _matmul(act, df, "tn", F32, f"down_dw{tag}", shape=(D_FF, D_MODEL, S),
                       tiles=(D_FF // 2, D_MODEL, min(S, TOKEN_TK)),
                       o_spec=pl.BlockSpec((D_FF // 2, D_MODEL), lambda m, j, k: (2 * i + m, j)),
                       o_shape=(2 * D_FF, D_MODEL), prev=None if prev is None else prev["w_down"])
    du_a, du_b, dcwa, dcwb, dcba, dcbb = _conv_gate_bwd(u, dact, w["conv_w"][i], w["conv_b"][i:i + 1],
                                                        f"conv_gate_bwd{tag}")
    dh, d_w_up = None, None if prev is None else prev["w_up"]
    for half, du in enumerate((du_a, du_b)):
        dh = _matmul(du, w["w_up"], "nt", F32, f"up_dx{tag}{'ab'[half]}", shape=(S, D_MODEL, D_FF),
                     tiles=(1024, D_MODEL, UP_TN), add=dh,
                     b_spec=pl.BlockSpec((None, D_MODEL, UP_TN), lambda m, j, k, half=half: (2 * half + k, i, 0)))
        d_w_up = _matmul(h, du, "tn", F32, f"up_dw{tag}{'ab'[half]}", shape=(D_MODEL, D_FF, S),
                         tiles=(D_MODEL, UP_TN, min(S, TOKEN_TK)),
                         o_spec=pl.BlockSpec((None, D_MODEL, UP_TN), lambda m, j, k, half=half: (2 * half + j, i, 0)),
                         o_shape=(4, 2 * D_MODEL, UP_TN), prev=d_w_up)
    res = _norm_mod_bwd(dh, x_in, dx_out, w["norm_ffn"][i:i + 1], mod, J_SC2, f"norm_ffn_bwd{tag}", branch=branch)
    dx_in, dsc2, dsh2, dgam = res[:4]
    grads = dict(w_down=d_w_down, w_up=d_w_up, norm_ffn=dgam,
                 conv_w=jnp.concatenate([dcwa[0:3], dcwb[0:3]], axis=1),
                 conv_b=jnp.concatenate([dcba, dcbb], axis=1))
    return dx_in, res[4:], (dsh2, dsc2), grads


def _local_step(x, tgt, w, mods):
    buckets = jnp.asarray(_bucket_maps())
    S = x.shape[0]

    h1 = _norm_mod(x, w["norm_mix"][0:1], mods[0], J_SC1, J_SH1, "norm_mix0", views=True)
    geo = []
    for g, (_, dil) in enumerate(A_CONFIGS):
        tm = min(1024, S // dil)
        geo.append((dil, tm, S // dil // tm))
    w_cols = [pl.BlockSpec((None, D_MODEL, IN_A_TN), lambda m, j, k, g=g: ((4 * g + j) // 3, 0, (4 * g + j) % 3))
              for g in range(3)]
    qkv = [_matmul(h1[g], w["w_in_a"], "nn", BF16, f"in_a{g}", shape=(S, 3 * D_MODEL, D_MODEL),
                   tiles=(tm, IN_A_TN, D_MODEL),
                   a_spec=pl.BlockSpec((tm, D_MODEL), lambda m, j, k, n=n_i: (m % n, m // n)), b_spec=w_cols[g],
                   o_spec=pl.BlockSpec((tm, IN_A_TN), lambda m, j, k, n=n_i: (m % n, (m // n) * 4 + j)),
                   o_shape=(S // dil, dil * 3 * D_MODEL))
           for g, (dil, tm, n_i) in enumerate(geo)]
    bias = _bias_build(w["rel_bias"], buckets, "bias_build")
    os_, ls_ = zip(*[_attn_fwd(qkv[g], bias, g, f"attn_fwd{g}") for g in range(3)])
    omix, omix4, omix16, ltot, ltot4, ltot16 = _mix_fwd(os_, ls_, "mix_fwd")
    y0 = _matmul(omix, w["w_out_a"], "nn", F32, "out_a")
    x1, h2 = _norm_mod(x, w["norm_ffn"][0:1], mods[0], J_SC2, J_SH2, "norm_ffn0", resid=y0, gate_mod=mods[0],
                       j_gate=J_G1)
    u0, act0, f0 = _ffn_fwd(h2, w, 0, "0")

    x2, h3 = _norm_mod(x1, w["norm_mix"][1:2], mods[1], J_SC1, J_SH1, "norm_mix1", resid=f0, gate_mod=mods[0],
                       j_gate=J_G2)
    proj = _matmul(h3, w["w_in_b"], "nn", F32, "in_b", tiles=(512, B_IN_PAD, D_MODEL))
    o_gla, states = _gla_fwd(proj, w["w_gate_b"], w["b_gate_b"], "gla_fwd")
    on = _gla_out(o_gla, proj, w["gnorm_b"], "gla_out")
    y1 = _matmul(on, w["w_out_b"], "nn", F32, "out_b")
    x3, h4 = _norm_mod(x2, w["norm_ffn"][1:2], mods[1], J_SC2, J_SH2, "norm_ffn1", resid=y1, gate_mod=mods[1],
                       j_gate=J_G1)
    u1, act1, f1 = _ffn_fwd(h4, w, 1, "1")

    dx4, loss, d_norm_final, df1, dg2_1 = _final_loss(x3, f1, mods[1], J_G2, w["norm_final"], tgt, "final_loss")

    dx3, (dy1, dg1_1), (dsh2_1, dsc2_1), g_ffn1 = _ffn_bwd(dx4, df1, u1, act1, h4, x3, mods[1], w, 1, "1", None,
                                                           (y1, mods[1], J_G1))
    d_on = _matmul(dy1, w["w_out_b"], "nt", F32, "out_b_dx")
    d_w_out_b = _matmul(on, dy1, "tn", F32, "out_b_dw")
    d_ogla, d_r, d_gnorm = _gla_out_bwd(o_gla, proj, w["gnorm_b"], d_on, "gla_out_bwd")
    dq, dk, dv, dz, d_b_gate = _gla_bwd(proj, w["w_gate_b"], w["b_gate_b"], states, d_ogla, "gla_bwd")
    d_glr = _matmul(dz, w["w_gate_b"], "nt", BF16, "gate_dx")
    d_w_gate = _matmul(proj[:, 3072:3200], dz, "tn", F32, "gate_dw")
    dproj = jnp.concatenate([dq, dk, dv, d_r, d_glr], axis=1)
    dh3 = _matmul(dproj, w["w_in_b"], "nt", F32, "in_b_dx", tiles=(1024, D_MODEL, B_IN_PAD))
    d_w_in_b = _matmul(h3, dproj, "tn", F32, "in_b_dw")
    dx2, dsc1_1, dsh1_1, d_nmix1, df0, dg2_0 = _norm_mod_bwd(
        dh3, x2, dx3, w["norm_mix"][1:2], mods[1], J_SC1, "norm_mix_bwd1", branch=(f0, mods[0], J_G2))
    dmod1 = jnp.concatenate([dsh1_1, dsc1_1, dg1_1, dsh2_1, dsc2_1, dg2_1], axis=1)

    dx1, (dy0, dg1_0), (dsh2_0, dsc2_0), g_ffn0 = _ffn_bwd(dx2, df0, u0, act0, h2, x1, mods[0], w, 0, "0", g_ffn1,
                                                           (y0, mods[0], J_G1))
    d_omix = _matmul(dy0, w["w_out_a"], "nt", F32, "out_a_dx")
    d_w_out_a = _matmul(omix, dy0, "tn", F32, "out_a_dw")
    d_omix4, d_omix16 = _to_views(d_omix, "d_omix_views")
    d_omix_v, omix_v, ltot_v = (d_omix, d_omix4, d_omix16), (omix, omix4, omix16), (ltot, ltot4, ltot16)
    dqkv, dbs = zip(*[_attn_bwd(qkv[g], bias, d_omix_v[g], omix_v[g], ltot_v[g], g, f"attn_bwd{g}")
                      for g in range(3)])
    d_rel_bias = _bias_bwd(jnp.concatenate(dbs, axis=0), buckets, "bias_bwd")
    dh1, d_w_in_a = [], None
    for g, (dil, tm, n_i) in enumerate(geo):
        dh1.append(_matmul(
            dqkv[g], w["w_in_a"], "nt", F32, f"in_a_dx{g}", shape=(S, D_MODEL, 3 * D_MODEL),
            tiles=(tm, D_MODEL, IN_A_TN),
            a_spec=pl.BlockSpec((tm, IN_A_TN), lambda m, j, k, n=n_i: (m % n, (m // n) * 4 + k)),
            b_spec=pl.BlockSpec((None, D_MODEL, IN_A_TN), lambda m, j, k, g=g: ((4 * g + k) // 3, j, (4 * g + k) % 3)),
            o_spec=pl.BlockSpec((tm, D_MODEL), lambda m, j, k, n=n_i: (m % n, m // n)),
            o_shape=(S // dil, dil * D_MODEL)))
        tk = min(TOKEN_TK, S // dil)
        n_k = S // dil // tk
        d_w_in_a = _matmul(
            h1[g], dqkv[g], "tn", F32, f"in_a_dw{g}", shape=(D_MODEL, 3 * D_MODEL, S), tiles=(D_MODEL, IN_A_TN, tk),
            a_spec=pl.BlockSpec((tk, D_MODEL), lambda m, j, k, n=n_k: (k % n, k // n)),
            b_spec=pl.BlockSpec((tk, IN_A_TN), lambda m, j, k, n=n_k: (k % n, (k // n) * 4 + j)),
            o_spec=pl.BlockSpec((None, D_MODEL, IN_A_TN), lambda m, j, k, g=g: ((4 * g + j) // 3, m, (4 * g + j) % 3)),
            o_shape=(4, D_MODEL, 9 * D_MODEL // 4), prev=d_w_in_a)
    dx0, dsc1_0, dsh1_0, d_nmix0 = _norm_mod_bwd(dh1[0], x, dx1, w["norm_mix"][0:1], mods[0], J_SC1, "norm_mix_bwd0",
                                                 dh_views=dh1[1:])
    dmod0 = jnp.concatenate([dsh1_0, dsc1_0, dg1_0, dsh2_0, dsc2_0, dg2_0], axis=1)

    grads = dict(
        w_in_a=d_w_in_a, w_out_a=d_w_out_a, rel_bias=d_rel_bias, w_in_b=d_w_in_b, w_gate_b=d_w_gate,
        b_gate_b=d_b_gate, gnorm_b=d_gnorm, w_out_b=d_w_out_b,
        norm_mix=jnp.concatenate([d_nmix0, d_nmix1], axis=0),
        norm_ffn=jnp.concatenate([g_ffn0["norm_ffn"], g_ffn1["norm_ffn"]], axis=0),
        b_ada=jnp.concatenate([dmod0, dmod1], axis=0),
        w_up=g_ffn0["w_up"],
        conv_w=jnp.stack([g_ffn0["conv_w"], g_ffn1["conv_w"]]),
        conv_b=jnp.concatenate([g_ffn0["conv_b"], g_ffn1["conv_b"]], axis=0),
        w_down=g_ffn0["w_down"],
        norm_final=d_norm_final)
    return loss, dx0, grads


N_CHIPS = 4
N_DEV = 8
WEIGHTS = ("w_in_a", "w_out_a", "rel_bias", "w_in_b", "w_gate_b", "b_gate_b", "gnorm_b", "w_out_b", "norm_mix",
           "norm_ffn", "w_ada", "b_ada", "w_up", "conv_w", "conv_b", "w_down", "norm_final")
SHARD_AXIS = dict(w_in_a=2, w_out_a=1, w_in_b=2, w_gate_b=2, b_gate_b=1, gnorm_b=1, w_out_b=1, w_ada=2, w_up=2,
                  conv_w=2, w_down=1)
SHARDED = tuple(n for n in WEIGHTS if n in SHARD_AXIS)
REPLICATED = tuple(n for n in WEIGHTS if n not in SHARD_AXIS)
BIG = ("w_in_a", "w_out_a", "w_in_b", "w_out_b", "w_up", "w_down")
SMALL = ("w_gate_b", "b_gate_b", "gnorm_b", "conv_w")
SMALL_FULL = dict(w_gate_b=(1, 16, 512), b_gate_b=(1, 512), gnorm_b=(1, 256), conv_w=(2, 3, 5632))
R_SMALL = 16
R_TINY = 72
LOSS_SLOT = 72960
R_TINY_SHARD = 40
W_IN_B_PAD = 896

COMM_VIEW = dict(
    w_in_a=((4096, 2304), 1024, 512, 512),
    w_out_a=((1024, 1024), 256, 128, 128),
    w_in_b=((4096, W_IN_B_PAD), 1024, 512, 512),
    w_out_b=((1024, 1024), 256, 128, 128),
    w_up=((8192, 1408), 2048, 1024, 1024),
    w_down=((5632, 1024), 704, 2816, 704))


def _pack(arrs, rows):
    flat = jnp.concatenate([a.reshape(-1) for a in arrs])
    return jnp.pad(flat, (0, rows * LANES - flat.shape[0])).reshape(rows, LANES)


def _unpack(flat2d, shapes):
    flat = flat2d.reshape(-1)
    out, off = [], 0
    for shp in shapes:
        n = math.prod(shp)
        out.append(flat[off:off + n].reshape(shp))
        off += n
    return out


def _chip_slice(a, axis, k):
    n = a.shape[axis] // N_CHIPS
    return lax.slice_in_dim(a, k * n, (k + 1) * n, axis=axis)


def _place():
    mx, my, mc = lax.axis_index("x"), lax.axis_index("y"), lax.axis_index("c")
    chips = [(1 - mx, my), (mx, 1 - my), (1 - mx, 1 - my)]
    return mx, my, mc, chips


def _rcopy(src, dst, send_sem, recv_sem, dev):
    return pltpu.make_async_remote_copy(src_ref=src, dst_ref=dst, send_sem=send_sem, recv_sem=recv_sem,
                                        device_id=dev, device_id_type=MESH)


def _comm_call(body, name, ins, out_shapes, n_sems, in_place=False):
    n_in, n_out = len(ins), len(out_shapes)

    def wrapped(*refs):
        body(refs[:n_in], refs[n_in:n_in + n_out], *refs[n_in + n_out:])

    return pl.pallas_call(
        wrapped, name=name, in_specs=[HBM] * n_in, out_specs=[HBM] * n_out, out_shape=out_shapes,
        input_output_aliases={i: i for i in range(n_in)} if in_place else {},
        scratch_shapes=[pltpu.SemaphoreType.DMA((n_sems,)), pltpu.SemaphoreType.DMA((n_sems,))])(*ins)


DMA_CHUNK_BYTES = 2 * 1024 * 1024


def _rows(ref, start, size):
    return ref.at[pl.ds(pl.multiple_of(start, 16), size), :]


def _block(ref, name, k, h):
    _, bk, bh, nr = COMM_VIEW[name]
    return _rows(ref, bk * k + bh * h, nr)


def _chunks(nr, row_bytes):
    n = 1
    while nr % (2 * n) == 0 and (nr // (2 * n)) % 16 == 0 and (nr // n) * row_bytes > DMA_CHUNK_BYTES:
        n *= 2
    return [(i * (nr // n), nr // n) for i in range(n)]


def _gather_big(views, name):
    names = BIG

    def body(x_refs, out_refs, send_sems, recv_sems):
        mx, my, mc, chips = _place()
        chip = 2 * mx + my
        sibling = (mx, my, 1 - mc)
        sends = []
        for a, n in enumerate(names):
            for j, (cx, cy) in enumerate(chips):
                blk = _block(out_refs[a], n, chip, mc)
                cp = _rcopy(blk, blk, send_sems.at[6 * a + j], recv_sems.at[6 * a + j], (cx, cy, mc))
                cp.start()
                sends.append(cp)
        for a, n in enumerate(names):
            for j, (cx, cy) in enumerate(chips):
                blk = _block(out_refs[a], n, 2 * cx + cy, mc)
                _rcopy(blk, blk, send_sems.at[6 * a + j], recv_sems.at[6 * a + j], sibling).wait_recv()
                cp = _rcopy(blk, blk, send_sems.at[6 * a + 3 + j], recv_sems.at[6 * a + 3 + j], sibling)
                cp.start()
                sends.append(cp)
        for a, n in enumerate(names):
            for j, (cx, cy) in enumerate(chips):
                blk = _block(out_refs[a], n, 2 * cx + cy, 1 - mc)
                _rcopy(blk, blk, send_sems.at[6 * a + 3 + j], recv_sems.at[6 * a + 3 + j], sibling).wait_recv()
        for cp in sends:
            cp.wait_send()

    outs = _comm_call(body, name, [views[n] for n in names],
                      [jax.ShapeDtypeStruct(views[n].shape, views[n].dtype) for n in names], 6 * len(names),
                      in_place=True)
    return dict(zip(names, outs))


def _rs_pair_exchange(views, name):
    names = BIG

    def body(g_refs, recv_refs, send_sems, recv_sems):
        mx, my, mc, _ = _place()
        sibling = (mx, my, 1 - mc)
        for a, n in enumerate(names):
            (_, cols), _, _, nr = COMM_VIEW[n]
            for k in range(N_CHIPS):
                src = _block(g_refs[a], n, k, 1 - mc)
                for start, size in _chunks(nr, cols * 4):
                    _rcopy(src.at[pl.ds(start, size), :], recv_refs[a].at[k, pl.ds(start, size), :],
                           send_sems.at[a], recv_sems.at[a], sibling).start()
        for a in range(len(names)):
            _rcopy(recv_refs[a], recv_refs[a], send_sems.at[a], recv_sems.at[a], sibling).wait()

    outs = _comm_call(body, name, [views[n] for n in names],
                      [jax.ShapeDtypeStruct((N_CHIPS, COMM_VIEW[n][3], COMM_VIEW[n][0][1]), F32) for n in names],
                      len(names))
    return dict(zip(names, outs))


def _pair_add(view, recv, c_idx, n, out_dtype, name):
    (_, cols), bk, bh, nr = COMM_VIEW[n]
    tr = _pick(math.gcd(bk, bh, nr), 256, 8)

    def body(c_ref, g_ref, r_ref, o_ref):
        o_ref[...] = (g_ref[...] + r_ref[...]).astype(o_ref.dtype)

    piece = pl.BlockSpec((None, tr, cols), lambda k, i, c_ref: (k, i, 0))
    return pl.pallas_call(
        body, name=name,
        grid_spec=pltpu.PrefetchScalarGridSpec(
            num_scalar_prefetch=1, grid=(N_CHIPS, nr // tr),
            in_specs=[pl.BlockSpec((tr, cols), lambda k, i, c_ref: ((bk * k + bh * c_ref[0]) // tr + i, 0)), piece],
            out_specs=piece),
        out_shape=jax.ShapeDtypeStruct((N_CHIPS, nr, cols), out_dtype), compiler_params=_params(2))(
            c_idx, view, recv)


def _rs_chip_exchange(q, name):
    names = BIG

    def body(q_refs, out_refs, send_sems, recv_sems):
        mx, my, mc, chips = _place()
        chip = 2 * mx + my
        sends = []
        for a in range(len(names)):
            for j, (cx, cy) in enumerate(chips):
                cp = _rcopy(q_refs[a].at[2 * cx + cy], out_refs[a].at[chip], send_sems.at[3 * a + j],
                            recv_sems.at[3 * a + j], (cx, cy, mc))
                cp.start()
                sends.append(cp)
        for a in range(len(names)):
            for j, (cx, cy) in enumerate(chips):
                blk = out_refs[a].at[2 * cx + cy]
                _rcopy(blk, blk, send_sems.at[3 * a + j], recv_sems.at[3 * a + j], (cx, cy, mc)).wait_recv()
        for cp in sends:
            cp.wait_send()

    outs = _comm_call(body, name, [q[n] for n in names],
                      [jax.ShapeDtypeStruct(q[n].shape, q[n].dtype) for n in names], 3 * len(names))
    return dict(zip(names, outs))


def _rs_pair_gather(r, name):
    names = BIG

    def body(r_refs, out_refs, send_sems, recv_sems):
        mx, my, mc, _ = _place()
        sibling = (mx, my, 1 - mc)
        for a, n in enumerate(names):
            (_, cols), _, _, nr = COMM_VIEW[n]
            for start, size in _chunks(nr, cols * 4):
                rows = _rows(out_refs[a], mc * nr + start, size)
                _rcopy(rows, rows, send_sems.at[a], recv_sems.at[a], sibling).start()
        for a, n in enumerate(names):
            nr = COMM_VIEW[n][3]
            _rcopy(_rows(out_refs[a], mc * nr, nr), _rows(out_refs[a], (1 - mc) * nr, nr), send_sems.at[a],
                   recv_sems.at[a], sibling).wait()

    outs = _comm_call(body, name, [r[n] for n in names],
                      [jax.ShapeDtypeStruct(r[n].shape, F32) for n in names], len(names), in_place=True)
    return dict(zip(names, outs))


def _gather8(x, reduce, name):
    rows = x.shape[0]

    def body(x_ref, out_ref, *rest):
        if reduce:
            buf_ref, send_sems, recv_sems = rest
        else:
            (send_sems, recv_sems), buf_ref = rest, out_ref
        mx, my, mc, _ = _place()
        me = 4 * mx + 2 * my + mc
        buf_ref[me] = x_ref[...]
        peers = []
        for j in range(1, N_DEV):
            px = 1 - mx if j & 4 else mx
            py = 1 - my if j & 2 else my
            pc = 1 - mc if j & 1 else mc
            peers.append((px, py, pc))
        sends = [_rcopy(x_ref, buf_ref.at[me], send_sems.at[j], recv_sems.at[j], p) for j, p in enumerate(peers)]
        for cp in sends:
            cp.start()
        for j, (px, py, pc) in enumerate(peers):
            _rcopy(x_ref, buf_ref.at[4 * px + 2 * py + pc], send_sems.at[j], recv_sems.at[j], (px, py, pc)).wait_recv()
        for cp in sends:
            cp.wait_send()
        if reduce:
            acc = buf_ref[0]
            for d in range(1, N_DEV):
                acc = acc + buf_ref[d]
            out_ref[...] = acc

    vmem = pl.BlockSpec(memory_space=pltpu.VMEM)
    sems = [pltpu.SemaphoreType.DMA((N_DEV - 1,)), pltpu.SemaphoreType.DMA((N_DEV - 1,))]
    if reduce:
        out_shape = jax.ShapeDtypeStruct((rows, LANES), F32)
        scratch = [pltpu.VMEM((N_DEV, rows, LANES), F32)] + sems
    else:
        out_shape = jax.ShapeDtypeStruct((N_DEV, rows, LANES), F32)
        scratch = sems
    return pl.pallas_call(body, name=name, in_specs=[vmem], out_specs=vmem, out_shape=out_shape,
                          scratch_shapes=scratch)(x)


def _sum4(p, q, chip, core, name):
    _, nr, cols = p.shape
    tr = _pick(nr, 256, 8)

    def body(chip_ref, core_ref, p0, p1, p2, p3, own, o_ref):
        s = [jnp.where(chip_ref[0] == k, own[...], pk[...]).astype(F32) for k, pk in enumerate((p0, p1, p2, p3))]
        o_ref[...] = ((s[0] + s[1]) + s[2]) + s[3]

    return pl.pallas_call(
        body, name=name,
        grid_spec=pltpu.PrefetchScalarGridSpec(
            num_scalar_prefetch=2, grid=(nr // tr,),
            in_specs=[pl.BlockSpec((None, tr, cols), lambda i, ch, co, k=k: (jnp.where(ch[0] == k, k ^ 1, k), i, 0))
                      for k in range(N_CHIPS)]
            + [pl.BlockSpec((None, tr, cols), lambda i, ch, co: (ch[0], i, 0))],
            out_specs=pl.BlockSpec((tr, cols), lambda i, ch, co: (co[0] * (nr // tr) + i, 0))),
        out_shape=jax.ShapeDtypeStruct((2 * nr, cols), F32), compiler_params=_params(1))(chip, core, p, p, p, p, q)


def _place_shard(shard, chip, n, name):
    (rows, cols), bk, bh, nr = COMM_VIEW[n]
    tr = _pick(math.gcd(bk, bh, nr), 256, 16)

    def body(chip_ref, x_ref, o_ref):
        o_ref[...] = x_ref[...].astype(BF16)

    return pl.pallas_call(
        body, name=name,
        grid_spec=pltpu.PrefetchScalarGridSpec(
            num_scalar_prefetch=1, grid=(2, nr // tr),
            in_specs=[pl.BlockSpec((tr, cols), lambda h, i, ch: (h * (nr // tr) + i, 0))],
            out_specs=pl.BlockSpec((tr, cols), lambda h, i, ch: ((bk * ch[0] + bh * h) // tr + i, 0))),
        out_shape=jax.ShapeDtypeStruct((rows, cols), BF16), compiler_params=_params(2))(chip, shard)


def _adamw(w, g, m, v, name):
    rows, cols = w.shape
    tr = _pick(rows, max(8, (1 << 20) // (4 * cols)), 8)

    def body(w_ref, g_ref, m_ref, v_ref, d_ref, mo_ref, vo_ref):
        gv = g_ref[...]
        mn = ADAM_B1 * m_ref[...] + (1.0 - ADAM_B1) * gv
        vn = ADAM_B2 * v_ref[...] + (1.0 - ADAM_B2) * (gv * gv)
        m_hat = mn / (1.0 - ADAM_B1 ** ADAM_STEP)
        v_hat = vn / (1.0 - ADAM_B2 ** ADAM_STEP)
        d_ref[...] = -ADAM_LR * (m_hat / (jnp.sqrt(v_hat) + ADAM_EPS) + ADAM_WD * w_ref[...])
        mo_ref[...] = mn
        vo_ref[...] = vn

    shape = jax.ShapeDtypeStruct(w.shape, F32)
    return pl.pallas_call(
        body, name=name, grid=(rows // tr,), in_specs=[_row(tr, cols)] * 4, out_specs=[_row(tr, cols)] * 3,
        out_shape=[shape] * 3, compiler_params=_params(1))(w, g, m, v)


W_IN_B_SHARD = 772


def _shard_view(n, a):
    return a.reshape(-1, a.shape[-1])


def _gather_weights(p, chip):
    shards = {n: _shard_view(n, p[n]) for n in BIG}
    shards["w_in_b"] = jnp.pad(shards["w_in_b"], ((0, 0), (0, W_IN_B_PAD - W_IN_B_SHARD)))
    big = _gather_big({n: _place_shard(shards[n], chip, n, f"place_{n}") for n in BIG}, "gather_weights")
    small = _gather8(_pack([p[n] for n in SMALL], R_SMALL), False, "gather_small")
    pieces = [_unpack(small[2 * k], [p[n].shape for n in SMALL]) for k in range(N_CHIPS)]
    full = {n: jnp.concatenate([pieces[k][i] for k in range(N_CHIPS)], axis=SHARD_AXIS[n])
            for i, n in enumerate(SMALL)}
    wb = big["w_in_b"].reshape(N_CHIPS, D_MODEL, W_IN_B_PAD)
    wb = jnp.concatenate([wb[k, :, :W_IN_B_SHARD] for k in range(N_CHIPS)], axis=1)
    return dict(
        w_in_a=big["w_in_a"].reshape(N_CHIPS, D_MODEL, -1), w_out_a=big["w_out_a"], w_out_b=big["w_out_b"],
        w_in_b=jnp.concatenate([wb[:, :2048], wb[:, 2064:3088], wb[:, 2048:2064],
                                jnp.zeros((D_MODEL, B_IN_PAD - 3088), BF16)], axis=1),
        w_up=big["w_up"].reshape(N_CHIPS, 2 * D_MODEL, -1), w_down=big["w_down"],
        w_gate_b=jnp.pad(full["w_gate_b"][0], ((0, 128 - B_GATE_RANK), (0, 0))),
        b_gate_b=full["b_gate_b"], gnorm_b=full["gnorm_b"], conv_w=full["conv_w"],
        rel_bias=p["rel_bias"], norm_mix=p["norm_mix"], norm_ffn=p["norm_ffn"], b_ada=p["b_ada"],
        conv_b=p["conv_b"], norm_final=p["norm_final"].reshape(1, D_MODEL))


def _grad_views(g):
    gb = g["w_in_b"]
    gb = jnp.concatenate([gb[:, :2048], gb[:, 3072:3088], gb[:, 2048:3072]], axis=1)
    gb = jnp.stack([jnp.pad(gb[:, k * W_IN_B_SHARD:(k + 1) * W_IN_B_SHARD], ((0, 0), (0, W_IN_B_PAD - W_IN_B_SHARD)))
                    for k in range(N_CHIPS)])
    views = {n: g[n].reshape(COMM_VIEW[n][0]) for n in BIG if n != "w_in_b"}
    views["w_in_b"] = gb.reshape(COMM_VIEW["w_in_b"][0])
    return views


def _tiny_grads(g):
    out = {n: g[n] for n in REPLICATED if n != "norm_final"}
    out.update(norm_final=g["norm_final"].reshape(D_MODEL), w_gate_b=g["w_gate_b"][:B_GATE_RANK][None],
               b_gate_b=g["b_gate_b"], gnorm_b=g["gnorm_b"], conv_w=g["conv_w"])
    return out


def kernel(x, c, w_in_a, w_out_a, rel_bias, w_in_b, w_gate_b, b_gate_b, gnorm_b, w_out_b, norm_mix, norm_ffn, w_ada, b_ada, w_up, conv_w, conv_b, w_down, norm_final, loss_target, m_w_in_a, m_w_out_a, m_rel_bias, m_w_in_b, m_w_gate_b, m_b_gate_b, m_gnorm_b, m_w_out_b, m_norm_mix, m_norm_ffn, m_w_ada, m_b_ada, m_w_up, m_conv_w, m_conv_b, m_w_down, m_norm_final, v_w_in_a, v_w_out_a, v_rel_bias, v_w_in_b, v_w_gate_b, v_b_gate_b, v_gnorm_b, v_w_out_b, v_norm_mix, v_norm_ffn, v_w_ada, v_b_ada, v_w_up, v_conv_w, v_conv_b, v_w_down, v_norm_final):
    p = dict(zip(WEIGHTS, (w_in_a, w_out_a, rel_bias, w_in_b, w_gate_b, b_gate_b, gnorm_b, w_out_b, norm_mix,
                           norm_ffn, w_ada, b_ada, w_up, conv_w, conv_b, w_down, norm_final)))
    pm = dict(zip(WEIGHTS, (m_w_in_a, m_w_out_a, m_rel_bias, m_w_in_b, m_w_gate_b, m_b_gate_b, m_gnorm_b, m_w_out_b,
                            m_norm_mix, m_norm_ffn, m_w_ada, m_b_ada, m_w_up, m_conv_w, m_conv_b, m_w_down,
                            m_norm_final)))
    pv = dict(zip(WEIGHTS, (v_w_in_a, v_w_out_a, v_rel_bias, v_w_in_b, v_w_gate_b, v_b_gate_b, v_gnorm_b, v_w_out_b,
                            v_norm_mix, v_norm_ffn, v_w_ada, v_b_ada, v_w_up, v_conv_w, v_conv_b, v_w_down,
                            v_norm_final)))
    S = x.shape[1]

    chip = 2 * lax.axis_index("x") + lax.axis_index("y")
    core = lax.axis_index("c").astype(jnp.int32).reshape(1)
    chip_s = chip.astype(jnp.int32).reshape(1)

    me = 2 * chip + lax.axis_index("c")

    def own_columns(a):
        return lax.dynamic_slice_in_dim(a, chip * ADA_TN, ADA_TN, axis=a.ndim - 1)

    c_all = _gather8(jnp.pad(c, ((0, 7), (0, 0))), False, "gather_c")[:, 0, :]
    mod_cols = _ada_mod_shard(c_all, w_ada, own_columns(b_ada)[:, None, :], "ada_mod")
    mod_all = _gather8(mod_cols.reshape(-1, LANES), False, "gather_mod").reshape(N_CHIPS, 2, 2, N_DEV, ADA_TN)
    mod_mine = lax.dynamic_index_in_dim(mod_all[:, 0], me, axis=2, keepdims=False)
    mods = [jnp.broadcast_to(mod_mine[:, l].reshape(1, 6 * D_MODEL), (8, 6 * D_MODEL)) for l in range(2)]

    w = _gather_weights(p, chip_s)
    loss, dx0, grads = _local_step(x.reshape(S, D_MODEL), loss_target.reshape(S, D_MODEL), w, mods)

    dmod_rows = 2 * 6 * D_MODEL // LANES
    dmod_all = _gather8(jnp.pad(grads["b_ada"].reshape(dmod_rows, LANES), ((0, 16 - dmod_rows), (0, 0))), False,
                        "gather_dmod")[:, :dmod_rows].reshape(N_DEV, 2, 6 * D_MODEL)
    g_w_ada = _ada_grad_shard(c_all, own_columns(dmod_all).transpose(1, 0, 2), "ada_grad")

    views = _grad_views(grads)
    recv = _rs_pair_exchange(views, "grads_pair_exchange")
    pair = {n: _pair_add(views[n], recv[n], core, n, BF16, f"grads_pair_add_{n}") for n in BIG}
    from_chips = _rs_chip_exchange(pair, "grads_chip_exchange")
    g_big = _rs_pair_gather({n: _sum4(from_chips[n], pair[n], chip_s, core, f"grads_chip_sum_{n}") for n in BIG},
                            "grads_pair_gather")
    g_big["w_in_b"] = g_big["w_in_b"][:, :W_IN_B_SHARD]

    tiny = _tiny_grads(grads)
    tiny_names = SMALL + REPLICATED
    tiny_full = {n: SMALL_FULL[n] if n in SMALL_FULL else p[n].shape for n in tiny_names}
    tiny_sum = _gather8(_pack([tiny[n] for n in tiny_names] + [loss[0, 0:1]], R_TINY), True, "grads_tiny_sum")
    g_tiny = dict(zip(tiny_names, _unpack(tiny_sum, [tiny_full[n] for n in tiny_names])))
    for n in SMALL:
        width = p[n].shape[SHARD_AXIS[n]]
        g_tiny[n] = lax.dynamic_slice_in_dim(g_tiny[n], chip * width, width, axis=SHARD_AXIS[n])
    total_loss = tiny_sum.reshape(-1)[LOSS_SLOT]

    g_big["w_ada"] = _shard_view("w_ada", g_w_ada)
    out = {}
    for n in BIG + ("w_ada",):
        res = _adamw(_shard_view(n, p[n]), g_big[n], _shard_view(n, pm[n]), _shard_view(n, pv[n]), f"adamw_{n}")
        out[n] = [t.reshape(p[n].shape) for t in (g_big[n],) + tuple(res)]
    res = _adamw(*[_pack([d[n] for n in tiny_names], R_TINY_SHARD) for d in (p, g_tiny, pm, pv)], "adamw_tiny")
    unpacked = [_unpack(t, [p[n].shape for n in tiny_names]) for t in res]
    for i, n in enumerate(tiny_names):
        out[n] = [g_tiny[n]] + [u[i] for u in unpacked]

    return (total_loss, dx0.reshape(x.shape), *[out[n][0] for n in WEIGHTS], *[out[n][1] for n in WEIGHTS],
            *[out[n][2] for n in WEIGHTS], *[out[n][3] for n in WEIGHTS])
```

```python
import functools
import math

import numpy as np
import jax
import jax.numpy as jnp
from jax import lax
from jax.experimental import pallas as pl
from jax.experimental.pallas import tpu as pltpu

F32 = jnp.float32
BF16 = jnp.bfloat16
MESH = pl.DeviceIdType.MESH

D_MODEL = 1024
A_CONFIGS = ((128, 1), (512, 4), (2048, 16))
A_HEADS = 16
A_HEAD_DIM = 64
A_BLK = 128
N_BUCKETS = 32
MAX_DISTANCE = 2048
B_HEADS = 4
B_DK = 128
B_DV = 256
B_QK = 512
B_V = 1024
B_GATE_RANK = 16
B_TAU = 16.0
B_CHUNK = 64
B_IN_PAD = 3200
D_FF = 2816
EPS = 1e-6
NEG_INF = -1e30
ADAM_LR = 0.001
ADAM_B1 = 0.9
ADAM_B2 = 0.999
ADAM_EPS = 1e-08
ADAM_WD = 0.01
ADAM_STEP = 10

LANES = 1024
VMEM_LIMIT = 48 * 1024 * 1024
ROW_TILE = 256
GLA_ROWS = 512

HBM = pl.BlockSpec(memory_space=pl.ANY)


def _params(n_axes):
    return pltpu.CompilerParams(dimension_semantics=("arbitrary",) * n_axes, vmem_limit_bytes=VMEM_LIMIT)


def _pick(n, cap, mult=128):
    best = None
    for t in range(mult, min(n, cap) + 1, mult):
        if n % t == 0:
            best = t
    return n if best is None else best


def _matmul(a, b, mode, out_dtype, name, shape=None, tiles=None, a_spec=None, b_spec=None, o_spec=None, o_shape=None,
            prev=None, add=None):
    dims = {"nn": (((1,), (0,)), ((), ())), "nt": NT, "tn": TN}[mode]
    if shape is None:
        if mode == "nn":
            (M, K), (_, N) = a.shape, b.shape
        elif mode == "nt":
            (M, K), (N, _) = a.shape, b.shape
        else:
            (K, M), (_, N) = a.shape, b.shape
    else:
        M, N, K = shape
    if tiles is None:
        tiles = (_pick(M, 1024, 128 if mode == "tn" else 8), _pick(N, 1536), _pick(K, 1024 if mode != "tn" else 2048))
    tm, tn, tk = tiles
    nk = K // tk
    if a_spec is None:
        a_spec = pl.BlockSpec((tk, tm), lambda i, j, k: (k, i)) if mode == "tn" else pl.BlockSpec(
            (tm, tk), lambda i, j, k: (i, k))
    if b_spec is None:
        b_spec = pl.BlockSpec((tn, tk), lambda i, j, k: (j, k)) if mode == "nt" else pl.BlockSpec(
            (tk, tn), lambda i, j, k: (k, j))
    if o_spec is None:
        o_spec = pl.BlockSpec((tm, tn), lambda i, j, k: (i, j))
        o_shape = (M, N)

    has_add = add is not None

    def body(a_ref, b_ref, *rest):
        part = lax.dot_general(a_ref[...].astype(BF16), b_ref[...].astype(BF16), dims, preferred_element_type=F32)

        def finish(total):
            if has_add:
                total = total + rest[0][...]
            return total.astype(out_dtype)

        if nk == 1:
            rest[-1][...] = finish(part)
            return
        o_ref, acc_ref = rest[-2:]
        k = pl.program_id(2)

        @pl.when(k == 0)
        def _():
            acc_ref[...] = part

        @pl.when(k > 0)
        def _():
            acc_ref[...] += part

        @pl.when(k == nk - 1)
        def _():
            o_ref[...] = finish(acc_ref[...])

    ins, in_specs, aliases = [a, b], [a_spec, b_spec], {}
    if has_add:
        ins.append(add)
        in_specs.append(o_spec)
    if prev is not None:
        aliases = {len(ins): 0}
        ins.append(prev)
        in_specs.append(HBM)
    return pl.pallas_call(
        body, name=name, grid=(M // tm, N // tn, nk), in_specs=in_specs, out_specs=o_spec,
        out_shape=jax.ShapeDtypeStruct(o_shape, out_dtype),
        scratch_shapes=[pltpu.VMEM((tm, tn), F32)] if nk > 1 else [],
        input_output_aliases=aliases, compiler_params=_params(3))(*ins)


def _row(tr, d=D_MODEL):
    return pl.BlockSpec((tr, d), lambda i: (i, 0))


def _vec(d=D_MODEL):
    return pl.BlockSpec((1, d), lambda i: (0, 0))


def _modspec(j):
    return pl.BlockSpec((8, D_MODEL), lambda i: (0, j))


def _silu(x):
    return x * jax.nn.sigmoid(x)


def _dsilu(x):
    s = jax.nn.sigmoid(x)
    return s * (1.0 + x * (1.0 - s))


ADA_TN = 6 * D_MODEL // 4


def _ada_mod_shard(c_all, w_ada, b_ada, name):
    def body(c_ref, w_ref, b_ref, o_ref):
        o_ref[...] = _dot(_silu(c_ref[...]), w_ref[...]) + b_ref[...]

    return pl.pallas_call(
        body, name=name, grid=(2,),
        in_specs=[pl.BlockSpec((N_DEV, D_MODEL), lambda l: (0, 0)),
                  pl.BlockSpec((None, D_MODEL, ADA_TN), lambda l: (l, 0, 0)),
                  pl.BlockSpec((None, 1, ADA_TN), lambda l: (l, 0, 0))],
        out_specs=pl.BlockSpec((None, N_DEV, ADA_TN), lambda l: (l, 0, 0)),
        out_shape=jax.ShapeDtypeStruct((2, N_DEV, ADA_TN), F32), compiler_params=_params(1))(c_all, w_ada, b_ada)


def _ada_grad_shard(c_all, dmod, name):
    def body(c_ref, d_ref, o_ref):
        o_ref[...] = _dot(_silu(c_ref[...]), d_ref[...], TN)

    return pl.pallas_call(
        body, name=name, grid=(2,),
        in_specs=[pl.BlockSpec((N_DEV, D_MODEL), lambda l: (0, 0)),
                  pl.BlockSpec((None, N_DEV, ADA_TN), lambda l: (l, 0, 0))],
        out_specs=pl.BlockSpec((None, D_MODEL, ADA_TN), lambda l: (l, 0, 0)),
        out_shape=jax.ShapeDtypeStruct((2, D_MODEL, ADA_TN), F32), compiler_params=_params(1))(c_all, dmod)


def _view_spec(tr, d, width=D_MODEL):
    return pl.BlockSpec((tr // d, d * width), lambda i: (i, 0))


def _view_shape(S, d, dtype, width=D_MODEL):
    return jax.ShapeDtypeStruct((S // d, d * width), dtype)


LANE_TILE = 128
N_LANE_TILES = D_MODEL // LANE_TILE


def _token_scratch(tr):
    return pltpu.VMEM((N_LANE_TILES, tr, LANE_TILE), F32)


def _scratch_put(scr_ref, val):
    for c in range(N_LANE_TILES):
        scr_ref[c] = val[:, c * LANE_TILE:(c + 1) * LANE_TILE]


def _scratch_get(scr_ref):
    return jnp.concatenate([scr_ref[c] for c in range(N_LANE_TILES)], axis=1)


def _store_view(scr_ref, out_ref, d):
    n = scr_ref.shape[1] // d
    for r in range(d):
        for c in range(N_LANE_TILES):
            lo = r * D_MODEL + c * LANE_TILE
            out_ref[:, lo:lo + LANE_TILE] = scr_ref.at[c][pl.ds(r, n, stride=d), :].astype(out_ref.dtype)


def _load_view(view_ref, scr_ref, d):
    n = scr_ref.shape[1] // d
    for r in range(d):
        for c in range(N_LANE_TILES):
            lo = r * D_MODEL + c * LANE_TILE
            scr_ref.at[c][pl.ds(r, n, stride=d), :] = view_ref[:, lo:lo + LANE_TILE].astype(F32)


def _norm_mod(x, gamma, mod, j_sc, j_sh, name, resid=None, gate_mod=None, j_gate=None, views=False):
    S = x.shape[0]
    tr = ROW_TILE
    has_res = resid is not None

    def body(*refs):
        if has_res:
            x_ref, y_ref, gate_ref, g_ref, sc_ref, sh_ref, xo_ref, h_ref = refs
            xn = x_ref[...] + gate_ref[0:1, :] * y_ref[...]
            xo_ref[...] = xn
        elif views:
            x_ref, g_ref, sc_ref, sh_ref, h_ref, h4_ref, h16_ref, scr_ref = refs
            xn = x_ref[...]
        else:
            x_ref, g_ref, sc_ref, sh_ref, h_ref = refs
            xn = x_ref[...]
        r = lax.rsqrt(jnp.mean(xn * xn, axis=-1, keepdims=True) + EPS)
        n = (xn * r) * g_ref[...]
        h = n * (1.0 + sc_ref[0:1, :]) + sh_ref[0:1, :]
        h_ref[...] = h.astype(BF16)
        if views:
            _scratch_put(scr_ref, h)
            _store_view(scr_ref, h4_ref, 4)
            _store_view(scr_ref, h16_ref, 16)

    scratch = []
    if has_res:
        ins = [x, resid, gate_mod, gamma, mod, mod]
        in_specs = [_row(tr), _row(tr), _modspec(j_gate), _vec(), _modspec(j_sc), _modspec(j_sh)]
        out_specs = [_row(tr), _row(tr)]
        out_shape = [jax.ShapeDtypeStruct((S, D_MODEL), F32), jax.ShapeDtypeStruct((S, D_MODEL), BF16)]
    else:
        ins = [x, gamma, mod, mod]
        in_specs = [_row(tr), _vec(), _modspec(j_sc), _modspec(j_sh)]
        out_specs = _row(tr)
        out_shape = jax.ShapeDtypeStruct((S, D_MODEL), BF16)
        if views:
            out_specs = [_row(tr), _view_spec(tr, 4), _view_spec(tr, 16)]
            out_shape = [out_shape, _view_shape(S, 4, BF16), _view_shape(S, 16, BF16)]
            scratch = [_token_scratch(tr)]
    return pl.pallas_call(body, name=name, grid=(S // tr,), in_specs=in_specs, out_specs=out_specs,
                          out_shape=out_shape, scratch_shapes=scratch, compiler_params=_params(1))(*ins)


def _final_loss(x, resid, mod, j_gate, gamma, tgt, name):
    S = x.shape[0]
    tr = ROW_TILE

    def body(x_ref, y_ref, gate_ref, g_ref, t_ref, dx_ref, loss_ref, dg_ref, dy_ref, dgate_ref):
        @pl.when(pl.program_id(0) == 0)
        def _():
            loss_ref[...] = jnp.zeros_like(loss_ref)
            dg_ref[...] = jnp.zeros_like(dg_ref)
            dgate_ref[...] = jnp.zeros_like(dgate_ref)

        yv = y_ref[...]
        xn = x_ref[...] + gate_ref[0:1, :] * yv
        r = lax.rsqrt(jnp.mean(xn * xn, axis=-1, keepdims=True) + EPS)
        xhat = xn * r
        err = xhat * g_ref[...] - t_ref[...]
        loss_ref[...] += 0.5 * jnp.sum(jnp.mean(err * err, axis=-1, keepdims=True))
        dy = err * (1.0 / D_MODEL)
        dg_ref[...] += jnp.sum(dy * xhat, axis=0, keepdims=True)
        dxh = dy * g_ref[...]
        dx = r * (dxh - xhat * jnp.mean(dxh * xhat, axis=-1, keepdims=True))
        dx_ref[...] = dx
        dy_ref[...] = (gate_ref[0:1, :] * dx).astype(BF16)
        dgate_ref[...] += jnp.sum(dx * yv, axis=0, keepdims=True)

    vec = jax.ShapeDtypeStruct((1, D_MODEL), F32)
    return pl.pallas_call(
        body, name=name, grid=(S // tr,),
        in_specs=[_row(tr), _row(tr), _modspec(j_gate), _vec(), _row(tr)],
        out_specs=[_row(tr), pl.BlockSpec((1, 128), lambda i: (0, 0)), _vec(), _row(tr), _vec()],
        out_shape=[jax.ShapeDtypeStruct((S, D_MODEL), F32), jax.ShapeDtypeStruct((1, 128), F32), vec,
                   jax.ShapeDtypeStruct((S, D_MODEL), BF16), vec],
        compiler_params=_params(1))(x, resid, mod, gamma, tgt)


def _norm_mod_bwd(dh, x, dx_res, gamma, mod, j_sc, name, dh_views=None, branch=None):
    S = x.shape[0]
    tr = ROW_TILE
    n_views = 0 if dh_views is None else 2
    n_branch = 0 if branch is None else 2

    def body(dh_ref, *refs):
        x_ref, dr_ref, g_ref, sc_ref = refs[n_views:n_views + 4]
        dx_ref, dsc_ref, dsh_ref, dg_ref = refs[n_views + 4 + n_branch:n_views + 8 + n_branch]

        @pl.when(pl.program_id(0) == 0)
        def _():
            dsc_ref[...] = jnp.zeros_like(dsc_ref)
            dsh_ref[...] = jnp.zeros_like(dsh_ref)
            dg_ref[...] = jnp.zeros_like(dg_ref)

        xv = x_ref[...]
        dh_v = dh_ref[...]
        if n_views:
            scr_ref = refs[-1]
            for view_ref, d in zip(refs[:2], (4, 16)):
                _load_view(view_ref, scr_ref, d)
                dh_v = dh_v + _scratch_get(scr_ref)
        r = lax.rsqrt(jnp.mean(xv * xv, axis=-1, keepdims=True) + EPS)
        xhat = xv * r
        dsh_ref[...] += jnp.sum(dh_v, axis=0, keepdims=True)
        dsc_ref[...] += jnp.sum(dh_v * (xhat * g_ref[...]), axis=0, keepdims=True)
        dn = dh_v * (1.0 + sc_ref[0:1, :])
        dg_ref[...] += jnp.sum(dn * xhat, axis=0, keepdims=True)
        dxh = dn * g_ref[...]
        dx = dr_ref[...] + r * (dxh - xhat * jnp.mean(dxh * xhat, axis=-1, keepdims=True))
        dx_ref[...] = dx
        if n_branch:
            y_ref, gate_ref = refs[n_views + 4:n_views + 6]
            dy_ref, dgate_ref = refs[n_views + 10:n_views + 12]

            @pl.when(pl.program_id(0) == 0)
            def _():
                dgate_ref[...] = jnp.zeros_like(dgate_ref)

            dy_ref[...] = (gate_ref[0:1, :] * dx).astype(BF16)
            dgate_ref[...] += jnp.sum(dx * y_ref[...], axis=0, keepdims=True)

    vec = jax.ShapeDtypeStruct((1, D_MODEL), F32)
    views = [] if dh_views is None else list(dh_views)
    view_specs = [_view_spec(tr, 4), _view_spec(tr, 16)] if views else []
    ins = [dh, *views, x, dx_res, gamma, mod]
    in_specs = [_row(tr)] + view_specs + [_row(tr), _row(tr), _vec(), _modspec(j_sc)]
    out_specs = [_row(tr), _vec(), _vec(), _vec()]
    out_shape = [jax.ShapeDtypeStruct((S, D_MODEL), F32), vec, vec, vec]
    if branch is not None:
        y, gate_mod, j_gate = branch
        ins += [y, gate_mod]
        in_specs += [_row(tr), _modspec(j_gate)]
        out_specs += [_row(tr), _vec()]
        out_shape += [jax.ShapeDtypeStruct((S, D_MODEL), BF16), vec]
    return pl.pallas_call(
        body, name=name, grid=(S // tr,), in_specs=in_specs, out_specs=out_specs, out_shape=out_shape,
        scratch_shapes=[_token_scratch(tr)] if views else [], compiler_params=_params(1))(*ins)


def _shift_down(u, halo, s):
    r = pltpu.roll(u, s, 0)
    hr = pltpu.roll(halo, s, 0)
    rid = lax.broadcasted_iota(jnp.int32, hr.shape, 0)
    top = jnp.where(rid < s, hr, r[0:8])
    return jnp.concatenate([top, r[8:]], axis=0)


def _conv3(u, halo, w_ref, b_ref):
    u1 = _shift_down(u, halo, 1)
    u2 = _shift_down(u, halo, 2)
    return b_ref[...] + ((w_ref[0:1, :] * u2 + w_ref[1:2, :] * u1) + w_ref[2:3, :] * u), u1, u2


CONV_TC = 1408


def _conv_specs(tr, S):
    nh = D_FF // CONV_TC
    hb = tr // 8

    def cur(off):
        return pl.BlockSpec((tr, CONV_TC), lambda j, i: (i, j + off))

    def halo(off):
        return pl.BlockSpec((8, CONV_TC), lambda j, i: (jnp.maximum(i * hb - 1, 0), j + off))

    def w(off):
        return pl.BlockSpec((3, CONV_TC), lambda j, i: (0, j + off))

    def b(off):
        return pl.BlockSpec((1, CONV_TC), lambda j, i: (0, j + off))

    return nh, cur, halo, w, b


def _conv_gate(u, conv_w, conv_b, name):
    S = u.shape[0]
    tr = ROW_TILE
    nh, cur, halo, w, b = _conv_specs(tr, S)

    def body(ua_ref, ha_ref, ub_ref, hb_ref, wa_ref, wb_ref, ba_ref, bb_ref, o_ref):
        first = pl.program_id(1) == 0
        ha = jnp.where(first, 0.0, ha_ref[...])
        hbv = jnp.where(first, 0.0, hb_ref[...])
        a, _, _ = _conv3(ua_ref[...], ha, wa_ref, ba_ref)
        bb, _, _ = _conv3(ub_ref[...], hbv, wb_ref, bb_ref)
        o_ref[...] = (_silu(a) * bb).astype(BF16)

    return pl.pallas_call(
        body, name=name, grid=(nh, S // tr),
        in_specs=[cur(0), halo(0), cur(nh), halo(nh), w(0), w(nh), b(0), b(nh)],
        out_specs=pl.BlockSpec((tr, CONV_TC), lambda j, i: (i, j)),
        out_shape=jax.ShapeDtypeStruct((S, D_FF), BF16), compiler_params=_params(2))(
            u, u, u, u, conv_w, conv_w, conv_b, conv_b)


def _conv_gate_bwd(u, dact, conv_w, conv_b, name):
    S = u.shape[0]
    tr = ROW_TILE
    nh, cur, halo, w, b = _conv_specs(tr, S)
    hb = tr // 8
    nlast = S // 8 - 1
    nsteps = S // tr

    def after(off):
        return pl.BlockSpec((8, CONV_TC), lambda j, i: (jnp.minimum((i + 1) * hb, nlast), j + off))

    def body(ua_ref, ha_ref, na_ref, ub_ref, hb_ref, nb_ref, wa_ref, wb_ref, ba_ref, bb_ref, da_ref, dn_ref,
             dua_ref, dub_ref, dwa_ref, dwb_ref, dba_ref, dbb_ref):
        first = pl.program_id(1) == 0
        last = pl.program_id(1) == nsteps - 1

        @pl.when(first)
        def _():
            for r in (dwa_ref, dwb_ref, dba_ref, dbb_ref):
                r[...] = jnp.zeros_like(r)

        ha = jnp.where(first, 0.0, ha_ref[...])
        hbv = jnp.where(first, 0.0, hb_ref[...])
        ua = jnp.concatenate([ua_ref[...], na_ref[...]], axis=0)
        ub = jnp.concatenate([ub_ref[...], nb_ref[...]], axis=0)
        a, ua1, ua2 = _conv3(ua, ha, wa_ref, ba_ref)
        bb, ub1, ub2 = _conv3(ub, hbv, wb_ref, bb_ref)
        dact_v = jnp.concatenate([da_ref[...], jnp.where(last, 0.0, dn_ref[...])], axis=0)
        da = dact_v * bb * _dsilu(a)
        db = dact_v * _silu(a)
        n = tr + 8
        for d, x0, x1, x2, w_ref, du_ref, dw_ref, dbias_ref in (
                (da, ua, ua1, ua2, wa_ref, dua_ref, dwa_ref, dba_ref),
                (db, ub, ub1, ub2, wb_ref, dub_ref, dwb_ref, dbb_ref)):
            d1 = pltpu.roll(d, n - 1, 0)
            d2 = pltpu.roll(d, n - 2, 0)
            du_ref[...] = ((w_ref[2:3, :] * d + w_ref[1:2, :] * d1) + w_ref[0:1, :] * d2)[:tr].astype(BF16)
            dt = d[:tr]
            dbias_ref[...] += jnp.sum(dt, axis=0, keepdims=True)
            dw_ref[0:1, :] += jnp.sum(dt * x2[:tr], axis=0, keepdims=True)
            dw_ref[1:2, :] += jnp.sum(dt * x1[:tr], axis=0, keepdims=True)
            dw_ref[2:3, :] += jnp.sum(dt * x0[:tr], axis=0, keepdims=True)

    half = pl.BlockSpec((tr, CONV_TC), lambda j, i: (i, j))
    half_after = pl.BlockSpec((8, CONV_TC), lambda j, i: (jnp.minimum((i + 1) * hb, nlast), j))
    dw = pl.BlockSpec((8, CONV_TC), lambda j, i: (0, j))
    dbs = pl.BlockSpec((1, CONV_TC), lambda j, i: (0, j))
    f = lambda r, c: jax.ShapeDtypeStruct((r, c), F32)
    du = jax.ShapeDtypeStruct((S, D_FF), BF16)
    return pl.pallas_call(
        body, name=name, grid=(nh, nsteps),
        in_specs=[cur(0), halo(0), after(0), cur(nh), halo(nh), after(nh), w(0), w(nh), b(0), b(nh), half,
                  half_after],
        out_specs=[half, half, dw, dw, dbs, dbs],
        out_shape=[du, du, f(8, D_FF), f(8, D_FF), f(1, D_FF), f(1, D_FF)],
        compiler_params=_params(2))(u, u, u, u, u, u, conv_w, conv_w, conv_b, conv_b, dact, dact)


def _bucket_maps():
    qi = np.arange(A_BLK)[:, None]
    ki = np.arange(2 * A_BLK)[None, :]
    steps = np.clip(qi + A_BLK - ki, 0, A_BLK)
    out = []
    max_exact = N_BUCKETS // 2
    for _, dil in A_CONFIGS:
        dist = steps * dil
        n = np.maximum(dist, max_exact).astype(np.float32)
        large = max_exact + (np.log(n / np.float32(max_exact)) / np.float32(math.log(MAX_DISTANCE / max_exact))
                             * np.float32(N_BUCKETS - max_exact)).astype(np.int32)
        large = np.minimum(large, N_BUCKETS - 1)
        out.append(np.where(dist < max_exact, dist, large))
    return np.stack(out).astype(np.int32)


def _bias_build(rel_bias, buckets, name):
    ng = len(A_CONFIGS)

    def body(t_ref, bk_ref, o_ref):
        gh = pl.program_id(0) * A_HEADS + pl.program_id(1)
        bk = bk_ref[0]
        acc = jnp.zeros((A_BLK, 2 * A_BLK), F32)
        for b in range(N_BUCKETS):
            acc = jnp.where(bk == b, t_ref[b, gh], acc)
        o_ref[0] = acc

    return pl.pallas_call(
        body, name=name, grid=(ng, A_HEADS),
        in_specs=[pl.BlockSpec(memory_space=pltpu.SMEM), pl.BlockSpec((1, A_BLK, 2 * A_BLK), lambda g, h: (g, 0, 0))],
        out_specs=pl.BlockSpec((1, A_BLK, 2 * A_BLK), lambda g, h: (g * A_HEADS + h, 0, 0)),
        out_shape=jax.ShapeDtypeStruct((ng * A_HEADS, A_BLK, 2 * A_BLK), F32),
        compiler_params=_params(2))(rel_bias, buckets)


def _bias_bwd(dbias, buckets, name):
    ng = len(A_CONFIGS)

    def body(d_ref, bk_ref, o_ref):
        gh = pl.program_id(0) * A_HEADS + pl.program_id(1)
        bk = bk_ref[0]
        d = d_ref[0]
        for b in range(N_BUCKETS):
            o_ref[b, gh] = jnp.sum(jnp.where(bk == b, d, 0.0))

    return pl.pallas_call(
        body, name=name, grid=(ng, A_HEADS),
        in_specs=[pl.BlockSpec((1, A_BLK, 2 * A_BLK), lambda g, h: (g * A_HEADS + h, 0, 0)),
                  pl.BlockSpec((1, A_BLK, 2 * A_BLK), lambda g, h: (g, 0, 0))],
        out_specs=pl.BlockSpec(memory_space=pltpu.SMEM),
        out_shape=jax.ShapeDtypeStruct((N_BUCKETS, ng * A_HEADS), F32),
        compiler_params=_params(2))(dbias, buckets)


def _attn_mask(b):
    qi = lax.broadcasted_iota(jnp.int32, (A_BLK, 2 * A_BLK), 0)
    ki = lax.broadcasted_iota(jnp.int32, (A_BLK, 2 * A_BLK), 1)
    band = (ki >= qi) & (ki <= qi + A_BLK)
    return band & ((b > 0) | (ki >= A_BLK))


def _first_head_lanes():
    return lax.broadcasted_iota(jnp.int32, (A_BLK, 2 * A_HEAD_DIM), 1) < A_HEAD_DIM


def _attn_in_specs(g, dil):
    W = A_HEADS * A_HEAD_DIM

    def spec(t, prev, nb):
        def im(r, b):
            bb = jnp.minimum(b, nb - 1)
            if prev:
                bb = jnp.maximum(bb - 1, 0)
            return (bb, r * 3 + t)
        return pl.BlockSpec((A_BLK, W), im)

    return lambda nb: [spec(0, False, nb), spec(1, False, nb), spec(1, True, nb), spec(2, False, nb),
                       spec(2, True, nb)]


def _attn_fwd(qv, bias, g, name):
    _, dil = A_CONFIGS[g]
    L = qv.shape[0]
    nb = L // A_BLK
    W = A_HEADS * A_HEAD_DIM

    def body(q_ref, kc_ref, kp_ref, vc_ref, vp_ref, bias_ref, o_ref, l_ref):
        mask = _attn_mask(pl.program_id(1))
        first = _first_head_lanes()
        for j in range(A_HEADS // 2):
            ps = slice(j * 2 * A_HEAD_DIM, (j + 1) * 2 * A_HEAD_DIM)
            q2 = q_ref[:, ps] * 0.125
            k2 = jnp.concatenate([kp_ref[:, ps], kc_ref[:, ps]], axis=0)
            v2 = jnp.concatenate([vp_ref[:, ps], vc_ref[:, ps]], axis=0)
            o_pair, l_pair = [], []
            for e in range(2):
                qh = jnp.where(first if e == 0 else ~first, q2, jnp.zeros_like(q2))
                s = lax.dot_general(qh, k2, NT, preferred_element_type=F32) + bias_ref[2 * j + e]
                s = jnp.where(mask, s, NEG_INF)
                m = jnp.max(s, axis=-1, keepdims=True)
                p = jnp.exp(s - m)
                den = jnp.sum(p, axis=-1, keepdims=True)
                o_pair.append(jnp.dot(p.astype(BF16), v2, preferred_element_type=F32) / den)
                l_pair.append(m + jnp.log(den))
            o_ref[:, ps] = jnp.where(first, o_pair[0], o_pair[1])
            l_ref[:, ps] = jnp.where(first, l_pair[0], l_pair[1])

    out_spec = pl.BlockSpec((A_BLK, W), lambda r, b: (b, r))
    return pl.pallas_call(
        body, name=name, grid=(dil, nb),
        in_specs=_attn_in_specs(g, dil)(nb) + [pl.BlockSpec((A_HEADS, A_BLK, 2 * A_BLK), lambda r, b: (g, 0, 0))],
        out_specs=[out_spec, out_spec],
        out_shape=[jax.ShapeDtypeStruct((L, dil * W), F32)] * 2,
        compiler_params=_params(2))(qv, qv, qv, qv, qv, bias)


def _mix_fwd(os, ls, name):
    S = os[0].shape[0]
    tr = ROW_TILE

    def body(o0, o1, o2, l0, l1, l2, om_ref, om4_ref, om16_ref, lt_ref, lt4_ref, lt16_ref, s_o1, s_o2, s_l1, s_l2):
        for view_ref, scr_ref, d in ((o1, s_o1, 4), (o2, s_o2, 16), (l1, s_l1, 4), (l2, s_l2, 16)):
            _load_view(view_ref, scr_ref, d)
        a, b, c = l0[...], _scratch_get(s_l1), _scratch_get(s_l2)
        m = jnp.maximum(jnp.maximum(a, b), c)
        ea, eb, ec = jnp.exp(a - m), jnp.exp(b - m), jnp.exp(c - m)
        z = (ea + eb) + ec
        om = ((ea / z) * o0[...] + (eb / z) * _scratch_get(s_o1)) + (ec / z) * _scratch_get(s_o2)
        lt = m + jnp.log(z)
        om_ref[...] = om
        lt_ref[...] = lt
        _scratch_put(s_o1, om)
        _scratch_put(s_l1, lt)
        for scr_ref, v4_ref, v16_ref in ((s_o1, om4_ref, om16_ref), (s_l1, lt4_ref, lt16_ref)):
            _store_view(scr_ref, v4_ref, 4)
            _store_view(scr_ref, v16_ref, 16)

    ins = [_row(tr), _view_spec(tr, 4), _view_spec(tr, 16)]
    outs = [jax.ShapeDtypeStruct((S, D_MODEL), F32), _view_shape(S, 4, F32), _view_shape(S, 16, F32)]
    return pl.pallas_call(
        body, name=name, grid=(S // tr,), in_specs=ins * 2, out_specs=ins * 2, out_shape=outs * 2,
        scratch_shapes=[_token_scratch(tr)] * 4, compiler_params=_params(1))(*os, *ls)


def _to_views(x, name):
    S = x.shape[0]
    tr = ROW_TILE

    def body(x_ref, v4_ref, v16_ref, scr_ref):
        _scratch_put(scr_ref, x_ref[...])
        _store_view(scr_ref, v4_ref, 4)
        _store_view(scr_ref, v16_ref, 16)

    return pl.pallas_call(
        body, name=name, grid=(S // tr,), in_specs=[_row(tr)], out_specs=[_view_spec(tr, 4), _view_spec(tr, 16)],
        out_shape=[_view_shape(S, 4, F32), _view_shape(S, 16, F32)], scratch_shapes=[_token_scratch(tr)],
        compiler_params=_params(1))(x)


def _attn_bwd(qv, bias, d_o, omix, ltot, g, name, exchange=None):
    _, dil = A_CONFIGS[g]
    L = qv.shape[0]
    nb = L // A_BLK
    W = A_HEADS * A_HEAD_DIM
    ex_names = () if exchange is None else tuple(exchange)
    n_ex = len(ex_names)

    def body(q_ref, kc_ref, kp_ref, vc_ref, vp_ref, bias_ref, do_ref, om_ref, lt_ref, *rest):
        ex_in, (dqkv_ref, db_ref), ex_out = rest[:n_ex], rest[n_ex:n_ex + 2], rest[n_ex + 2:2 * n_ex + 2]
        cq_ref, ck_ref, cv_ref = rest[2 * n_ex + 2:2 * n_ex + 5]
        r, b = pl.program_id(0), pl.program_id(1)
        dq_ref, dk_ref, dv_ref = (dqkv_ref.at[:, t * W:(t + 1) * W] for t in range(3))

        @pl.when((r == 0) & (b == 0))
        def _():
            db_ref[...] = jnp.zeros_like(db_ref)
            if n_ex:
                _chip_exchange(ex_in, ex_out, rest[-2], rest[-1], True, False)

        @pl.when(b == 0)
        def _():
            cq_ref[...] = jnp.zeros_like(cq_ref)
            ck_ref[...] = jnp.zeros_like(ck_ref)
            cv_ref[...] = jnp.zeros_like(cv_ref)

        dq_ref[...] = cq_ref[...]

        @pl.when(b < nb)
        def _():
            mask = _attn_mask(b)
            first = _first_head_lanes()
            for j in range(A_HEADS // 2):
                ps = slice(j * 2 * A_HEAD_DIM, (j + 1) * 2 * A_HEAD_DIM)
                q2 = q_ref[:, ps] * 0.125
                k2 = jnp.concatenate([kp_ref[:, ps], kc_ref[:, ps]], axis=0)
                v2 = jnp.concatenate([vp_ref[:, ps], vc_ref[:, ps]], axis=0)
                do2, om2 = do_ref[:, ps], om_ref[:, ps]
                dq_pair, dk2, dv2 = [], None, None
                for e in range(2):
                    mine = first if e == 0 else ~first
                    h = 2 * j + e
                    qh = jnp.where(mine, q2, jnp.zeros_like(q2))
                    s = lax.dot_general(qh, k2, NT, preferred_element_type=F32) + bias_ref[h]
                    s = jnp.where(mask, s, NEG_INF)
                    wp = jnp.exp(s - lt_ref[:, h * A_HEAD_DIM:h * A_HEAD_DIM + 1])
                    do_h = jnp.where(mine, do2, 0.0)
                    t_h = jnp.sum(do_h * om2, axis=-1, keepdims=True)
                    do_b = do_h.astype(BF16)
                    dp = lax.dot_general(do_b, v2, NT, preferred_element_type=F32)
                    ds = wp * (dp - t_h)
                    db_ref[h] += ds
                    ds_b = ds.astype(BF16)
                    dv_e = lax.dot_general(wp.astype(BF16), do_b, TN, preferred_element_type=F32)
                    dk_e = lax.dot_general(ds_b, qh, TN, preferred_element_type=F32)
                    dv2 = dv_e if dv2 is None else dv2 + dv_e
                    dk2 = dk_e if dk2 is None else dk2 + dk_e
                    dq_pair.append(jnp.dot(ds_b, k2, preferred_element_type=F32))
                cq_ref[:, ps] = (jnp.where(first, dq_pair[0], dq_pair[1]) * 0.125).astype(BF16)
                dk_ref[:, ps] = (ck_ref[:, ps] + dk2[:A_BLK]).astype(BF16)
                dv_ref[:, ps] = (cv_ref[:, ps] + dv2[:A_BLK]).astype(BF16)
                ck_ref[:, ps] = dk2[A_BLK:]
                cv_ref[:, ps] = dv2[A_BLK:]

        @pl.when(b == nb)
        def _():
            dk_ref[...] = ck_ref[...].astype(BF16)
            dv_ref[...] = cv_ref[...].astype(BF16)

        if n_ex:
            @pl.when((r == dil - 1) & (b == nb))
            def _():
                _chip_exchange(ex_in, ex_out, rest[-2], rest[-1], False, True)

    act = pl.BlockSpec((A_BLK, W), lambda r, b: (jnp.minimum(b, nb - 1), r))
    lag = pl.BlockSpec((A_BLK, 3 * W), lambda r, b: (jnp.maximum(b - 1, 0), r))
    full = pl.BlockSpec((A_HEADS, A_BLK, 2 * A_BLK), lambda r, b: (0, 0, 0))
    in_specs = _attn_in_specs(g, dil)(nb) + [pl.BlockSpec((A_HEADS, A_BLK, 2 * A_BLK), lambda r, b: (g, 0, 0)),
                                             act, act, act]
    ex_arrays = [exchange[n] for n in ex_names]
    scratch = [pltpu.VMEM((A_BLK, W), BF16), pltpu.VMEM((A_BLK, W), F32), pltpu.VMEM((A_BLK, W), F32)]
    if n_ex:
        scratch += [pltpu.SemaphoreType.DMA((3 * n_ex,)), pltpu.SemaphoreType.DMA((3 * n_ex,))]
    res = pl.pallas_call(
        body, name=name, grid=(dil, nb + 1), in_specs=in_specs + [HBM] * n_ex, out_specs=[lag, full] + [HBM] * n_ex,
        out_shape=[jax.ShapeDtypeStruct((L, dil * 3 * W), BF16),
                   jax.ShapeDtypeStruct((A_HEADS, A_BLK, 2 * A_BLK), F32)]
        + [jax.ShapeDtypeStruct(a.shape, a.dtype) for a in ex_arrays],
        scratch_shapes=scratch, compiler_params=_params(2))(qv, qv, qv, qv, qv, bias, d_o, omix, ltot, *ex_arrays)
    if n_ex:
        return res[0], res[1], dict(zip(ex_names, res[2:]))
    return res


NT = (((1,), (1,)), ((), ()))
TN = (((0,), (0,)), ((), ()))


def _dot(a, b, dims=(((1,), (0,)), ((), ()))):
    return lax.dot_general(a.astype(BF16), b.astype(BF16), dims, preferred_element_type=F32)


def _gla_gates(glr, wg_ref, bg_ref):
    z = _dot(glr, wg_ref[...]) + bg_ref[...]
    log_sig = -(jnp.maximum(-z, 0.0) + jnp.log1p(jnp.exp(-jnp.abs(z))))
    return z, log_sig / B_TAU


def _gla_chunk(q, k, gk):
    row = lax.broadcasted_iota(jnp.int32, (B_CHUNK, B_CHUNK), 0)
    col = lax.broadcasted_iota(jnp.int32, (B_CHUNK, B_CHUNK), 1)
    causal = row >= col
    bcum = jnp.dot(causal.astype(F32), gk, precision=lax.Precision.HIGHEST, preferred_element_type=F32)
    bl = bcum[B_CHUNK - 1:B_CHUNK, :]
    qt = (q * (B_DK ** -0.5)) * jnp.exp(bcum)
    kt = k * jnp.exp(-bcum)
    kd = k * jnp.exp(bl - bcum)
    a = jnp.where(causal, _dot(qt, kt, NT), 0.0)
    return causal, bcum, bl, qt, kt, kd, a


def _gla_specs(tg):
    q = pl.BlockSpec((tg, B_DK), lambda h, i: (i, h))
    k = pl.BlockSpec((tg, B_DK), lambda h, i: (i, B_HEADS + h))
    v = pl.BlockSpec((tg, B_DV), lambda h, i: (i, B_HEADS + h))
    glr = pl.BlockSpec((tg, 128), lambda h, i: (i, 24))
    wg = pl.BlockSpec((128, B_DK), lambda h, i: (0, h))
    bg = pl.BlockSpec((1, B_DK), lambda h, i: (0, h))
    return [q, k, v, glr, wg, bg]


def _gla_fwd(proj, w_gate, b_gate, name):
    S = proj.shape[0]
    tg = GLA_ROWS
    nc = tg // B_CHUNK

    def body(q_ref, k_ref, v_ref, glr_ref, wg_ref, bg_ref, o_ref, st_ref, state_ref):
        @pl.when(pl.program_id(1) == 0)
        def _():
            state_ref[...] = jnp.zeros_like(state_ref)

        _, gk_all = _gla_gates(glr_ref[...], wg_ref, bg_ref)
        st = state_ref[...]
        for c in range(nc):
            rows = slice(c * B_CHUNK, (c + 1) * B_CHUNK)
            v = v_ref[rows, :]
            _, _, bl, qt, _, kd, a = _gla_chunk(q_ref[rows, :], k_ref[rows, :], gk_all[rows, :])
            o_ref[rows, :] = _dot(a, v) + _dot(qt, st, NT)
            st_ref[c, 0] = st
            st = st * jnp.exp(bl) + _dot(v, kd, TN)
        state_ref[...] = st

    return pl.pallas_call(
        body, name=name, grid=(B_HEADS, S // tg), in_specs=_gla_specs(tg),
        out_specs=[pl.BlockSpec((tg, B_DV), lambda h, i: (i, h)),
                   pl.BlockSpec((nc, 1, B_DV, B_DK), lambda h, i: (i, h, 0, 0))],
        out_shape=[jax.ShapeDtypeStruct((S, B_V), F32),
                   jax.ShapeDtypeStruct((S // B_CHUNK, B_HEADS, B_DV, B_DK), F32)],
        scratch_shapes=[pltpu.VMEM((B_DV, B_DK), F32)], compiler_params=_params(2))(
            proj, proj, proj, proj, w_gate, b_gate)


def _gla_bwd(proj, w_gate, b_gate, states, d_o, name):
    S = proj.shape[0]
    tg = GLA_ROWS
    nc = tg // B_CHUNK
    ni = S // tg

    def rev(spec):
        return pl.BlockSpec(spec.block_shape, lambda h, i, im=spec.index_map: im(h, ni - 1 - i))

    def body(q_ref, k_ref, v_ref, glr_ref, wg_ref, bg_ref, st_ref, do_ref,
             dq_ref, dk_ref, dv_ref, dz_ref, dbg_ref, dstate_ref):
        @pl.when(pl.program_id(1) == 0)
        def _():
            dstate_ref[...] = jnp.zeros_like(dstate_ref)
            dbg_ref[...] = jnp.zeros_like(dbg_ref)

        z_all, gk_all = _gla_gates(glr_ref[...], wg_ref, bg_ref)
        dst = dstate_ref[...]
        for c in range(nc - 1, -1, -1):
            rows = slice(c * B_CHUNK, (c + 1) * B_CHUNK)
            v = v_ref[rows, :]
            d_out = do_ref[rows, :]
            st = st_ref[c, 0]
            causal, bcum, bl, qt, kt, kd, a = _gla_chunk(q_ref[rows, :], k_ref[rows, :], gk_all[rows, :])
            da = jnp.where(causal, _dot(d_out, v, NT), 0.0)
            dv_ref[rows, :] = (_dot(a, d_out, TN) + _dot(kd, dst, NT)).astype(BF16)
            dqt = _dot(da, kt) + _dot(d_out, st)
            dkt = _dot(da, qt, TN)
            dkd = _dot(v, dst)
            dec = jnp.exp(bl)
            ddec = jnp.sum(dst * st, axis=0, keepdims=True)
            dst = dst * dec + _dot(d_out, qt, TN)
            dq_ref[rows, :] = (dqt * jnp.exp(bcum) * (B_DK ** -0.5)).astype(BF16)
            dk_ref[rows, :] = (dkt * jnp.exp(-bcum) + dkd * jnp.exp(bl - bcum)).astype(BF16)
            db = (dqt * qt - dkt * kt) - dkd * kd
            dbl = jnp.sum(dkd * kd, axis=0, keepdims=True) + dec * ddec
            upper = jnp.logical_not(causal) | (lax.broadcasted_iota(jnp.int32, (B_CHUNK, B_CHUNK), 0)
                                               == lax.broadcasted_iota(jnp.int32, (B_CHUNK, B_CHUNK), 1))
            dgk = jnp.dot(upper.astype(F32), db, precision=lax.Precision.HIGHEST, preferred_element_type=F32) + dbl
            dz = dgk * (1.0 / B_TAU) * jax.nn.sigmoid(-z_all[rows, :])
            dz_ref[rows, :] = dz
            dbg_ref[...] += jnp.sum(dz, axis=0, keepdims=True)
        dstate_ref[...] = dst

    qs = pl.BlockSpec((tg, B_DK), lambda h, i: (i, h))
    vs = pl.BlockSpec((tg, B_DV), lambda h, i: (i, h))
    in_specs = [rev(s) if n < 4 else s for n, s in enumerate(_gla_specs(tg))]
    in_specs += [rev(pl.BlockSpec((nc, 1, B_DV, B_DK), lambda h, i: (i, h, 0, 0))), rev(vs)]
    return pl.pallas_call(
        body, name=name, grid=(B_HEADS, ni), in_specs=in_specs,
        out_specs=[rev(qs), rev(qs), rev(vs), rev(qs), pl.BlockSpec((1, B_DK), lambda h, i: (0, h))],
        out_shape=[jax.ShapeDtypeStruct((S, B_QK), BF16), jax.ShapeDtypeStruct((S, B_QK), BF16),
                   jax.ShapeDtypeStruct((S, B_V), BF16), jax.ShapeDtypeStruct((S, B_QK), F32),
                   jax.ShapeDtypeStruct((1, B_QK), F32)],
        scratch_shapes=[pltpu.VMEM((B_DV, B_DK), F32)], compiler_params=_params(2))(
            proj, proj, proj, proj, w_gate, b_gate, states, d_o)


def _gla_out(o, proj, gnorm, name):
    S = o.shape[0]
    tr = ROW_TILE

    def body(o_ref, r_ref, g_ref, y_ref):
        for h in range(B_HEADS):
            hs = slice(h * B_DV, (h + 1) * B_DV)
            oh = o_ref[:, hs]
            rs = lax.rsqrt(jnp.mean(oh * oh, axis=-1, keepdims=True) + EPS)
            y_ref[:, hs] = (((oh * rs) * g_ref[...]) * _silu(r_ref[:, hs])).astype(BF16)

    return pl.pallas_call(
        body, name=name, grid=(S // tr,),
        in_specs=[_row(tr), pl.BlockSpec((tr, B_V), lambda i: (i, 2)), _vec(B_DV)], out_specs=_row(tr),
        out_shape=jax.ShapeDtypeStruct((S, B_V), BF16), compiler_params=_params(1))(o, proj, gnorm)


def _gla_out_bwd(o, proj, gnorm, d_y, name):
    S = o.shape[0]
    tr = ROW_TILE

    def body(o_ref, r_ref, g_ref, dy_ref, do_ref, dr_ref, dg_ref):
        @pl.when(pl.program_id(0) == 0)
        def _():
            dg_ref[...] = jnp.zeros_like(dg_ref)

        for h in range(B_HEADS):
            hs = slice(h * B_DV, (h + 1) * B_DV)
            oh, rv, dyv = o_ref[:, hs], r_ref[:, hs], dy_ref[:, hs]
            rs = lax.rsqrt(jnp.mean(oh * oh, axis=-1, keepdims=True) + EPS)
            xhat = oh * rs
            dr_ref[:, hs] = (dyv * (xhat * g_ref[...]) * _dsilu(rv)).astype(BF16)
            dn = dyv * _silu(rv)
            dg_ref[...] += jnp.sum(dn * xhat, axis=0, keepdims=True)
            dxh = dn * g_ref[...]
            do_ref[:, hs] = rs * (dxh - xhat * jnp.mean(dxh * xhat, axis=-1, keepdims=True))

    return pl.pallas_call(
        body, name=name, grid=(S // tr,),
        in_specs=[_row(tr), pl.BlockSpec((tr, B_V), lambda i: (i, 2)), _vec(B_DV), _row(tr)],
        out_specs=[_row(tr), _row(tr), _vec(B_DV)],
        out_shape=[jax.ShapeDtypeStruct((S, B_V), F32), jax.ShapeDtypeStruct((S, B_V), BF16),
                   jax.ShapeDtypeStruct((1, B_DV), F32)],
        compiler_params=_params(1))(o, proj, gnorm, d_y)


J_SH1, J_SC1, J_G1, J_SH2, J_SC2, J_G2 = range(6)


IN_A_TN = 768
UP_TN = 2 * D_FF // 4
TOKEN_TK = 2048


def _ffn_fwd(h, w, i, tag):
    S = h.shape[0]
    u = _matmul(h, w["w_up"], "nn", F32, f"up{tag}", shape=(S, 2 * D_FF, D_MODEL), tiles=(1024, UP_TN, D_MODEL),
                b_spec=pl.BlockSpec((None, D_MODEL, UP_TN), lambda m, j, k: (j, i, 0)))
    act = _conv_gate(u, w["conv_w"][i], w["conv_b"][i:i + 1], f"conv_gate{tag}")
    f = _matmul(act, w["w_down"], "nn", F32, f"down{tag}", shape=(S, D_MODEL, D_FF), tiles=(1024, D_MODEL, D_FF),
                b_spec=pl.BlockSpec((D_FF, D_MODEL), lambda m, j, k: (i, j)))
    return u, act, f


def _ffn_bwd(dx_out, df, u, act, h, x_in, mod, w, i, tag, prev, branch):
    S = h.shape[0]
    dact = _matmul(df, w["w_down"], "nt", F32, f"down_dx{tag}", shape=(S, D_FF, D_MODEL),
                   tiles=(1024, D_FF // 2, D_MODEL),
                   b_spec=pl.BlockSpec((D_FF // 2, D_MODEL), lambda m, j, k: (2 * i + j, k)))
    d_w_down = _matmul(act, df, "tn", F32, f"down_dw{tag}", shape=(D_FF, D_MODEL, S),
                       tiles=(D_FF // 2, D_MODEL, min(S, TOKEN_TK)),
                       o_spec=pl.BlockSpec((D_FF // 2, D_MODEL), lambda m, j, k: (2 * i + m, j)),
                       o_shape=(2 * D_FF, D_MODEL), prev=None if prev is None else prev["w_down"])
    du_a, du_b, dcwa, dcwb, dcba, dcbb = _conv_gate_bwd(u, dact, w["conv_w"][i], w["conv_b"][i:i + 1],
                                                        f"conv_gate_bwd{tag}")
    dh, d_w_up = None, None if prev is None else prev["w_up"]
    for half, du in enumerate((du_a, du_b)):
        dh = _matmul(du, w["w_up"], "nt", F32, f"up_dx{tag}{'ab'[half]}", shape=(S, D_MODEL, D_FF),
                     tiles=(1024, D_MODEL, UP_TN), add=dh,
                     b_spec=pl.BlockSpec((None, D_MODEL, UP_TN), lambda m, j, k, half=half: (2 * half + k, i, 0)))
        d_w_up = _matmul(h, du, "tn", F32, f"up_dw{tag}{'ab'[half]}", shape=(D_MODEL, D_FF, S),
                         tiles=(D_MODEL, UP_TN, min(S, TOKEN_TK)),
                         o_spec=pl.BlockSpec((None, D_MODEL, UP_TN), lambda m, j, k, half=half: (2 * half + j, i, 0)),
                         o_shape=(4, 2 * D_MODEL, UP_TN), prev=d_w_up)
    res = _norm_mod_bwd(dh, x_in, dx_out, w["norm_ffn"][i:i + 1], mod, J_SC2, f"norm_ffn_bwd{tag}", branch=branch)
    dx_in, dsc2, dsh2, dgam = res[:4]
    grads = dict(w_down=d_w_down, w_up=d_w_up, norm_ffn=dgam,
                 conv_w=jnp.concatenate([dcwa[0:3], dcwb[0:3]], axis=1),
                 conv_b=jnp.concatenate([dcba, dcbb], axis=1))
    return dx_in, res[4:], (dsh2, dsc2), grads


def _local_step(x, tgt, w, mods, early=None):
    buckets = jnp.asarray(_bucket_maps())
    S = x.shape[0]

    h1 = _norm_mod(x, w["norm_mix"][0:1], mods[0], J_SC1, J_SH1, "norm_mix0", views=True)
    geo = []
    for g, (_, dil) in enumerate(A_CONFIGS):
        tm = min(1024, S // dil)
        geo.append((dil, tm, S // dil // tm))
    w_cols = [pl.BlockSpec((None, D_MODEL, IN_A_TN), lambda m, j, k, g=g: ((4 * g + j) // 3, 0, (4 * g + j) % 3))
              for g in range(3)]
    qkv = [_matmul(h1[g], w["w_in_a"], "nn", BF16, f"in_a{g}", shape=(S, 3 * D_MODEL, D_MODEL),
                   tiles=(tm, IN_A_TN, D_MODEL),
                   a_spec=pl.BlockSpec((tm, D_MODEL), lambda m, j, k, n=n_i: (m % n, m // n)), b_spec=w_cols[g],
                   o_spec=pl.BlockSpec((tm, IN_A_TN), lambda m, j, k, n=n_i: (m % n, (m // n) * 4 + j)),
                   o_shape=(S // dil, dil * 3 * D_MODEL))
           for g, (dil, tm, n_i) in enumerate(geo)]
    bias = _bias_build(w["rel_bias"], buckets, "bias_build")
    os_, ls_ = zip(*[_attn_fwd(qkv[g], bias, g, f"attn_fwd{g}") for g in range(3)])
    omix, omix4, omix16, ltot, ltot4, ltot16 = _mix_fwd(os_, ls_, "mix_fwd")
    y0 = _matmul(omix, w["w_out_a"], "nn", F32, "out_a")
    x1, h2 = _norm_mod(x, w["norm_ffn"][0:1], mods[0], J_SC2, J_SH2, "norm_ffn0", resid=y0, gate_mod=mods[0],
                       j_gate=J_G1)
    u0, act0, f0 = _ffn_fwd(h2, w, 0, "0")

    x2, h3 = _norm_mod(x1, w["norm_mix"][1:2], mods[1], J_SC1, J_SH1, "norm_mix1", resid=f0, gate_mod=mods[0],
                       j_gate=J_G2)
    proj = _matmul(h3, w["w_in_b"], "nn", F32, "in_b", tiles=(512, B_IN_PAD, D_MODEL))
    o_gla, states = _gla_fwd(proj, w["w_gate_b"], w["b_gate_b"], "gla_fwd")
    on = _gla_out(o_gla, proj, w["gnorm_b"], "gla_out")
    y1 = _matmul(on, w["w_out_b"], "nn", F32, "out_b")
    x3, h4 = _norm_mod(x2, w["norm_ffn"][1:2], mods[1], J_SC2, J_SH2, "norm_ffn1", resid=y1, gate_mod=mods[1],
                       j_gate=J_G1)
    u1, act1, f1 = _ffn_fwd(h4, w, 1, "1")

    dx4, loss, d_norm_final, df1, dg2_1 = _final_loss(x3, f1, mods[1], J_G2, w["norm_final"], tgt, "final_loss")

    dx3, (dy1, dg1_1), (dsh2_1, dsc2_1), g_ffn1 = _ffn_bwd(dx4, df1, u1, act1, h4, x3, mods[1], w, 1, "1", None,
                                                           (y1, mods[1], J_G1))
    d_on = _matmul(dy1, w["w_out_b"], "nt", F32, "out_b_dx")
    d_w_out_b = _matmul(on, dy1, "tn", F32, "out_b_dw")
    d_ogla, d_r, d_gnorm = _gla_out_bwd(o_gla, proj, w["gnorm_b"], d_on, "gla_out_bwd")
    dq, dk, dv, dz, d_b_gate = _gla_bwd(proj, w["w_gate_b"], w["b_gate_b"], states, d_ogla, "gla_bwd")
    d_glr = _matmul(dz, w["w_gate_b"], "nt", BF16, "gate_dx")
    d_w_gate = _matmul(proj[:, 3072:3200], dz, "tn", F32, "gate_dw")
    dproj = jnp.concatenate([dq, dk, dv, d_r, d_glr], axis=1)
    dh3 = _matmul(dproj, w["w_in_b"], "nt", F32, "in_b_dx", tiles=(1024, D_MODEL, B_IN_PAD))
    d_w_in_b = _matmul(h3, dproj, "tn", F32, "in_b_dw")
    dx2, dsc1_1, dsh1_1, d_nmix1, df0, dg2_0 = _norm_mod_bwd(
        dh3, x2, dx3, w["norm_mix"][1:2], mods[1], J_SC1, "norm_mix_bwd1", branch=(f0, mods[0], J_G2))
    dmod1 = jnp.concatenate([dsh1_1, dsc1_1, dg1_1, dsh2_1, dsc2_1, dg2_1], axis=1)

    dx1, (dy0, dg1_0), (dsh2_0, dsc2_0), g_ffn0 = _ffn_bwd(dx2, df0, u0, act0, h2, x1, mods[0], w, 0, "0", g_ffn1,
                                                           (y0, mods[0], J_G1))
    d_omix = _matmul(dy0, w["w_out_a"], "nt", F32, "out_a_dx")
    d_w_out_a = _matmul(omix, dy0, "tn", F32, "out_a_dw")
    d_omix4, d_omix16 = _to_views(d_omix, "d_omix_views")
    d_omix_v, omix_v, ltot_v = (d_omix, d_omix4, d_omix16), (omix, omix4, omix16), (ltot, ltot4, ltot16)
    early_sums = None if early is None else early(dict(w_in_b=d_w_in_b, w_out_b=d_w_out_b, w_up=g_ffn0["w_up"],
                                                       w_down=g_ffn0["w_down"]))
    res0 = _attn_bwd(qkv[0], bias, d_omix, omix, ltot, 0, "attn_bwd0", exchange=early_sums)
    early_arrived = None if early is None else res0[2]
    dqkv, dbs = zip(res0[:2], *[_attn_bwd(qkv[g], bias, d_omix_v[g], omix_v[g], ltot_v[g], g, f"attn_bwd{g}")
                                for g in (1, 2)])
    d_rel_bias = _bias_bwd(jnp.concatenate(dbs, axis=0), buckets, "bias_bwd")
    dh1, d_w_in_a = [], None
    for g, (dil, tm, n_i) in enumerate(geo):
        dh1.append(_matmul(
            dqkv[g], w["w_in_a"], "nt", F32, f"in_a_dx{g}", shape=(S, D_MODEL, 3 * D_MODEL),
            tiles=(tm, D_MODEL, IN_A_TN),
            a_spec=pl.BlockSpec((tm, IN_A_TN), lambda m, j, k, n=n_i: (m % n, (m // n) * 4 + k)),
            b_spec=pl.BlockSpec((None, D_MODEL, IN_A_TN), lambda m, j, k, g=g: ((4 * g + k) // 3, j, (4 * g + k) % 3)),
            o_spec=pl.BlockSpec((tm, D_MODEL), lambda m, j, k, n=n_i: (m % n, m // n)),
            o_shape=(S // dil, dil * D_MODEL)))
        tk = min(TOKEN_TK, S // dil)
        n_k = S // dil // tk
        d_w_in_a = _matmul(
            h1[g], dqkv[g], "tn", F32, f"in_a_dw{g}", shape=(D_MODEL, 3 * D_MODEL, S), tiles=(D_MODEL, IN_A_TN, tk),
            a_spec=pl.BlockSpec((tk, D_MODEL), lambda m, j, k, n=n_k: (k % n, k // n)),
            b_spec=pl.BlockSpec((tk, IN_A_TN), lambda m, j, k, n=n_k: (k % n, (k // n) * 4 + j)),
            o_spec=pl.BlockSpec((None, D_MODEL, IN_A_TN), lambda m, j, k, g=g: ((4 * g + j) // 3, m, (4 * g + j) % 3)),
            o_shape=(4, D_MODEL, 9 * D_MODEL // 4), prev=d_w_in_a)
    dx0, dsc1_0, dsh1_0, d_nmix0 = _norm_mod_bwd(dh1[0], x, dx1, w["norm_mix"][0:1], mods[0], J_SC1, "norm_mix_bwd0",
                                                 dh_views=dh1[1:])
    dmod0 = jnp.concatenate([dsh1_0, dsc1_0, dg1_0, dsh2_0, dsc2_0, dg2_0], axis=1)

    grads = dict(
        w_in_a=d_w_in_a, w_out_a=d_w_out_a, rel_bias=d_rel_bias, w_in_b=d_w_in_b, w_gate_b=d_w_gate,
        b_gate_b=d_b_gate, gnorm_b=d_gnorm, w_out_b=d_w_out_b,
        norm_mix=jnp.concatenate([d_nmix0, d_nmix1], axis=0),
        norm_ffn=jnp.concatenate([g_ffn0["norm_ffn"], g_ffn1["norm_ffn"]], axis=0),
        b_ada=jnp.concatenate([dmod0, dmod1], axis=0),
        w_up=g_ffn0["w_up"],
        conv_w=jnp.stack([g_ffn0["conv_w"], g_ffn1["conv_w"]]),
        conv_b=jnp.concatenate([g_ffn0["conv_b"], g_ffn1["conv_b"]], axis=0),
        w_down=g_ffn0["w_down"],
        norm_final=d_norm_final)
    return loss, dx0, grads, early_sums, early_arrived


N_CHIPS = 4
N_DEV = 8
WEIGHTS = ("w_in_a", "w_out_a", "rel_bias", "w_in_b", "w_gate_b", "b_gate_b", "gnorm_b", "w_out_b", "norm_mix",
           "norm_ffn", "w_ada", "b_ada", "w_up", "conv_w", "conv_b", "w_down", "norm_final")
SHARD_AXIS = dict(w_in_a=2, w_out_a=1, w_in_b=2, w_gate_b=2, b_gate_b=1, gnorm_b=1, w_out_b=1, w_ada=2, w_up=2,
                  conv_w=2, w_down=1)
SHARDED = tuple(n for n in WEIGHTS if n in SHARD_AXIS)
REPLICATED = tuple(n for n in WEIGHTS if n not in SHARD_AXIS)
BIG = ("w_in_a", "w_out_a", "w_in_b", "w_out_b", "w_up", "w_down")
SMALL = ("w_gate_b", "b_gate_b", "gnorm_b", "conv_w")
SMALL_FULL = dict(w_gate_b=(1, 16, 512), b_gate_b=(1, 512), gnorm_b=(1, 256), conv_w=(2, 3, 5632))
R_SMALL = 16
R_TINY = 72
LOSS_SLOT = 72960
R_TINY_SHARD = 40
W_IN_B_PAD = 896

COMM_VIEW = dict(
    w_in_a=((4096, 2304), 1024, 512, 512),
    w_out_a=((1024, 1024), 256, 128, 128),
    w_in_b=((4096, W_IN_B_PAD), 1024, 512, 512),
    w_out_b=((1024, 1024), 256, 128, 128),
    w_up=((8192, 1408), 2048, 1024, 1024),
    w_down=((5632, 1024), 704, 2816, 704))


def _pack(arrs, rows):
    flat = jnp.concatenate([a.reshape(-1) for a in arrs])
    return jnp.pad(flat, (0, rows * LANES - flat.shape[0])).reshape(rows, LANES)


def _unpack(flat2d, shapes):
    flat = flat2d.reshape(-1)
    out, off = [], 0
    for shp in shapes:
        n = math.prod(shp)
        out.append(flat[off:off + n].reshape(shp))
        off += n
    return out


def _chip_slice(a, axis, k):
    n = a.shape[axis] // N_CHIPS
    return lax.slice_in_dim(a, k * n, (k + 1) * n, axis=axis)


def _place():
    mx, my, mc = lax.axis_index("x"), lax.axis_index("y"), lax.axis_index("c")
    chips = [(1 - mx, my), (mx, 1 - my), (1 - mx, 1 - my)]
    return mx, my, mc, chips


def _rcopy(src, dst, send_sem, recv_sem, dev):
    return pltpu.make_async_remote_copy(src_ref=src, dst_ref=dst, send_sem=send_sem, recv_sem=recv_sem,
                                        device_id=dev, device_id_type=MESH)


def _comm_call(body, name, ins, out_shapes, n_sems, in_place=False):
    n_in, n_out = len(ins), len(out_shapes)

    def wrapped(*refs):
        body(refs[:n_in], refs[n_in:n_in + n_out], *refs[n_in + n_out:])

    return pl.pallas_call(
        wrapped, name=name, in_specs=[HBM] * n_in, out_specs=[HBM] * n_out, out_shape=out_shapes,
        input_output_aliases={i: i for i in range(n_in)} if in_place else {},
        scratch_shapes=[pltpu.SemaphoreType.DMA((n_sems,)), pltpu.SemaphoreType.DMA((n_sems,))])(*ins)


DMA_CHUNK_BYTES = 2 * 1024 * 1024


def _rows(ref, start, size):
    return ref.at[pl.ds(pl.multiple_of(start, 16), size), :]


def _block(ref, name, k, h):
    _, bk, bh, nr = COMM_VIEW[name]
    return _rows(ref, bk * k + bh * h, nr)


def _chunks(nr, row_bytes):
    n = 1
    while nr % (2 * n) == 0 and (nr // (2 * n)) % 16 == 0 and (nr // n) * row_bytes > DMA_CHUNK_BYTES:
        n *= 2
    return [(i * (nr // n), nr // n) for i in range(n)]


def _gather_big(views, name):
    names = BIG

    def body(x_refs, out_refs, send_sems, recv_sems):
        mx, my, mc, chips = _place()
        chip = 2 * mx + my
        sibling = (mx, my, 1 - mc)
        sends = []
        for a, n in enumerate(names):
            for j, (cx, cy) in enumerate(chips):
                blk = _block(out_refs[a], n, chip, mc)
                cp = _rcopy(blk, blk, send_sems.at[6 * a + j], recv_sems.at[6 * a + j], (cx, cy, mc))
                cp.start()
                sends.append(cp)
        for a, n in enumerate(names):
            for j, (cx, cy) in enumerate(chips):
                blk = _block(out_refs[a], n, 2 * cx + cy, mc)
                _rcopy(blk, blk, send_sems.at[6 * a + j], recv_sems.at[6 * a + j], sibling).wait_recv()
                cp = _rcopy(blk, blk, send_sems.at[6 * a + 3 + j], recv_sems.at[6 * a + 3 + j], sibling)
                cp.start()
                sends.append(cp)
        for a, n in enumerate(names):
            for j, (cx, cy) in enumerate(chips):
                blk = _block(out_refs[a], n, 2 * cx + cy, 1 - mc)
                _rcopy(blk, blk, send_sems.at[6 * a + 3 + j], recv_sems.at[6 * a + 3 + j], sibling).wait_recv()
        for cp in sends:
            cp.wait_send()

    outs = _comm_call(body, name, [views[n] for n in names],
                      [jax.ShapeDtypeStruct(views[n].shape, views[n].dtype) for n in names], 6 * len(names),
                      in_place=True)
    return dict(zip(names, outs))


def _rs_pair_exchange(views, name):
    names = tuple(views)

    def body(g_refs, recv_refs, send_sems, recv_sems):
        mx, my, mc, _ = _place()
        sibling = (mx, my, 1 - mc)
        for a, n in enumerate(names):
            (_, cols), _, _, nr = COMM_VIEW[n]
            for k in range(N_CHIPS):
                src = _block(g_refs[a], n, k, 1 - mc)
                for start, size in _chunks(nr, cols * 4):
                    _rcopy(src.at[pl.ds(start, size), :], recv_refs[a].at[k, pl.ds(start, size), :],
                           send_sems.at[a], recv_sems.at[a], sibling).start()
        for a in range(len(names)):
            _rcopy(recv_refs[a], recv_refs[a], send_sems.at[a], recv_sems.at[a], sibling).wait()

    outs = _comm_call(body, name, [views[n] for n in names],
                      [jax.ShapeDtypeStruct((N_CHIPS, COMM_VIEW[n][3], COMM_VIEW[n][0][1]), F32) for n in names],
                      len(names))
    return dict(zip(names, outs))


def _pair_add(view, recv, c_idx, n, out_dtype, name):
    (_, cols), bk, bh, nr = COMM_VIEW[n]
    tr = _pick(math.gcd(bk, bh, nr), 256, 8)

    def body(c_ref, g_ref, r_ref, o_ref):
        o_ref[...] = (g_ref[...] + r_ref[...]).astype(o_ref.dtype)

    piece = pl.BlockSpec((None, tr, cols), lambda k, i, c_ref: (k, i, 0))
    return pl.pallas_call(
        body, name=name,
        grid_spec=pltpu.PrefetchScalarGridSpec(
            num_scalar_prefetch=1, grid=(N_CHIPS, nr // tr),
            in_specs=[pl.BlockSpec((tr, cols), lambda k, i, c_ref: ((bk * k + bh * c_ref[0]) // tr + i, 0)), piece],
            out_specs=piece),
        out_shape=jax.ShapeDtypeStruct((N_CHIPS, nr, cols), out_dtype), compiler_params=_params(2))(
            c_idx, view, recv)


def _chip_exchange(q_refs, out_refs, send_sems, recv_sems, start, wait):
    mx, my, mc, chips = _place()
    chip = 2 * mx + my
    sends = [_rcopy(q_refs[a].at[2 * cx + cy], out_refs[a].at[chip], send_sems.at[3 * a + j], recv_sems.at[3 * a + j],
                    (cx, cy, mc)) for a in range(len(q_refs)) for j, (cx, cy) in enumerate(chips)]
    if start:
        for cp in sends:
            cp.start()
    if wait:
        for a in range(len(q_refs)):
            for j, (cx, cy) in enumerate(chips):
                blk = out_refs[a].at[2 * cx + cy]
                _rcopy(blk, blk, send_sems.at[3 * a + j], recv_sems.at[3 * a + j], (cx, cy, mc)).wait_recv()
        for cp in sends:
            cp.wait_send()


def _rs_chip_exchange(q, name):
    names = tuple(q)

    def body(q_refs, out_refs, send_sems, recv_sems):
        _chip_exchange(q_refs, out_refs, send_sems, recv_sems, True, True)

    outs = _comm_call(body, name, [q[n] for n in names],
                      [jax.ShapeDtypeStruct(q[n].shape, q[n].dtype) for n in names], 3 * len(names))
    return dict(zip(names, outs))


def _rs_pair_gather(r, name):
    names = tuple(r)

    def body(r_refs, out_refs, send_sems, recv_sems):
        mx, my, mc, _ = _place()
        sibling = (mx, my, 1 - mc)
        for a, n in enumerate(names):
            (_, cols), _, _, nr = COMM_VIEW[n]
            for start, size in _chunks(nr, cols * 4):
                rows = _rows(out_refs[a], mc * nr + start, size)
                _rcopy(rows, rows, send_sems.at[a], recv_sems.at[a], sibling).start()
        for a, n in enumerate(names):
            nr = COMM_VIEW[n][3]
            _rcopy(_rows(out_refs[a], mc * nr, nr), _rows(out_refs[a], (1 - mc) * nr, nr), send_sems.at[a],
                   recv_sems.at[a], sibling).wait()

    outs = _comm_call(body, name, [r[n] for n in names],
                      [jax.ShapeDtypeStruct(r[n].shape, F32) for n in names], len(names), in_place=True)
    return dict(zip(names, outs))


def _gather8(x, reduce, name):
    rows = x.shape[0]

    def body(x_ref, out_ref, *rest):
        if reduce:
            buf_ref, send_sems, recv_sems = rest
        else:
            (send_sems, recv_sems), buf_ref = rest, out_ref
        mx, my, mc, _ = _place()
        me = 4 * mx + 2 * my + mc
        buf_ref[me] = x_ref[...]
        peers = []
        for j in range(1, N_DEV):
            px = 1 - mx if j & 4 else mx
            py = 1 - my if j & 2 else my
            pc = 1 - mc if j & 1 else mc
            peers.append((px, py, pc))
        sends = [_rcopy(x_ref, buf_ref.at[me], send_sems.at[j], recv_sems.at[j], p) for j, p in enumerate(peers)]
        for cp in sends:
            cp.start()
        for j, (px, py, pc) in enumerate(peers):
            _rcopy(x_ref, buf_ref.at[4 * px + 2 * py + pc], send_sems.at[j], recv_sems.at[j], (px, py, pc)).wait_recv()
        for cp in sends:
            cp.wait_send()
        if reduce:
            acc = buf_ref[0]
            for d in range(1, N_DEV):
                acc = acc + buf_ref[d]
            out_ref[...] = acc

    vmem = pl.BlockSpec(memory_space=pltpu.VMEM)
    sems = [pltpu.SemaphoreType.DMA((N_DEV - 1,)), pltpu.SemaphoreType.DMA((N_DEV - 1,))]
    if reduce:
        out_shape = jax.ShapeDtypeStruct((rows, LANES), F32)
        scratch = [pltpu.VMEM((N_DEV, rows, LANES), F32)] + sems
    else:
        out_shape = jax.ShapeDtypeStruct((N_DEV, rows, LANES), F32)
        scratch = sems
    return pl.pallas_call(body, name=name, in_specs=[vmem], out_specs=vmem, out_shape=out_shape,
                          scratch_shapes=scratch)(x)


def _sum4(p, q, chip, core, name):
    _, nr, cols = p.shape
    tr = _pick(nr, 256, 8)

    def body(chip_ref, core_ref, p0, p1, p2, p3, own, o_ref):
        s = [jnp.where(chip_ref[0] == k, own[...], pk[...]).astype(F32) for k, pk in enumerate((p0, p1, p2, p3))]
        o_ref[...] = ((s[0] + s[1]) + s[2]) + s[3]

    return pl.pallas_call(
        body, name=name,
        grid_spec=pltpu.PrefetchScalarGridSpec(
            num_scalar_prefetch=2, grid=(nr // tr,),
            in_specs=[pl.BlockSpec((None, tr, cols), lambda i, ch, co, k=k: (jnp.where(ch[0] == k, k ^ 1, k), i, 0))
                      for k in range(N_CHIPS)]
            + [pl.BlockSpec((None, tr, cols), lambda i, ch, co: (ch[0], i, 0))],
            out_specs=pl.BlockSpec((tr, cols), lambda i, ch, co: (co[0] * (nr // tr) + i, 0))),
        out_shape=jax.ShapeDtypeStruct((2 * nr, cols), F32), compiler_params=_params(1))(chip, core, p, p, p, p, q)


def _place_shard(shard, chip, n, name):
    (rows, cols), bk, bh, nr = COMM_VIEW[n]
    tr = _pick(math.gcd(bk, bh, nr), 256, 16)

    def body(chip_ref, x_ref, o_ref):
        o_ref[...] = x_ref[...].astype(BF16)

    return pl.pallas_call(
        body, name=name,
        grid_spec=pltpu.PrefetchScalarGridSpec(
            num_scalar_prefetch=1, grid=(2, nr // tr),
            in_specs=[pl.BlockSpec((tr, cols), lambda h, i, ch: (h * (nr // tr) + i, 0))],
            out_specs=pl.BlockSpec((tr, cols), lambda h, i, ch: ((bk * ch[0] + bh * h) // tr + i, 0))),
        out_shape=jax.ShapeDtypeStruct((rows, cols), BF16), compiler_params=_params(2))(chip, shard)


def _adamw(w, g, m, v, name):
    rows, cols = w.shape
    tr = _pick(rows, max(8, (1 << 20) // (4 * cols)), 8)

    def body(w_ref, g_ref, m_ref, v_ref, d_ref, mo_ref, vo_ref):
        gv = g_ref[...]
        mn = ADAM_B1 * m_ref[...] + (1.0 - ADAM_B1) * gv
        vn = ADAM_B2 * v_ref[...] + (1.0 - ADAM_B2) * (gv * gv)
        m_hat = mn / (1.0 - ADAM_B1 ** ADAM_STEP)
        v_hat = vn / (1.0 - ADAM_B2 ** ADAM_STEP)
        d_ref[...] = -ADAM_LR * (m_hat / (jnp.sqrt(v_hat) + ADAM_EPS) + ADAM_WD * w_ref[...])
        mo_ref[...] = mn
        vo_ref[...] = vn

    shape = jax.ShapeDtypeStruct(w.shape, F32)
    return pl.pallas_call(
        body, name=name, grid=(rows // tr,), in_specs=[_row(tr, cols)] * 4, out_specs=[_row(tr, cols)] * 3,
        out_shape=[shape] * 3, compiler_params=_params(1))(w, g, m, v)


W_IN_B_SHARD = 772


def _shard_view(n, a):
    return a.reshape(-1, a.shape[-1])


def _gather_weights(p, chip):
    shards = {n: _shard_view(n, p[n]) for n in BIG}
    shards["w_in_b"] = jnp.pad(shards["w_in_b"], ((0, 0), (0, W_IN_B_PAD - W_IN_B_SHARD)))
    big = _gather_big({n: _place_shard(shards[n], chip, n, f"place_{n}") for n in BIG}, "gather_weights")
    small = _gather8(_pack([p[n] for n in SMALL], R_SMALL), False, "gather_small")
    pieces = [_unpack(small[2 * k], [p[n].shape for n in SMALL]) for k in range(N_CHIPS)]
    full = {n: jnp.concatenate([pieces[k][i] for k in range(N_CHIPS)], axis=SHARD_AXIS[n])
            for i, n in enumerate(SMALL)}
    wb = big["w_in_b"].reshape(N_CHIPS, D_MODEL, W_IN_B_PAD)
    wb = jnp.concatenate([wb[k, :, :W_IN_B_SHARD] for k in range(N_CHIPS)], axis=1)
    return dict(
        w_in_a=big["w_in_a"].reshape(N_CHIPS, D_MODEL, -1), w_out_a=big["w_out_a"], w_out_b=big["w_out_b"],
        w_in_b=jnp.concatenate([wb[:, :2048], wb[:, 2064:3088], wb[:, 2048:2064],
                                jnp.zeros((D_MODEL, B_IN_PAD - 3088), BF16)], axis=1),
        w_up=big["w_up"].reshape(N_CHIPS, 2 * D_MODEL, -1), w_down=big["w_down"],
        w_gate_b=jnp.pad(full["w_gate_b"][0], ((0, 128 - B_GATE_RANK), (0, 0))),
        b_gate_b=full["b_gate_b"], gnorm_b=full["gnorm_b"], conv_w=full["conv_w"],
        rel_bias=p["rel_bias"], norm_mix=p["norm_mix"], norm_ffn=p["norm_ffn"], b_ada=p["b_ada"],
        conv_b=p["conv_b"], norm_final=p["norm_final"].reshape(1, D_MODEL))


EARLY = ("w_in_b", "w_out_b", "w_up", "w_down")
LATE = ("w_in_a", "w_out_a")


def _grad_views(g, names):
    views = {n: g[n].reshape(COMM_VIEW[n][0]) for n in names if n != "w_in_b"}
    if "w_in_b" in names:
        gb = g["w_in_b"]
        gb = jnp.concatenate([gb[:, :2048], gb[:, 3072:3088], gb[:, 2048:3072]], axis=1)
        gb = jnp.stack([jnp.pad(gb[:, k * W_IN_B_SHARD:(k + 1) * W_IN_B_SHARD],
                                ((0, 0), (0, W_IN_B_PAD - W_IN_B_SHARD))) for k in range(N_CHIPS)])
        views["w_in_b"] = gb.reshape(COMM_VIEW["w_in_b"][0])
    return views


def _tiny_grads(g):
    out = {n: g[n] for n in REPLICATED if n != "norm_final"}
    out.update(norm_final=g["norm_final"].reshape(D_MODEL), w_gate_b=g["w_gate_b"][:B_GATE_RANK][None],
               b_gate_b=g["b_gate_b"], gnorm_b=g["gnorm_b"], conv_w=g["conv_w"])
    return out


def kernel(x, c, w_in_a, w_out_a, rel_bias, w_in_b, w_gate_b, b_gate_b, gnorm_b, w_out_b, norm_mix, norm_ffn, w_ada, b_ada, w_up, conv_w, conv_b, w_down, norm_final, loss_target, m_w_in_a, m_w_out_a, m_rel_bias, m_w_in_b, m_w_gate_b, m_b_gate_b, m_gnorm_b, m_w_out_b, m_norm_mix, m_norm_ffn, m_w_ada, m_b_ada, m_w_up, m_conv_w, m_conv_b, m_w_down, m_norm_final, v_w_in_a, v_w_out_a, v_rel_bias, v_w_in_b, v_w_gate_b, v_b_gate_b, v_gnorm_b, v_w_out_b, v_norm_mix, v_norm_ffn, v_w_ada, v_b_ada, v_w_up, v_conv_w, v_conv_b, v_w_down, v_norm_final):
    p = dict(zip(WEIGHTS, (w_in_a, w_out_a, rel_bias, w_in_b, w_gate_b, b_gate_b, gnorm_b, w_out_b, norm_mix,
                           norm_ffn, w_ada, b_ada, w_up, conv_w, conv_b, w_down, norm_final)))
    pm = dict(zip(WEIGHTS, (m_w_in_a, m_w_out_a, m_rel_bias, m_w_in_b, m_w_gate_b, m_b_gate_b, m_gnorm_b, m_w_out_b,
                            m_norm_mix, m_norm_ffn, m_w_ada, m_b_ada, m_w_up, m_conv_w, m_conv_b, m_w_down,
                            m_norm_final)))
    pv = dict(zip(WEIGHTS, (v_w_in_a, v_w_out_a, v_rel_bias, v_w_in_b, v_w_gate_b, v_b_gate_b, v_gnorm_b, v_w_out_b,
                            v_norm_mix, v_norm_ffn, v_w_ada, v_b_ada, v_w_up, v_conv_w, v_conv_b, v_w_down,
                            v_norm_final)))
    S = x.shape[1]

    chip = 2 * lax.axis_index("x") + lax.axis_index("y")
    core = lax.axis_index("c").astype(jnp.int32).reshape(1)
    chip_s = chip.astype(jnp.int32).reshape(1)

    me = 2 * chip + lax.axis_index("c")

    def own_columns(a):
        return lax.dynamic_slice_in_dim(a, chip * ADA_TN, ADA_TN, axis=a.ndim - 1)

    c_all = _gather8(jnp.pad(c, ((0, 7), (0, 0))), False, "gather_c")[:, 0, :]
    mod_cols = _ada_mod_shard(c_all, w_ada, own_columns(b_ada)[:, None, :], "ada_mod")
    mod_all = _gather8(mod_cols.reshape(-1, LANES), False, "gather_mod").reshape(N_CHIPS, 2, 2, N_DEV, ADA_TN)
    mod_mine = lax.dynamic_index_in_dim(mod_all[:, 0], me, axis=2, keepdims=False)
    mods = [jnp.broadcast_to(mod_mine[:, l].reshape(1, 6 * D_MODEL), (8, 6 * D_MODEL)) for l in range(2)]

    w = _gather_weights(p, chip_s)
    def pair_sums(g, names, tag):
        views = _grad_views(g, names)
        recv = _rs_pair_exchange(views, f"grads_pair_exchange{tag}")
        return {n: _pair_add(views[n], recv[n], core, n, BF16, f"grads_pair_add_{n}") for n in names}

    loss, dx0, grads, pair, from_chips = _local_step(
        x.reshape(S, D_MODEL), loss_target.reshape(S, D_MODEL), w, mods, early=lambda g: pair_sums(g, EARLY, "_early"))

    dmod_rows = 2 * 6 * D_MODEL // LANES
    dmod_all = _gather8(jnp.pad(grads["b_ada"].reshape(dmod_rows, LANES), ((0, 16 - dmod_rows), (0, 0))), False,
                        "gather_dmod")[:, :dmod_rows].reshape(N_DEV, 2, 6 * D_MODEL)
    g_w_ada = _ada_grad_shard(c_all, own_columns(dmod_all).transpose(1, 0, 2), "ada_grad")

    late = pair_sums(grads, LATE, "")
    pair.update(late)
    from_chips.update(_rs_chip_exchange(late, "grads_chip_exchange"))
    g_big = _rs_pair_gather({n: _sum4(from_chips[n], pair[n], chip_s, core, f"grads_chip_sum_{n}") for n in BIG},
                            "grads_pair_gather")
    g_big["w_in_b"] = g_big["w_in_b"][:, :W_IN_B_SHARD]

    tiny = _tiny_grads(grads)
    tiny_names = SMALL + REPLICATED
    tiny_full = {n: SMALL_FULL[n] if n in SMALL_FULL else p[n].shape for n in tiny_names}
    tiny_sum = _gather8(_pack([tiny[n] for n in tiny_names] + [loss[0, 0:1]], R_TINY), True, "grads_tiny_sum")
    g_tiny = dict(zip(tiny_names, _unpack(tiny_sum, [tiny_full[n] for n in tiny_names])))
    for n in SMALL:
        width = p[n].shape[SHARD_AXIS[n]]
        g_tiny[n] = lax.dynamic_slice_in_dim(g_tiny[n], chip * width, width, axis=SHARD_AXIS[n])
    total_loss = tiny_sum.reshape(-1)[LOSS_SLOT]

    g_big["w_ada"] = _shard_view("w_ada", g_w_ada)
    out = {}
    for n in BIG + ("w_ada",):
        res = _adamw(_shard_view(n, p[n]), g_big[n], _shard_view(n, pm[n]), _shard_view(n, pv[n]), f"adamw_{n}")
        out[n] = [t.reshape(p[n].shape) for t in (g_big[n],) + tuple(res)]
    res = _adamw(*[_pack([d[n] for n in tiny_names], R_TINY_SHARD) for d in (p, g_tiny, pm, pv)], "adamw_tiny")
    unpacked = [_unpack(t, [p[n].shape for n in tiny_names]) for t in res]
    for i, n in enumerate(tiny_names):
        out[n] = [g_tiny[n]] + [u[i] for u in unpacked]

    return (total_loss, dx0.reshape(x.shape), *[out[n][0] for n in WEIGHTS], *[out[n][1] for n in WEIGHTS],
            *[out[n][2] for n in WEIGHTS], *[out[n][3] for n in WEIGHTS])
```

```python
import functools
import math

import numpy as np
import jax
import jax.numpy as jnp
from jax import lax
from jax.experimental import pallas as pl
from jax.experimental.pallas import tpu as pltpu

F32 = jnp.float32
BF16 = jnp.bfloat16
MESH = pl.DeviceIdType.MESH

D_MODEL = 1024
A_CONFIGS = ((128, 1), (512, 4), (2048, 16))
A_HEADS = 16
A_HEAD_DIM = 64
A_BLK = 128
N_BUCKETS = 32
MAX_DISTANCE = 2048
B_HEADS = 4
B_DK = 128
B_DV = 256
B_QK = 512
B_V = 1024
B_GATE_RANK = 16
B_TAU = 16.0
B_CHUNK = 64
B_IN_PAD = 3200
D_FF = 2816
EPS = 1e-6
NEG_INF = -1e30
ADAM_LR = 0.001
ADAM_B1 = 0.9
ADAM_B2 = 0.999
ADAM_EPS = 1e-08
ADAM_WD = 0.01
ADAM_STEP = 10

LANES = 1024
VMEM_LIMIT = 48 * 1024 * 1024
ROW_TILE = 256
GLA_ROWS = 512

HBM = pl.BlockSpec(memory_space=pl.ANY)


def _params(n_axes):
    return pltpu.CompilerParams(dimension_semantics=("arbitrary",) * n_axes, vmem_limit_bytes=VMEM_LIMIT)


def _pick(n, cap, mult=128):
    best = None
    for t in range(mult, min(n, cap) + 1, mult):
        if n % t == 0:
            best = t
    return n if best is None else best


def _matmul(a, b, mode, out_dtype, name, shape=None, tiles=None, a_spec=None, b_spec=None, o_spec=None, o_shape=None,
            prev=None, add=None):
    dims = {"nn": (((1,), (0,)), ((), ())), "nt": NT, "tn": TN}[mode]
    if shape is None:
        if mode == "nn":
            (M, K), (_, N) = a.shape, b.shape
        elif mode == "nt":
            (M, K), (N, _) = a.shape, b.shape
        else:
            (K, M), (_, N) = a.shape, b.shape
    else:
        M, N, K = shape
    if tiles is None:
        tiles = (_pick(M, 1024, 128 if mode == "tn" else 8), _pick(N, 1536), _pick(K, 1024 if mode != "tn" else 2048))
    tm, tn, tk = tiles
    nk = K // tk
    if a_spec is None:
        a_spec = pl.BlockSpec((tk, tm), lambda i, j, k: (k, i)) if mode == "tn" else pl.BlockSpec(
            (tm, tk), lambda i, j, k: (i, k))
    if b_spec is None:
        b_spec = pl.BlockSpec((tn, tk), lambda i, j, k: (j, k)) if mode == "nt" else pl.BlockSpec(
            (tk, tn), lambda i, j, k: (k, j))
    if o_spec is None:
        o_spec = pl.BlockSpec((tm, tn), lambda i, j, k: (i, j))
        o_shape = (M, N)

    has_add = add is not None

    def body(a_ref, b_ref, *rest):
        part = lax.dot_general(a_ref[...].astype(BF16), b_ref[...].astype(BF16), dims, preferred_element_type=F32)

        def finish(total):
            if has_add:
                total = total + rest[0][...]
            return total.astype(out_dtype)

        if nk == 1:
            rest[-1][...] = finish(part)
            return
        o_ref, acc_ref = rest[-2:]
        k = pl.program_id(2)

        @pl.when(k == 0)
        def _():
            acc_ref[...] = part

        @pl.when(k > 0)
        def _():
            acc_ref[...] += part

        @pl.when(k == nk - 1)
        def _():
            o_ref[...] = finish(acc_ref[...])

    ins, in_specs, aliases = [a, b], [a_spec, b_spec], {}
    if has_add:
        ins.append(add)
        in_specs.append(o_spec)
    if prev is not None:
        aliases = {len(ins): 0}
        ins.append(prev)
        in_specs.append(HBM)
    return pl.pallas_call(
        body, name=name, grid=(M // tm, N // tn, nk), in_specs=in_specs, out_specs=o_spec,
        out_shape=jax.ShapeDtypeStruct(o_shape, out_dtype),
        scratch_shapes=[pltpu.VMEM((tm, tn), F32)] if nk > 1 else [],
        input_output_aliases=aliases, compiler_params=_params(3))(*ins)


def _row(tr, d=D_MODEL):
    return pl.BlockSpec((tr, d), lambda i: (i, 0))


def _vec(d=D_MODEL):
    return pl.BlockSpec((1, d), lambda i: (0, 0))


def _modspec(j):
    return pl.BlockSpec((8, D_MODEL), lambda i: (0, j))


def _silu(x):
    return x * jax.nn.sigmoid(x)


def _dsilu(x):
    s = jax.nn.sigmoid(x)
    return s * (1.0 + x * (1.0 - s))


ADA_TN = 6 * D_MODEL // 4


def _ada_mod_shard(c_all, w_ada, b_ada, name):
    def body(c_ref, w_ref, b_ref, o_ref):
        o_ref[...] = _dot(_silu(c_ref[...]), w_ref[...]) + b_ref[...]

    return pl.pallas_call(
        body, name=name, grid=(2,),
        in_specs=[pl.BlockSpec((N_DEV, D_MODEL), lambda l: (0, 0)),
                  pl.BlockSpec((None, D_MODEL, ADA_TN), lambda l: (l, 0, 0)),
                  pl.BlockSpec((None, 1, ADA_TN), lambda l: (l, 0, 0))],
        out_specs=pl.BlockSpec((None, N_DEV, ADA_TN), lambda l: (l, 0, 0)),
        out_shape=jax.ShapeDtypeStruct((2, N_DEV, ADA_TN), F32), compiler_params=_params(1))(c_all, w_ada, b_ada)


def _ada_grad_shard(c_all, dmod, name):
    def body(c_ref, d_ref, o_ref):
        o_ref[...] = _dot(_silu(c_ref[...]), d_ref[...], TN)

    return pl.pallas_call(
        body, name=name, grid=(2,),
        in_specs=[pl.BlockSpec((N_DEV, D_MODEL), lambda l: (0, 0)),
                  pl.BlockSpec((None, N_DEV, ADA_TN), lambda l: (l, 0, 0))],
        out_specs=pl.BlockSpec((None, D_MODEL, ADA_TN), lambda l: (l, 0, 0)),
        out_shape=jax.ShapeDtypeStruct((2, D_MODEL, ADA_TN), F32), compiler_params=_params(1))(c_all, dmod)


def _view_spec(tr, d, width=D_MODEL):
    return pl.BlockSpec((tr // d, d * width), lambda i: (i, 0))


def _view_shape(S, d, dtype, width=D_MODEL):
    return jax.ShapeDtypeStruct((S // d, d * width), dtype)


LANE_TILE = 128
N_LANE_TILES = D_MODEL // LANE_TILE


def _token_scratch(tr):
    return pltpu.VMEM((N_LANE_TILES, tr, LANE_TILE), F32)


def _scratch_put(scr_ref, val):
    for c in range(N_LANE_TILES):
        scr_ref[c] = val[:, c * LANE_TILE:(c + 1) * LANE_TILE]


def _scratch_get(scr_ref):
    return jnp.concatenate([scr_ref[c] for c in range(N_LANE_TILES)], axis=1)


def _store_view(scr_ref, out_ref, d):
    n = scr_ref.shape[1] // d
    for r in range(d):
        for c in range(N_LANE_TILES):
            lo = r * D_MODEL + c * LANE_TILE
            out_ref[:, lo:lo + LANE_TILE] = scr_ref.at[c][pl.ds(r, n, stride=d), :].astype(out_ref.dtype)


def _load_view(view_ref, scr_ref, d):
    n = scr_ref.shape[1] // d
    for r in range(d):
        for c in range(N_LANE_TILES):
            lo = r * D_MODEL + c * LANE_TILE
            scr_ref.at[c][pl.ds(r, n, stride=d), :] = view_ref[:, lo:lo + LANE_TILE].astype(F32)


def _norm_mod(x, gamma, mod, j_sc, j_sh, name, resid=None, gate_mod=None, j_gate=None, views=False):
    S = x.shape[0]
    tr = ROW_TILE
    has_res = resid is not None

    def body(*refs):
        if has_res:
            x_ref, y_ref, gate_ref, g_ref, sc_ref, sh_ref, xo_ref, h_ref = refs
            xn = x_ref[...] + gate_ref[0:1, :] * y_ref[...]
            xo_ref[...] = xn
        elif views:
            x_ref, g_ref, sc_ref, sh_ref, h_ref, h4_ref, h16_ref, scr_ref = refs
            xn = x_ref[...]
        else:
            x_ref, g_ref, sc_ref, sh_ref, h_ref = refs
            xn = x_ref[...]
        r = lax.rsqrt(jnp.mean(xn * xn, axis=-1, keepdims=True) + EPS)
        n = (xn * r) * g_ref[...]
        h = n * (1.0 + sc_ref[0:1, :]) + sh_ref[0:1, :]
        h_ref[...] = h.astype(BF16)
        if views:
            _scratch_put(scr_ref, h)
            _store_view(scr_ref, h4_ref, 4)
            _store_view(scr_ref, h16_ref, 16)

    scratch = []
    if has_res:
        ins = [x, resid, gate_mod, gamma, mod, mod]
        in_specs = [_row(tr), _row(tr), _modspec(j_gate), _vec(), _modspec(j_sc), _modspec(j_sh)]
        out_specs = [_row(tr), _row(tr)]
        out_shape = [jax.ShapeDtypeStruct((S, D_MODEL), F32), jax.ShapeDtypeStruct((S, D_MODEL), BF16)]
    else:
        ins = [x, gamma, mod, mod]
        in_specs = [_row(tr), _vec(), _modspec(j_sc), _modspec(j_sh)]
        out_specs = _row(tr)
        out_shape = jax.ShapeDtypeStruct((S, D_MODEL), BF16)
        if views:
            out_specs = [_row(tr), _view_spec(tr, 4), _view_spec(tr, 16)]
            out_shape = [out_shape, _view_shape(S, 4, BF16), _view_shape(S, 16, BF16)]
            scratch = [_token_scratch(tr)]
    return pl.pallas_call(body, name=name, grid=(S // tr,), in_specs=in_specs, out_specs=out_specs,
                          out_shape=out_shape, scratch_shapes=scratch, compiler_params=_params(1))(*ins)


def _final_loss(x, resid, mod, j_gate, gamma, tgt, name):
    S = x.shape[0]
    tr = ROW_TILE

    def body(x_ref, y_ref, gate_ref, g_ref, t_ref, dx_ref, loss_ref, dg_ref, dy_ref, dgate_ref):
        @pl.when(pl.program_id(0) == 0)
        def _():
            loss_ref[...] = jnp.zeros_like(loss_ref)
            dg_ref[...] = jnp.zeros_like(dg_ref)
            dgate_ref[...] = jnp.zeros_like(dgate_ref)

        yv = y_ref[...]
        xn = x_ref[...] + gate_ref[0:1, :] * yv
        r = lax.rsqrt(jnp.mean(xn * xn, axis=-1, keepdims=True) + EPS)
        xhat = xn * r
        err = xhat * g_ref[...] - t_ref[...]
        loss_ref[...] += 0.5 * jnp.sum(jnp.mean(err * err, axis=-1, keepdims=True))
        dy = err * (1.0 / D_MODEL)
        dg_ref[...] += jnp.sum(dy * xhat, axis=0, keepdims=True)
        dxh = dy * g_ref[...]
        dx = r * (dxh - xhat * jnp.mean(dxh * xhat, axis=-1, keepdims=True))
        dx_ref[...] = dx
        dy_ref[...] = (gate_ref[0:1, :] * dx).astype(BF16)
        dgate_ref[...] += jnp.sum(dx * yv, axis=0, keepdims=True)

    vec = jax.ShapeDtypeStruct((1, D_MODEL), F32)
    return pl.pallas_call(
        body, name=name, grid=(S // tr,),
        in_specs=[_row(tr), _row(tr), _modspec(j_gate), _vec(), _row(tr)],
        out_specs=[_row(tr), pl.BlockSpec((1, 128), lambda i: (0, 0)), _vec(), _row(tr), _vec()],
        out_shape=[jax.ShapeDtypeStruct((S, D_MODEL), F32), jax.ShapeDtypeStruct((1, 128), F32), vec,
                   jax.ShapeDtypeStruct((S, D_MODEL), BF16), vec],
        compiler_params=_params(1))(x, resid, mod, gamma, tgt)


def _norm_mod_bwd(dh, x, dx_res, gamma, mod, j_sc, name, dh_views=None, branch=None, side=None):
    S = x.shape[0]
    tr = ROW_TILE
    n_views = 0 if dh_views is None else 2
    n_branch = 0 if branch is None else 2
    side_names = () if side is None else tuple(side)
    n_side = len(side_names)
    n_in = n_views + 4 + n_branch + n_side
    n_steps = S // tr

    def body(dh_ref, *refs):
        x_ref, dr_ref, g_ref, sc_ref = refs[n_views:n_views + 4]
        dx_ref, dsc_ref, dsh_ref, dg_ref = refs[n_in:n_in + 4]

        def side_job(start, wait):
            _pair_exchange(side_names, refs[n_in - n_side:n_in], refs[n_in + 4 + n_branch:n_in + 4 + n_branch + n_side],
                           refs[-2], refs[-1], start, wait)

        @pl.when(pl.program_id(0) == 0)
        def _():
            dsc_ref[...] = jnp.zeros_like(dsc_ref)
            dsh_ref[...] = jnp.zeros_like(dsh_ref)
            dg_ref[...] = jnp.zeros_like(dg_ref)
            if n_side:
                side_job(True, False)

        xv = x_ref[...]
        dh_v = dh_ref[...]
        if n_views:
            scr_ref = refs[-3] if n_side else refs[-1]
            for view_ref, d in zip(refs[:2], (4, 16)):
                _load_view(view_ref, scr_ref, d)
                dh_v = dh_v + _scratch_get(scr_ref)
        r = lax.rsqrt(jnp.mean(xv * xv, axis=-1, keepdims=True) + EPS)
        xhat = xv * r
        dsh_ref[...] += jnp.sum(dh_v, axis=0, keepdims=True)
        dsc_ref[...] += jnp.sum(dh_v * (xhat * g_ref[...]), axis=0, keepdims=True)
        dn = dh_v * (1.0 + sc_ref[0:1, :])
        dg_ref[...] += jnp.sum(dn * xhat, axis=0, keepdims=True)
        dxh = dn * g_ref[...]
        dx = dr_ref[...] + r * (dxh - xhat * jnp.mean(dxh * xhat, axis=-1, keepdims=True))
        dx_ref[...] = dx
        if n_branch:
            y_ref, gate_ref = refs[n_views + 4:n_views + 6]
            dy_ref, dgate_ref = refs[n_in + 4:n_in + 6]

            @pl.when(pl.program_id(0) == 0)
            def _():
                dgate_ref[...] = jnp.zeros_like(dgate_ref)

            dy_ref[...] = (gate_ref[0:1, :] * dx).astype(BF16)
            dgate_ref[...] += jnp.sum(dx * y_ref[...], axis=0, keepdims=True)
        if n_side:
            @pl.when(pl.program_id(0) == n_steps - 1)
            def _():
                side_job(False, True)

    vec = jax.ShapeDtypeStruct((1, D_MODEL), F32)
    views = [] if dh_views is None else list(dh_views)
    view_specs = [_view_spec(tr, 4), _view_spec(tr, 16)] if views else []
    ins = [dh, *views, x, dx_res, gamma, mod]
    in_specs = [_row(tr)] + view_specs + [_row(tr), _row(tr), _vec(), _modspec(j_sc)]
    out_specs = [_row(tr), _vec(), _vec(), _vec()]
    out_shape = [jax.ShapeDtypeStruct((S, D_MODEL), F32), vec, vec, vec]
    if branch is not None:
        y, gate_mod, j_gate = branch
        ins += [y, gate_mod]
        in_specs += [_row(tr), _modspec(j_gate)]
        out_specs += [_row(tr), _vec()]
        out_shape += [jax.ShapeDtypeStruct((S, D_MODEL), BF16), vec]
    scratch = [_token_scratch(tr)] if views else []
    if n_side:
        ins += [side[n] for n in side_names]
        in_specs += [HBM] * n_side
        out_specs += [HBM] * n_side
        out_shape += _pair_recv_shapes(side_names)
        scratch += [pltpu.SemaphoreType.DMA((n_side,)), pltpu.SemaphoreType.DMA((n_side,))]
    res = pl.pallas_call(
        body, name=name, grid=(n_steps,), in_specs=in_specs, out_specs=out_specs, out_shape=out_shape,
        scratch_shapes=scratch, compiler_params=_params(1))(*ins)
    if n_side:
        return res[:len(res) - n_side], dict(zip(side_names, res[len(res) - n_side:]))
    return res


def _shift_down(u, halo, s):
    r = pltpu.roll(u, s, 0)
    hr = pltpu.roll(halo, s, 0)
    rid = lax.broadcasted_iota(jnp.int32, hr.shape, 0)
    top = jnp.where(rid < s, hr, r[0:8])
    return jnp.concatenate([top, r[8:]], axis=0)


def _conv3(u, halo, w_ref, b_ref):
    u1 = _shift_down(u, halo, 1)
    u2 = _shift_down(u, halo, 2)
    return b_ref[...] + ((w_ref[0:1, :] * u2 + w_ref[1:2, :] * u1) + w_ref[2:3, :] * u), u1, u2


CONV_TC = 1408


def _conv_specs(tr, S):
    nh = D_FF // CONV_TC
    hb = tr // 8

    def cur(off):
        return pl.BlockSpec((tr, CONV_TC), lambda j, i: (i, j + off))

    def halo(off):
        return pl.BlockSpec((8, CONV_TC), lambda j, i: (jnp.maximum(i * hb - 1, 0), j + off))

    def w(off):
        return pl.BlockSpec((3, CONV_TC), lambda j, i: (0, j + off))

    def b(off):
        return pl.BlockSpec((1, CONV_TC), lambda j, i: (0, j + off))

    return nh, cur, halo, w, b


def _conv_gate(u, conv_w, conv_b, name):
    S = u.shape[0]
    tr = ROW_TILE
    nh, cur, halo, w, b = _conv_specs(tr, S)

    def body(ua_ref, ha_ref, ub_ref, hb_ref, wa_ref, wb_ref, ba_ref, bb_ref, o_ref):
        first = pl.program_id(1) == 0
        ha = jnp.where(first, 0.0, ha_ref[...])
        hbv = jnp.where(first, 0.0, hb_ref[...])
        a, _, _ = _conv3(ua_ref[...], ha, wa_ref, ba_ref)
        bb, _, _ = _conv3(ub_ref[...], hbv, wb_ref, bb_ref)
        o_ref[...] = (_silu(a) * bb).astype(BF16)

    return pl.pallas_call(
        body, name=name, grid=(nh, S // tr),
        in_specs=[cur(0), halo(0), cur(nh), halo(nh), w(0), w(nh), b(0), b(nh)],
        out_specs=pl.BlockSpec((tr, CONV_TC), lambda j, i: (i, j)),
        out_shape=jax.ShapeDtypeStruct((S, D_FF), BF16), compiler_params=_params(2))(
            u, u, u, u, conv_w, conv_w, conv_b, conv_b)


def _conv_gate_bwd(u, dact, conv_w, conv_b, name):
    S = u.shape[0]
    tr = ROW_TILE
    nh, cur, halo, w, b = _conv_specs(tr, S)
    hb = tr // 8
    nlast = S // 8 - 1
    nsteps = S // tr

    def after(off):
        return pl.BlockSpec((8, CONV_TC), lambda j, i: (jnp.minimum((i + 1) * hb, nlast), j + off))

    def body(ua_ref, ha_ref, na_ref, ub_ref, hb_ref, nb_ref, wa_ref, wb_ref, ba_ref, bb_ref, da_ref, dn_ref,
             dua_ref, dub_ref, dwa_ref, dwb_ref, dba_ref, dbb_ref):
        first = pl.program_id(1) == 0
        last = pl.program_id(1) == nsteps - 1

        @pl.when(first)
        def _():
            for r in (dwa_ref, dwb_ref, dba_ref, dbb_ref):
                r[...] = jnp.zeros_like(r)

        ha = jnp.where(first, 0.0, ha_ref[...])
        hbv = jnp.where(first, 0.0, hb_ref[...])
        ua = jnp.concatenate([ua_ref[...], na_ref[...]], axis=0)
        ub = jnp.concatenate([ub_ref[...], nb_ref[...]], axis=0)
        a, ua1, ua2 = _conv3(ua, ha, wa_ref, ba_ref)
        bb, ub1, ub2 = _conv3(ub, hbv, wb_ref, bb_ref)
        dact_v = jnp.concatenate([da_ref[...], jnp.where(last, 0.0, dn_ref[...])], axis=0)
        da = dact_v * bb * _dsilu(a)
        db = dact_v * _silu(a)
        n = tr + 8
        for d, x0, x1, x2, w_ref, du_ref, dw_ref, dbias_ref in (
                (da, ua, ua1, ua2, wa_ref, dua_ref, dwa_ref, dba_ref),
                (db, ub, ub1, ub2, wb_ref, dub_ref, dwb_ref, dbb_ref)):
            d1 = pltpu.roll(d, n - 1, 0)
            d2 = pltpu.roll(d, n - 2, 0)
            du_ref[...] = ((w_ref[2:3, :] * d + w_ref[1:2, :] * d1) + w_ref[0:1, :] * d2)[:tr].astype(BF16)
            dt = d[:tr]
            dbias_ref[...] += jnp.sum(dt, axis=0, keepdims=True)
            dw_ref[0:1, :] += jnp.sum(dt * x2[:tr], axis=0, keepdims=True)
            dw_ref[1:2, :] += jnp.sum(dt * x1[:tr], axis=0, keepdims=True)
            dw_ref[2:3, :] += jnp.sum(dt * x0[:tr], axis=0, keepdims=True)

    half = pl.BlockSpec((tr, CONV_TC), lambda j, i: (i, j))
    half_after = pl.BlockSpec((8, CONV_TC), lambda j, i: (jnp.minimum((i + 1) * hb, nlast), j))
    dw = pl.BlockSpec((8, CONV_TC), lambda j, i: (0, j))
    dbs = pl.BlockSpec((1, CONV_TC), lambda j, i: (0, j))
    f = lambda r, c: jax.ShapeDtypeStruct((r, c), F32)
    du = jax.ShapeDtypeStruct((S, D_FF), BF16)
    return pl.pallas_call(
        body, name=name, grid=(nh, nsteps),
        in_specs=[cur(0), halo(0), after(0), cur(nh), halo(nh), after(nh), w(0), w(nh), b(0), b(nh), half,
                  half_after],
        out_specs=[half, half, dw, dw, dbs, dbs],
        out_shape=[du, du, f(8, D_FF), f(8, D_FF), f(1, D_FF), f(1, D_FF)],
        compiler_params=_params(2))(u, u, u, u, u, u, conv_w, conv_w, conv_b, conv_b, dact, dact)


def _bucket_maps():
    qi = np.arange(A_BLK)[:, None]
    ki = np.arange(2 * A_BLK)[None, :]
    steps = np.clip(qi + A_BLK - ki, 0, A_BLK)
    out = []
    max_exact = N_BUCKETS // 2
    for _, dil in A_CONFIGS:
        dist = steps * dil
        n = np.maximum(dist, max_exact).astype(np.float32)
        large = max_exact + (np.log(n / np.float32(max_exact)) / np.float32(math.log(MAX_DISTANCE / max_exact))
                             * np.float32(N_BUCKETS - max_exact)).astype(np.int32)
        large = np.minimum(large, N_BUCKETS - 1)
        out.append(np.where(dist < max_exact, dist, large))
    return np.stack(out).astype(np.int32)


def _bias_build(rel_bias, buckets, name):
    ng = len(A_CONFIGS)

    def body(t_ref, bk_ref, o_ref):
        gh = pl.program_id(0) * A_HEADS + pl.program_id(1)
        bk = bk_ref[0]
        acc = jnp.zeros((A_BLK, 2 * A_BLK), F32)
        for b in range(N_BUCKETS):
            acc = jnp.where(bk == b, t_ref[b, gh], acc)
        o_ref[0] = acc

    return pl.pallas_call(
        body, name=name, grid=(ng, A_HEADS),
        in_specs=[pl.BlockSpec(memory_space=pltpu.SMEM), pl.BlockSpec((1, A_BLK, 2 * A_BLK), lambda g, h: (g, 0, 0))],
        out_specs=pl.BlockSpec((1, A_BLK, 2 * A_BLK), lambda g, h: (g * A_HEADS + h, 0, 0)),
        out_shape=jax.ShapeDtypeStruct((ng * A_HEADS, A_BLK, 2 * A_BLK), F32),
        compiler_params=_params(2))(rel_bias, buckets)


def _bias_bwd(dbias, buckets, name):
    ng = len(A_CONFIGS)

    def body(d_ref, bk_ref, o_ref):
        gh = pl.program_id(0) * A_HEADS + pl.program_id(1)
        bk = bk_ref[0]
        d = d_ref[0]
        for b in range(N_BUCKETS):
            o_ref[b, gh] = jnp.sum(jnp.where(bk == b, d, 0.0))

    return pl.pallas_call(
        body, name=name, grid=(ng, A_HEADS),
        in_specs=[pl.BlockSpec((1, A_BLK, 2 * A_BLK), lambda g, h: (g * A_HEADS + h, 0, 0)),
                  pl.BlockSpec((1, A_BLK, 2 * A_BLK), lambda g, h: (g, 0, 0))],
        out_specs=pl.BlockSpec(memory_space=pltpu.SMEM),
        out_shape=jax.ShapeDtypeStruct((N_BUCKETS, ng * A_HEADS), F32),
        compiler_params=_params(2))(dbias, buckets)


def _attn_mask(b):
    qi = lax.broadcasted_iota(jnp.int32, (A_BLK, 2 * A_BLK), 0)
    ki = lax.broadcasted_iota(jnp.int32, (A_BLK, 2 * A_BLK), 1)
    band = (ki >= qi) & (ki <= qi + A_BLK)
    return band & ((b > 0) | (ki >= A_BLK))


def _first_head_lanes():
    return lax.broadcasted_iota(jnp.int32, (A_BLK, 2 * A_HEAD_DIM), 1) < A_HEAD_DIM


def _attn_in_specs(g, dil):
    W = A_HEADS * A_HEAD_DIM

    def spec(t, prev, nb):
        def im(r, b):
            bb = jnp.minimum(b, nb - 1)
            if prev:
                bb = jnp.maximum(bb - 1, 0)
            return (bb, r * 3 + t)
        return pl.BlockSpec((A_BLK, W), im)

    return lambda nb: [spec(0, False, nb), spec(1, False, nb), spec(1, True, nb), spec(2, False, nb),
                       spec(2, True, nb)]


def _attn_fwd(qv, bias, g, name):
    _, dil = A_CONFIGS[g]
    L = qv.shape[0]
    nb = L // A_BLK
    W = A_HEADS * A_HEAD_DIM

    def body(q_ref, kc_ref, kp_ref, vc_ref, vp_ref, bias_ref, o_ref, l_ref):
        mask = _attn_mask(pl.program_id(1))
        first = _first_head_lanes()
        for j in range(A_HEADS // 2):
            ps = slice(j * 2 * A_HEAD_DIM, (j + 1) * 2 * A_HEAD_DIM)
            q2 = q_ref[:, ps] * 0.125
            k2 = jnp.concatenate([kp_ref[:, ps], kc_ref[:, ps]], axis=0)
            v2 = jnp.concatenate([vp_ref[:, ps], vc_ref[:, ps]], axis=0)
            o_pair, l_pair = [], []
            for e in range(2):
                qh = jnp.where(first if e == 0 else ~first, q2, jnp.zeros_like(q2))
                s = lax.dot_general(qh, k2, NT, preferred_element_type=F32) + bias_ref[2 * j + e]
                s = jnp.where(mask, s, NEG_INF)
                m = jnp.max(s, axis=-1, keepdims=True)
                p = jnp.exp(s - m)
                den = jnp.sum(p, axis=-1, keepdims=True)
                o_pair.append(jnp.dot(p.astype(BF16), v2, preferred_element_type=F32) / den)
                l_pair.append(m + jnp.log(den))
            o_ref[:, ps] = jnp.where(first, o_pair[0], o_pair[1])
            l_ref[:, ps] = jnp.where(first, l_pair[0], l_pair[1])

    out_spec = pl.BlockSpec((A_BLK, W), lambda r, b: (b, r))
    return pl.pallas_call(
        body, name=name, grid=(dil, nb),
        in_specs=_attn_in_specs(g, dil)(nb) + [pl.BlockSpec((A_HEADS, A_BLK, 2 * A_BLK), lambda r, b: (g, 0, 0))],
        out_specs=[out_spec, out_spec],
        out_shape=[jax.ShapeDtypeStruct((L, dil * W), F32)] * 2,
        compiler_params=_params(2))(qv, qv, qv, qv, qv, bias)


def _mix_fwd(os, ls, name):
    S = os[0].shape[0]
    tr = ROW_TILE

    def body(o0, o1, o2, l0, l1, l2, om_ref, om4_ref, om16_ref, lt_ref, lt4_ref, lt16_ref, s_o1, s_o2, s_l1, s_l2):
        for view_ref, scr_ref, d in ((o1, s_o1, 4), (o2, s_o2, 16), (l1, s_l1, 4), (l2, s_l2, 16)):
            _load_view(view_ref, scr_ref, d)
        a, b, c = l0[...], _scratch_get(s_l1), _scratch_get(s_l2)
        m = jnp.maximum(jnp.maximum(a, b), c)
        ea, eb, ec = jnp.exp(a - m), jnp.exp(b - m), jnp.exp(c - m)
        z = (ea + eb) + ec
        om = ((ea / z) * o0[...] + (eb / z) * _scratch_get(s_o1)) + (ec / z) * _scratch_get(s_o2)
        lt = m + jnp.log(z)
        om_ref[...] = om
        lt_ref[...] = lt
        _scratch_put(s_o1, om)
        _scratch_put(s_l1, lt)
        for scr_ref, v4_ref, v16_ref in ((s_o1, om4_ref, om16_ref), (s_l1, lt4_ref, lt16_ref)):
            _store_view(scr_ref, v4_ref, 4)
            _store_view(scr_ref, v16_ref, 16)

    ins = [_row(tr), _view_spec(tr, 4), _view_spec(tr, 16)]
    outs = [jax.ShapeDtypeStruct((S, D_MODEL), F32), _view_shape(S, 4, F32), _view_shape(S, 16, F32)]
    return pl.pallas_call(
        body, name=name, grid=(S // tr,), in_specs=ins * 2, out_specs=ins * 2, out_shape=outs * 2,
        scratch_shapes=[_token_scratch(tr)] * 4, compiler_params=_params(1))(*os, *ls)


def _to_views(x, name):
    S = x.shape[0]
    tr = ROW_TILE

    def body(x_ref, v4_ref, v16_ref, scr_ref):
        _scratch_put(scr_ref, x_ref[...])
        _store_view(scr_ref, v4_ref, 4)
        _store_view(scr_ref, v16_ref, 16)

    return pl.pallas_call(
        body, name=name, grid=(S // tr,), in_specs=[_row(tr)], out_specs=[_view_spec(tr, 4), _view_spec(tr, 16)],
        out_shape=[_view_shape(S, 4, F32), _view_shape(S, 16, F32)], scratch_shapes=[_token_scratch(tr)],
        compiler_params=_params(1))(x)


def _attn_bwd(qv, bias, d_o, omix, ltot, g, name, exchange=None):
    _, dil = A_CONFIGS[g]
    L = qv.shape[0]
    nb = L // A_BLK
    W = A_HEADS * A_HEAD_DIM
    ex_names = () if exchange is None else tuple(exchange)
    n_ex = len(ex_names)

    def body(q_ref, kc_ref, kp_ref, vc_ref, vp_ref, bias_ref, do_ref, om_ref, lt_ref, *rest):
        ex_in, (dqkv_ref, db_ref), ex_out = rest[:n_ex], rest[n_ex:n_ex + 2], rest[n_ex + 2:2 * n_ex + 2]
        cq_ref, ck_ref, cv_ref = rest[2 * n_ex + 2:2 * n_ex + 5]
        r, b = pl.program_id(0), pl.program_id(1)
        dq_ref, dk_ref, dv_ref = (dqkv_ref.at[:, t * W:(t + 1) * W] for t in range(3))

        @pl.when((r == 0) & (b == 0))
        def _():
            db_ref[...] = jnp.zeros_like(db_ref)
            if n_ex:
                _chip_exchange(ex_in, ex_out, rest[-2], rest[-1], True, False)

        @pl.when(b == 0)
        def _():
            cq_ref[...] = jnp.zeros_like(cq_ref)
            ck_ref[...] = jnp.zeros_like(ck_ref)
            cv_ref[...] = jnp.zeros_like(cv_ref)

        dq_ref[...] = cq_ref[...]

        @pl.when(b < nb)
        def _():
            mask = _attn_mask(b)
            first = _first_head_lanes()
            for j in range(A_HEADS // 2):
                ps = slice(j * 2 * A_HEAD_DIM, (j + 1) * 2 * A_HEAD_DIM)
                q2 = q_ref[:, ps] * 0.125
                k2 = jnp.concatenate([kp_ref[:, ps], kc_ref[:, ps]], axis=0)
                v2 = jnp.concatenate([vp_ref[:, ps], vc_ref[:, ps]], axis=0)
                do2, om2 = do_ref[:, ps], om_ref[:, ps]
                dq_pair, dk2, dv2 = [], None, None
                for e in range(2):
                    mine = first if e == 0 else ~first
                    h = 2 * j + e
                    qh = jnp.where(mine, q2, jnp.zeros_like(q2))
                    s = lax.dot_general(qh, k2, NT, preferred_element_type=F32) + bias_ref[h]
                    s = jnp.where(mask, s, NEG_INF)
                    wp = jnp.exp(s - lt_ref[:, h * A_HEAD_DIM:h * A_HEAD_DIM + 1])
                    do_h = jnp.where(mine, do2, 0.0)
                    t_h = jnp.sum(do_h * om2, axis=-1, keepdims=True)
                    do_b = do_h.astype(BF16)
                    dp = lax.dot_general(do_b, v2, NT, preferred_element_type=F32)
                    ds = wp * (dp - t_h)
                    db_ref[h] += ds
                    ds_b = ds.astype(BF16)
                    dv_e = lax.dot_general(wp.astype(BF16), do_b, TN, preferred_element_type=F32)
                    dk_e = lax.dot_general(ds_b, qh, TN, preferred_element_type=F32)
                    dv2 = dv_e if dv2 is None else dv2 + dv_e
                    dk2 = dk_e if dk2 is None else dk2 + dk_e
                    dq_pair.append(jnp.dot(ds_b, k2, preferred_element_type=F32))
                cq_ref[:, ps] = (jnp.where(first, dq_pair[0], dq_pair[1]) * 0.125).astype(BF16)
                dk_ref[:, ps] = (ck_ref[:, ps] + dk2[:A_BLK]).astype(BF16)
                dv_ref[:, ps] = (cv_ref[:, ps] + dv2[:A_BLK]).astype(BF16)
                ck_ref[:, ps] = dk2[A_BLK:]
                cv_ref[:, ps] = dv2[A_BLK:]

        @pl.when(b == nb)
        def _():
            dk_ref[...] = ck_ref[...].astype(BF16)
            dv_ref[...] = cv_ref[...].astype(BF16)

        if n_ex:
            @pl.when((r == dil - 1) & (b == nb))
            def _():
                _chip_exchange(ex_in, ex_out, rest[-2], rest[-1], False, True)

    act = pl.BlockSpec((A_BLK, W), lambda r, b: (jnp.minimum(b, nb - 1), r))
    lag = pl.BlockSpec((A_BLK, 3 * W), lambda r, b: (jnp.maximum(b - 1, 0), r))
    full = pl.BlockSpec((A_HEADS, A_BLK, 2 * A_BLK), lambda r, b: (0, 0, 0))
    in_specs = _attn_in_specs(g, dil)(nb) + [pl.BlockSpec((A_HEADS, A_BLK, 2 * A_BLK), lambda r, b: (g, 0, 0)),
                                             act, act, act]
    ex_arrays = [exchange[n] for n in ex_names]
    scratch = [pltpu.VMEM((A_BLK, W), BF16), pltpu.VMEM((A_BLK, W), F32), pltpu.VMEM((A_BLK, W), F32)]
    if n_ex:
        scratch += [pltpu.SemaphoreType.DMA((3 * n_ex,)), pltpu.SemaphoreType.DMA((3 * n_ex,))]
    res = pl.pallas_call(
        body, name=name, grid=(dil, nb + 1), in_specs=in_specs + [HBM] * n_ex, out_specs=[lag, full] + [HBM] * n_ex,
        out_shape=[jax.ShapeDtypeStruct((L, dil * 3 * W), BF16),
                   jax.ShapeDtypeStruct((A_HEADS, A_BLK, 2 * A_BLK), F32)]
        + [jax.ShapeDtypeStruct(a.shape, a.dtype) for a in ex_arrays],
        scratch_shapes=scratch, compiler_params=_params(2))(qv, qv, qv, qv, qv, bias, d_o, omix, ltot, *ex_arrays)
    if n_ex:
        return res[0], res[1], dict(zip(ex_names, res[2:]))
    return res


NT = (((1,), (1,)), ((), ()))
TN = (((0,), (0,)), ((), ()))


def _dot(a, b, dims=(((1,), (0,)), ((), ()))):
    return lax.dot_general(a.astype(BF16), b.astype(BF16), dims, preferred_element_type=F32)


def _gla_gates(glr, wg_ref, bg_ref):
    z = _dot(glr, wg_ref[...]) + bg_ref[...]
    log_sig = -(jnp.maximum(-z, 0.0) + jnp.log1p(jnp.exp(-jnp.abs(z))))
    return z, log_sig / B_TAU


def _gla_chunk(q, k, gk):
    row = lax.broadcasted_iota(jnp.int32, (B_CHUNK, B_CHUNK), 0)
    col = lax.broadcasted_iota(jnp.int32, (B_CHUNK, B_CHUNK), 1)
    causal = row >= col
    bcum = jnp.dot(causal.astype(F32), gk, precision=lax.Precision.HIGHEST, preferred_element_type=F32)
    bl = bcum[B_CHUNK - 1:B_CHUNK, :]
    qt = (q * (B_DK ** -0.5)) * jnp.exp(bcum)
    kt = k * jnp.exp(-bcum)
    kd = k * jnp.exp(bl - bcum)
    a = jnp.where(causal, _dot(qt, kt, NT), 0.0)
    return causal, bcum, bl, qt, kt, kd, a


def _gla_specs(tg):
    q = pl.BlockSpec((tg, B_DK), lambda h, i: (i, h))
    k = pl.BlockSpec((tg, B_DK), lambda h, i: (i, B_HEADS + h))
    v = pl.BlockSpec((tg, B_DV), lambda h, i: (i, B_HEADS + h))
    glr = pl.BlockSpec((tg, 128), lambda h, i: (i, 24))
    wg = pl.BlockSpec((128, B_DK), lambda h, i: (0, h))
    bg = pl.BlockSpec((1, B_DK), lambda h, i: (0, h))
    return [q, k, v, glr, wg, bg]


def _gla_fwd(proj, w_gate, b_gate, name):
    S = proj.shape[0]
    tg = GLA_ROWS
    nc = tg // B_CHUNK

    def body(q_ref, k_ref, v_ref, glr_ref, wg_ref, bg_ref, o_ref, st_ref, state_ref):
        @pl.when(pl.program_id(1) == 0)
        def _():
            state_ref[...] = jnp.zeros_like(state_ref)

        _, gk_all = _gla_gates(glr_ref[...], wg_ref, bg_ref)
        st = state_ref[...]
        for c in range(nc):
            rows = slice(c * B_CHUNK, (c + 1) * B_CHUNK)
            v = v_ref[rows, :]
            _, _, bl, qt, _, kd, a = _gla_chunk(q_ref[rows, :], k_ref[rows, :], gk_all[rows, :])
            o_ref[rows, :] = _dot(a, v) + _dot(qt, st, NT)
            st_ref[c, 0] = st
            st = st * jnp.exp(bl) + _dot(v, kd, TN)
        state_ref[...] = st

    return pl.pallas_call(
        body, name=name, grid=(B_HEADS, S // tg), in_specs=_gla_specs(tg),
        out_specs=[pl.BlockSpec((tg, B_DV), lambda h, i: (i, h)),
                   pl.BlockSpec((nc, 1, B_DV, B_DK), lambda h, i: (i, h, 0, 0))],
        out_shape=[jax.ShapeDtypeStruct((S, B_V), F32),
                   jax.ShapeDtypeStruct((S // B_CHUNK, B_HEADS, B_DV, B_DK), F32)],
        scratch_shapes=[pltpu.VMEM((B_DV, B_DK), F32)], compiler_params=_params(2))(
            proj, proj, proj, proj, w_gate, b_gate)


def _gla_bwd(proj, w_gate, b_gate, states, d_o, name):
    S = proj.shape[0]
    tg = GLA_ROWS
    nc = tg // B_CHUNK
    ni = S // tg

    def rev(spec):
        return pl.BlockSpec(spec.block_shape, lambda h, i, im=spec.index_map: im(h, ni - 1 - i))

    def body(q_ref, k_ref, v_ref, glr_ref, wg_ref, bg_ref, st_ref, do_ref,
             dq_ref, dk_ref, dv_ref, dz_ref, dbg_ref, dstate_ref):
        @pl.when(pl.program_id(1) == 0)
        def _():
            dstate_ref[...] = jnp.zeros_like(dstate_ref)
            dbg_ref[...] = jnp.zeros_like(dbg_ref)

        z_all, gk_all = _gla_gates(glr_ref[...], wg_ref, bg_ref)
        dst = dstate_ref[...]
        for c in range(nc - 1, -1, -1):
            rows = slice(c * B_CHUNK, (c + 1) * B_CHUNK)
            v = v_ref[rows, :]
            d_out = do_ref[rows, :]
            st = st_ref[c, 0]
            causal, bcum, bl, qt, kt, kd, a = _gla_chunk(q_ref[rows, :], k_ref[rows, :], gk_all[rows, :])
            da = jnp.where(causal, _dot(d_out, v, NT), 0.0)
            dv_ref[rows, :] = (_dot(a, d_out, TN) + _dot(kd, dst, NT)).astype(BF16)
            dqt = _dot(da, kt) + _dot(d_out, st)
            dkt = _dot(da, qt, TN)
            dkd = _dot(v, dst)
            dec = jnp.exp(bl)
            ddec = jnp.sum(dst * st, axis=0, keepdims=True)
            dst = dst * dec + _dot(d_out, qt, TN)
            dq_ref[rows, :] = (dqt * jnp.exp(bcum) * (B_DK ** -0.5)).astype(BF16)
            dk_ref[rows, :] = (dkt * jnp.exp(-bcum) + dkd * jnp.exp(bl - bcum)).astype(BF16)
            db = (dqt * qt - dkt * kt) - dkd * kd
            dbl = jnp.sum(dkd * kd, axis=0, keepdims=True) + dec * ddec
            upper = jnp.logical_not(causal) | (lax.broadcasted_iota(jnp.int32, (B_CHUNK, B_CHUNK), 0)
                                               == lax.broadcasted_iota(jnp.int32, (B_CHUNK, B_CHUNK), 1))
            dgk = jnp.dot(upper.astype(F32), db, precision=lax.Precision.HIGHEST, preferred_element_type=F32) + dbl
            dz = dgk * (1.0 / B_TAU) * jax.nn.sigmoid(-z_all[rows, :])
            dz_ref[rows, :] = dz
            dbg_ref[...] += jnp.sum(dz, axis=0, keepdims=True)
        dstate_ref[...] = dst

    qs = pl.BlockSpec((tg, B_DK), lambda h, i: (i, h))
    vs = pl.BlockSpec((tg, B_DV), lambda h, i: (i, h))
    in_specs = [rev(s) if n < 4 else s for n, s in enumerate(_gla_specs(tg))]
    in_specs += [rev(pl.BlockSpec((nc, 1, B_DV, B_DK), lambda h, i: (i, h, 0, 0))), rev(vs)]
    return pl.pallas_call(
        body, name=name, grid=(B_HEADS, ni), in_specs=in_specs,
        out_specs=[rev(qs), rev(qs), rev(vs), rev(qs), pl.BlockSpec((1, B_DK), lambda h, i: (0, h))],
        out_shape=[jax.ShapeDtypeStruct((S, B_QK), BF16), jax.ShapeDtypeStruct((S, B_QK), BF16),
                   jax.ShapeDtypeStruct((S, B_V), BF16), jax.ShapeDtypeStruct((S, B_QK), F32),
                   jax.ShapeDtypeStruct((1, B_QK), F32)],
        scratch_shapes=[pltpu.VMEM((B_DV, B_DK), F32)], compiler_params=_params(2))(
            proj, proj, proj, proj, w_gate, b_gate, states, d_o)


def _gla_out(o, proj, gnorm, name):
    S = o.shape[0]
    tr = ROW_TILE

    def body(o_ref, r_ref, g_ref, y_ref):
        for h in range(B_HEADS):
            hs = slice(h * B_DV, (h + 1) * B_DV)
            oh = o_ref[:, hs]
            rs = lax.rsqrt(jnp.mean(oh * oh, axis=-1, keepdims=True) + EPS)
            y_ref[:, hs] = (((oh * rs) * g_ref[...]) * _silu(r_ref[:, hs])).astype(BF16)

    return pl.pallas_call(
        body, name=name, grid=(S // tr,),
        in_specs=[_row(tr), pl.BlockSpec((tr, B_V), lambda i: (i, 2)), _vec(B_DV)], out_specs=_row(tr),
        out_shape=jax.ShapeDtypeStruct((S, B_V), BF16), compiler_params=_params(1))(o, proj, gnorm)


def _gla_out_bwd(o, proj, gnorm, d_y, name):
    S = o.shape[0]
    tr = ROW_TILE

    def body(o_ref, r_ref, g_ref, dy_ref, do_ref, dr_ref, dg_ref):
        @pl.when(pl.program_id(0) == 0)
        def _():
            dg_ref[...] = jnp.zeros_like(dg_ref)

        for h in range(B_HEADS):
            hs = slice(h * B_DV, (h + 1) * B_DV)
            oh, rv, dyv = o_ref[:, hs], r_ref[:, hs], dy_ref[:, hs]
            rs = lax.rsqrt(jnp.mean(oh * oh, axis=-1, keepdims=True) + EPS)
            xhat = oh * rs
            dr_ref[:, hs] = (dyv * (xhat * g_ref[...]) * _dsilu(rv)).astype(BF16)
            dn = dyv * _silu(rv)
            dg_ref[...] += jnp.sum(dn * xhat, axis=0, keepdims=True)
            dxh = dn * g_ref[...]
            do_ref[:, hs] = rs * (dxh - xhat * jnp.mean(dxh * xhat, axis=-1, keepdims=True))

    return pl.pallas_call(
        body, name=name, grid=(S // tr,),
        in_specs=[_row(tr), pl.BlockSpec((tr, B_V), lambda i: (i, 2)), _vec(B_DV), _row(tr)],
        out_specs=[_row(tr), _row(tr), _vec(B_DV)],
        out_shape=[jax.ShapeDtypeStruct((S, B_V), F32), jax.ShapeDtypeStruct((S, B_V), BF16),
                   jax.ShapeDtypeStruct((1, B_DV), F32)],
        compiler_params=_params(1))(o, proj, gnorm, d_y)


J_SH1, J_SC1, J_G1, J_SH2, J_SC2, J_G2 = range(6)


IN_A_TN = 768
UP_TN = 2 * D_FF // 4
TOKEN_TK = 2048


def _ffn_fwd(h, w, i, tag):
    S = h.shape[0]
    u = _matmul(h, w["w_up"], "nn", F32, f"up{tag}", shape=(S, 2 * D_FF, D_MODEL), tiles=(1024, UP_TN, D_MODEL),
                b_spec=pl.BlockSpec((None, D_MODEL, UP_TN), lambda m, j, k: (j, i, 0)))
    act = _conv_gate(u, w["conv_w"][i], w["conv_b"][i:i + 1], f"conv_gate{tag}")
    f = _matmul(act, w["w_down"], "nn", F32, f"down{tag}", shape=(S, D_MODEL, D_FF), tiles=(1024, D_MODEL, D_FF),
                b_spec=pl.BlockSpec((D_FF, D_MODEL), lambda m, j, k: (i, j)))
    return u, act, f


def _ffn_bwd(dx_out, df, u, act, h, x_in, mod, w, i, tag, prev, branch, side_fn=None):
    S = h.shape[0]
    dact = _matmul(df, w["w_down"], "nt", F32, f"down_dx{tag}", shape=(S, D_FF, D_MODEL),
                   tiles=(1024, D_FF // 2, D_MODEL),
                   b_spec=pl.BlockSpec((D_FF // 2, D_MODEL), lambda m, j, k: (2 * i + j, k)))
    d_w_down = _matmul(act, df, "tn", F32, f"down_dw{tag}", shape=(D_FF, D_MODEL, S),
                       tiles=(D_FF // 2, D_MODEL, min(S, TOKEN_TK)),
                       o_spec=pl.BlockSpec((D_FF // 2, D_MODEL), lambda m, j, k: (2 * i + m, j)),
                       o_shape=(2 * D_FF, D_MODEL), prev=None if prev is None else prev["w_down"])
    du_a, du_b, dcwa, dcwb, dcba, dcbb = _conv_gate_bwd(u, dact, w["conv_w"][i], w["conv_b"][i:i + 1],
                                                        f"conv_gate_bwd{tag}")
    dh, d_w_up = None, None if prev is None else prev["w_up"]
    for half, du in enumerate((du_a, du_b)):
        dh = _matmul(du, w["w_up"], "nt", F32, f"up_dx{tag}{'ab'[half]}", shape=(S, D_MODEL, D_FF),
                     tiles=(1024, D_MODEL, UP_TN), add=dh,
                     b_spec=pl.BlockSpec((None, D_MODEL, UP_TN), lambda m, j, k, half=half: (2 * half + k, i, 0)))
        d_w_up = _matmul(h, du, "tn", F32, f"up_dw{tag}{'ab'[half]}", shape=(D_MODEL, D_FF, S),
                         tiles=(D_MODEL, UP_TN, min(S, TOKEN_TK)),
                         o_spec=pl.BlockSpec((None, D_MODEL, UP_TN), lambda m, j, k, half=half: (2 * half + j, i, 0)),
                         o_shape=(4, 2 * D_MODEL, UP_TN), prev=d_w_up)
    side = None if side_fn is None else side_fn(d_w_up, d_w_down)
    res = _norm_mod_bwd(dh, x_in, dx_out, w["norm_ffn"][i:i + 1], mod, J_SC2, f"norm_ffn_bwd{tag}", branch=branch,
                        side=side)
    res, arrived = res if side is not None else (res, None)
    dx_in, dsc2, dsh2, dgam = res[:4]
    grads = dict(w_down=d_w_down, w_up=d_w_up, norm_ffn=dgam,
                 conv_w=jnp.concatenate([dcwa[0:3], dcwb[0:3]], axis=1),
                 conv_b=jnp.concatenate([dcba, dcbb], axis=1), side=(side, arrived))
    return dx_in, res[4:], (dsh2, dsc2), grads


def _local_step(x, tgt, w, mods, early=None):
    buckets = jnp.asarray(_bucket_maps())
    S = x.shape[0]

    h1 = _norm_mod(x, w["norm_mix"][0:1], mods[0], J_SC1, J_SH1, "norm_mix0", views=True)
    geo = []
    for g, (_, dil) in enumerate(A_CONFIGS):
        tm = min(1024, S // dil)
        geo.append((dil, tm, S // dil // tm))
    w_cols = [pl.BlockSpec((None, D_MODEL, IN_A_TN), lambda m, j, k, g=g: ((4 * g + j) // 3, 0, (4 * g + j) % 3))
              for g in range(3)]
    qkv = [_matmul(h1[g], w["w_in_a"], "nn", BF16, f"in_a{g}", shape=(S, 3 * D_MODEL, D_MODEL),
                   tiles=(tm, IN_A_TN, D_MODEL),
                   a_spec=pl.BlockSpec((tm, D_MODEL), lambda m, j, k, n=n_i: (m % n, m // n)), b_spec=w_cols[g],
                   o_spec=pl.BlockSpec((tm, IN_A_TN), lambda m, j, k, n=n_i: (m % n, (m // n) * 4 + j)),
                   o_shape=(S // dil, dil * 3 * D_MODEL))
           for g, (dil, tm, n_i) in enumerate(geo)]
    bias = _bias_build(w["rel_bias"], buckets, "bias_build")
    os_, ls_ = zip(*[_attn_fwd(qkv[g], bias, g, f"attn_fwd{g}") for g in range(3)])
    omix, omix4, omix16, ltot, ltot4, ltot16 = _mix_fwd(os_, ls_, "mix_fwd")
    y0 = _matmul(omix, w["w_out_a"], "nn", F32, "out_a")
    x1, h2 = _norm_mod(x, w["norm_ffn"][0:1], mods[0], J_SC2, J_SH2, "norm_ffn0", resid=y0, gate_mod=mods[0],
                       j_gate=J_G1)
    u0, act0, f0 = _ffn_fwd(h2, w, 0, "0")

    x2, h3 = _norm_mod(x1, w["norm_mix"][1:2], mods[1], J_SC1, J_SH1, "norm_mix1", resid=f0, gate_mod=mods[0],
                       j_gate=J_G2)
    proj = _matmul(h3, w["w_in_b"], "nn", F32, "in_b", tiles=(512, B_IN_PAD, D_MODEL))
    o_gla, states = _gla_fwd(proj, w["w_gate_b"], w["b_gate_b"], "gla_fwd")
    on = _gla_out(o_gla, proj, w["gnorm_b"], "gla_out")
    y1 = _matmul(on, w["w_out_b"], "nn", F32, "out_b")
    x3, h4 = _norm_mod(x2, w["norm_ffn"][1:2], mods[1], J_SC2, J_SH2, "norm_ffn1", resid=y1, gate_mod=mods[1],
                       j_gate=J_G1)
    u1, act1, f1 = _ffn_fwd(h4, w, 1, "1")

    dx4, loss, d_norm_final, df1, dg2_1 = _final_loss(x3, f1, mods[1], J_G2, w["norm_final"], tgt, "final_loss")

    dx3, (dy1, dg1_1), (dsh2_1, dsc2_1), g_ffn1 = _ffn_bwd(dx4, df1, u1, act1, h4, x3, mods[1], w, 1, "1", None,
                                                           (y1, mods[1], J_G1))
    d_on = _matmul(dy1, w["w_out_b"], "nt", F32, "out_b_dx")
    d_w_out_b = _matmul(on, dy1, "tn", F32, "out_b_dw")
    d_ogla, d_r, d_gnorm = _gla_out_bwd(o_gla, proj, w["gnorm_b"], d_on, "gla_out_bwd")
    dq, dk, dv, dz, d_b_gate = _gla_bwd(proj, w["w_gate_b"], w["b_gate_b"], states, d_ogla, "gla_bwd")
    d_glr = _matmul(dz, w["w_gate_b"], "nt", BF16, "gate_dx")
    d_w_gate = _matmul(proj[:, 3072:3200], dz, "tn", F32, "gate_dw")
    dproj = jnp.concatenate([dq, dk, dv, d_r, d_glr], axis=1)
    dh3 = _matmul(dproj, w["w_in_b"], "nt", F32, "in_b_dx", tiles=(1024, D_MODEL, B_IN_PAD))
    d_w_in_b = _matmul(h3, dproj, "tn", F32, "in_b_dw")
    dx2, dsc1_1, dsh1_1, d_nmix1, df0, dg2_0 = _norm_mod_bwd(
        dh3, x2, dx3, w["norm_mix"][1:2], mods[1], J_SC1, "norm_mix_bwd1", branch=(f0, mods[0], J_G2))
    dmod1 = jnp.concatenate([dsh1_1, dsc1_1, dg1_1, dsh2_1, dsc2_1, dg2_1], axis=1)

    side_fn = None if early is None else (
        lambda up, down: early[0](dict(w_in_b=d_w_in_b, w_out_b=d_w_out_b, w_up=up, w_down=down)))
    dx1, (dy0, dg1_0), (dsh2_0, dsc2_0), g_ffn0 = _ffn_bwd(dx2, df0, u0, act0, h2, x1, mods[0], w, 0, "0", g_ffn1,
                                                           (y0, mods[0], J_G1), side_fn)
    d_omix = _matmul(dy0, w["w_out_a"], "nt", F32, "out_a_dx")
    d_w_out_a = _matmul(omix, dy0, "tn", F32, "out_a_dw")
    d_omix4, d_omix16 = _to_views(d_omix, "d_omix_views")
    d_omix_v, omix_v, ltot_v = (d_omix, d_omix4, d_omix16), (omix, omix4, omix16), (ltot, ltot4, ltot16)
    early_sums = None if early is None else early[1](*g_ffn0["side"])
    res0 = _attn_bwd(qkv[0], bias, d_omix, omix, ltot, 0, "attn_bwd0", exchange=early_sums)
    early_arrived = None if early is None else res0[2]
    dqkv, dbs = zip(res0[:2], *[_attn_bwd(qkv[g], bias, d_omix_v[g], omix_v[g], ltot_v[g], g, f"attn_bwd{g}")
                                for g in (1, 2)])
    d_rel_bias = _bias_bwd(jnp.concatenate(dbs, axis=0), buckets, "bias_bwd")
    dh1, d_w_in_a = [], None
    for g, (dil, tm, n_i) in enumerate(geo):
        dh1.append(_matmul(
            dqkv[g], w["w_in_a"], "nt", F32, f"in_a_dx{g}", shape=(S, D_MODEL, 3 * D_MODEL),
            tiles=(tm, D_MODEL, IN_A_TN),
            a_spec=pl.BlockSpec((tm, IN_A_TN), lambda m, j, k, n=n_i: (m % n, (m // n) * 4 + k)),
            b_spec=pl.BlockSpec((None, D_MODEL, IN_A_TN), lambda m, j, k, g=g: ((4 * g + k) // 3, j, (4 * g + k) % 3)),
            o_spec=pl.BlockSpec((tm, D_MODEL), lambda m, j, k, n=n_i: (m % n, m // n)),
            o_shape=(S // dil, dil * D_MODEL)))
        tk = min(TOKEN_TK, S // dil)
        n_k = S // dil // tk
        d_w_in_a = _matmul(
            h1[g], dqkv[g], "tn", F32, f"in_a_dw{g}", shape=(D_MODEL, 3 * D_MODEL, S), tiles=(D_MODEL, IN_A_TN, tk),
            a_spec=pl.BlockSpec((tk, D_MODEL), lambda m, j, k, n=n_k: (k % n, k // n)),
            b_spec=pl.BlockSpec((tk, IN_A_TN), lambda m, j, k, n=n_k: (k % n, (k // n) * 4 + j)),
            o_spec=pl.BlockSpec((None, D_MODEL, IN_A_TN), lambda m, j, k, g=g: ((4 * g + j) // 3, m, (4 * g + j) % 3)),
            o_shape=(4, D_MODEL, 9 * D_MODEL // 4), prev=d_w_in_a)
    dx0, dsc1_0, dsh1_0, d_nmix0 = _norm_mod_bwd(dh1[0], x, dx1, w["norm_mix"][0:1], mods[0], J_SC1, "norm_mix_bwd0",
                                                 dh_views=dh1[1:])
    dmod0 = jnp.concatenate([dsh1_0, dsc1_0, dg1_0, dsh2_0, dsc2_0, dg2_0], axis=1)

    grads = dict(
        w_in_a=d_w_in_a, w_out_a=d_w_out_a, rel_bias=d_rel_bias, w_in_b=d_w_in_b, w_gate_b=d_w_gate,
        b_gate_b=d_b_gate, gnorm_b=d_gnorm, w_out_b=d_w_out_b,
        norm_mix=jnp.concatenate([d_nmix0, d_nmix1], axis=0),
        norm_ffn=jnp.concatenate([g_ffn0["norm_ffn"], g_ffn1["norm_ffn"]], axis=0),
        b_ada=jnp.concatenate([dmod0, dmod1], axis=0),
        w_up=g_ffn0["w_up"],
        conv_w=jnp.stack([g_ffn0["conv_w"], g_ffn1["conv_w"]]),
        conv_b=jnp.concatenate([g_ffn0["conv_b"], g_ffn1["conv_b"]], axis=0),
        w_down=g_ffn0["w_down"],
        norm_final=d_norm_final)
    return loss, dx0, grads, early_sums, early_arrived


N_CHIPS = 4
N_DEV = 8
WEIGHTS = ("w_in_a", "w_out_a", "rel_bias", "w_in_b", "w_gate_b", "b_gate_b", "gnorm_b", "w_out_b", "norm_mix",
           "norm_ffn", "w_ada", "b_ada", "w_up", "conv_w", "conv_b", "w_down", "norm_final")
SHARD_AXIS = dict(w_in_a=2, w_out_a=1, w_in_b=2, w_gate_b=2, b_gate_b=1, gnorm_b=1, w_out_b=1, w_ada=2, w_up=2,
                  conv_w=2, w_down=1)
SHARDED = tuple(n for n in WEIGHTS if n in SHARD_AXIS)
REPLICATED = tuple(n for n in WEIGHTS if n not in SHARD_AXIS)
BIG = ("w_in_a", "w_out_a", "w_in_b", "w_out_b", "w_up", "w_down")
SMALL = ("w_gate_b", "b_gate_b", "gnorm_b", "conv_w")
SMALL_FULL = dict(w_gate_b=(1, 16, 512), b_gate_b=(1, 512), gnorm_b=(1, 256), conv_w=(2, 3, 5632))
R_SMALL = 16
R_TINY = 72
LOSS_SLOT = 72960
R_TINY_SHARD = 40
W_IN_B_PAD = 896

COMM_VIEW = dict(
    w_in_a=((4096, 2304), 1024, 512, 512),
    w_out_a=((1024, 1024), 256, 128, 128),
    w_in_b=((4096, W_IN_B_PAD), 1024, 512, 512),
    w_out_b=((1024, 1024), 256, 128, 128),
    w_up=((8192, 1408), 2048, 1024, 1024),
    w_down=((5632, 1024), 704, 2816, 704))


def _pack(arrs, rows):
    flat = jnp.concatenate([a.reshape(-1) for a in arrs])
    return jnp.pad(flat, (0, rows * LANES - flat.shape[0])).reshape(rows, LANES)


def _unpack(flat2d, shapes):
    flat = flat2d.reshape(-1)
    out, off = [], 0
    for shp in shapes:
        n = math.prod(shp)
        out.append(flat[off:off + n].reshape(shp))
        off += n
    return out


def _chip_slice(a, axis, k):
    n = a.shape[axis] // N_CHIPS
    return lax.slice_in_dim(a, k * n, (k + 1) * n, axis=axis)


def _place():
    mx, my, mc = lax.axis_index("x"), lax.axis_index("y"), lax.axis_index("c")
    chips = [(1 - mx, my), (mx, 1 - my), (1 - mx, 1 - my)]
    return mx, my, mc, chips


def _rcopy(src, dst, send_sem, recv_sem, dev):
    return pltpu.make_async_remote_copy(src_ref=src, dst_ref=dst, send_sem=send_sem, recv_sem=recv_sem,
                                        device_id=dev, device_id_type=MESH)


def _comm_call(body, name, ins, out_shapes, n_sems, in_place=False):
    n_in, n_out = len(ins), len(out_shapes)

    def wrapped(*refs):
        body(refs[:n_in], refs[n_in:n_in + n_out], *refs[n_in + n_out:])

    return pl.pallas_call(
        wrapped, name=name, in_specs=[HBM] * n_in, out_specs=[HBM] * n_out, out_shape=out_shapes,
        input_output_aliases={i: i for i in range(n_in)} if in_place else {},
        scratch_shapes=[pltpu.SemaphoreType.DMA((n_sems,)), pltpu.SemaphoreType.DMA((n_sems,))])(*ins)


DMA_CHUNK_BYTES = 2 * 1024 * 1024


def _rows(ref, start, size):
    return ref.at[pl.ds(pl.multiple_of(start, 16), size), :]


def _block(ref, name, k, h):
    _, bk, bh, nr = COMM_VIEW[name]
    return _rows(ref, bk * k + bh * h, nr)


def _chunks(nr, row_bytes):
    n = 1
    while nr % (2 * n) == 0 and (nr // (2 * n)) % 16 == 0 and (nr // n) * row_bytes > DMA_CHUNK_BYTES:
        n *= 2
    return [(i * (nr // n), nr // n) for i in range(n)]


def _gather_big(views, name):
    names = BIG

    def body(x_refs, out_refs, send_sems, recv_sems):
        mx, my, mc, chips = _place()
        chip = 2 * mx + my
        sibling = (mx, my, 1 - mc)
        sends = []
        for a, n in enumerate(names):
            for j, (cx, cy) in enumerate(chips):
                blk = _block(out_refs[a], n, chip, mc)
                cp = _rcopy(blk, blk, send_sems.at[6 * a + j], recv_sems.at[6 * a + j], (cx, cy, mc))
                cp.start()
                sends.append(cp)
        for a, n in enumerate(names):
            for j, (cx, cy) in enumerate(chips):
                blk = _block(out_refs[a], n, 2 * cx + cy, mc)
                _rcopy(blk, blk, send_sems.at[6 * a + j], recv_sems.at[6 * a + j], sibling).wait_recv()
                cp = _rcopy(blk, blk, send_sems.at[6 * a + 3 + j], recv_sems.at[6 * a + 3 + j], sibling)
                cp.start()
                sends.append(cp)
        for a, n in enumerate(names):
            for j, (cx, cy) in enumerate(chips):
                blk = _block(out_refs[a], n, 2 * cx + cy, 1 - mc)
                _rcopy(blk, blk, send_sems.at[6 * a + 3 + j], recv_sems.at[6 * a + 3 + j], sibling).wait_recv()
        for cp in sends:
            cp.wait_send()

    outs = _comm_call(body, name, [views[n] for n in names],
                      [jax.ShapeDtypeStruct(views[n].shape, views[n].dtype) for n in names], 6 * len(names),
                      in_place=True)
    return dict(zip(names, outs))


def _pair_exchange(names, g_refs, recv_refs, send_sems, recv_sems, start, wait):
    mx, my, mc, _ = _place()
    sibling = (mx, my, 1 - mc)
    if start:
        for a, n in enumerate(names):
            (_, cols), _, _, nr = COMM_VIEW[n]
            for k in range(N_CHIPS):
                src = _block(g_refs[a], n, k, 1 - mc)
                for lo, size in _chunks(nr, cols * 4):
                    _rcopy(src.at[pl.ds(lo, size), :], recv_refs[a].at[k, pl.ds(lo, size), :],
                           send_sems.at[a], recv_sems.at[a], sibling).start()
    if wait:
        for a in range(len(names)):
            _rcopy(recv_refs[a], recv_refs[a], send_sems.at[a], recv_sems.at[a], sibling).wait()


def _pair_recv_shapes(names):
    return [jax.ShapeDtypeStruct((N_CHIPS, COMM_VIEW[n][3], COMM_VIEW[n][0][1]), F32) for n in names]


def _rs_pair_exchange(views, name):
    names = tuple(views)

    def body(g_refs, recv_refs, send_sems, recv_sems):
        _pair_exchange(names, g_refs, recv_refs, send_sems, recv_sems, True, True)

    outs = _comm_call(body, name, [views[n] for n in names], _pair_recv_shapes(names), len(names))
    return dict(zip(names, outs))


def _pair_add(view, recv, c_idx, n, out_dtype, name):
    (_, cols), bk, bh, nr = COMM_VIEW[n]
    tr = _pick(math.gcd(bk, bh, nr), 256, 8)

    def body(c_ref, g_ref, r_ref, o_ref):
        o_ref[...] = (g_ref[...] + r_ref[...]).astype(o_ref.dtype)

    piece = pl.BlockSpec((None, tr, cols), lambda k, i, c_ref: (k, i, 0))
    return pl.pallas_call(
        body, name=name,
        grid_spec=pltpu.PrefetchScalarGridSpec(
            num_scalar_prefetch=1, grid=(N_CHIPS, nr // tr),
            in_specs=[pl.BlockSpec((tr, cols), lambda k, i, c_ref: ((bk * k + bh * c_ref[0]) // tr + i, 0)), piece],
            out_specs=piece),
        out_shape=jax.ShapeDtypeStruct((N_CHIPS, nr, cols), out_dtype), compiler_params=_params(2))(
            c_idx, view, recv)


def _chip_exchange(q_refs, out_refs, send_sems, recv_sems, start, wait):
    mx, my, mc, chips = _place()
    chip = 2 * mx + my
    sends = [_rcopy(q_refs[a].at[2 * cx + cy], out_refs[a].at[chip], send_sems.at[3 * a + j], recv_sems.at[3 * a + j],
                    (cx, cy, mc)) for a in range(len(q_refs)) for j, (cx, cy) in enumerate(chips)]
    if start:
        for cp in sends:
            cp.start()
    if wait:
        for a in range(len(q_refs)):
            for j, (cx, cy) in enumerate(chips):
                blk = out_refs[a].at[2 * cx + cy]
                _rcopy(blk, blk, send_sems.at[3 * a + j], recv_sems.at[3 * a + j], (cx, cy, mc)).wait_recv()
        for cp in sends:
            cp.wait_send()


def _rs_chip_exchange(q, name):
    names = tuple(q)

    def body(q_refs, out_refs, send_sems, recv_sems):
        _chip_exchange(q_refs, out_refs, send_sems, recv_sems, True, True)

    outs = _comm_call(body, name, [q[n] for n in names],
                      [jax.ShapeDtypeStruct(q[n].shape, q[n].dtype) for n in names], 3 * len(names))
    return dict(zip(names, outs))


def _rs_pair_gather(r, name):
    names = tuple(r)

    def body(r_refs, out_refs, send_sems, recv_sems):
        mx, my, mc, _ = _place()
        sibling = (mx, my, 1 - mc)
        for a, n in enumerate(names):
            (_, cols), _, _, nr = COMM_VIEW[n]
            for start, size in _chunks(nr, cols * 4):
                rows = _rows(out_refs[a], mc * nr + start, size)
                _rcopy(rows, rows, send_sems.at[a], recv_sems.at[a], sibling).start()
        for a, n in enumerate(names):
            nr = COMM_VIEW[n][3]
            _rcopy(_rows(out_refs[a], mc * nr, nr), _rows(out_refs[a], (1 - mc) * nr, nr), send_sems.at[a],
                   recv_sems.at[a], sibling).wait()

    outs = _comm_call(body, name, [r[n] for n in names],
                      [jax.ShapeDtypeStruct(r[n].shape, F32) for n in names], len(names), in_place=True)
    return dict(zip(names, outs))


def _gather8(x, reduce, name):
    rows = x.shape[0]

    def body(x_ref, out_ref, *rest):
        if reduce:
            buf_ref, send_sems, recv_sems = rest
        else:
            (send_sems, recv_sems), buf_ref = rest, out_ref
        mx, my, mc, _ = _place()
        me = 4 * mx + 2 * my + mc
        buf_ref[me] = x_ref[...]
        peers = []
        for j in range(1, N_DEV):
            px = 1 - mx if j & 4 else mx
            py = 1 - my if j & 2 else my
            pc = 1 - mc if j & 1 else mc
            peers.append((px, py, pc))
        sends = [_rcopy(x_ref, buf_ref.at[me], send_sems.at[j], recv_sems.at[j], p) for j, p in enumerate(peers)]
        for cp in sends:
            cp.start()
        for j, (px, py, pc) in enumerate(peers):
            _rcopy(x_ref, buf_ref.at[4 * px + 2 * py + pc], send_sems.at[j], recv_sems.at[j], (px, py, pc)).wait_recv()
        for cp in sends:
            cp.wait_send()
        if reduce:
            acc = buf_ref[0]
            for d in range(1, N_DEV):
                acc = acc + buf_ref[d]
            out_ref[...] = acc

    vmem = pl.BlockSpec(memory_space=pltpu.VMEM)
    sems = [pltpu.SemaphoreType.DMA((N_DEV - 1,)), pltpu.SemaphoreType.DMA((N_DEV - 1,))]
    if reduce:
        out_shape = jax.ShapeDtypeStruct((rows, LANES), F32)
        scratch = [pltpu.VMEM((N_DEV, rows, LANES), F32)] + sems
    else:
        out_shape = jax.ShapeDtypeStruct((N_DEV, rows, LANES), F32)
        scratch = sems
    return pl.pallas_call(body, name=name, in_specs=[vmem], out_specs=vmem, out_shape=out_shape,
                          scratch_shapes=scratch)(x)


def _sum4(p, q, chip, core, name):
    _, nr, cols = p.shape
    tr = _pick(nr, 256, 8)

    def body(chip_ref, core_ref, p0, p1, p2, p3, own, o_ref):
        s = [jnp.where(chip_ref[0] == k, own[...], pk[...]).astype(F32) for k, pk in enumerate((p0, p1, p2, p3))]
        o_ref[...] = ((s[0] + s[1]) + s[2]) + s[3]

    return pl.pallas_call(
        body, name=name,
        grid_spec=pltpu.PrefetchScalarGridSpec(
            num_scalar_prefetch=2, grid=(nr // tr,),
            in_specs=[pl.BlockSpec((None, tr, cols), lambda i, ch, co, k=k: (jnp.where(ch[0] == k, k ^ 1, k), i, 0))
                      for k in range(N_CHIPS)]
            + [pl.BlockSpec((None, tr, cols), lambda i, ch, co: (ch[0], i, 0))],
            out_specs=pl.BlockSpec((tr, cols), lambda i, ch, co: (co[0] * (nr // tr) + i, 0))),
        out_shape=jax.ShapeDtypeStruct((2 * nr, cols), F32), compiler_params=_params(1))(chip, core, p, p, p, p, q)


def _place_shard(shard, chip, n, name):
    (rows, cols), bk, bh, nr = COMM_VIEW[n]
    tr = _pick(math.gcd(bk, bh, nr), 256, 16)

    def body(chip_ref, x_ref, o_ref):
        o_ref[...] = x_ref[...].astype(BF16)

    return pl.pallas_call(
        body, name=name,
        grid_spec=pltpu.PrefetchScalarGridSpec(
            num_scalar_prefetch=1, grid=(2, nr // tr),
            in_specs=[pl.BlockSpec((tr, cols), lambda h, i, ch: (h * (nr // tr) + i, 0))],
            out_specs=pl.BlockSpec((tr, cols), lambda h, i, ch: ((bk * ch[0] + bh * h) // tr + i, 0))),
        out_shape=jax.ShapeDtypeStruct((rows, cols), BF16), compiler_params=_params(2))(chip, shard)


def _adamw(w, g, m, v, name):
    rows, cols = w.shape
    tr = _pick(rows, max(8, (1 << 20) // (4 * cols)), 8)

    def body(w_ref, g_ref, m_ref, v_ref, d_ref, mo_ref, vo_ref):
        gv = g_ref[...]
        mn = ADAM_B1 * m_ref[...] + (1.0 - ADAM_B1) * gv
        vn = ADAM_B2 * v_ref[...] + (1.0 - ADAM_B2) * (gv * gv)
        m_hat = mn / (1.0 - ADAM_B1 ** ADAM_STEP)
        v_hat = vn / (1.0 - ADAM_B2 ** ADAM_STEP)
        d_ref[...] = -ADAM_LR * (m_hat / (jnp.sqrt(v_hat) + ADAM_EPS) + ADAM_WD * w_ref[...])
        mo_ref[...] = mn
        vo_ref[...] = vn

    shape = jax.ShapeDtypeStruct(w.shape, F32)
    return pl.pallas_call(
        body, name=name, grid=(rows // tr,), in_specs=[_row(tr, cols)] * 4, out_specs=[_row(tr, cols)] * 3,
        out_shape=[shape] * 3, compiler_params=_params(1))(w, g, m, v)


W_IN_B_SHARD = 772


def _shard_view(n, a):
    return a.reshape(-1, a.shape[-1])


def _gather_weights(p, chip):
    shards = {n: _shard_view(n, p[n]) for n in BIG}
    shards["w_in_b"] = jnp.pad(shards["w_in_b"], ((0, 0), (0, W_IN_B_PAD - W_IN_B_SHARD)))
    big = _gather_big({n: _place_shard(shards[n], chip, n, f"place_{n}") for n in BIG}, "gather_weights")
    small = _gather8(_pack([p[n] for n in SMALL], R_SMALL), False, "gather_small")
    pieces = [_unpack(small[2 * k], [p[n].shape for n in SMALL]) for k in range(N_CHIPS)]
    full = {n: jnp.concatenate([pieces[k][i] for k in range(N_CHIPS)], axis=SHARD_AXIS[n])
            for i, n in enumerate(SMALL)}
    wb = big["w_in_b"].reshape(N_CHIPS, D_MODEL, W_IN_B_PAD)
    wb = jnp.concatenate([wb[k, :, :W_IN_B_SHARD] for k in range(N_CHIPS)], axis=1)
    return dict(
        w_in_a=big["w_in_a"].reshape(N_CHIPS, D_MODEL, -1), w_out_a=big["w_out_a"], w_out_b=big["w_out_b"],
        w_in_b=jnp.concatenate([wb[:, :2048], wb[:, 2064:3088], wb[:, 2048:2064],
                                jnp.zeros((D_MODEL, B_IN_PAD - 3088), BF16)], axis=1),
        w_up=big["w_up"].reshape(N_CHIPS, 2 * D_MODEL, -1), w_down=big["w_down"],
        w_gate_b=jnp.pad(full["w_gate_b"][0], ((0, 128 - B_GATE_RANK), (0, 0))),
        b_gate_b=full["b_gate_b"], gnorm_b=full["gnorm_b"], conv_w=full["conv_w"],
        rel_bias=p["rel_bias"], norm_mix=p["norm_mix"], norm_ffn=p["norm_ffn"], b_ada=p["b_ada"],
        conv_b=p["conv_b"], norm_final=p["norm_final"].reshape(1, D_MODEL))


EARLY = ("w_in_b", "w_out_b", "w_up", "w_down")
LATE = ("w_in_a", "w_out_a")


def _grad_views(g, names):
    views = {n: g[n].reshape(COMM_VIEW[n][0]) for n in names if n != "w_in_b"}
    if "w_in_b" in names:
        gb = g["w_in_b"]
        gb = jnp.concatenate([gb[:, :2048], gb[:, 3072:3088], gb[:, 2048:3072]], axis=1)
        gb = jnp.stack([jnp.pad(gb[:, k * W_IN_B_SHARD:(k + 1) * W_IN_B_SHARD],
                                ((0, 0), (0, W_IN_B_PAD - W_IN_B_SHARD))) for k in range(N_CHIPS)])
        views["w_in_b"] = gb.reshape(COMM_VIEW["w_in_b"][0])
    return views


def _tiny_grads(g):
    out = {n: g[n] for n in REPLICATED if n != "norm_final"}
    out.update(norm_final=g["norm_final"].reshape(D_MODEL), w_gate_b=g["w_gate_b"][:B_GATE_RANK][None],
               b_gate_b=g["b_gate_b"], gnorm_b=g["gnorm_b"], conv_w=g["conv_w"])
    return out


def kernel(x, c, w_in_a, w_out_a, rel_bias, w_in_b, w_gate_b, b_gate_b, gnorm_b, w_out_b, norm_mix, norm_ffn, w_ada, b_ada, w_up, conv_w, conv_b, w_down, norm_final, loss_target, m_w_in_a, m_w_out_a, m_rel_bias, m_w_in_b, m_w_gate_b, m_b_gate_b, m_gnorm_b, m_w_out_b, m_norm_mix, m_norm_ffn, m_w_ada, m_b_ada, m_w_up, m_conv_w, m_conv_b, m_w_down, m_norm_final, v_w_in_a, v_w_out_a, v_rel_bias, v_w_in_b, v_w_gate_b, v_b_gate_b, v_gnorm_b, v_w_out_b, v_norm_mix, v_norm_ffn, v_w_ada, v_b_ada, v_w_up, v_conv_w, v_conv_b, v_w_down, v_norm_final):
    p = dict(zip(WEIGHTS, (w_in_a, w_out_a, rel_bias, w_in_b, w_gate_b, b_gate_b, gnorm_b, w_out_b, norm_mix,
                           norm_ffn, w_ada, b_ada, w_up, conv_w, conv_b, w_down, norm_final)))
    pm = dict(zip(WEIGHTS, (m_w_in_a, m_w_out_a, m_rel_bias, m_w_in_b, m_w_gate_b, m_b_gate_b, m_gnorm_b, m_w_out_b,
                            m_norm_mix, m_norm_ffn, m_w_ada, m_b_ada, m_w_up, m_conv_w, m_conv_b, m_w_down,
                            m_norm_final)))
    pv = dict(zip(WEIGHTS, (v_w_in_a, v_w_out_a, v_rel_bias, v_w_in_b, v_w_gate_b, v_b_gate_b, v_gnorm_b, v_w_out_b,
                            v_norm_mix, v_norm_ffn, v_w_ada, v_b_ada, v_w_up, v_conv_w, v_conv_b, v_w_down,
                            v_norm_final)))
    S = x.shape[1]

    chip = 2 * lax.axis_index("x") + lax.axis_index("y")
    core = lax.axis_index("c").astype(jnp.int32).reshape(1)
    chip_s = chip.astype(jnp.int32).reshape(1)

    me = 2 * chip + lax.axis_index("c")

    def own_columns(a):
        return lax.dynamic_slice_in_dim(a, chip * ADA_TN, ADA_TN, axis=a.ndim - 1)

    c_all = _gather8(jnp.pad(c, ((0, 7), (0, 0))), False, "gather_c")[:, 0, :]
    mod_cols = _ada_mod_shard(c_all, w_ada, own_columns(b_ada)[:, None, :], "ada_mod")
    mod_all = _gather8(mod_cols.reshape(-1, LANES), False, "gather_mod").reshape(N_CHIPS, 2, 2, N_DEV, ADA_TN)
    mod_mine = lax.dynamic_index_in_dim(mod_all[:, 0], me, axis=2, keepdims=False)
    mods = [jnp.broadcast_to(mod_mine[:, l].reshape(1, 6 * D_MODEL), (8, 6 * D_MODEL)) for l in range(2)]

    w = _gather_weights(p, chip_s)
    def pair_add(views, recv):
        return {n: _pair_add(views[n], recv[n], core, n, BF16, f"grads_pair_add_{n}") for n in views}

    loss, dx0, grads, pair, from_chips = _local_step(
        x.reshape(S, D_MODEL), loss_target.reshape(S, D_MODEL), w, mods,
        early=(lambda g: _grad_views(g, EARLY), pair_add))

    dmod_rows = 2 * 6 * D_MODEL // LANES
    dmod_all = _gather8(jnp.pad(grads["b_ada"].reshape(dmod_rows, LANES), ((0, 16 - dmod_rows), (0, 0))), False,
                        "gather_dmod")[:, :dmod_rows].reshape(N_DEV, 2, 6 * D_MODEL)
    g_w_ada = _ada_grad_shard(c_all, own_columns(dmod_all).transpose(1, 0, 2), "ada_grad")

    late_views = _grad_views(grads, LATE)
    late = pair_add(late_views, _rs_pair_exchange(late_views, "grads_pair_exchange"))
    pair.update(late)
    from_chips.update(_rs_chip_exchange(late, "grads_chip_exchange"))
    g_big = _rs_pair_gather({n: _sum4(from_chips[n], pair[n], chip_s, core, f"grads_chip_sum_{n}") for n in BIG},
                            "grads_pair_gather")
    g_big["w_in_b"] = g_big["w_in_b"][:, :W_IN_B_SHARD]

    tiny = _tiny_grads(grads)
    tiny_names = SMALL + REPLICATED
    tiny_full = {n: SMALL_FULL[n] if n in SMALL_FULL else p[n].shape for n in tiny_names}
    tiny_sum = _gather8(_pack([tiny[n] for n in tiny_names] + [loss[0, 0:1]], R_TINY), True, "grads_tiny_sum")
    g_tiny = dict(zip(tiny_names, _unpack(tiny_sum, [tiny_full[n] for n in tiny_names])))
    for n in SMALL:
        width = p[n].shape[SHARD_AXIS[n]]
        g_tiny[n] = lax.dynamic_slice_in_dim(g_tiny[n], chip * width, width, axis=SHARD_AXIS[n])
    total_loss = tiny_sum.reshape(-1)[LOSS_SLOT]

    g_big["w_ada"] = _shard_view("w_ada", g_w_ada)
    out = {}
    for n in BIG + ("w_ada",):
        res = _adamw(_shard_view(n, p[n]), g_big[n], _shard_view(n, pm[n]), _shard_view(n, pv[n]), f"adamw_{n}")
        out[n] = [t.reshape(p[n].shape) for t in (g_big[n],) + tuple(res)]
    res = _adamw(*[_pack([d[n] for n in tiny_names], R_TINY_SHARD) for d in (p, g_tiny, pm, pv)], "adamw_tiny")
    unpacked = [_unpack(t, [p[n].shape for n in tiny_names]) for t in res]
    for i, n in enumerate(tiny_names):
        out[n] = [g_tiny[n]] + [u[i] for u in unpacked]

    return (total_loss, dx0.reshape(x.shape), *[out[n][0] for n in WEIGHTS], *[out[n][1] for n in WEIGHTS],
            *[out[n][2] for n in WEIGHTS], *[out[n][3] for n in WEIGHTS])
```

```python
import functools
import math

import numpy as np
import jax
import jax.numpy as jnp
from jax import lax
from jax.experimental import pallas as pl
from jax.experimental.pallas import tpu as pltpu

F32 = jnp.float32
BF16 = jnp.bfloat16
MESH = pl.DeviceIdType.MESH

D_MODEL = 1024
A_CONFIGS = ((128, 1), (512, 4), (2048, 16))
A_HEADS = 16
A_HEAD_DIM = 64
A_BLK = 128
N_BUCKETS = 32
MAX_DISTANCE = 2048
B_HEADS = 4
B_DK = 128
B_DV = 256
B_QK = 512
B_V = 1024
B_GATE_RANK = 16
B_TAU = 16.0
B_CHUNK = 64
B_IN_PAD = 3200
D_FF = 2816
EPS = 1e-6
NEG_INF = -1e30
ADAM_LR = 0.001
ADAM_B1 = 0.9
ADAM_B2 = 0.999
ADAM_EPS = 1e-08
ADAM_WD = 0.01
ADAM_STEP = 10

LANES = 1024
VMEM_LIMIT = 48 * 1024 * 1024
ROW_TILE = 256
GLA_ROWS = 512

HBM = pl.BlockSpec(memory_space=pl.ANY)


def _params(n_axes):
    return pltpu.CompilerParams(dimension_semantics=("arbitrary",) * n_axes, vmem_limit_bytes=VMEM_LIMIT)


def _pick(n, cap, mult=128):
    best = None
    for t in range(mult, min(n, cap) + 1, mult):
        if n % t == 0:
            best = t
    return n if best is None else best


def _matmul(a, b, mode, out_dtype, name, shape=None, tiles=None, a_spec=None, b_spec=None, o_spec=None, o_shape=None,
            prev=None, add=None):
    dims = {"nn": (((1,), (0,)), ((), ())), "nt": NT, "tn": TN}[mode]
    if shape is None:
        if mode == "nn":
            (M, K), (_, N) = a.shape, b.shape
        elif mode == "nt":
            (M, K), (N, _) = a.shape, b.shape
        else:
            (K, M), (_, N) = a.shape, b.shape
    else:
        M, N, K = shape
    if tiles is None:
        tiles = (_pick(M, 1024, 128 if mode == "tn" else 8), _pick(N, 1536), _pick(K, 1024 if mode != "tn" else 2048))
    tm, tn, tk = tiles
    nk = K // tk
    if a_spec is None:
        a_spec = pl.BlockSpec((tk, tm), lambda i, j, k: (k, i)) if mode == "tn" else pl.BlockSpec(
            (tm, tk), lambda i, j, k: (i, k))
    if b_spec is None:
        b_spec = pl.BlockSpec((tn, tk), lambda i, j, k: (j, k)) if mode == "nt" else pl.BlockSpec(
            (tk, tn), lambda i, j, k: (k, j))
    if o_spec is None:
        o_spec = pl.BlockSpec((tm, tn), lambda i, j, k: (i, j))
        o_shape = (M, N)

    has_add = add is not None

    def body(a_ref, b_ref, *rest):
        part = lax.dot_general(a_ref[...].astype(BF16), b_ref[...].astype(BF16), dims, preferred_element_type=F32)

        def finish(total):
            if has_add:
                total = total + rest[0][...]
            return total.astype(out_dtype)

        if nk == 1:
            rest[-1][...] = finish(part)
            return
        o_ref, acc_ref = rest[-2:]
        k = pl.program_id(2)

        @pl.when(k == 0)
        def _():
            acc_ref[...] = part

        @pl.when(k > 0)
        def _():
            acc_ref[...] += part

        @pl.when(k == nk - 1)
        def _():
            o_ref[...] = finish(acc_ref[...])

    ins, in_specs, aliases = [a, b], [a_spec, b_spec], {}
    if has_add:
        ins.append(add)
        in_specs.append(o_spec)
    if prev is not None:
        aliases = {len(ins): 0}
        ins.append(prev)
        in_specs.append(HBM)
    return pl.pallas_call(
        body, name=name, grid=(M // tm, N // tn, nk), in_specs=in_specs, out_specs=o_spec,
        out_shape=jax.ShapeDtypeStruct(o_shape, out_dtype),
        scratch_shapes=[pltpu.VMEM((tm, tn), F32)] if nk > 1 else [],
        input_output_aliases=aliases, compiler_params=_params(3))(*ins)


def _row(tr, d=D_MODEL):
    return pl.BlockSpec((tr, d), lambda i: (i, 0))


def _vec(d=D_MODEL):
    return pl.BlockSpec((1, d), lambda i: (0, 0))


def _modspec(j):
    return pl.BlockSpec((8, D_MODEL), lambda i: (0, j))


def _silu(x):
    return x * jax.nn.sigmoid(x)


def _dsilu(x):
    s = jax.nn.sigmoid(x)
    return s * (1.0 + x * (1.0 - s))


ADA_TN = 6 * D_MODEL // 4


def _ada_mod_shard(c_all, w_ada, b_ada, name):
    def body(c_ref, w_ref, b_ref, o_ref):
        o_ref[...] = _dot(_silu(c_ref[...]), w_ref[...]) + b_ref[...]

    return pl.pallas_call(
        body, name=name, grid=(2,),
        in_specs=[pl.BlockSpec((N_DEV, D_MODEL), lambda l: (0, 0)),
                  pl.BlockSpec((None, D_MODEL, ADA_TN), lambda l: (l, 0, 0)),
                  pl.BlockSpec((None, 1, ADA_TN), lambda l: (l, 0, 0))],
        out_specs=pl.BlockSpec((None, N_DEV, ADA_TN), lambda l: (l, 0, 0)),
        out_shape=jax.ShapeDtypeStruct((2, N_DEV, ADA_TN), F32), compiler_params=_params(1))(c_all, w_ada, b_ada)


def _ada_grad_shard(c_all, dmod, name):
    def body(c_ref, d_ref, o_ref):
        o_ref[...] = _dot(_silu(c_ref[...]), d_ref[...], TN)

    return pl.pallas_call(
        body, name=name, grid=(2,),
        in_specs=[pl.BlockSpec((N_DEV, D_MODEL), lambda l: (0, 0)),
                  pl.BlockSpec((None, N_DEV, ADA_TN), lambda l: (l, 0, 0))],
        out_specs=pl.BlockSpec((None, D_MODEL, ADA_TN), lambda l: (l, 0, 0)),
        out_shape=jax.ShapeDtypeStruct((2, D_MODEL, ADA_TN), F32), compiler_params=_params(1))(c_all, dmod)


def _view_spec(tr, d, width=D_MODEL):
    return pl.BlockSpec((tr // d, d * width), lambda i: (i, 0))


def _view_shape(S, d, dtype, width=D_MODEL):
    return jax.ShapeDtypeStruct((S // d, d * width), dtype)


LANE_TILE = 128
N_LANE_TILES = D_MODEL // LANE_TILE


def _token_scratch(tr):
    return pltpu.VMEM((N_LANE_TILES, tr, LANE_TILE), F32)


def _scratch_put(scr_ref, val):
    for c in range(N_LANE_TILES):
        scr_ref[c] = val[:, c * LANE_TILE:(c + 1) * LANE_TILE]


def _scratch_get(scr_ref):
    return jnp.concatenate([scr_ref[c] for c in range(N_LANE_TILES)], axis=1)


def _store_view(scr_ref, out_ref, d):
    n = scr_ref.shape[1] // d
    for r in range(d):
        for c in range(N_LANE_TILES):
            lo = r * D_MODEL + c * LANE_TILE
            out_ref[:, lo:lo + LANE_TILE] = scr_ref.at[c][pl.ds(r, n, stride=d), :].astype(out_ref.dtype)


def _load_view(view_ref, scr_ref, d):
    n = scr_ref.shape[1] // d
    for r in range(d):
        for c in range(N_LANE_TILES):
            lo = r * D_MODEL + c * LANE_TILE
            scr_ref.at[c][pl.ds(r, n, stride=d), :] = view_ref[:, lo:lo + LANE_TILE].astype(F32)


def _norm_mod(x, gamma, mod, j_sc, j_sh, name, resid=None, gate_mod=None, j_gate=None, views=False):
    S = x.shape[0]
    tr = ROW_TILE
    has_res = resid is not None

    def body(*refs):
        if has_res:
            x_ref, y_ref, gate_ref, g_ref, sc_ref, sh_ref, xo_ref, h_ref = refs
            xn = x_ref[...] + gate_ref[0:1, :] * y_ref[...]
            xo_ref[...] = xn
        elif views:
            x_ref, g_ref, sc_ref, sh_ref, h_ref, h4_ref, h16_ref, scr_ref = refs
            xn = x_ref[...]
        else:
            x_ref, g_ref, sc_ref, sh_ref, h_ref = refs
            xn = x_ref[...]
        r = lax.rsqrt(jnp.mean(xn * xn, axis=-1, keepdims=True) + EPS)
        n = (xn * r) * g_ref[...]
        h = n * (1.0 + sc_ref[0:1, :]) + sh_ref[0:1, :]
        h_ref[...] = h.astype(BF16)
        if views:
            _scratch_put(scr_ref, h)
            _store_view(scr_ref, h4_ref, 4)
            _store_view(scr_ref, h16_ref, 16)

    scratch = []
    if has_res:
        ins = [x, resid, gate_mod, gamma, mod, mod]
        in_specs = [_row(tr), _row(tr), _modspec(j_gate), _vec(), _modspec(j_sc), _modspec(j_sh)]
        out_specs = [_row(tr), _row(tr)]
        out_shape = [jax.ShapeDtypeStruct((S, D_MODEL), F32), jax.ShapeDtypeStruct((S, D_MODEL), BF16)]
    else:
        ins = [x, gamma, mod, mod]
        in_specs = [_row(tr), _vec(), _modspec(j_sc), _modspec(j_sh)]
        out_specs = _row(tr)
        out_shape = jax.ShapeDtypeStruct((S, D_MODEL), BF16)
        if views:
            out_specs = [_row(tr), _view_spec(tr, 4), _view_spec(tr, 16)]
            out_shape = [out_shape, _view_shape(S, 4, BF16), _view_shape(S, 16, BF16)]
            scratch = [_token_scratch(tr)]
    return pl.pallas_call(body, name=name, grid=(S // tr,), in_specs=in_specs, out_specs=out_specs,
                          out_shape=out_shape, scratch_shapes=scratch, compiler_params=_params(1))(*ins)


def _final_loss(x, resid, mod, j_gate, gamma, tgt, name):
    S = x.shape[0]
    tr = ROW_TILE

    def body(x_ref, y_ref, gate_ref, g_ref, t_ref, dx_ref, loss_ref, dg_ref, dy_ref, dgate_ref):
        @pl.when(pl.program_id(0) == 0)
        def _():
            loss_ref[...] = jnp.zeros_like(loss_ref)
            dg_ref[...] = jnp.zeros_like(dg_ref)
            dgate_ref[...] = jnp.zeros_like(dgate_ref)

        yv = y_ref[...]
        xn = x_ref[...] + gate_ref[0:1, :] * yv
        r = lax.rsqrt(jnp.mean(xn * xn, axis=-1, keepdims=True) + EPS)
        xhat = xn * r
        err = xhat * g_ref[...] - t_ref[...]
        loss_ref[...] += 0.5 * jnp.sum(jnp.mean(err * err, axis=-1, keepdims=True))
        dy = err * (1.0 / D_MODEL)
        dg_ref[...] += jnp.sum(dy * xhat, axis=0, keepdims=True)
        dxh = dy * g_ref[...]
        dx = r * (dxh - xhat * jnp.mean(dxh * xhat, axis=-1, keepdims=True))
        dx_ref[...] = dx
        dy_ref[...] = (gate_ref[0:1, :] * dx).astype(BF16)
        dgate_ref[...] += jnp.sum(dx * yv, axis=0, keepdims=True)

    vec = jax.ShapeDtypeStruct((1, D_MODEL), F32)
    return pl.pallas_call(
        body, name=name, grid=(S // tr,),
        in_specs=[_row(tr), _row(tr), _modspec(j_gate), _vec(), _row(tr)],
        out_specs=[_row(tr), pl.BlockSpec((1, 128), lambda i: (0, 0)), _vec(), _row(tr), _vec()],
        out_shape=[jax.ShapeDtypeStruct((S, D_MODEL), F32), jax.ShapeDtypeStruct((1, 128), F32), vec,
                   jax.ShapeDtypeStruct((S, D_MODEL), BF16), vec],
        compiler_params=_params(1))(x, resid, mod, gamma, tgt)


def _norm_mod_bwd(dh, x, dx_res, gamma, mod, j_sc, name, dh_views=None, branch=None, side=None):
    S = x.shape[0]
    tr = ROW_TILE
    n_views = 0 if dh_views is None else 2
    n_branch = 0 if branch is None else 2
    side_names = () if side is None else tuple(side)
    n_side = len(side_names)
    n_in = n_views + 4 + n_branch + n_side
    n_steps = S // tr

    def body(dh_ref, *refs):
        x_ref, dr_ref, g_ref, sc_ref = refs[n_views:n_views + 4]
        dx_ref, dsc_ref, dsh_ref, dg_ref = refs[n_in:n_in + 4]

        def side_job(start, wait):
            _pair_exchange(side_names, refs[n_in - n_side:n_in], refs[n_in + 4 + n_branch:n_in + 4 + n_branch + n_side],
                           refs[-2], refs[-1], start, wait)

        @pl.when(pl.program_id(0) == 0)
        def _():
            dsc_ref[...] = jnp.zeros_like(dsc_ref)
            dsh_ref[...] = jnp.zeros_like(dsh_ref)
            dg_ref[...] = jnp.zeros_like(dg_ref)
            if n_side:
                side_job(True, False)

        xv = x_ref[...]
        dh_v = dh_ref[...]
        if n_views:
            scr_ref = refs[-3] if n_side else refs[-1]
            for view_ref, d in zip(refs[:2], (4, 16)):
                _load_view(view_ref, scr_ref, d)
                dh_v = dh_v + _scratch_get(scr_ref)
        r = lax.rsqrt(jnp.mean(xv * xv, axis=-1, keepdims=True) + EPS)
        xhat = xv * r
        dsh_ref[...] += jnp.sum(dh_v, axis=0, keepdims=True)
        dsc_ref[...] += jnp.sum(dh_v * (xhat * g_ref[...]), axis=0, keepdims=True)
        dn = dh_v * (1.0 + sc_ref[0:1, :])
        dg_ref[...] += jnp.sum(dn * xhat, axis=0, keepdims=True)
        dxh = dn * g_ref[...]
        dx = dr_ref[...] + r * (dxh - xhat * jnp.mean(dxh * xhat, axis=-1, keepdims=True))
        dx_ref[...] = dx
        if n_branch:
            y_ref, gate_ref = refs[n_views + 4:n_views + 6]
            dy_ref, dgate_ref = refs[n_in + 4:n_in + 6]

            @pl.when(pl.program_id(0) == 0)
            def _():
                dgate_ref[...] = jnp.zeros_like(dgate_ref)

            dy_ref[...] = (gate_ref[0:1, :] * dx).astype(BF16)
            dgate_ref[...] += jnp.sum(dx * y_ref[...], axis=0, keepdims=True)
        if n_side:
            @pl.when(pl.program_id(0) == n_steps - 1)
            def _():
                side_job(False, True)

    vec = jax.ShapeDtypeStruct((1, D_MODEL), F32)
    views = [] if dh_views is None else list(dh_views)
    view_specs = [_view_spec(tr, 4), _view_spec(tr, 16)] if views else []
    ins = [dh, *views, x, dx_res, gamma, mod]
    in_specs = [_row(tr)] + view_specs + [_row(tr), _row(tr), _vec(), _modspec(j_sc)]
    out_specs = [_row(tr), _vec(), _vec(), _vec()]
    out_shape = [jax.ShapeDtypeStruct((S, D_MODEL), F32), vec, vec, vec]
    if branch is not None:
        y, gate_mod, j_gate = branch
        ins += [y, gate_mod]
        in_specs += [_row(tr), _modspec(j_gate)]
        out_specs += [_row(tr), _vec()]
        out_shape += [jax.ShapeDtypeStruct((S, D_MODEL), BF16), vec]
    scratch = [_token_scratch(tr)] if views else []
    if n_side:
        ins += [side[n] for n in side_names]
        in_specs += [HBM] * n_side
        out_specs += [HBM] * n_side
        out_shape += _pair_recv_shapes(side_names)
        scratch += [pltpu.SemaphoreType.DMA((n_side,)), pltpu.SemaphoreType.DMA((n_side,))]
    res = pl.pallas_call(
        body, name=name, grid=(n_steps,), in_specs=in_specs, out_specs=out_specs, out_shape=out_shape,
        scratch_shapes=scratch, compiler_params=_params(1))(*ins)
    if n_side:
        return res[:len(res) - n_side], dict(zip(side_names, res[len(res) - n_side:]))
    return res


def _shift_down(u, halo, s):
    r = pltpu.roll(u, s, 0)
    hr = pltpu.roll(halo, s, 0)
    rid = lax.broadcasted_iota(jnp.int32, hr.shape, 0)
    top = jnp.where(rid < s, hr, r[0:8])
    return jnp.concatenate([top, r[8:]], axis=0)


def _conv3(u, halo, w_ref, b_ref):
    u1 = _shift_down(u, halo, 1)
    u2 = _shift_down(u, halo, 2)
    return b_ref[...] + ((w_ref[0:1, :] * u2 + w_ref[1:2, :] * u1) + w_ref[2:3, :] * u), u1, u2


CONV_TC = 1408


def _conv_specs(tr, S):
    nh = D_FF // CONV_TC
    hb = tr // 8

    def cur(off):
        return pl.BlockSpec((tr, CONV_TC), lambda j, i: (i, j + off))

    def halo(off):
        return pl.BlockSpec((8, CONV_TC), lambda j, i: (jnp.maximum(i * hb - 1, 0), j + off))

    def w(off):
        return pl.BlockSpec((3, CONV_TC), lambda j, i: (0, j + off))

    def b(off):
        return pl.BlockSpec((1, CONV_TC), lambda j, i: (0, j + off))

    return nh, cur, halo, w, b


def _conv_gate(u, conv_w, conv_b, name):
    S = u.shape[0]
    tr = ROW_TILE
    nh, cur, halo, w, b = _conv_specs(tr, S)

    def body(ua_ref, ha_ref, ub_ref, hb_ref, wa_ref, wb_ref, ba_ref, bb_ref, o_ref):
        first = pl.program_id(1) == 0
        ha = jnp.where(first, 0.0, ha_ref[...])
        hbv = jnp.where(first, 0.0, hb_ref[...])
        a, _, _ = _conv3(ua_ref[...], ha, wa_ref, ba_ref)
        bb, _, _ = _conv3(ub_ref[...], hbv, wb_ref, bb_ref)
        o_ref[...] = (_silu(a) * bb).astype(BF16)

    return pl.pallas_call(
        body, name=name, grid=(nh, S // tr),
        in_specs=[cur(0), halo(0), cur(nh), halo(nh), w(0), w(nh), b(0), b(nh)],
        out_specs=pl.BlockSpec((tr, CONV_TC), lambda j, i: (i, j)),
        out_shape=jax.ShapeDtypeStruct((S, D_FF), BF16), compiler_params=_params(2))(
            u, u, u, u, conv_w, conv_w, conv_b, conv_b)


def _conv_gate_bwd(u, dact, conv_w, conv_b, name):
    S = u.shape[0]
    tr = ROW_TILE
    nh, cur, halo, w, b = _conv_specs(tr, S)
    hb = tr // 8
    nlast = S // 8 - 1
    nsteps = S // tr

    def after(off):
        return pl.BlockSpec((8, CONV_TC), lambda j, i: (jnp.minimum((i + 1) * hb, nlast), j + off))

    def body(ua_ref, ha_ref, na_ref, ub_ref, hb_ref, nb_ref, wa_ref, wb_ref, ba_ref, bb_ref, da_ref, dn_ref,
             dua_ref, dub_ref, dwa_ref, dwb_ref, dba_ref, dbb_ref):
        first = pl.program_id(1) == 0
        last = pl.program_id(1) == nsteps - 1

        @pl.when(first)
        def _():
            for r in (dwa_ref, dwb_ref, dba_ref, dbb_ref):
                r[...] = jnp.zeros_like(r)

        ha = jnp.where(first, 0.0, ha_ref[...])
        hbv = jnp.where(first, 0.0, hb_ref[...])
        ua = jnp.concatenate([ua_ref[...], na_ref[...]], axis=0)
        ub = jnp.concatenate([ub_ref[...], nb_ref[...]], axis=0)
        a, ua1, ua2 = _conv3(ua, ha, wa_ref, ba_ref)
        bb, ub1, ub2 = _conv3(ub, hbv, wb_ref, bb_ref)
        dact_v = jnp.concatenate([da_ref[...], jnp.where(last, 0.0, dn_ref[...])], axis=0)
        da = dact_v * bb * _dsilu(a)
        db = dact_v * _silu(a)
        n = tr + 8
        for d, x0, x1, x2, w_ref, du_ref, dw_ref, dbias_ref in (
                (da, ua, ua1, ua2, wa_ref, dua_ref, dwa_ref, dba_ref),
                (db, ub, ub1, ub2, wb_ref, dub_ref, dwb_ref, dbb_ref)):
            d1 = pltpu.roll(d, n - 1, 0)
            d2 = pltpu.roll(d, n - 2, 0)
            du_ref[...] = ((w_ref[2:3, :] * d + w_ref[1:2, :] * d1) + w_ref[0:1, :] * d2)[:tr].astype(BF16)
            dt = d[:tr]
            dbias_ref[...] += jnp.sum(dt, axis=0, keepdims=True)
            dw_ref[0:1, :] += jnp.sum(dt * x2[:tr], axis=0, keepdims=True)
            dw_ref[1:2, :] += jnp.sum(dt * x1[:tr], axis=0, keepdims=True)
            dw_ref[2:3, :] += jnp.sum(dt * x0[:tr], axis=0, keepdims=True)

    half = pl.BlockSpec((tr, CONV_TC), lambda j, i: (i, j))
    half_after = pl.BlockSpec((8, CONV_TC), lambda j, i: (jnp.minimum((i + 1) * hb, nlast), j))
    dw = pl.BlockSpec((8, CONV_TC), lambda j, i: (0, j))
    dbs = pl.BlockSpec((1, CONV_TC), lambda j, i: (0, j))
    f = lambda r, c: jax.ShapeDtypeStruct((r, c), F32)
    du = jax.ShapeDtypeStruct((S, D_FF), BF16)
    return pl.pallas_call(
        body, name=name, grid=(nh, nsteps),
        in_specs=[cur(0), halo(0), after(0), cur(nh), halo(nh), after(nh), w(0), w(nh), b(0), b(nh), half,
                  half_after],
        out_specs=[half, half, dw, dw, dbs, dbs],
        out_shape=[du, du, f(8, D_FF), f(8, D_FF), f(1, D_FF), f(1, D_FF)],
        compiler_params=_params(2))(u, u, u, u, u, u, conv_w, conv_w, conv_b, conv_b, dact, dact)


def _bucket_maps():
    qi = np.arange(A_BLK)[:, None]
    ki = np.arange(2 * A_BLK)[None, :]
    steps = np.clip(qi + A_BLK - ki, 0, A_BLK)
    out = []
    max_exact = N_BUCKETS // 2
    for _, dil in A_CONFIGS:
        dist = steps * dil
        n = np.maximum(dist, max_exact).astype(np.float32)
        large = max_exact + (np.log(n / np.float32(max_exact)) / np.float32(math.log(MAX_DISTANCE / max_exact))
                             * np.float32(N_BUCKETS - max_exact)).astype(np.int32)
        large = np.minimum(large, N_BUCKETS - 1)
        out.append(np.where(dist < max_exact, dist, large))
    return np.stack(out).astype(np.int32)


def _bias_build(rel_bias, buckets, name):
    ng = len(A_CONFIGS)

    def body(t_ref, bk_ref, o_ref):
        gh = pl.program_id(0) * A_HEADS + pl.program_id(1)
        bk = bk_ref[0]
        acc = jnp.zeros((A_BLK, 2 * A_BLK), F32)
        for b in range(N_BUCKETS):
            acc = jnp.where(bk == b, t_ref[b, gh], acc)
        o_ref[0] = acc

    return pl.pallas_call(
        body, name=name, grid=(ng, A_HEADS),
        in_specs=[pl.BlockSpec(memory_space=pltpu.SMEM), pl.BlockSpec((1, A_BLK, 2 * A_BLK), lambda g, h: (g, 0, 0))],
        out_specs=pl.BlockSpec((1, A_BLK, 2 * A_BLK), lambda g, h: (g * A_HEADS + h, 0, 0)),
        out_shape=jax.ShapeDtypeStruct((ng * A_HEADS, A_BLK, 2 * A_BLK), F32),
        compiler_params=_params(2))(rel_bias, buckets)


def _bias_bwd(dbias, buckets, name):
    ng = len(A_CONFIGS)

    def body(d_ref, bk_ref, o_ref):
        gh = pl.program_id(0) * A_HEADS + pl.program_id(1)
        bk = bk_ref[0]
        d = d_ref[0]
        for b in range(N_BUCKETS):
            o_ref[b, gh] = jnp.sum(jnp.where(bk == b, d, 0.0))

    return pl.pallas_call(
        body, name=name, grid=(ng, A_HEADS),
        in_specs=[pl.BlockSpec((1, A_BLK, 2 * A_BLK), lambda g, h: (g * A_HEADS + h, 0, 0)),
                  pl.BlockSpec((1, A_BLK, 2 * A_BLK), lambda g, h: (g, 0, 0))],
        out_specs=pl.BlockSpec(memory_space=pltpu.SMEM),
        out_shape=jax.ShapeDtypeStruct((N_BUCKETS, ng * A_HEADS), F32),
        compiler_params=_params(2))(dbias, buckets)


def _attn_mask(b):
    qi = lax.broadcasted_iota(jnp.int32, (A_BLK, 2 * A_BLK), 0)
    ki = lax.broadcasted_iota(jnp.int32, (A_BLK, 2 * A_BLK), 1)
    band = (ki >= qi) & (ki <= qi + A_BLK)
    return band & ((b > 0) | (ki >= A_BLK))


def _first_head_lanes():
    return lax.broadcasted_iota(jnp.int32, (A_BLK, 2 * A_HEAD_DIM), 1) < A_HEAD_DIM


def _attn_in_specs(g, dil):
    W = A_HEADS * A_HEAD_DIM

    def spec(t, prev, nb):
        def im(r, b):
            bb = jnp.minimum(b, nb - 1)
            if prev:
                bb = jnp.maximum(bb - 1, 0)
            return (bb, r * 3 + t)
        return pl.BlockSpec((A_BLK, W), im)

    return lambda nb: [spec(0, False, nb), spec(1, False, nb), spec(1, True, nb), spec(2, False, nb),
                       spec(2, True, nb)]


def _attn_fwd(qv, bias, g, name):
    _, dil = A_CONFIGS[g]
    L = qv.shape[0]
    nb = L // A_BLK
    W = A_HEADS * A_HEAD_DIM

    def body(q_ref, kc_ref, kp_ref, vc_ref, vp_ref, bias_ref, o_ref, l_ref):
        mask = _attn_mask(pl.program_id(1))
        first = _first_head_lanes()
        for j in range(A_HEADS // 2):
            ps = slice(j * 2 * A_HEAD_DIM, (j + 1) * 2 * A_HEAD_DIM)
            q2 = q_ref[:, ps] * 0.125
            k2 = jnp.concatenate([kp_ref[:, ps], kc_ref[:, ps]], axis=0)
            v2 = jnp.concatenate([vp_ref[:, ps], vc_ref[:, ps]], axis=0)
            o_pair, l_pair = [], []
            for e in range(2):
                qh = jnp.where(first if e == 0 else ~first, q2, jnp.zeros_like(q2))
                s = lax.dot_general(qh, k2, NT, preferred_element_type=F32) + bias_ref[2 * j + e]
                s = jnp.where(mask, s, NEG_INF)
                m = jnp.max(s, axis=-1, keepdims=True)
                p = jnp.exp(s - m)
                den = jnp.sum(p, axis=-1, keepdims=True)
                o_pair.append(jnp.dot(p.astype(BF16), v2, preferred_element_type=F32) / den)
                l_pair.append(m + jnp.log(den))
            o_ref[:, ps] = jnp.where(first, o_pair[0], o_pair[1])
            l_ref[:, ps] = jnp.where(first, l_pair[0], l_pair[1])

    out_spec = pl.BlockSpec((A_BLK, W), lambda r, b: (b, r))
    return pl.pallas_call(
        body, name=name, grid=(dil, nb),
        in_specs=_attn_in_specs(g, dil)(nb) + [pl.BlockSpec((A_HEADS, A_BLK, 2 * A_BLK), lambda r, b: (g, 0, 0))],
        out_specs=[out_spec, out_spec],
        out_shape=[jax.ShapeDtypeStruct((L, dil * W), F32)] * 2,
        compiler_params=_params(2))(qv, qv, qv, qv, qv, bias)


def _mix_fwd(os, ls, name):
    S = os[0].shape[0]
    tr = ROW_TILE

    def body(o0, o1, o2, l0, l1, l2, om_ref, om4_ref, om16_ref, lt_ref, lt4_ref, lt16_ref, s_o1, s_o2, s_l1, s_l2):
        for view_ref, scr_ref, d in ((o1, s_o1, 4), (o2, s_o2, 16), (l1, s_l1, 4), (l2, s_l2, 16)):
            _load_view(view_ref, scr_ref, d)
        a, b, c = l0[...], _scratch_get(s_l1), _scratch_get(s_l2)
        m = jnp.maximum(jnp.maximum(a, b), c)
        ea, eb, ec = jnp.exp(a - m), jnp.exp(b - m), jnp.exp(c - m)
        z = (ea + eb) + ec
        om = ((ea / z) * o0[...] + (eb / z) * _scratch_get(s_o1)) + (ec / z) * _scratch_get(s_o2)
        lt = m + jnp.log(z)
        om_ref[...] = om
        lt_ref[...] = lt
        _scratch_put(s_o1, om)
        _scratch_put(s_l1, lt)
        for scr_ref, v4_ref, v16_ref in ((s_o1, om4_ref, om16_ref), (s_l1, lt4_ref, lt16_ref)):
            _store_view(scr_ref, v4_ref, 4)
            _store_view(scr_ref, v16_ref, 16)

    ins = [_row(tr), _view_spec(tr, 4), _view_spec(tr, 16)]
    outs = [jax.ShapeDtypeStruct((S, D_MODEL), F32), _view_shape(S, 4, F32), _view_shape(S, 16, F32)]
    return pl.pallas_call(
        body, name=name, grid=(S // tr,), in_specs=ins * 2, out_specs=ins * 2, out_shape=outs * 2,
        scratch_shapes=[_token_scratch(tr)] * 4, compiler_params=_params(1))(*os, *ls)


def _to_views(x, name):
    S = x.shape[0]
    tr = ROW_TILE

    def body(x_ref, v4_ref, v16_ref, scr_ref):
        _scratch_put(scr_ref, x_ref[...])
        _store_view(scr_ref, v4_ref, 4)
        _store_view(scr_ref, v16_ref, 16)

    return pl.pallas_call(
        body, name=name, grid=(S // tr,), in_specs=[_row(tr)], out_specs=[_view_spec(tr, 4), _view_spec(tr, 16)],
        out_shape=[_view_shape(S, 4, F32), _view_shape(S, 16, F32)], scratch_shapes=[_token_scratch(tr)],
        compiler_params=_params(1))(x)


def _attn_bwd(qv, bias, d_o, omix, ltot, g, name, exchange=None):
    _, dil = A_CONFIGS[g]
    L = qv.shape[0]
    nb = L // A_BLK
    W = A_HEADS * A_HEAD_DIM
    ex_names = () if exchange is None else tuple(exchange)
    n_ex = len(ex_names)

    def body(q_ref, kc_ref, kp_ref, vc_ref, vp_ref, bias_ref, do_ref, om_ref, lt_ref, *rest):
        ex_in, (dqkv_ref, db_ref), ex_out = rest[:n_ex], rest[n_ex:n_ex + 2], rest[n_ex + 2:2 * n_ex + 2]
        cq_ref, ck_ref, cv_ref = rest[2 * n_ex + 2:2 * n_ex + 5]
        r, b = pl.program_id(0), pl.program_id(1)
        dq_ref, dk_ref, dv_ref = (dqkv_ref.at[:, t * W:(t + 1) * W] for t in range(3))

        @pl.when((r == 0) & (b == 0))
        def _():
            db_ref[...] = jnp.zeros_like(db_ref)
            if n_ex:
                _chip_exchange(ex_in, ex_out, rest[-2], rest[-1], True, False)

        @pl.when(b == 0)
        def _():
            cq_ref[...] = jnp.zeros_like(cq_ref)
            ck_ref[...] = jnp.zeros_like(ck_ref)
            cv_ref[...] = jnp.zeros_like(cv_ref)

        dq_ref[...] = cq_ref[...]

        @pl.when(b < nb)
        def _():
            mask = _attn_mask(b)
            first = _first_head_lanes()
            for j in range(A_HEADS // 2):
                ps = slice(j * 2 * A_HEAD_DIM, (j + 1) * 2 * A_HEAD_DIM)
                q2 = q_ref[:, ps] * 0.125
                k2 = jnp.concatenate([kp_ref[:, ps], kc_ref[:, ps]], axis=0)
                v2 = jnp.concatenate([vp_ref[:, ps], vc_ref[:, ps]], axis=0)
                do2, om2 = do_ref[:, ps], om_ref[:, ps]
                dq_pair, dk2, dv2 = [], None, None
                for e in range(2):
                    mine = first if e == 0 else ~first
                    h = 2 * j + e
                    qh = jnp.where(mine, q2, jnp.zeros_like(q2))
                    s = lax.dot_general(qh, k2, NT, preferred_element_type=F32) + bias_ref[h]
                    s = jnp.where(mask, s, NEG_INF)
                    wp = jnp.exp(s - lt_ref[:, h * A_HEAD_DIM:h * A_HEAD_DIM + 1])
                    do_h = jnp.where(mine, do2, 0.0)
                    t_h = jnp.sum(do_h * om2, axis=-1, keepdims=True)
                    do_b = do_h.astype(BF16)
                    dp = lax.dot_general(do_b, v2, NT, preferred_element_type=F32)
                    ds = wp * (dp - t_h)
                    db_ref[h] += ds
                    ds_b = ds.astype(BF16)
                    dv_e = lax.dot_general(wp.astype(BF16), do_b, TN, preferred_element_type=F32)
                    dk_e = lax.dot_general(ds_b, qh, TN, preferred_element_type=F32)
                    dv2 = dv_e if dv2 is None else dv2 + dv_e
                    dk2 = dk_e if dk2 is None else dk2 + dk_e
                    dq_pair.append(jnp.dot(ds_b, k2, preferred_element_type=F32))
                cq_ref[:, ps] = (jnp.where(first, dq_pair[0], dq_pair[1]) * 0.125).astype(BF16)
                dk_ref[:, ps] = (ck_ref[:, ps] + dk2[:A_BLK]).astype(BF16)
                dv_ref[:, ps] = (cv_ref[:, ps] + dv2[:A_BLK]).astype(BF16)
                ck_ref[:, ps] = dk2[A_BLK:]
                cv_ref[:, ps] = dv2[A_BLK:]

        @pl.when(b == nb)
        def _():
            dk_ref[...] = ck_ref[...].astype(BF16)
            dv_ref[...] = cv_ref[...].astype(BF16)

        if n_ex:
            @pl.when((r == dil - 1) & (b == nb))
            def _():
                _chip_exchange(ex_in, ex_out, rest[-2], rest[-1], False, True)

    act = pl.BlockSpec((A_BLK, W), lambda r, b: (jnp.minimum(b, nb - 1), r))
    lag = pl.BlockSpec((A_BLK, 3 * W), lambda r, b: (jnp.maximum(b - 1, 0), r))
    full = pl.BlockSpec((A_HEADS, A_BLK, 2 * A_BLK), lambda r, b: (0, 0, 0))
    in_specs = _attn_in_specs(g, dil)(nb) + [pl.BlockSpec((A_HEADS, A_BLK, 2 * A_BLK), lambda r, b: (g, 0, 0)),
                                             act, act, act]
    ex_arrays = [exchange[n] for n in ex_names]
    scratch = [pltpu.VMEM((A_BLK, W), BF16), pltpu.VMEM((A_BLK, W), F32), pltpu.VMEM((A_BLK, W), F32)]
    if n_ex:
        scratch += [pltpu.SemaphoreType.DMA((3 * n_ex,)), pltpu.SemaphoreType.DMA((3 * n_ex,))]
    res = pl.pallas_call(
        body, name=name, grid=(dil, nb + 1), in_specs=in_specs + [HBM] * n_ex, out_specs=[lag, full] + [HBM] * n_ex,
        out_shape=[jax.ShapeDtypeStruct((L, dil * 3 * W), BF16),
                   jax.ShapeDtypeStruct((A_HEADS, A_BLK, 2 * A_BLK), F32)]
        + [jax.ShapeDtypeStruct(a.shape, a.dtype) for a in ex_arrays],
        scratch_shapes=scratch, compiler_params=_params(2))(qv, qv, qv, qv, qv, bias, d_o, omix, ltot, *ex_arrays)
    if n_ex:
        return res[0], res[1], dict(zip(ex_names, res[2:]))
    return res


NT = (((1,), (1,)), ((), ()))
TN = (((0,), (0,)), ((), ()))


def _dot(a, b, dims=(((1,), (0,)), ((), ()))):
    return lax.dot_general(a.astype(BF16), b.astype(BF16), dims, preferred_element_type=F32)


def _gla_gates(glr, wg_ref, bg_ref):
    z = _dot(glr, wg_ref[...]) + bg_ref[...]
    log_sig = -(jnp.maximum(-z, 0.0) + jnp.log1p(jnp.exp(-jnp.abs(z))))
    return z, log_sig / B_TAU


def _gla_chunk(q, k, gk):
    row = lax.broadcasted_iota(jnp.int32, (B_CHUNK, B_CHUNK), 0)
    col = lax.broadcasted_iota(jnp.int32, (B_CHUNK, B_CHUNK), 1)
    causal = row >= col
    bcum = jnp.dot(causal.astype(F32), gk, precision=lax.Precision.HIGHEST, preferred_element_type=F32)
    bl = bcum[B_CHUNK - 1:B_CHUNK, :]
    qt = (q * (B_DK ** -0.5)) * jnp.exp(bcum)
    kt = k * jnp.exp(-bcum)
    kd = k * jnp.exp(bl - bcum)
    a = jnp.where(causal, _dot(qt, kt, NT), 0.0)
    return causal, bcum, bl, qt, kt, kd, a


GLA_HP = 2
GLA_QW, GLA_VW = GLA_HP * B_DK, GLA_HP * B_DV


def _gla_specs(tg):
    n_pairs = B_HEADS // GLA_HP
    q = pl.BlockSpec((tg, GLA_QW), lambda h, i: (i, h))
    k = pl.BlockSpec((tg, GLA_QW), lambda h, i: (i, n_pairs + h))
    v = pl.BlockSpec((tg, GLA_VW), lambda h, i: (i, n_pairs + h))
    glr = pl.BlockSpec((tg, 128), lambda h, i: (i, 24))
    wg = pl.BlockSpec((128, GLA_QW), lambda h, i: (0, h))
    bg = pl.BlockSpec((1, GLA_QW), lambda h, i: (0, h))
    return [q, k, v, glr, wg, bg]


def _gla_fwd(proj, w_gate, b_gate, name):
    S = proj.shape[0]
    tg = GLA_ROWS
    nc = tg // B_CHUNK

    def body(q_ref, k_ref, v_ref, glr_ref, wg_ref, bg_ref, o_ref, st_ref, state_ref):
        @pl.when(pl.program_id(1) == 0)
        def _():
            state_ref[...] = jnp.zeros_like(state_ref)

        _, gk_all = _gla_gates(glr_ref[...], wg_ref, bg_ref)
        st = [state_ref[e] for e in range(GLA_HP)]
        for c in range(nc):
            rows = slice(c * B_CHUNK, (c + 1) * B_CHUNK)
            for e in range(GLA_HP):
                ks, vs = slice(e * B_DK, (e + 1) * B_DK), slice(e * B_DV, (e + 1) * B_DV)
                v = v_ref[rows, vs]
                _, _, bl, qt, _, kd, a = _gla_chunk(q_ref[rows, ks], k_ref[rows, ks], gk_all[rows, ks])
                o_ref[rows, vs] = _dot(a, v) + _dot(qt, st[e], NT)
                st_ref[c, e] = st[e]
                st[e] = st[e] * jnp.exp(bl) + _dot(v, kd, TN)
        for e in range(GLA_HP):
            state_ref[e] = st[e]

    return pl.pallas_call(
        body, name=name, grid=(B_HEADS // GLA_HP, S // tg), in_specs=_gla_specs(tg),
        out_specs=[pl.BlockSpec((tg, GLA_VW), lambda h, i: (i, h)),
                   pl.BlockSpec((nc, GLA_HP, B_DV, B_DK), lambda h, i: (i, h, 0, 0))],
        out_shape=[jax.ShapeDtypeStruct((S, B_V), F32),
                   jax.ShapeDtypeStruct((S // B_CHUNK, B_HEADS, B_DV, B_DK), F32)],
        scratch_shapes=[pltpu.VMEM((GLA_HP, B_DV, B_DK), F32)], compiler_params=_params(2))(
            proj, proj, proj, proj, w_gate, b_gate)


def _gla_bwd(proj, w_gate, b_gate, states, d_o, name):
    S = proj.shape[0]
    tg = GLA_ROWS
    nc = tg // B_CHUNK
    ni = S // tg

    def rev(spec):
        return pl.BlockSpec(spec.block_shape, lambda h, i, im=spec.index_map: im(h, ni - 1 - i))

    def body(q_ref, k_ref, v_ref, glr_ref, wg_ref, bg_ref, st_ref, do_ref,
             dq_ref, dk_ref, dv_ref, dz_ref, dbg_ref, dstate_ref):
        @pl.when(pl.program_id(1) == 0)
        def _():
            dstate_ref[...] = jnp.zeros_like(dstate_ref)
            dbg_ref[...] = jnp.zeros_like(dbg_ref)

        z_all, gk_all = _gla_gates(glr_ref[...], wg_ref, bg_ref)
        dsts = [dstate_ref[e] for e in range(GLA_HP)]
        for c in range(nc - 1, -1, -1):
            rows = slice(c * B_CHUNK, (c + 1) * B_CHUNK)
            for e in range(GLA_HP):
                ks, vs = slice(e * B_DK, (e + 1) * B_DK), slice(e * B_DV, (e + 1) * B_DV)
                dst = dsts[e]
                v = v_ref[rows, vs]
                d_out = do_ref[rows, vs]
                st = st_ref[c, e]
                causal, bcum, bl, qt, kt, kd, a = _gla_chunk(q_ref[rows, ks], k_ref[rows, ks], gk_all[rows, ks])
                da = jnp.where(causal, _dot(d_out, v, NT), 0.0)
                dv_ref[rows, vs] = (_dot(a, d_out, TN) + _dot(kd, dst, NT)).astype(BF16)
                dqt = _dot(da, kt) + _dot(d_out, st)
                dkt = _dot(da, qt, TN)
                dkd = _dot(v, dst)
                dec = jnp.exp(bl)
                ddec = jnp.sum(dst * st, axis=0, keepdims=True)
                dsts[e] = dst * dec + _dot(d_out, qt, TN)
                dq_ref[rows, ks] = (dqt * jnp.exp(bcum) * (B_DK ** -0.5)).astype(BF16)
                dk_ref[rows, ks] = (dkt * jnp.exp(-bcum) + dkd * jnp.exp(bl - bcum)).astype(BF16)
                db = (dqt * qt - dkt * kt) - dkd * kd
                dbl = jnp.sum(dkd * kd, axis=0, keepdims=True) + dec * ddec
                upper = (lax.broadcasted_iota(jnp.int32, (B_CHUNK, B_CHUNK), 0)
                         <= lax.broadcasted_iota(jnp.int32, (B_CHUNK, B_CHUNK), 1))
                dgk = jnp.dot(upper.astype(F32), db, precision=lax.Precision.HIGHEST,
                              preferred_element_type=F32) + dbl
                dz = dgk * (1.0 / B_TAU) * jax.nn.sigmoid(-z_all[rows, ks])
                dz_ref[rows, ks] = dz
                dbg_ref[:, ks] += jnp.sum(dz, axis=0, keepdims=True)
        for e in range(GLA_HP):
            dstate_ref[e] = dsts[e]

    qs = pl.BlockSpec((tg, GLA_QW), lambda h, i: (i, h))
    vs_spec = pl.BlockSpec((tg, GLA_VW), lambda h, i: (i, h))
    in_specs = [rev(s) if n < 4 else s for n, s in enumerate(_gla_specs(tg))]
    in_specs += [rev(pl.BlockSpec((nc, GLA_HP, B_DV, B_DK), lambda h, i: (i, h, 0, 0))), rev(vs_spec)]
    return pl.pallas_call(
        body, name=name, grid=(B_HEADS // GLA_HP, ni), in_specs=in_specs,
        out_specs=[rev(qs), rev(qs), rev(vs_spec), rev(qs), pl.BlockSpec((1, GLA_QW), lambda h, i: (0, h))],
        out_shape=[jax.ShapeDtypeStruct((S, B_QK), BF16), jax.ShapeDtypeStruct((S, B_QK), BF16),
                   jax.ShapeDtypeStruct((S, B_V), BF16), jax.ShapeDtypeStruct((S, B_QK), F32),
                   jax.ShapeDtypeStruct((1, B_QK), F32)],
        scratch_shapes=[pltpu.VMEM((GLA_HP, B_DV, B_DK), F32)], compiler_params=_params(2))(
            proj, proj, proj, proj, w_gate, b_gate, states, d_o)


def _gla_out(o, proj, gnorm, name):
    S = o.shape[0]
    tr = ROW_TILE

    def body(o_ref, r_ref, g_ref, y_ref):
        for h in range(B_HEADS):
            hs = slice(h * B_DV, (h + 1) * B_DV)
            oh = o_ref[:, hs]
            rs = lax.rsqrt(jnp.mean(oh * oh, axis=-1, keepdims=True) + EPS)
            y_ref[:, hs] = (((oh * rs) * g_ref[...]) * _silu(r_ref[:, hs])).astype(BF16)

    return pl.pallas_call(
        body, name=name, grid=(S // tr,),
        in_specs=[_row(tr), pl.BlockSpec((tr, B_V), lambda i: (i, 2)), _vec(B_DV)], out_specs=_row(tr),
        out_shape=jax.ShapeDtypeStruct((S, B_V), BF16), compiler_params=_params(1))(o, proj, gnorm)


def _gla_out_bwd(o, proj, gnorm, d_y, name):
    S = o.shape[0]
    tr = ROW_TILE

    def body(o_ref, r_ref, g_ref, dy_ref, do_ref, dr_ref, dg_ref):
        @pl.when(pl.program_id(0) == 0)
        def _():
            dg_ref[...] = jnp.zeros_like(dg_ref)

        for h in range(B_HEADS):
            hs = slice(h * B_DV, (h + 1) * B_DV)
            oh, rv, dyv = o_ref[:, hs], r_ref[:, hs], dy_ref[:, hs]
            rs = lax.rsqrt(jnp.mean(oh * oh, axis=-1, keepdims=True) + EPS)
            xhat = oh * rs
            dr_ref[:, hs] = (dyv * (xhat * g_ref[...]) * _dsilu(rv)).astype(BF16)
            dn = dyv * _silu(rv)
            dg_ref[...] += jnp.sum(dn * xhat, axis=0, keepdims=True)
            dxh = dn * g_ref[...]
            do_ref[:, hs] = rs * (dxh - xhat * jnp.mean(dxh * xhat, axis=-1, keepdims=True))

    return pl.pallas_call(
        body, name=name, grid=(S // tr,),
        in_specs=[_row(tr), pl.BlockSpec((tr, B_V), lambda i: (i, 2)), _vec(B_DV), _row(tr)],
        out_specs=[_row(tr), _row(tr), _vec(B_DV)],
        out_shape=[jax.ShapeDtypeStruct((S, B_V), F32), jax.ShapeDtypeStruct((S, B_V), BF16),
                   jax.ShapeDtypeStruct((1, B_DV), F32)],
        compiler_params=_params(1))(o, proj, gnorm, d_y)


J_SH1, J_SC1, J_G1, J_SH2, J_SC2, J_G2 = range(6)


IN_A_TN = 768
UP_TN = 2 * D_FF // 4
TOKEN_TK = 2048


def _ffn_fwd(h, w, i, tag):
    S = h.shape[0]
    u = _matmul(h, w["w_up"], "nn", F32, f"up{tag}", shape=(S, 2 * D_FF, D_MODEL), tiles=(1024, UP_TN, D_MODEL),
                b_spec=pl.BlockSpec((None, D_MODEL, UP_TN), lambda m, j, k: (j, i, 0)))
    act = _conv_gate(u, w["conv_w"][i], w["conv_b"][i:i + 1], f"conv_gate{tag}")
    f = _matmul(act, w["w_down"], "nn", F32, f"down{tag}", shape=(S, D_MODEL, D_FF), tiles=(1024, D_MODEL, D_FF),
                b_spec=pl.BlockSpec((D_FF, D_MODEL), lambda m, j, k: (i, j)))
    return u, act, f


def _ffn_bwd(dx_out, df, u, act, h, x_in, mod, w, i, tag, prev, branch, side_fn=None):
    S = h.shape[0]
    dact = _matmul(df, w["w_down"], "nt", F32, f"down_dx{tag}", shape=(S, D_FF, D_MODEL),
                   tiles=(1024, D_FF // 2, D_MODEL),
                   b_spec=pl.BlockSpec((D_FF // 2, D_MODEL), lambda m, j, k: (2 * i + j, k)))
    d_w_down = _matmul(act, df, "tn", F32, f"down_dw{tag}", shape=(D_FF, D_MODEL, S),
                       tiles=(D_FF // 2, D_MODEL, min(S, TOKEN_TK)),
                       o_spec=pl.BlockSpec((D_FF // 2, D_MODEL), lambda m, j, k: (2 * i + m, j)),
                       o_shape=(2 * D_FF, D_MODEL), prev=None if prev is None else prev["w_down"])
    du_a, du_b, dcwa, dcwb, dcba, dcbb = _conv_gate_bwd(u, dact, w["conv_w"][i], w["conv_b"][i:i + 1],
                                                        f"conv_gate_bwd{tag}")
    dh, d_w_up = None, None if prev is None else prev["w_up"]
    for half, du in enumerate((du_a, du_b)):
        dh = _matmul(du, w["w_up"], "nt", F32, f"up_dx{tag}{'ab'[half]}", shape=(S, D_MODEL, D_FF),
                     tiles=(1024, D_MODEL, UP_TN), add=dh,
                     b_spec=pl.BlockSpec((None, D_MODEL, UP_TN), lambda m, j, k, half=half: (2 * half + k, i, 0)))
        d_w_up = _matmul(h, du, "tn", F32, f"up_dw{tag}{'ab'[half]}", shape=(D_MODEL, D_FF, S),
                         tiles=(D_MODEL, UP_TN, min(S, TOKEN_TK)),
                         o_spec=pl.BlockSpec((None, D_MODEL, UP_TN), lambda m, j, k, half=half: (2 * half + j, i, 0)),
                         o_shape=(4, 2 * D_MODEL, UP_TN), prev=d_w_up)
    side = None if side_fn is None else side_fn(d_w_up, d_w_down)
    res = _norm_mod_bwd(dh, x_in, dx_out, w["norm_ffn"][i:i + 1], mod, J_SC2, f"norm_ffn_bwd{tag}", branch=branch,
                        side=side)
    res, arrived = res if side is not None else (res, None)
    dx_in, dsc2, dsh2, dgam = res[:4]
    grads = dict(w_down=d_w_down, w_up=d_w_up, norm_ffn=dgam,
                 conv_w=jnp.concatenate([dcwa[0:3], dcwb[0:3]], axis=1),
                 conv_b=jnp.concatenate([dcba, dcbb], axis=1), side=(side, arrived))
    return dx_in, res[4:], (dsh2, dsc2), grads


def _local_step(x, tgt, w, mods, early=None):
    buckets = jnp.asarray(_bucket_maps())
    S = x.shape[0]

    h1 = _norm_mod(x, w["norm_mix"][0:1], mods[0], J_SC1, J_SH1, "norm_mix0", views=True)
    geo = []
    for g, (_, dil) in enumerate(A_CONFIGS):
        tm = min(1024, S // dil)
        geo.append((dil, tm, S // dil // tm))
    w_cols = [pl.BlockSpec((None, D_MODEL, IN_A_TN), lambda m, j, k, g=g: ((4 * g + j) // 3, 0, (4 * g + j) % 3))
              for g in range(3)]
    qkv = [_matmul(h1[g], w["w_in_a"], "nn", BF16, f"in_a{g}", shape=(S, 3 * D_MODEL, D_MODEL),
                   tiles=(tm, IN_A_TN, D_MODEL),
                   a_spec=pl.BlockSpec((tm, D_MODEL), lambda m, j, k, n=n_i: (m % n, m // n)), b_spec=w_cols[g],
                   o_spec=pl.BlockSpec((tm, IN_A_TN), lambda m, j, k, n=n_i: (m % n, (m // n) * 4 + j)),
                   o_shape=(S // dil, dil * 3 * D_MODEL))
           for g, (dil, tm, n_i) in enumerate(geo)]
    bias = _bias_build(w["rel_bias"], buckets, "bias_build")
    os_, ls_ = zip(*[_attn_fwd(qkv[g], bias, g, f"attn_fwd{g}") for g in range(3)])
    omix, omix4, omix16, ltot, ltot4, ltot16 = _mix_fwd(os_, ls_, "mix_fwd")
    y0 = _matmul(omix, w["w_out_a"], "nn", F32, "out_a")
    x1, h2 = _norm_mod(x, w["norm_ffn"][0:1], mods[0], J_SC2, J_SH2, "norm_ffn0", resid=y0, gate_mod=mods[0],
                       j_gate=J_G1)
    u0, act0, f0 = _ffn_fwd(h2, w, 0, "0")

    x2, h3 = _norm_mod(x1, w["norm_mix"][1:2], mods[1], J_SC1, J_SH1, "norm_mix1", resid=f0, gate_mod=mods[0],
                       j_gate=J_G2)
    proj = _matmul(h3, w["w_in_b"], "nn", F32, "in_b", tiles=(512, B_IN_PAD, D_MODEL))
    o_gla, states = _gla_fwd(proj, w["w_gate_b"], w["b_gate_b"], "gla_fwd")
    on = _gla_out(o_gla, proj, w["gnorm_b"], "gla_out")
    y1 = _matmul(on, w["w_out_b"], "nn", F32, "out_b")
    x3, h4 = _norm_mod(x2, w["norm_ffn"][1:2], mods[1], J_SC2, J_SH2, "norm_ffn1", resid=y1, gate_mod=mods[1],
                       j_gate=J_G1)
    u1, act1, f1 = _ffn_fwd(h4, w, 1, "1")

    dx4, loss, d_norm_final, df1, dg2_1 = _final_loss(x3, f1, mods[1], J_G2, w["norm_final"], tgt, "final_loss")

    dx3, (dy1, dg1_1), (dsh2_1, dsc2_1), g_ffn1 = _ffn_bwd(dx4, df1, u1, act1, h4, x3, mods[1], w, 1, "1", None,
                                                           (y1, mods[1], J_G1))
    d_on = _matmul(dy1, w["w_out_b"], "nt", F32, "out_b_dx")
    d_w_out_b = _matmul(on, dy1, "tn", F32, "out_b_dw")
    d_ogla, d_r, d_gnorm = _gla_out_bwd(o_gla, proj, w["gnorm_b"], d_on, "gla_out_bwd")
    dq, dk, dv, dz, d_b_gate = _gla_bwd(proj, w["w_gate_b"], w["b_gate_b"], states, d_ogla, "gla_bwd")
    d_glr = _matmul(dz, w["w_gate_b"], "nt", BF16, "gate_dx")
    d_w_gate = _matmul(proj[:, 3072:3200], dz, "tn", F32, "gate_dw")
    dproj = jnp.concatenate([dq, dk, dv, d_r, d_glr], axis=1)
    dh3 = _matmul(dproj, w["w_in_b"], "nt", F32, "in_b_dx", tiles=(1024, D_MODEL, B_IN_PAD))
    d_w_in_b = _matmul(h3, dproj, "tn", F32, "in_b_dw")
    dx2, dsc1_1, dsh1_1, d_nmix1, df0, dg2_0 = _norm_mod_bwd(
        dh3, x2, dx3, w["norm_mix"][1:2], mods[1], J_SC1, "norm_mix_bwd1", branch=(f0, mods[0], J_G2))
    dmod1 = jnp.concatenate([dsh1_1, dsc1_1, dg1_1, dsh2_1, dsc2_1, dg2_1], axis=1)

    side_fn = None if early is None else (
        lambda up, down: early[0](dict(w_in_b=d_w_in_b, w_out_b=d_w_out_b, w_up=up, w_down=down)))
    dx1, (dy0, dg1_0), (dsh2_0, dsc2_0), g_ffn0 = _ffn_bwd(dx2, df0, u0, act0, h2, x1, mods[0], w, 0, "0", g_ffn1,
                                                           (y0, mods[0], J_G1), side_fn)
    d_omix = _matmul(dy0, w["w_out_a"], "nt", F32, "out_a_dx")
    d_w_out_a = _matmul(omix, dy0, "tn", F32, "out_a_dw")
    d_omix4, d_omix16 = _to_views(d_omix, "d_omix_views")
    d_omix_v, omix_v, ltot_v = (d_omix, d_omix4, d_omix16), (omix, omix4, omix16), (ltot, ltot4, ltot16)
    early_sums = None if early is None else early[1](*g_ffn0["side"])
    res0 = _attn_bwd(qkv[0], bias, d_omix, omix, ltot, 0, "attn_bwd0", exchange=early_sums)
    early_arrived = None if early is None else res0[2]
    dqkv, dbs = zip(res0[:2], *[_attn_bwd(qkv[g], bias, d_omix_v[g], omix_v[g], ltot_v[g], g, f"attn_bwd{g}")
                                for g in (1, 2)])
    d_rel_bias = _bias_bwd(jnp.concatenate(dbs, axis=0), buckets, "bias_bwd")
    dh1, d_w_in_a = [], None
    for g, (dil, tm, n_i) in enumerate(geo):
        dh1.append(_matmul(
            dqkv[g], w["w_in_a"], "nt", F32, f"in_a_dx{g}", shape=(S, D_MODEL, 3 * D_MODEL),
            tiles=(tm, D_MODEL, IN_A_TN),
            a_spec=pl.BlockSpec((tm, IN_A_TN), lambda m, j, k, n=n_i: (m % n, (m // n) * 4 + k)),
            b_spec=pl.BlockSpec((None, D_MODEL, IN_A_TN), lambda m, j, k, g=g: ((4 * g + k) // 3, j, (4 * g + k) % 3)),
            o_spec=pl.BlockSpec((tm, D_MODEL), lambda m, j, k, n=n_i: (m % n, m // n)),
            o_shape=(S // dil, dil * D_MODEL)))
        tk = min(TOKEN_TK, S // dil)
        n_k = S // dil // tk
        d_w_in_a = _matmul(
            h1[g], dqkv[g], "tn", F32, f"in_a_dw{g}", shape=(D_MODEL, 3 * D_MODEL, S), tiles=(D_MODEL, IN_A_TN, tk),
            a_spec=pl.BlockSpec((tk, D_MODEL), lambda m, j, k, n=n_k: (k % n, k // n)),
            b_spec=pl.BlockSpec((tk, IN_A_TN), lambda m, j, k, n=n_k: (k % n, (k // n) * 4 + j)),
            o_spec=pl.BlockSpec((None, D_MODEL, IN_A_TN), lambda m, j, k, g=g: ((4 * g + j) // 3, m, (4 * g + j) % 3)),
            o_shape=(4, D_MODEL, 9 * D_MODEL // 4), prev=d_w_in_a)
    dx0, dsc1_0, dsh1_0, d_nmix0 = _norm_mod_bwd(dh1[0], x, dx1, w["norm_mix"][0:1], mods[0], J_SC1, "norm_mix_bwd0",
                                                 dh_views=dh1[1:])
    dmod0 = jnp.concatenate([dsh1_0, dsc1_0, dg1_0, dsh2_0, dsc2_0, dg2_0], axis=1)

    grads = dict(
        w_in_a=d_w_in_a, w_out_a=d_w_out_a, rel_bias=d_rel_bias, w_in_b=d_w_in_b, w_gate_b=d_w_gate,
        b_gate_b=d_b_gate, gnorm_b=d_gnorm, w_out_b=d_w_out_b,
        norm_mix=jnp.concatenate([d_nmix0, d_nmix1], axis=0),
        norm_ffn=jnp.concatenate([g_ffn0["norm_ffn"], g_ffn1["norm_ffn"]], axis=0),
        b_ada=jnp.concatenate([dmod0, dmod1], axis=0),
        w_up=g_ffn0["w_up"],
        conv_w=jnp.stack([g_ffn0["conv_w"], g_ffn1["conv_w"]]),
        conv_b=jnp.concatenate([g_ffn0["conv_b"], g_ffn1["conv_b"]], axis=0),
        w_down=g_ffn0["w_down"],
        norm_final=d_norm_final)
    return loss, dx0, grads, early_sums, early_arrived


N_CHIPS = 4
N_DEV = 8
WEIGHTS = ("w_in_a", "w_out_a", "rel_bias", "w_in_b", "w_gate_b", "b_gate_b", "gnorm_b", "w_out_b", "norm_mix",
           "norm_ffn", "w_ada", "b_ada", "w_up", "conv_w", "conv_b", "w_down", "norm_final")
SHARD_AXIS = dict(w_in_a=2, w_out_a=1, w_in_b=2, w_gate_b=2, b_gate_b=1, gnorm_b=1, w_out_b=1, w_ada=2, w_up=2,
                  conv_w=2, w_down=1)
SHARDED = tuple(n for n in WEIGHTS if n in SHARD_AXIS)
REPLICATED = tuple(n for n in WEIGHTS if n not in SHARD_AXIS)
BIG = ("w_in_a", "w_out_a", "w_in_b", "w_out_b", "w_up", "w_down")
SMALL = ("w_gate_b", "b_gate_b", "gnorm_b", "conv_w")
SMALL_FULL = dict(w_gate_b=(1, 16, 512), b_gate_b=(1, 512), gnorm_b=(1, 256), conv_w=(2, 3, 5632))
R_SMALL = 16
R_TINY = 72
LOSS_SLOT = 72960
R_TINY_SHARD = 40
W_IN_B_PAD = 896

COMM_VIEW = dict(
    w_in_a=((4096, 2304), 1024, 512, 512),
    w_out_a=((1024, 1024), 256, 128, 128),
    w_in_b=((4096, W_IN_B_PAD), 1024, 512, 512),
    w_out_b=((1024, 1024), 256, 128, 128),
    w_up=((8192, 1408), 2048, 1024, 1024),
    w_down=((5632, 1024), 704, 2816, 704))


def _pack(arrs, rows):
    flat = jnp.concatenate([a.reshape(-1) for a in arrs])
    return jnp.pad(flat, (0, rows * LANES - flat.shape[0])).reshape(rows, LANES)


def _unpack(flat2d, shapes):
    flat = flat2d.reshape(-1)
    out, off = [], 0
    for shp in shapes:
        n = math.prod(shp)
        out.append(flat[off:off + n].reshape(shp))
        off += n
    return out


def _chip_slice(a, axis, k):
    n = a.shape[axis] // N_CHIPS
    return lax.slice_in_dim(a, k * n, (k + 1) * n, axis=axis)


def _place():
    mx, my, mc = lax.axis_index("x"), lax.axis_index("y"), lax.axis_index("c")
    chips = [(1 - mx, my), (mx, 1 - my), (1 - mx, 1 - my)]
    return mx, my, mc, chips


def _rcopy(src, dst, send_sem, recv_sem, dev):
    return pltpu.make_async_remote_copy(src_ref=src, dst_ref=dst, send_sem=send_sem, recv_sem=recv_sem,
                                        device_id=dev, device_id_type=MESH)


def _comm_call(body, name, ins, out_shapes, n_sems, in_place=False):
    n_in, n_out = len(ins), len(out_shapes)

    def wrapped(*refs):
        body(refs[:n_in], refs[n_in:n_in + n_out], *refs[n_in + n_out:])

    return pl.pallas_call(
        wrapped, name=name, in_specs=[HBM] * n_in, out_specs=[HBM] * n_out, out_shape=out_shapes,
        input_output_aliases={i: i for i in range(n_in)} if in_place else {},
        scratch_shapes=[pltpu.SemaphoreType.DMA((n_sems,)), pltpu.SemaphoreType.DMA((n_sems,))])(*ins)


DMA_CHUNK_BYTES = 2 * 1024 * 1024


def _rows(ref, start, size):
    return ref.at[pl.ds(pl.multiple_of(start, 16), size), :]


def _block(ref, name, k, h):
    _, bk, bh, nr = COMM_VIEW[name]
    return _rows(ref, bk * k + bh * h, nr)


def _chunks(nr, row_bytes):
    n = 1
    while nr % (2 * n) == 0 and (nr // (2 * n)) % 16 == 0 and (nr // n) * row_bytes > DMA_CHUNK_BYTES:
        n *= 2
    return [(i * (nr // n), nr // n) for i in range(n)]


def _gather_big(views, name):
    names = BIG

    def body(x_refs, out_refs, send_sems, recv_sems):
        mx, my, mc, chips = _place()
        chip = 2 * mx + my
        sibling = (mx, my, 1 - mc)
        sends = []
        for a, n in enumerate(names):
            for j, (cx, cy) in enumerate(chips):
                blk = _block(out_refs[a], n, chip, mc)
                cp = _rcopy(blk, blk, send_sems.at[6 * a + j], recv_sems.at[6 * a + j], (cx, cy, mc))
                cp.start()
                sends.append(cp)
        for a, n in enumerate(names):
            for j, (cx, cy) in enumerate(chips):
                blk = _block(out_refs[a], n, 2 * cx + cy, mc)
                _rcopy(blk, blk, send_sems.at[6 * a + j], recv_sems.at[6 * a + j], sibling).wait_recv()
                cp = _rcopy(blk, blk, send_sems.at[6 * a + 3 + j], recv_sems.at[6 * a + 3 + j], sibling)
                cp.start()
                sends.append(cp)
        for a, n in enumerate(names):
            for j, (cx, cy) in enumerate(chips):
                blk = _block(out_refs[a], n, 2 * cx + cy, 1 - mc)
                _rcopy(blk, blk, send_sems.at[6 * a + 3 + j], recv_sems.at[6 * a + 3 + j], sibling).wait_recv()
        for cp in sends:
            cp.wait_send()

    outs = _comm_call(body, name, [views[n] for n in names],
                      [jax.ShapeDtypeStruct(views[n].shape, views[n].dtype) for n in names], 6 * len(names),
                      in_place=True)
    return dict(zip(names, outs))


def _pair_exchange(names, g_refs, recv_refs, send_sems, recv_sems, start, wait):
    mx, my, mc, _ = _place()
    sibling = (mx, my, 1 - mc)
    if start:
        for a, n in enumerate(names):
            (_, cols), _, _, nr = COMM_VIEW[n]
            for k in range(N_CHIPS):
                src = _block(g_refs[a], n, k, 1 - mc)
                for lo, size in _chunks(nr, cols * 4):
                    _rcopy(src.at[pl.ds(lo, size), :], recv_refs[a].at[k, pl.ds(lo, size), :],
                           send_sems.at[a], recv_sems.at[a], sibling).start()
    if wait:
        for a in range(len(names)):
            _rcopy(recv_refs[a], recv_refs[a], send_sems.at[a], recv_sems.at[a], sibling).wait()


def _pair_recv_shapes(names):
    return [jax.ShapeDtypeStruct((N_CHIPS, COMM_VIEW[n][3], COMM_VIEW[n][0][1]), F32) for n in names]


def _rs_pair_exchange(views, name):
    names = tuple(views)

    def body(g_refs, recv_refs, send_sems, recv_sems):
        _pair_exchange(names, g_refs, recv_refs, send_sems, recv_sems, True, True)

    outs = _comm_call(body, name, [views[n] for n in names], _pair_recv_shapes(names), len(names))
    return dict(zip(names, outs))


def _pair_add(view, recv, c_idx, n, out_dtype, name):
    (_, cols), bk, bh, nr = COMM_VIEW[n]
    tr = _pick(math.gcd(bk, bh, nr), 256, 8)

    def body(c_ref, g_ref, r_ref, o_ref):
        o_ref[...] = (g_ref[...] + r_ref[...]).astype(o_ref.dtype)

    piece = pl.BlockSpec((None, tr, cols), lambda k, i, c_ref: (k, i, 0))
    return pl.pallas_call(
        body, name=name,
        grid_spec=pltpu.PrefetchScalarGridSpec(
            num_scalar_prefetch=1, grid=(N_CHIPS, nr // tr),
            in_specs=[pl.BlockSpec((tr, cols), lambda k, i, c_ref: ((bk * k + bh * c_ref[0]) // tr + i, 0)), piece],
            out_specs=piece),
        out_shape=jax.ShapeDtypeStruct((N_CHIPS, nr, cols), out_dtype), compiler_params=_params(2))(
            c_idx, view, recv)


def _chip_exchange(q_refs, out_refs, send_sems, recv_sems, start, wait):
    mx, my, mc, chips = _place()
    chip = 2 * mx + my
    sends = [_rcopy(q_refs[a].at[2 * cx + cy], out_refs[a].at[chip], send_sems.at[3 * a + j], recv_sems.at[3 * a + j],
                    (cx, cy, mc)) for a in range(len(q_refs)) for j, (cx, cy) in enumerate(chips)]
    if start:
        for cp in sends:
            cp.start()
    if wait:
        for a in range(len(q_refs)):
            for j, (cx, cy) in enumerate(chips):
                blk = out_refs[a].at[2 * cx + cy]
                _rcopy(blk, blk, send_sems.at[3 * a + j], recv_sems.at[3 * a + j], (cx, cy, mc)).wait_recv()
        for cp in sends:
            cp.wait_send()


def _rs_chip_exchange(q, name):
    names = tuple(q)

    def body(q_refs, out_refs, send_sems, recv_sems):
        _chip_exchange(q_refs, out_refs, send_sems, recv_sems, True, True)

    outs = _comm_call(body, name, [q[n] for n in names],
                      [jax.ShapeDtypeStruct(q[n].shape, q[n].dtype) for n in names], 3 * len(names))
    return dict(zip(names, outs))


def _rs_pair_gather(r, name):
    names = tuple(r)

    def body(r_refs, out_refs, send_sems, recv_sems):
        mx, my, mc, _ = _place()
        sibling = (mx, my, 1 - mc)
        for a, n in enumerate(names):
            (_, cols), _, _, nr = COMM_VIEW[n]
            for start, size in _chunks(nr, cols * 4):
                rows = _rows(out_refs[a], mc * nr + start, size)
                _rcopy(rows, rows, send_sems.at[a], recv_sems.at[a], sibling).start()
        for a, n in enumerate(names):
            nr = COMM_VIEW[n][3]
            _rcopy(_rows(out_refs[a], mc * nr, nr), _rows(out_refs[a], (1 - mc) * nr, nr), send_sems.at[a],
                   recv_sems.at[a], sibling).wait()

    outs = _comm_call(body, name, [r[n] for n in names],
                      [jax.ShapeDtypeStruct(r[n].shape, F32) for n in names], len(names), in_place=True)
    return dict(zip(names, outs))


def _gather8(x, reduce, name):
    rows = x.shape[0]

    def body(x_ref, out_ref, *rest):
        if reduce:
            buf_ref, send_sems, recv_sems = rest
        else:
            (send_sems, recv_sems), buf_ref = rest, out_ref
        mx, my, mc, _ = _place()
        me = 4 * mx + 2 * my + mc
        buf_ref[me] = x_ref[...]
        peers = []
        for j in range(1, N_DEV):
            px = 1 - mx if j & 4 else mx
            py = 1 - my if j & 2 else my
            pc = 1 - mc if j & 1 else mc
            peers.append((px, py, pc))
        sends = [_rcopy(x_ref, buf_ref.at[me], send_sems.at[j], recv_sems.at[j], p) for j, p in enumerate(peers)]
        for cp in sends:
            cp.start()
        for j, (px, py, pc) in enumerate(peers):
            _rcopy(x_ref, buf_ref.at[4 * px + 2 * py + pc], send_sems.at[j], recv_sems.at[j], (px, py, pc)).wait_recv()
        for cp in sends:
            cp.wait_send()
        if reduce:
            acc = buf_ref[0]
            for d in range(1, N_DEV):
                acc = acc + buf_ref[d]
            out_ref[...] = acc

    vmem = pl.BlockSpec(memory_space=pltpu.VMEM)
    sems = [pltpu.SemaphoreType.DMA((N_DEV - 1,)), pltpu.SemaphoreType.DMA((N_DEV - 1,))]
    if reduce:
        out_shape = jax.ShapeDtypeStruct((rows, LANES), F32)
        scratch = [pltpu.VMEM((N_DEV, rows, LANES), F32)] + sems
    else:
        out_shape = jax.ShapeDtypeStruct((N_DEV, rows, LANES), F32)
        scratch = sems
    return pl.pallas_call(body, name=name, in_specs=[vmem], out_specs=vmem, out_shape=out_shape,
                          scratch_shapes=scratch)(x)


def _sum4(p, q, chip, core, name):
    _, nr, cols = p.shape
    tr = _pick(nr, 256, 8)

    def body(chip_ref, core_ref, p0, p1, p2, p3, own, o_ref):
        s = [jnp.where(chip_ref[0] == k, own[...], pk[...]).astype(F32) for k, pk in enumerate((p0, p1, p2, p3))]
        o_ref[...] = ((s[0] + s[1]) + s[2]) + s[3]

    return pl.pallas_call(
        body, name=name,
        grid_spec=pltpu.PrefetchScalarGridSpec(
            num_scalar_prefetch=2, grid=(nr // tr,),
            in_specs=[pl.BlockSpec((None, tr, cols), lambda i, ch, co, k=k: (jnp.where(ch[0] == k, k ^ 1, k), i, 0))
                      for k in range(N_CHIPS)]
            + [pl.BlockSpec((None, tr, cols), lambda i, ch, co: (ch[0], i, 0))],
            out_specs=pl.BlockSpec((tr, cols), lambda i, ch, co: (co[0] * (nr // tr) + i, 0))),
        out_shape=jax.ShapeDtypeStruct((2 * nr, cols), F32), compiler_params=_params(1))(chip, core, p, p, p, p, q)


def _place_shard(shard, chip, n, name):
    (rows, cols), bk, bh, nr = COMM_VIEW[n]
    tr = _pick(math.gcd(bk, bh, nr), 256, 16)

    def body(chip_ref, x_ref, o_ref):
        o_ref[...] = x_ref[...].astype(BF16)

    return pl.pallas_call(
        body, name=name,
        grid_spec=pltpu.PrefetchScalarGridSpec(
            num_scalar_prefetch=1, grid=(2, nr // tr),
            in_specs=[pl.BlockSpec((tr, cols), lambda h, i, ch: (h * (nr // tr) + i, 0))],
            out_specs=pl.BlockSpec((tr, cols), lambda h, i, ch: ((bk * ch[0] + bh * h) // tr + i, 0))),
        out_shape=jax.ShapeDtypeStruct((rows, cols), BF16), compiler_params=_params(2))(chip, shard)


def _adamw(w, g, m, v, name):
    rows, cols = w.shape
    tr = _pick(rows, max(8, (1 << 20) // (4 * cols)), 8)

    def body(w_ref, g_ref, m_ref, v_ref, d_ref, mo_ref, vo_ref):
        gv = g_ref[...]
        mn = ADAM_B1 * m_ref[...] + (1.0 - ADAM_B1) * gv
        vn = ADAM_B2 * v_ref[...] + (1.0 - ADAM_B2) * (gv * gv)
        m_hat = mn / (1.0 - ADAM_B1 ** ADAM_STEP)
        v_hat = vn / (1.0 - ADAM_B2 ** ADAM_STEP)
        d_ref[...] = -ADAM_LR * (m_hat / (jnp.sqrt(v_hat) + ADAM_EPS) + ADAM_WD * w_ref[...])
        mo_ref[...] = mn
        vo_ref[...] = vn

    shape = jax.ShapeDtypeStruct(w.shape, F32)
    return pl.pallas_call(
        body, name=name, grid=(rows // tr,), in_specs=[_row(tr, cols)] * 4, out_specs=[_row(tr, cols)] * 3,
        out_shape=[shape] * 3, compiler_params=_params(1))(w, g, m, v)


W_IN_B_SHARD = 772


def _shard_view(n, a):
    return a.reshape(-1, a.shape[-1])


def _gather_weights(p, chip):
    shards = {n: _shard_view(n, p[n]) for n in BIG}
    shards["w_in_b"] = jnp.pad(shards["w_in_b"], ((0, 0), (0, W_IN_B_PAD - W_IN_B_SHARD)))
    big = _gather_big({n: _place_shard(shards[n], chip, n, f"place_{n}") for n in BIG}, "gather_weights")
    small = _gather8(_pack([p[n] for n in SMALL], R_SMALL), False, "gather_small")
    pieces = [_unpack(small[2 * k], [p[n].shape for n in SMALL]) for k in range(N_CHIPS)]
    full = {n: jnp.concatenate([pieces[k][i] for k in range(N_CHIPS)], axis=SHARD_AXIS[n])
            for i, n in enumerate(SMALL)}
    wb = big["w_in_b"].reshape(N_CHIPS, D_MODEL, W_IN_B_PAD)
    wb = jnp.concatenate([wb[k, :, :W_IN_B_SHARD] for k in range(N_CHIPS)], axis=1)
    return dict(
        w_in_a=big["w_in_a"].reshape(N_CHIPS, D_MODEL, -1), w_out_a=big["w_out_a"], w_out_b=big["w_out_b"],
        w_in_b=jnp.concatenate([wb[:, :2048], wb[:, 2064:3088], wb[:, 2048:2064],
                                jnp.zeros((D_MODEL, B_IN_PAD - 3088), BF16)], axis=1),
        w_up=big["w_up"].reshape(N_CHIPS, 2 * D_MODEL, -1), w_down=big["w_down"],
        w_gate_b=jnp.pad(full["w_gate_b"][0], ((0, 128 - B_GATE_RANK), (0, 0))),
        b_gate_b=full["b_gate_b"], gnorm_b=full["gnorm_b"], conv_w=full["conv_w"],
        rel_bias=p["rel_bias"], norm_mix=p["norm_mix"], norm_ffn=p["norm_ffn"], b_ada=p["b_ada"],
        conv_b=p["conv_b"], norm_final=p["norm_final"].reshape(1, D_MODEL))


EARLY = ("w_in_b", "w_out_b", "w_up", "w_down")
LATE = ("w_in_a", "w_out_a")


def _grad_views(g, names):
    views = {n: g[n].reshape(COMM_VIEW[n][0]) for n in names if n != "w_in_b"}
    if "w_in_b" in names:
        gb = g["w_in_b"]
        gb = jnp.concatenate([gb[:, :2048], gb[:, 3072:3088], gb[:, 2048:3072]], axis=1)
        gb = jnp.stack([jnp.pad(gb[:, k * W_IN_B_SHARD:(k + 1) * W_IN_B_SHARD],
                                ((0, 0), (0, W_IN_B_PAD - W_IN_B_SHARD))) for k in range(N_CHIPS)])
        views["w_in_b"] = gb.reshape(COMM_VIEW["w_in_b"][0])
    return views


def _tiny_grads(g):
    out = {n: g[n] for n in REPLICATED if n != "norm_final"}
    out.update(norm_final=g["norm_final"].reshape(D_MODEL), w_gate_b=g["w_gate_b"][:B_GATE_RANK][None],
               b_gate_b=g["b_gate_b"], gnorm_b=g["gnorm_b"], conv_w=g["conv_w"])
    return out


def kernel(x, c, w_in_a, w_out_a, rel_bias, w_in_b, w_gate_b, b_gate_b, gnorm_b, w_out_b, norm_mix, norm_ffn, w_ada, b_ada, w_up, conv_w, conv_b, w_down, norm_final, loss_target, m_w_in_a, m_w_out_a, m_rel_bias, m_w_in_b, m_w_gate_b, m_b_gate_b, m_gnorm_b, m_w_out_b, m_norm_mix, m_norm_ffn, m_w_ada, m_b_ada, m_w_up, m_conv_w, m_conv_b, m_w_down, m_norm_final, v_w_in_a, v_w_out_a, v_rel_bias, v_w_in_b, v_w_gate_b, v_b_gate_b, v_gnorm_b, v_w_out_b, v_norm_mix, v_norm_ffn, v_w_ada, v_b_ada, v_w_up, v_conv_w, v_conv_b, v_w_down, v_norm_final):
    p = dict(zip(WEIGHTS, (w_in_a, w_out_a, rel_bias, w_in_b, w_gate_b, b_gate_b, gnorm_b, w_out_b, norm_mix,
                           norm_ffn, w_ada, b_ada, w_up, conv_w, conv_b, w_down, norm_final)))
    pm = dict(zip(WEIGHTS, (m_w_in_a, m_w_out_a, m_rel_bias, m_w_in_b, m_w_gate_b, m_b_gate_b, m_gnorm_b, m_w_out_b,
                            m_norm_mix, m_norm_ffn, m_w_ada, m_b_ada, m_w_up, m_conv_w, m_conv_b, m_w_down,
                            m_norm_final)))
    pv = dict(zip(WEIGHTS, (v_w_in_a, v_w_out_a, v_rel_bias, v_w_in_b, v_w_gate_b, v_b_gate_b, v_gnorm_b, v_w_out_b,
                            v_norm_mix, v_norm_ffn, v_w_ada, v_b_ada, v_w_up, v_conv_w, v_conv_b, v_w_down,
                            v_norm_final)))
    S = x.shape[1]

    chip = 2 * lax.axis_index("x") + lax.axis_index("y")
    core = lax.axis_index("c").astype(jnp.int32).reshape(1)
    chip_s = chip.astype(jnp.int32).reshape(1)

    me = 2 * chip + lax.axis_index("c")

    def own_columns(a):
        return lax.dynamic_slice_in_dim(a, chip * ADA_TN, ADA_TN, axis=a.ndim - 1)

    c_all = _gather8(jnp.pad(c, ((0, 7), (0, 0))), False, "gather_c")[:, 0, :]
    mod_cols = _ada_mod_shard(c_all, w_ada, own_columns(b_ada)[:, None, :], "ada_mod")
    mod_all = _gather8(mod_cols.reshape(-1, LANES), False, "gather_mod").reshape(N_CHIPS, 2, 2, N_DEV, ADA_TN)
    mod_mine = lax.dynamic_index_in_dim(mod_all[:, 0], me, axis=2, keepdims=False)
    mods = [jnp.broadcast_to(mod_mine[:, l].reshape(1, 6 * D_MODEL), (8, 6 * D_MODEL)) for l in range(2)]

    w = _gather_weights(p, chip_s)
    def pair_add(views, recv):
        return {n: _pair_add(views[n], recv[n], core, n, BF16, f"grads_pair_add_{n}") for n in views}

    loss, dx0, grads, pair, from_chips = _local_step(
        x.reshape(S, D_MODEL), loss_target.reshape(S, D_MODEL), w, mods,
        early=(lambda g: _grad_views(g, EARLY), pair_add))

    dmod_rows = 2 * 6 * D_MODEL // LANES
    dmod_all = _gather8(jnp.pad(grads["b_ada"].reshape(dmod_rows, LANES), ((0, 16 - dmod_rows), (0, 0))), False,
                        "gather_dmod")[:, :dmod_rows].reshape(N_DEV, 2, 6 * D_MODEL)
    g_w_ada = _ada_grad_shard(c_all, own_columns(dmod_all).transpose(1, 0, 2), "ada_grad")

    late_views = _grad_views(grads, LATE)
    late = pair_add(late_views, _rs_pair_exchange(late_views, "grads_pair_exchange"))
    pair.update(late)
    from_chips.update(_rs_chip_exchange(late, "grads_chip_exchange"))
    g_big = _rs_pair_gather({n: _sum4(from_chips[n], pair[n], chip_s, core, f"grads_chip_sum_{n}") for n in BIG},
                            "grads_pair_gather")
    g_big["w_in_b"] = g_big["w_in_b"][:, :W_IN_B_SHARD]

    tiny = _tiny_grads(grads)
    tiny_names = SMALL + REPLICATED
    tiny_full = {n: SMALL_FULL[n] if n in SMALL_FULL else p[n].shape for n in tiny_names}
    tiny_sum = _gather8(_pack([tiny[n] for n in tiny_names] + [loss[0, 0:1]], R_TINY), True, "grads_tiny_sum")
    g_tiny = dict(zip(tiny_names, _unpack(tiny_sum, [tiny_full[n] for n in tiny_names])))
    for n in SMALL:
        width = p[n].shape[SHARD_AXIS[n]]
        g_tiny[n] = lax.dynamic_slice_in_dim(g_tiny[n], chip * width, width, axis=SHARD_AXIS[n])
    total_loss = tiny_sum.reshape(-1)[LOSS_SLOT]

    g_big["w_ada"] = _shard_view("w_ada", g_w_ada)
    out = {}
    for n in BIG + ("w_ada",):
        res = _adamw(_shard_view(n, p[n]), g_big[n], _shard_view(n, pm[n]), _shard_view(n, pv[n]), f"adamw_{n}")
        out[n] = [t.reshape(p[n].shape) for t in (g_big[n],) + tuple(res)]
    res = _adamw(*[_pack([d[n] for n in tiny_names], R_TINY_SHARD) for d in (p, g_tiny, pm, pv)], "adamw_tiny")
    unpacked = [_unpack(t, [p[n].shape for n in tiny_names]) for t in res]
    for i, n in enumerate(tiny_names):
        out[n] = [g_tiny[n]] + [u[i] for u in unpacked]

    return (total_loss, dx0.reshape(x.shape), *[out[n][0] for n in WEIGHTS], *[out[n][1] for n in WEIGHTS],
            *[out[n][2] for n in WEIGHTS], *[out[n][3] for n in WEIGHTS])
```
